```python
import math
import jax
import jax.numpy as jnp
from jax import lax
import numpy as np

D_MODEL = 1024
BATCH = 2
SEQ = 8192
DEPTH = 1
DEC_BATCH = 128
DEC_SEQ = 8
PAST_LEN = 2048
PAGE_SIZE = 128

RWKV_HEADS = 8
RWKV_HEAD_DIM = 64
RWKV_WIDTH = RWKV_HEADS * RWKV_HEAD_DIM
DECAY_LORA = 32
ICLR_LORA = 32
GATE_LORA = 96
RWKV_PROJ = 3 * RWKV_WIDTH + DECAY_LORA + ICLR_LORA + GATE_LORA
RWKV_SPLITS = [RWKV_WIDTH, 2 * RWKV_WIDTH, 3 * RWKV_WIDTH,
               3 * RWKV_WIDTH + DECAY_LORA, 3 * RWKV_WIDTH + DECAY_LORA + ICLR_LORA]
GN_EPS = 64e-5

DIFF_HEADS = 4
DIFF_DK = 64
DIFF_DV = 2 * DIFF_DK
DIFF_QK_WIDTH = DIFF_HEADS * 2 * DIFF_DK
DIFF_V_WIDTH = DIFF_HEADS * DIFF_DV
DIFF_PROJ = 2 * DIFF_QK_WIDTH + DIFF_V_WIDTH
DIFF_SCALE = DIFF_DK ** -0.5
Q_BLOCK = 128

MIX_WIDTH = RWKV_WIDTH + DIFF_V_WIDTH
IN_PROJ = RWKV_PROJ + DIFF_PROJ

N_EXPERTS = 256
N_GROUPS = 8
TOPK_GROUPS = 4
TOP_K = 8
EXPERT_DIM = 256
SHARED_DIM = 256
ROUTE_SCALE = 2.5
EXPERT_BLOCK = 128

NORM_EPS = 1e-6

kernel_name = 'hymba_rwkv7_diffattn_moe_step'


def rms_norm(x, g):
    x32 = x.astype(jnp.float32)
    y = x32 * lax.rsqrt(jnp.mean(x32 * x32, axis=-1, keepdims=True) + NORM_EPS)
    return (y * g.astype(jnp.float32)).astype(x.dtype)


def modulate(h, shift, scale):
    return h * (1.0 + scale[:, None, :]) + shift[:, None, :]


def wkv_step(state, inp):
    r, decay, k, v, a_vec, b_vec = inp
    sa = jnp.einsum('bhvk,bhk->bhv', state, a_vec)
    state = (state * decay[:, :, None, :] + sa[..., None] * b_vec[:, :, None, :]
             + v[..., None] * k[:, :, None, :])
    return state, jnp.einsum('bhvk,bhk->bhv', state, r)


def rwkv7_time_mix(p, prev_row, wkv0, mu, w0, w_lora_up, a0, a_lora_up, g_lora_up,
                   k_k, k_a, r_k, lnx_g, lnx_b):
    b, t = p.shape[:2]
    p_prev = jnp.concatenate([prev_row.astype(p.dtype), p[:, :-1]], axis=1)
    xs = p + (p_prev - p) * mu
    r, k, v, xw, xa, xg = jnp.split(xs, RWKV_SPLITS, axis=-1)
    w = -jax.nn.softplus(-(w0 + jnp.tanh(xw) @ w_lora_up)) - 0.5
    a = jax.nn.sigmoid(a0 + xa @ a_lora_up)
    g = jax.nn.sigmoid(xg) @ g_lora_up

    def heads(z):
        return z.reshape(b, t, RWKV_HEADS, RWKV_HEAD_DIM).astype(jnp.float32)

    r, w, k, v, a = heads(r), heads(w), heads(k), heads(v), heads(a)
    kk = k * k_k.reshape(RWKV_HEADS, RWKV_HEAD_DIM).astype(jnp.float32)
    kk = kk / jnp.maximum(jnp.sqrt(jnp.sum(kk * kk, axis=-1, keepdims=True)), 1e-12)
    k = k * (1.0 + (a - 1.0) * k_a.reshape(RWKV_HEADS, RWKV_HEAD_DIM).astype(jnp.float32))
    decay = jnp.exp(-jnp.exp(w))

    def tm(z):
        return jnp.moveaxis(z, 1, 0)

    state, y = lax.scan(wkv_step, wkv0.astype(jnp.float32),
                        (tm(r), tm(decay), tm(k), tm(v), tm(-kk), tm(kk * a)))
    y = jnp.moveaxis(y, 0, 1)
    mean = jnp.mean(y, axis=-1, keepdims=True)
    var = jnp.mean(jnp.square(y - mean), axis=-1, keepdims=True)
    y = ((y - mean) * lax.rsqrt(var + GN_EPS)).reshape(b, t, RWKV_WIDTH)
    y = y * lnx_g.astype(jnp.float32) + lnx_b.astype(jnp.float32)
    bonus = jnp.sum(r * k * r_k.astype(jnp.float32), axis=-1, keepdims=True) * v
    y = (y + bonus.reshape(b, t, RWKV_WIDTH)) * g.astype(jnp.float32)
    return y.astype(p.dtype), p[:, -1:], state


def diff_attention(q, k, v, q_pos, k_pos, lam):
    s = jnp.einsum('bqhcd,bkhcd->bhcqk', q, k).astype(jnp.float32) * DIFF_SCALE
    s = jnp.where(k_pos[None, :] <= q_pos[:, None], s, -jnp.inf)
    p = jax.nn.softmax(s, axis=-1)
    attn = p[:, :, 0] - lam * p[:, :, 1]
    return jnp.einsum('bhqk,bkhd->bqhd', attn.astype(v.dtype), v)


def prompt_diff_attention(q, k, v, lam):
    b, t = q.shape[:2]
    nb = t // Q_BLOCK
    q_blocks = jnp.swapaxes(q.reshape(b, nb, Q_BLOCK, DIFF_HEADS, 2, DIFF_DK), 0, 1)
    q_pos = jnp.arange(t).reshape(nb, Q_BLOCK)
    k_pos = jnp.arange(t)
    out = lax.map(lambda blk: diff_attention(blk[0], k, v, blk[1], k_pos, lam), (q_blocks, q_pos))
    return jnp.swapaxes(out, 0, 1).reshape(b, t, DIFF_HEADS, DIFF_DV)


def routed_experts(hf, top_e, gate, we_gate, we_up, we_down):
    t, d = hf.shape
    n_assign = t * TOP_K
    expert_flat = top_e.reshape(-1)
    token_flat = jnp.arange(n_assign, dtype=jnp.int32) // TOP_K
    gate_flat = gate.reshape(-1)
    order = jnp.argsort(expert_flat)
    sorted_e = expert_flat[order]
    counts = jnp.bincount(expert_flat, length=N_EXPERTS)
    padded = (counts + EXPERT_BLOCK - 1) // EXPERT_BLOCK * EXPERT_BLOCK
    start_sorted = jnp.cumsum(counts) - counts
    end_padded = jnp.cumsum(padded)
    start_padded = end_padded - padded
    dest = start_padded[sorted_e] + (jnp.arange(n_assign) - start_sorted[sorted_e])
    n_rows = (n_assign + N_EXPERTS * (EXPERT_BLOCK - 1) + EXPERT_BLOCK - 1) // EXPERT_BLOCK * EXPERT_BLOCK
    n_blocks = n_rows // EXPERT_BLOCK
    row_token = jnp.full((n_rows,), t, jnp.int32).at[dest].set(token_flat[order])
    row_gate = jnp.zeros((n_rows,), jnp.float32).at[dest].set(gate_flat[order])
    block_expert = jnp.minimum(
        jnp.searchsorted(end_padded, jnp.arange(n_blocks) * EXPERT_BLOCK, side='right'), N_EXPERTS - 1)
    h_pad = jnp.concatenate([hf, jnp.zeros((1, d), hf.dtype)], axis=0)

    def expert_block(args):
        e, toks, g = args
        xb = h_pad[toks]
        yb = (jax.nn.silu(xb @ we_gate[e]) * (xb @ we_up[e])) @ we_down[e]
        return yb * g[:, None].astype(yb.dtype)

    y_rows = lax.map(expert_block, (block_expert,
                                    row_token.reshape(n_blocks, EXPERT_BLOCK),
                                    row_gate.reshape(n_blocks, EXPERT_BLOCK)))
    out = jnp.zeros((t + 1, d), y_rows.dtype).at[row_token].add(y_rows.reshape(n_rows, d))
    return out[:t]


def moe_ffn(h, w_router, router_bias, we_gate, we_up, we_down, ws_gate, ws_up, ws_down):
    lead = h.shape[:-1]
    hf = h.reshape(-1, h.shape[-1])
    t = hf.shape[0]
    scores = jax.nn.sigmoid((hf @ w_router).astype(jnp.float32))
    biased = scores + router_bias.astype(jnp.float32)
    group_score = lax.top_k(biased.reshape(t, N_GROUPS, N_EXPERTS // N_GROUPS), 2)[0].sum(-1)
    _, top_g = lax.top_k(group_score, TOPK_GROUPS)
    group_mask = jnp.any(top_g[..., None] == jnp.arange(N_GROUPS), axis=-2)
    expert_mask = jnp.repeat(group_mask, N_EXPERTS // N_GROUPS, axis=-1)
    _, top_e = lax.top_k(jnp.where(expert_mask, biased, -jnp.inf), TOP_K)
    gate = jnp.take_along_axis(scores, top_e, axis=-1)
    gate = gate / jnp.sum(gate, axis=-1, keepdims=True) * ROUTE_SCALE
    routed = routed_experts(hf, top_e, gate, we_gate, we_up, we_down)
    shared = (jax.nn.silu(hf @ ws_gate) * (hf @ ws_up)) @ ws_down
    return (shared + routed).reshape(*lead, -1)


def setup_inputs(seed: int = 0) -> dict:
    key = jax.random.key(seed)
    ks = iter(jax.random.split(key, 48))
    f32 = jnp.float32
    L = DEPTH

    def nrm(shape, scale=1.0):
        return jax.random.normal(next(ks), shape, f32) * scale

    def unif(shape, lo, hi):
        return jax.random.uniform(next(ks), shape, f32, lo, hi)

    n_pages = PAST_LEN // PAGE_SIZE
    n_used = DEC_BATCH * n_pages
    n_pool = n_used + n_used // 4
    x_prompt = nrm((BATCH, SEQ, D_MODEL))
    x_sample = nrm((DEC_BATCH, DEC_SEQ, D_MODEL))
    cache_k = nrm((L, n_pool, PAGE_SIZE, DIFF_HEADS, 2, DIFF_DK))
    cache_v = nrm((L, n_pool, PAGE_SIZE, DIFF_HEADS, DIFF_DV))
    state_wkv = nrm((L, DEC_BATCH, RWKV_HEADS, RWKV_HEAD_DIM, RWKV_HEAD_DIM), 0.5)
    state_shift = nrm((L, DEC_BATCH, 1, RWKV_PROJ))
    page_table = jax.random.permutation(next(ks), n_pool)[:n_used].reshape(DEC_BATCH, n_pages).astype(jnp.int32)
    return {
        'x_prompt': x_prompt,
        'x_sample': x_sample,
        'cache_k': cache_k,
        'cache_v': cache_v,
        'state_wkv': state_wkv,
        'state_shift': state_shift,
        'page_table': page_table,
        'c_prompt': nrm((BATCH, D_MODEL)),
        'c_sample': nrm((DEC_BATCH, D_MODEL)),
        'w_ada': nrm((L, D_MODEL, 6 * D_MODEL), 0.5 * D_MODEL ** -0.5),
        'b_ada': nrm((L, 6 * D_MODEL), 0.02),
        'norm1_g': 1.0 + nrm((L, D_MODEL), 0.1),
        'norm2_g': 1.0 + nrm((L, D_MODEL), 0.1),
        'w_in': nrm((L, D_MODEL, IN_PROJ), D_MODEL ** -0.5),
        'mu_shift': unif((L, RWKV_PROJ), 0.0, 1.0),
        'w0': unif((L, RWKV_WIDTH), -5.0, 0.0),
        'w_lora_up': nrm((L, DECAY_LORA, RWKV_WIDTH), 0.1),
        'a0': nrm((L, RWKV_WIDTH), 0.5),
        'a_lora_up': nrm((L, ICLR_LORA, RWKV_WIDTH), 0.5 * ICLR_LORA ** -0.5),
        'g_lora_up': nrm((L, GATE_LORA, RWKV_WIDTH), GATE_LORA ** -0.5),
        'k_k': 0.85 + nrm((L, RWKV_WIDTH), 0.05),
        'k_a': 1.0 + nrm((L, RWKV_WIDTH), 0.05),
        'r_k': nrm((L, RWKV_HEADS, RWKV_HEAD_DIM), 0.1),
        'lnx_g': 1.0 + nrm((L, RWKV_WIDTH), 0.1),
        'lnx_b': nrm((L, RWKV_WIDTH), 0.02),
        'qn_g': 1.0 + nrm((L, DIFF_DK), 0.1),
        'kn_g': 1.0 + nrm((L, DIFF_DK), 0.1),
        'lam_q1': nrm((L, DIFF_DK), 0.1),
        'lam_k1': nrm((L, DIFF_DK), 0.1),
        'lam_q2': nrm((L, DIFF_DK), 0.1),
        'lam_k2': nrm((L, DIFF_DK), 0.1),
        'subln_g': 1.0 + nrm((L, DIFF_DV), 0.1),
        'w_out': nrm((L, MIX_WIDTH, D_MODEL), MIX_WIDTH ** -0.5),
        'w_router': nrm((L, D_MODEL, N_EXPERTS), D_MODEL ** -0.5),
        'router_bias': nrm((L, N_EXPERTS), 0.01),
        'we_gate': nrm((L, N_EXPERTS, D_MODEL, EXPERT_DIM), D_MODEL ** -0.5),
        'we_up': nrm((L, N_EXPERTS, D_MODEL, EXPERT_DIM), D_MODEL ** -0.5),
        'we_down': nrm((L, N_EXPERTS, EXPERT_DIM, D_MODEL), EXPERT_DIM ** -0.5),
        'ws_gate': nrm((L, D_MODEL, SHARED_DIM), D_MODEL ** -0.5),
        'ws_up': nrm((L, D_MODEL, SHARED_DIM), D_MODEL ** -0.5),
        'ws_down': nrm((L, SHARED_DIM, D_MODEL), SHARED_DIM ** -0.5),
    }


def reference(x_prompt, x_sample, cache_k, cache_v, state_wkv, state_shift, page_table,
              c_prompt, c_sample, w_ada, b_ada, norm1_g, norm2_g, w_in, mu_shift, w0,
              w_lora_up, a0, a_lora_up, g_lora_up, k_k, k_a, r_k, lnx_g, lnx_b, qn_g, kn_g,
              lam_q1, lam_k1, lam_q2, lam_k2, subln_g, w_out, w_router, router_bias,
              we_gate, we_up, we_down, ws_gate, ws_up, ws_down):

    def attend_prompt(layer, q, k, v, lam):
        return prompt_diff_attention(q, k, v, lam)

    def attend_sample(layer, q, k, v, lam):
        db, s = q.shape[:2]
        past = page_table.shape[1] * PAGE_SIZE
        k_past = cache_k[layer, page_table].reshape(db, past, DIFF_HEADS, 2, DIFF_DK).astype(k.dtype)
        v_past = cache_v[layer, page_table].reshape(db, past, DIFF_HEADS, DIFF_DV).astype(v.dtype)
        k_all = jnp.concatenate([k_past, k], axis=1)
        v_all = jnp.concatenate([v_past, v], axis=1)
        return diff_attention(q, k_all, v_all, past + jnp.arange(s), jnp.arange(past + s), lam)

    def run_layer(layer, x, c, shift_prev, wkv0, attend):
        b, t = x.shape[:2]
        lam_init = 0.8 - 0.6 * math.exp(-0.3 * layer)
        lam = (jnp.exp(jnp.sum(lam_q1[layer].astype(jnp.float32) * lam_k1[layer].astype(jnp.float32)))
               - jnp.exp(jnp.sum(lam_q2[layer].astype(jnp.float32) * lam_k2[layer].astype(jnp.float32)))
               + lam_init)
        shift1, scale1, gate1, shift2, scale2, gate2 = jnp.split(
            jax.nn.silu(c) @ w_ada[layer] + b_ada[layer], 6, axis=-1)
        h = modulate(rms_norm(x, norm1_g[layer]), shift1, scale1)
        proj = h @ w_in[layer]
        y_rwkv, last_row, wkv = rwkv7_time_mix(
            proj[..., :RWKV_PROJ], shift_prev, wkv0, mu_shift[layer], w0[layer], w_lora_up[layer],
            a0[layer], a_lora_up[layer], g_lora_up[layer], k_k[layer], k_a[layer], r_k[layer],
            lnx_g[layer], lnx_b[layer])
        q = proj[..., RWKV_PROJ:RWKV_PROJ + DIFF_QK_WIDTH].reshape(b, t, DIFF_HEADS, 2, DIFF_DK)
        k = proj[..., RWKV_PROJ + DIFF_QK_WIDTH:RWKV_PROJ + 2 * DIFF_QK_WIDTH].reshape(b, t, DIFF_HEADS, 2, DIFF_DK)
        v = proj[..., RWKV_PROJ + 2 * DIFF_QK_WIDTH:].reshape(b, t, DIFF_HEADS, DIFF_DV)
        q = rms_norm(q, qn_g[layer])
        k = rms_norm(k, kn_g[layer])
        o = attend(layer, q, k, v, lam)
        o = rms_norm(o, subln_g[layer]) * (1.0 - lam_init)
        mixed = jnp.concatenate([y_rwkv, o.reshape(b, t, DIFF_V_WIDTH).astype(y_rwkv.dtype)], axis=-1) @ w_out[layer]
        x = x + gate1[:, None, :] * mixed
        h2 = modulate(rms_norm(x, norm2_g[layer]), shift2, scale2)
        x = x + gate2[:, None, :] * moe_ffn(h2, w_router[layer], router_bias[layer], we_gate[layer],
                                             we_up[layer], we_down[layer], ws_gate[layer],
                                             ws_up[layer], ws_down[layer])
        return x, k, v, wkv, last_row

    y_prompt, y_sample = x_prompt, x_sample
    kp, vp, wp, sp, ks, vs, ws, ss = [], [], [], [], [], [], [], []
    for layer in range(DEPTH):
        b = y_prompt.shape[0]
        zero_shift = jnp.zeros((b, 1, RWKV_PROJ), y_prompt.dtype)
        zero_wkv = jnp.zeros((b, RWKV_HEADS, RWKV_HEAD_DIM, RWKV_HEAD_DIM), jnp.float32)
        y_prompt, k_new, v_new, wkv_new, shift_new = run_layer(
            layer, y_prompt, c_prompt, zero_shift, zero_wkv, attend_prompt)
        kp.append(k_new)
        vp.append(v_new)
        wp.append(wkv_new)
        sp.append(shift_new)
        y_sample, k_new, v_new, wkv_new, shift_new = run_layer(
            layer, y_sample, c_sample, state_shift[layer], state_wkv[layer], attend_sample)
        ks.append(k_new)
        vs.append(v_new)
        ws.append(wkv_new)
        ss.append(shift_new)
    return (y_prompt, y_sample, jnp.stack(kp), jnp.stack(vp), jnp.stack(wp), jnp.stack(sp),
            jnp.stack(ks), jnp.stack(vs), jnp.stack(ws), jnp.stack(ss))
```

```python
import functools
import math

import jax
import jax.numpy as jnp
from jax import lax
from jax.experimental import pallas as pl
from jax.experimental.pallas import tpu as pltpu

F32 = jnp.float32
BF16 = jnp.bfloat16
I32 = jnp.int32

RWKV_HEADS = 8
RWKV_HEAD_DIM = 64
RWKV_WIDTH = RWKV_HEADS * RWKV_HEAD_DIM
DECAY_LORA = 32
ICLR_LORA = 32
GATE_LORA = 96
LORA_WIDTH = DECAY_LORA + ICLR_LORA + GATE_LORA
LORA_PAD = 256
RWKV_PROJ = 3 * RWKV_WIDTH + LORA_WIDTH
GN_EPS = 64e-5
DIFF_HEADS = 4
DIFF_DK = 64
DIFF_DV = 2 * DIFF_DK
DIFF_QK_WIDTH = DIFF_HEADS * 2 * DIFF_DK
DIFF_V_WIDTH = DIFF_HEADS * DIFF_DV
DIFF_SCALE = DIFF_DK ** -0.5
N_EXPERTS = 256
N_GROUPS = 8
GROUP_SIZE = N_EXPERTS // N_GROUPS
TOPK_GROUPS = 4
TOP_K = 8
ROUTE_SCALE = 2.5
NORM_EPS = 1e-6

LANES = 128
SUBLANES = 8
ROW_TILE = 8
VMEM_LIMIT = 48 * 1024 * 1024

EXPERT_BLOCK = 128
ROUTE_TILE = 128
NEG_INF = float("-inf")


def _bdot(a, b):
    return jnp.dot(a.astype(BF16), b.astype(BF16), preferred_element_type=F32)


def _bdot_nt(a, b):
    return lax.dot_general(a.astype(BF16), b.astype(BF16), (((1,), (1,)), ((), ())), preferred_element_type=F32)


def _bdot_tn(a, b):
    return lax.dot_general(a.astype(BF16), b.astype(BF16), (((0,), (0,)), ((), ())), preferred_element_type=F32)


def _sigmoid(x):
    return 1.0 / (1.0 + jnp.exp(-x))


def _silu(x):
    return x * _sigmoid(x)


def _params(*sem, vmem=VMEM_LIMIT):
    return pltpu.CompilerParams(dimension_semantics=sem, vmem_limit_bytes=vmem)


def _ada_body(c_ref, w_ref, b_ref, o_ref):
    o_ref[...] = _bdot(_silu(c_ref[...]), w_ref[...]) + b_ref[...]


def _ada(c, w, b):
    rows, d = c.shape
    n = w.shape[1]
    tn = 512
    return pl.pallas_call(
        _ada_body,
        grid=(n // tn,),
        in_specs=[pl.BlockSpec((rows, d), lambda j: (0, 0)),
                  pl.BlockSpec((d, tn), lambda j: (0, j)),
                  pl.BlockSpec((1, tn), lambda j: (0, j))],
        out_specs=pl.BlockSpec((rows, tn), lambda j: (0, j)),
        out_shape=jax.ShapeDtypeStruct((rows, n), F32),
        compiler_params=_params("parallel"),
        name="ada",
    )(c, w, b)


def _mod_spec(mod, tm, tiles_per_seq):
    if mod.ndim == 3:
        return pl.BlockSpec((None, 1, mod.shape[-1]), lambda i: (i // tiles_per_seq, 0, 0))
    return pl.BlockSpec((tm, mod.shape[-1]), lambda i: (i, 0))


def _full_spec(a):
    nd = a.ndim
    return pl.BlockSpec(a.shape, lambda *_: (0,) * nd)


def _rms(x, g):
    return x * lax.rsqrt(jnp.mean(x * x, axis=-1, keepdims=True) + NORM_EPS) * g


def _inproj_body(x_ref, g_ref, sh_ref, sc_ref, wm_ref, wl_ref, wq_ref, wk_ref, wv_ref, qg_ref, kg_ref, seg_ref,
                 pm_ref, pl_ref, q_ref, k_ref, v_ref, qb_ref, kb_ref, vb_ref):
    h = (_rms(x_ref[...], g_ref[...]) * (1.0 + sc_ref[...]) + sh_ref[...]).astype(BF16)
    pm_ref[...] = jnp.dot(h, wm_ref[...], preferred_element_type=F32)
    pl_ref[...] = jnp.dot(h, wl_ref[...], preferred_element_type=F32)
    seg = seg_ref[...]

    def head_norm(z, gain):
        ms = _bdot(z * z, seg) * (1.0 / DIFF_DK)
        return z * lax.rsqrt(ms + NORM_EPS) * gain

    q = head_norm(jnp.dot(h, wq_ref[...], preferred_element_type=F32), qg_ref[...])
    k = head_norm(jnp.dot(h, wk_ref[...], preferred_element_type=F32), kg_ref[...])
    v = jnp.dot(h, wv_ref[...], preferred_element_type=F32)
    q_ref[...] = q
    k_ref[...] = k
    v_ref[...] = v
    qb_ref[...] = (q * DIFF_SCALE).astype(BF16)
    kb_ref[...] = k.astype(BF16)
    vb_ref[...] = v.astype(BF16)


def _inproj(x, g, shift, scale, wts, tm, tiles_per_seq):
    t, d = x.shape
    wm, wl, wq, wk, wv, qg, kg, seg = wts
    row = lambda n: pl.BlockSpec((tm, n), lambda i: (i, 0))
    outs = [(3 * RWKV_WIDTH, F32), (LORA_PAD, F32), (DIFF_QK_WIDTH, F32), (DIFF_QK_WIDTH, F32), (DIFF_V_WIDTH, F32),
            (DIFF_QK_WIDTH, BF16), (DIFF_QK_WIDTH, BF16), (DIFF_V_WIDTH, BF16)]
    return pl.pallas_call(
        _inproj_body,
        grid=(t // tm,),
        in_specs=[row(d), _full_spec(g), _mod_spec(shift, tm, tiles_per_seq), _mod_spec(scale, tm, tiles_per_seq)]
                 + [_full_spec(a) for a in (wm, wl, wq, wk, wv, qg, kg, seg)],
        out_specs=[row(n) for n, _ in outs],
        out_shape=[jax.ShapeDtypeStruct((t, n), dt) for n, dt in outs],
        compiler_params=_params("parallel"),
        name="inproj",
    )(x, g, shift, scale, wm, wl, wq, wk, wv, qg, kg, seg)


def _split3(x):
    hi = x.astype(BF16)
    r1 = x - hi.astype(F32)
    mid = r1.astype(BF16)
    lo = (r1 - mid.astype(F32)).astype(BF16)
    return hi, mid, lo


def _rwkv_body(pm_ref, pl_ref, pm8_ref, pl8_ref, sm_ref, sl_ref, s0_ref,
               mum_ref, mul_ref, w0_ref, wupw_ref, a0_ref, wupa_ref, wupg_ref, kk_ref, ka_ref, rk_ref,
               lg_ref, lb_ref, seg_ref,
               y_ref, sout_ref, state_ref, *, chunk):
    c = pl.program_id(1)
    nc = pl.num_programs(1)
    C = chunk
    W = RWKV_WIDTH
    N = RWKV_HEAD_DIM

    @pl.when(c == 0)
    def _():
        state_ref[...] = s0_ref[...]

    first = c == 0
    prev_m = jnp.where(first, sm_ref[...], pm8_ref[SUBLANES - 1:SUBLANES, :])
    prev_l = jnp.where(first, sl_ref[...], pl8_ref[SUBLANES - 1:SUBLANES, :])

    def shifted(cur, prev_row):
        rows = lax.broadcasted_iota(I32, cur.shape, 0)
        return jnp.where(rows == 0, prev_row, pltpu.roll(cur, 1, 0))

    pm = pm_ref[...]
    plo = pl_ref[...]
    xm = pm + (shifted(pm, prev_m) - pm) * mum_ref[...]
    xl = plo + (shifted(plo, prev_l) - plo) * mul_ref[...]
    r = xm[:, 0:W]
    k = xm[:, W:2 * W]
    v = xm[:, 2 * W:3 * W]
    seg = seg_ref[...]

    z = -(w0_ref[...] + _bdot(jnp.tanh(xl), wupw_ref[...]))
    softplus = jnp.maximum(z, 0.0) + jnp.log(1.0 + jnp.exp(-jnp.abs(z)))
    w = -softplus - 0.5
    a = _sigmoid(a0_ref[...] + _bdot(xl, wupa_ref[...]))
    g = _bdot(_sigmoid(xl), wupg_ref[...])
    kk = k * kk_ref[...]
    kk = kk / jnp.maximum(jnp.sqrt(_bdot(kk * kk, seg)), 1e-12)
    k = k * (1.0 + (a - 1.0) * ka_ref[...])
    logdec = -jnp.exp(w)

    ti = lax.broadcasted_iota(I32, (C, C), 0)
    tj = lax.broadcasted_iota(I32, (C, C), 1)
    lower = (ti >= tj).astype(BF16)
    cum = sum(jnp.dot(lower, part, preferred_element_type=F32) for part in _split3(logdec))
    cum_end = cum[C - 1:C, :]
    a_t = -kk * jnp.exp(cum - logdec)
    r_t = r * jnp.exp(cum)
    inv = jnp.exp(-cum)
    b_t = kk * a * inv
    k_t = k * inv
    to_end = jnp.exp(cum_end - cum)
    b_e = kk * a * to_end
    k_e = k * to_end
    g_end = jnp.exp(cum_end)

    eye = (ti == tj).astype(F32)
    ri = lax.broadcasted_iota(I32, (2 * C, 2 * C), 0)
    ci = lax.broadcasted_iota(I32, (2 * C, 2 * C), 1)
    tr = jnp.where(ri >= C, ri - C, ri)
    tc = jnp.where(ci >= C, ci - C, ci)
    mask = jnp.logical_or(tr > tc, jnp.logical_and(ri >= C, tr == tc))
    zeros_cn = jnp.zeros((C, N), F32)
    levels = int(math.log2(C))
    ys = []
    for h in range(RWKV_HEADS):
        sl = slice(h * N, (h + 1) * N)
        ah, rh, bh, kh, vh = a_t[:, sl], r_t[:, sl], b_t[:, sl], k_t[:, sl], v[:, sl]
        m_all = _bdot_nt(jnp.concatenate([ah, rh], axis=0), jnp.concatenate([bh, kh], axis=0))
        m_all = jnp.where(mask, m_all, 0.0)
        m_top = m_all[0:C, :]
        m_bot = m_all[C:2 * C, :]
        n_ab = m_top[:, 0:C]
        t_inv = eye + n_ab
        power = n_ab
        for _ in range(levels - 1):
            power = _bdot(power, power)
            t_inv = t_inv + _bdot(t_inv, power)
        akv = _bdot(m_top, jnp.concatenate([zeros_cn, vh], axis=0))
        w_mat = _bdot(t_inv, ah)
        u0 = _bdot(t_inv, akv)
        s0 = state_ref[h]
        x = _bdot_nt(jnp.concatenate([w_mat, rh], axis=0), s0)
        u = x[0:C, :] + u0
        uv = jnp.concatenate([u, vh], axis=0)
        ys.append(x[C:2 * C, :] + _bdot(m_bot, uv))
        state_ref[h] = s0 * g_end[:, sl] + _bdot_tn(uv, jnp.concatenate([b_e[:, sl], k_e[:, sl]], axis=0))
    y = jnp.concatenate(ys, axis=1)

    mean = _bdot(y, seg) * (1.0 / N)
    yc = y - mean
    var = _bdot(yc * yc, seg) * (1.0 / N)
    yn = yc * lax.rsqrt(var + GN_EPS) * lg_ref[...] + lb_ref[...]
    bonus = _bdot(r * k * rk_ref[...], seg) * v
    y_ref[...] = ((yn + bonus) * g).astype(y_ref.dtype)

    @pl.when(c == nc - 1)
    def _():
        sout_ref[...] = state_ref[...]


def _rwkv(pm, plo, shift_m, shift_l, s0, wts, seq_len, chunk):
    t_total = pm.shape[0]
    nb = t_total // seq_len
    ncnk = seq_len // chunk
    c8 = chunk // SUBLANES
    prev8 = lambda b, c: (jnp.maximum(b * (seq_len // SUBLANES) + c * c8 - 1, 0), 0)
    cur = lambda b, c: (b * ncnk + c, 0)
    per_seq3 = lambda n: pl.BlockSpec((None, 1, n), lambda b, c: (b, 0, 0))
    state_spec = pl.BlockSpec((None, RWKV_HEADS, RWKV_HEAD_DIM, RWKV_HEAD_DIM), lambda b, c: (b, 0, 0, 0))
    return pl.pallas_call(
        functools.partial(_rwkv_body, chunk=chunk),
        grid=(nb, ncnk),
        in_specs=[pl.BlockSpec((chunk, 3 * RWKV_WIDTH), cur), pl.BlockSpec((chunk, LORA_PAD), cur),
                  pl.BlockSpec((SUBLANES, 3 * RWKV_WIDTH), prev8), pl.BlockSpec((SUBLANES, LORA_PAD), prev8),
                  per_seq3(3 * RWKV_WIDTH), per_seq3(LORA_PAD), state_spec]
                 + [_full_spec(a) for a in wts],
        out_specs=[pl.BlockSpec((chunk, RWKV_WIDTH), cur), state_spec],
        out_shape=[jax.ShapeDtypeStruct((t_total, RWKV_WIDTH), BF16 if chunk % 16 == 0 else F32),
                   jax.ShapeDtypeStruct(s0.shape, F32)],
        scratch_shapes=[pltpu.VMEM((RWKV_HEADS, RWKV_HEAD_DIM, RWKV_HEAD_DIM), F32)],
        compiler_params=_params("parallel", "arbitrary"),
        name="rwkv",
    )(pm, plo, pm, plo, shift_m, shift_l, s0, *wts)


def _lambda(lam_ref, lam_init):
    lv = lam_ref[...]
    return (jnp.exp(jnp.sum(lv[0:1, :] * lv[1:2, :], axis=-1, keepdims=True))
            - jnp.exp(jnp.sum(lv[2:3, :] * lv[3:4, :], axis=-1, keepdims=True)) + lam_init)


def _subln(o, g, lam_init):
    return o * lax.rsqrt(jnp.mean(o * o, axis=-1, keepdims=True) + NORM_EPS) * g * (1.0 - lam_init)


def _attn_prompt_body(qi_ref, ki_ref, q_ref, k_ref, v_ref, lam_ref, sg_ref, o_ref,
                      qm_ref, m_ref, l_ref, acc_ref, *, lam_init):
    p = pl.program_id(2)
    qi = qi_ref[p]
    ki = ki_ref[p]
    tq = q_ref.shape[0]
    tk = k_ref.shape[0]

    @pl.when(ki == 0)
    def _():
        q = q_ref[...]
        lane = lax.broadcasted_iota(I32, q.shape, 1)
        qm_ref[0] = jnp.where(lane < DIFF_DK, q, jnp.zeros_like(q))
        qm_ref[1] = jnp.where(lane >= DIFF_DK, q, jnp.zeros_like(q))
        m_ref[...] = jnp.full(m_ref.shape, NEG_INF, F32)
        l_ref[...] = jnp.zeros(l_ref.shape, F32)
        acc_ref[...] = jnp.zeros(acc_ref.shape, F32)

    def accumulate(masked):
        k = k_ref[...]
        v = v_ref[...]
        for c in range(2):
            s = lax.dot_general(qm_ref[c], k, (((1,), (1,)), ((), ())), preferred_element_type=F32)
            if masked:
                row = lax.broadcasted_iota(I32, (tq, tk), 0)
                col = lax.broadcasted_iota(I32, (tq, tk), 1)
                s = jnp.where(col <= row, s, NEG_INF)
            m_old = m_ref[c]
            m_new = jnp.maximum(m_old, jnp.max(s, axis=-1, keepdims=True))
            alpha = jnp.exp(m_old - m_new)
            pr = jnp.exp(s - m_new)
            l_ref[c] = alpha * l_ref[c] + jnp.sum(pr, axis=-1, keepdims=True)
            acc_ref[c] = alpha * acc_ref[c] + jnp.dot(pr.astype(BF16), v, preferred_element_type=F32)
            m_ref[c] = m_new

    @pl.when(ki < qi)
    def _():
        accumulate(False)

    @pl.when(ki == qi)
    def _():
        accumulate(True)
        lam = _lambda(lam_ref, lam_init)
        o = acc_ref[0] / l_ref[0] - lam * (acc_ref[1] / l_ref[1])
        o_ref[...] = _subln(o, sg_ref[...], lam_init).astype(o_ref.dtype)


def _attn_prompt(qb, kb, vb, lam_vecs, subln_g, seq_len, lam_init):
    t_total = qb.shape[0]
    nb = t_total // seq_len
    tq = min(512, seq_len)
    nq = seq_len // tq
    pairs = [(i, j) for i in range(nq) for j in range(i + 1)]
    qi_tab = jnp.asarray([a for a, _ in pairs], I32)
    ki_tab = jnp.asarray([b for _, b in pairs], I32)
    qmap = lambda b, h, p, qi, ki: (b * nq + qi[p], h)
    kmap = lambda b, h, p, qi, ki: (b * nq + ki[p], h)
    grid_spec = pltpu.PrefetchScalarGridSpec(
        num_scalar_prefetch=2,
        grid=(nb, DIFF_HEADS, len(pairs)),
        in_specs=[pl.BlockSpec((tq, DIFF_DV), qmap), pl.BlockSpec((tq, DIFF_DV), kmap),
                  pl.BlockSpec((tq, DIFF_DV), kmap),
                  pl.BlockSpec(lam_vecs.shape, lambda *_: (0, 0)), pl.BlockSpec(subln_g.shape, lambda *_: (0, 0))],
        out_specs=pl.BlockSpec((tq, DIFF_DV), qmap),
        scratch_shapes=[pltpu.VMEM((2, tq, DIFF_DV), BF16), pltpu.VMEM((2, tq, 1), F32),
                        pltpu.VMEM((2, tq, 1), F32), pltpu.VMEM((2, tq, DIFF_DV), F32)],
    )
    return pl.pallas_call(
        functools.partial(_attn_prompt_body, lam_init=lam_init),
        grid_spec=grid_spec,
        out_shape=jax.ShapeDtypeStruct((t_total, DIFF_V_WIDTH), BF16),
        compiler_params=_params("parallel", "parallel", "arbitrary"),
        name="attn_prompt",
    )(qi_tab, ki_tab, qb, kb, vb, lam_vecs, subln_g)


def _attn_sample_body(pt_ref, q_ref, k_ref, v_ref, lam_ref, sg_ref, *rest, n_pages, lam_init):
    kp_refs = rest[:n_pages]
    vp_refs = rest[n_pages:2 * n_pages]
    o_ref = rest[2 * n_pages]
    s_new = q_ref.shape[0]
    page = kp_refs[0].shape[0]
    nrow = DIFF_HEADS * 2 * s_new
    qt = jnp.concatenate([q_ref[...] * DIFF_SCALE] * (DIFF_HEADS * 2), axis=0)
    row = lax.broadcasted_iota(I32, qt.shape, 0)
    col = lax.broadcasted_iota(I32, qt.shape, 1)
    qbd = jnp.where(col // DIFF_DK == row // s_new, qt, 0.0).astype(BF16)
    pad = jnp.zeros((page - s_new, k_ref.shape[1]), F32)
    k_new = jnp.concatenate([k_ref[...], pad], axis=0)
    v_new = jnp.concatenate([v_ref[...], pad], axis=0)
    scores = [_bdot_nt(qbd, kr[...]) for kr in kp_refs]
    s_n = _bdot_nt(qbd, k_new)
    rn = lax.broadcasted_iota(I32, s_n.shape, 0)
    cn = lax.broadcasted_iota(I32, s_n.shape, 1)
    scores.append(jnp.where(cn <= rn % s_new, s_n, NEG_INF))
    m = functools.reduce(jnp.maximum, [jnp.max(s, axis=-1, keepdims=True) for s in scores])
    values = [vr[...] for vr in vp_refs] + [v_new]
    l = jnp.zeros_like(m)
    acc = jnp.zeros((nrow, v_new.shape[1]), F32)
    for s, val in zip(scores, values):
        pr = jnp.exp(s - m)
        l = l + jnp.sum(pr, axis=-1, keepdims=True)
        acc = acc + _bdot(pr, val)
    full = acc / l
    lam = _lambda(lam_ref, lam_init)
    outs = []
    for h in range(DIFF_HEADS):
        r0 = h * 2 * s_new
        cols = slice(h * DIFF_DV, (h + 1) * DIFF_DV)
        o = full[r0:r0 + s_new, cols] - lam * full[r0 + s_new:r0 + 2 * s_new, cols]
        outs.append(_subln(o, sg_ref[...], lam_init))
    o_ref[...] = jnp.concatenate(outs, axis=1)


def _attn_sample(q, k, v, cache_k, cache_v, page_table, page_offset, lam_vecs, subln_g, s_new, lam_init):
    nseq, n_pages = page_table.shape
    page = cache_k.shape[1]
    width = q.shape[1]
    new_spec = pl.BlockSpec((s_new, width), lambda b, pt: (b, 0))
    page_specs = [pl.BlockSpec((None, page, width), lambda b, pt, j=j: (pt[b, j] + page_offset, 0, 0))
                  for j in range(n_pages)]
    grid_spec = pltpu.PrefetchScalarGridSpec(
        num_scalar_prefetch=1,
        grid=(nseq,),
        in_specs=[new_spec, new_spec, new_spec,
                  pl.BlockSpec(lam_vecs.shape, lambda *_: (0, 0)), pl.BlockSpec(subln_g.shape, lambda *_: (0, 0))]
                 + page_specs + page_specs,
        out_specs=new_spec,
    )
    return pl.pallas_call(
        functools.partial(_attn_sample_body, n_pages=n_pages, lam_init=lam_init),
        grid_spec=grid_spec,
        out_shape=jax.ShapeDtypeStruct((nseq * s_new, width), F32),
        compiler_params=_params("parallel"),
        name="attn_sample",
    )(page_table, q, k, v, lam_vecs, subln_g, *([cache_k] * n_pages), *([cache_v] * n_pages))


def _outproj_body(yr_ref, o_ref, x_ref, g1_ref, g2n_ref, sh_ref, sc_ref, g2_ref, woa_ref, wob_ref,
                  wsg_ref, wsu_ref, wsd_ref, wrh_ref, wrl_ref,
                  base_ref, h2t_ref, lg_ref):
    tm = x_ref.shape[0]
    mixed = (jnp.dot(yr_ref[...].astype(BF16), woa_ref[...], preferred_element_type=F32)
             + jnp.dot(o_ref[...].astype(BF16), wob_ref[...], preferred_element_type=F32))
    x1 = x_ref[...] + g1_ref[...] * mixed
    h2 = _rms(x1, g2n_ref[...]) * (1.0 + sc_ref[...]) + sh_ref[...]
    h2b = h2.astype(BF16)
    hidden = _silu(jnp.dot(h2b, wsg_ref[...], preferred_element_type=F32)) * jnp.dot(
        h2b, wsu_ref[...], preferred_element_type=F32)
    shared = jnp.dot(hidden.astype(BF16), wsd_ref[...], preferred_element_type=F32)
    base_ref[...] = x1 + g2_ref[...] * shared
    h2l = (h2 - h2b.astype(F32)).astype(BF16)
    nt = lambda a, b: lax.dot_general(a, b, (((1,), (1,)), ((), ())), preferred_element_type=F32)
    lg_ref[...] = nt(wrh_ref[...], h2b) + nt(wrl_ref[...], h2b) + nt(wrh_ref[...], h2l)
    for c in range(ROW_TILE):
        h2t_ref[pl.ds(c, tm, stride=ROW_TILE), :] = h2[:, c * LANES:(c + 1) * LANES]


def _outproj(yr, o, x, gate1, norm2_g, shift2, scale2, gate2, wts, tm, tiles_per_seq):
    t, d = x.shape
    row = lambda n: pl.BlockSpec((tm, n), lambda i: (i, 0))
    mod = lambda m: _mod_spec(m, tm, tiles_per_seq)
    return pl.pallas_call(
        _outproj_body,
        grid=(t // tm,),
        in_specs=[row(RWKV_WIDTH), row(DIFF_V_WIDTH), row(d), mod(gate1), _full_spec(norm2_g), mod(shift2),
                  mod(scale2), mod(gate2)] + [_full_spec(a) for a in wts],
        out_specs=[row(d), pl.BlockSpec((tm * ROW_TILE, LANES), lambda i: (i, 0)),
                   pl.BlockSpec((N_EXPERTS, tm), lambda i: (0, i))],
        out_shape=[jax.ShapeDtypeStruct((t, d), F32),
                   jax.ShapeDtypeStruct((t * ROW_TILE, LANES), F32),
                   jax.ShapeDtypeStruct((N_EXPERTS, t), F32)],
        compiler_params=_params("parallel"),
        name="outproj",
    )(yr, o, x, gate1, norm2_g, shift2, scale2, gate2, *wts)


def _router_body(lg_ref, bias_ref, idx_ref, gate_ref, pos_ref, cnt_ref, run_ref):
    i = pl.program_id(0)
    tm = lg_ref.shape[1]

    @pl.when(i == 0)
    def _():
        run_ref[...] = jnp.zeros(run_ref.shape, F32)

    scores = _sigmoid(lg_ref[...])
    biased = scores + bias_ref[...]
    erow = lax.broadcasted_iota(I32, (N_EXPERTS, tm), 0)
    grow = lax.broadcasted_iota(I32, (GROUP_SIZE, tm), 0)

    def first_argmax(x, rows, limit):
        mx = jnp.max(x, axis=0, keepdims=True)
        return mx, jnp.min(jnp.where(x == mx, rows, limit), axis=0, keepdims=True)

    group_scores = []
    for gidx in range(N_GROUPS):
        xg = biased[gidx * GROUP_SIZE:(gidx + 1) * GROUP_SIZE, :]
        m1, i1 = first_argmax(xg, grow, GROUP_SIZE)
        m2 = jnp.max(jnp.where(grow == i1, NEG_INF, xg), axis=0, keepdims=True)
        group_scores.append(m1 + m2)
    gs = jnp.concatenate(group_scores, axis=0)
    g8 = lax.broadcasted_iota(I32, (N_GROUPS, tm), 0)
    chosen = jnp.zeros((N_GROUPS, tm), I32)
    for _ in range(TOPK_GROUPS):
        _, gi = first_argmax(gs, g8, N_GROUPS)
        hit = g8 == gi
        chosen = jnp.where(hit, 1, chosen)
        gs = jnp.where(hit, NEG_INF, gs)
    cand = jnp.concatenate(
        [jnp.where(chosen[gidx:gidx + 1, :] > 0, biased[gidx * GROUP_SIZE:(gidx + 1) * GROUP_SIZE, :], NEG_INF)
         for gidx in range(N_GROUPS)], axis=0)

    idxs, raws = [], []
    onehot = jnp.zeros((N_EXPERTS, tm), F32)
    for _ in range(TOP_K):
        _, ei = first_argmax(cand, erow, N_EXPERTS)
        hit = erow == ei
        idxs.append(ei)
        raws.append(jnp.sum(jnp.where(hit, scores, 0.0), axis=0, keepdims=True))
        onehot = onehot + hit.astype(F32)
        cand = jnp.where(hit, NEG_INF, cand)
    raw = jnp.concatenate(raws, axis=0)
    gate_ref[...] = raw / jnp.sum(raw, axis=0, keepdims=True) * ROUTE_SCALE
    idx_ref[...] = jnp.concatenate(idxs, axis=0)

    ti = lax.broadcasted_iota(I32, (tm, tm), 0)
    tj = lax.broadcasted_iota(I32, (tm, tm), 1)
    oh = onehot.astype(BF16)
    before = jnp.dot(oh, (ti < tj).astype(BF16), preferred_element_type=F32) + run_ref[...]
    run_ref[...] = run_ref[...] + jnp.dot(oh, jnp.ones((tm, tm), BF16), preferred_element_type=F32)
    pos_ref[...] = jnp.concatenate(
        [jnp.sum(jnp.where(erow == ei, before, 0.0), axis=0, keepdims=True) for ei in idxs], axis=0).astype(I32)
    cnt_ref[...] = run_ref[...]


def _router(logits_t, bias_col):
    t = logits_t.shape[1]
    tm = ROUTE_TILE
    tok = pl.BlockSpec((TOP_K, tm), lambda i: (0, i))
    return pl.pallas_call(
        _router_body,
        grid=(t // tm,),
        in_specs=[pl.BlockSpec((N_EXPERTS, tm), lambda i: (0, i)), _full_spec(bias_col)],
        out_specs=[tok, tok, tok, pl.BlockSpec((N_EXPERTS, tm), lambda i: (0, 0))],
        out_shape=[jax.ShapeDtypeStruct((TOP_K, t), I32), jax.ShapeDtypeStruct((TOP_K, t), F32),
                   jax.ShapeDtypeStruct((TOP_K, t), I32), jax.ShapeDtypeStruct((N_EXPERTS, tm), F32)],
        scratch_shapes=[pltpu.VMEM((N_EXPERTS, tm), F32)],
        compiler_params=_params("arbitrary"),
        name="router",
    )(logits_t, bias_col)


def _row_copy(src, src_row, dst, dst_row, sem):
    return pltpu.make_async_copy(src.at[pl.ds(pl.multiple_of(src_row * ROW_TILE, ROW_TILE), ROW_TILE), :],
                                 dst.at[pl.ds(pl.multiple_of(dst_row * ROW_TILE, ROW_TILE), ROW_TILE), :], sem)


def _dispatch_body(zflag_ref, h_ref, dest_ref, xs_ref, zero_ref, zsem, rsem):
    i = pl.program_id(0)
    tm = dest_ref.shape[1]
    blk_rows = EXPERT_BLOCK * ROW_TILE
    n_blocks = xs_ref.shape[0] // blk_rows

    def zero_copy(b):
        start = pl.multiple_of(b * blk_rows, blk_rows)
        return pltpu.make_async_copy(zero_ref, xs_ref.at[pl.ds(start, blk_rows), :], zsem)

    @pl.when(i == 0)
    def _():
        zero_ref[...] = jnp.zeros(zero_ref.shape, F32)

        def issue(b, carry):
            @pl.when(zflag_ref[b] > 0)
            def _():
                zero_copy(b).start()
            return carry

        def drain(b, carry):
            @pl.when(zflag_ref[b] > 0)
            def _():
                zero_copy(b).wait()
            return carry

        lax.fori_loop(0, n_blocks, issue, 0)
        lax.fori_loop(0, n_blocks, drain, 0)

    def issue_rows(t, carry):
        for j in range(TOP_K):
            _row_copy(h_ref, t, xs_ref, dest_ref[j, t], rsem).start()
        return carry

    lax.fori_loop(0, tm, issue_rows, 0)
    for _ in range(TOP_K):
        pltpu.make_async_copy(h_ref, xs_ref.at[pl.ds(0, tm * ROW_TILE), :], rsem).wait()


def _dispatch(h2t, dest, zero_flags, n_rows):
    t = dest.shape[1]
    tm = ROUTE_TILE
    grid_spec = pltpu.PrefetchScalarGridSpec(
        num_scalar_prefetch=1,
        grid=(t // tm,),
        in_specs=[pl.BlockSpec((tm * ROW_TILE, LANES), lambda i, *_: (i, 0)),
                  pl.BlockSpec((TOP_K, tm), lambda i, *_: (0, i), memory_space=pltpu.SMEM)],
        out_specs=pl.BlockSpec(memory_space=pl.ANY),
        scratch_shapes=[pltpu.VMEM((EXPERT_BLOCK * ROW_TILE, LANES), F32),
                        pltpu.SemaphoreType.DMA, pltpu.SemaphoreType.DMA],
    )
    return pl.pallas_call(
        _dispatch_body,
        grid_spec=grid_spec,
        out_shape=jax.ShapeDtypeStruct((n_rows * ROW_TILE, LANES), F32),
        compiler_params=_params("arbitrary"),
        name="dispatch",
    )(zero_flags, h2t, dest)


def _expert_body(be_ref, nu_ref, xs_ref, wg_ref, wu_ref, wd_ref, y_ref):
    i = pl.program_id(0)
    rows = EXPERT_BLOCK

    @pl.when(i < nu_ref[0])
    def _():
        x = jnp.concatenate([xs_ref[pl.ds(c, rows, stride=ROW_TILE), :] for c in range(ROW_TILE)], axis=1)
        xb = x.astype(BF16)
        hidden = _silu(_bdot(xb, wg_ref[...])) * _bdot(xb, wu_ref[...])
        y = _bdot(hidden, wd_ref[...])
        for c in range(ROW_TILE):
            y_ref[pl.ds(c, rows, stride=ROW_TILE), :] = y[:, c * LANES:(c + 1) * LANES]

    @pl.when(i >= nu_ref[0])
    def _():
        y_ref[...] = jnp.zeros(y_ref.shape, F32)


def _experts(xs, block_expert, n_used, we_gate, we_up, we_down):
    n_blocks = block_expert.shape[0]
    d, f = we_gate.shape[1], we_gate.shape[2]
    blk = (EXPERT_BLOCK * ROW_TILE, LANES)
    grid_spec = pltpu.PrefetchScalarGridSpec(
        num_scalar_prefetch=2,
        grid=(n_blocks,),
        in_specs=[pl.BlockSpec(blk, lambda i, be, nu: (jnp.minimum(i, nu[0] - 1), 0)),
                  pl.BlockSpec((None, d, f), lambda i, be, nu: (be[i], 0, 0)),
                  pl.BlockSpec((None, d, f), lambda i, be, nu: (be[i], 0, 0)),
                  pl.BlockSpec((None, f, d), lambda i, be, nu: (be[i], 0, 0))],
        out_specs=pl.BlockSpec(blk, lambda i, be, nu: (i, 0)),
    )
    return pl.pallas_call(
        _expert_body,
        grid_spec=grid_spec,
        out_shape=jax.ShapeDtypeStruct(xs.shape, F32),
        compiler_params=_params("arbitrary"),
        name="experts",
    )(block_expert, n_used, xs, we_gate, we_up, we_down)


def _combine_body(base_ref, g2_ref, gate_ref, dest_ref, y_ref, out_ref, buf_ref, sem):
    tm = base_ref.shape[0]
    slot_rows = tm * ROW_TILE

    def issue(t, carry):
        for j in range(TOP_K):
            _row_copy(y_ref, dest_ref[j, t], buf_ref, j * tm + t, sem).start()
        return carry

    lax.fori_loop(0, tm, issue, 0)
    pltpu.make_async_copy(y_ref.at[pl.ds(0, TOP_K * slot_rows), :], buf_ref, sem).wait()
    gates = gate_ref[...]
    gcols = [jnp.broadcast_to(gates[:, j:j + 1], (tm, LANES)) for j in range(TOP_K)]
    for c in range(ROW_TILE):
        cols = slice(c * LANES, (c + 1) * LANES)
        routed = sum(gcols[j] * buf_ref[pl.ds(j * slot_rows + c, tm, stride=ROW_TILE), :] for j in range(TOP_K))
        out_ref[:, cols] = base_ref[:, cols] + g2_ref[:, cols] * routed


def _combine(base, gate2, gates, dest, y_rows, tok_offset, tiles_per_seq):
    t, d = base.shape
    tm = ROUTE_TILE
    off = tok_offset // tm
    return pl.pallas_call(
        _combine_body,
        grid=(t // tm,),
        in_specs=[pl.BlockSpec((tm, d), lambda i: (i, 0)), _mod_spec(gate2, tm, tiles_per_seq),
                  pl.BlockSpec((tm, TOP_K), lambda i: (i + off, 0)),
                  pl.BlockSpec((TOP_K, tm), lambda i: (0, i + off), memory_space=pltpu.SMEM),
                  pl.BlockSpec(memory_space=pl.ANY)],
        out_specs=pl.BlockSpec((tm, d), lambda i: (i, 0)),
        out_shape=jax.ShapeDtypeStruct((t, d), F32),
        scratch_shapes=[pltpu.VMEM((TOP_K * tm * ROW_TILE, LANES), F32), pltpu.SemaphoreType.DMA],
        compiler_params=_params("arbitrary"),
        name="combine",
    )(base, gate2, gates, dest, y_rows)


def _segment_ones(width, seg):
    ids = jnp.arange(width) // seg
    return (ids[:, None] == ids[None, :]).astype(BF16)


def _pad_rows(w, start, total):
    return jnp.zeros((total, w.shape[1]), w.dtype).at[start:start + w.shape[0]].set(w)


def kernel(x_prompt, x_sample, cache_k, cache_v, state_wkv, state_shift, page_table, c_prompt, c_sample, w_ada, b_ada, norm1_g, norm2_g, w_in, mu_shift, w0, w_lora_up, a0, a_lora_up, g_lora_up, k_k, k_a, r_k, lnx_g, lnx_b, qn_g, kn_g, lam_q1, lam_k1, lam_q2, lam_k2, subln_g, w_out, w_router, router_bias, we_gate, we_up, we_down, ws_gate, ws_up, ws_down):
    depth = w_in.shape[0]
    nb, seq, d = x_prompt.shape
    db, dseq, _ = x_sample.shape
    n_pool, page = cache_k.shape[1], cache_k.shape[2]
    tp, ts = nb * seq, db * dseq
    w3 = 3 * RWKV_WIDTH
    row = lambda a: a.reshape(1, -1)
    seg512 = _segment_ones(RWKV_WIDTH, RWKV_HEAD_DIM)
    tm_p = min(256, seq)
    tm_s = min(256, ts)
    chunk_p = min(64, seq)

    xp = x_prompt.reshape(tp, d)
    xs = x_sample.reshape(ts, d)
    cache_k2 = cache_k.reshape(depth * n_pool, page, DIFF_QK_WIDTH)
    cache_v2 = cache_v.reshape(depth * n_pool, page, DIFF_V_WIDTH)
    n_cond = nb + db
    cond = jnp.concatenate([c_prompt, c_sample], axis=0)
    cond = jnp.pad(cond, ((0, -n_cond % SUBLANES), (0, 0)))

    outs = {name: [] for name in ("kp", "vp", "wp", "sp", "ks", "vs", "ws", "ss")}
    for layer in range(depth):
        lam_init = 0.8 - 0.6 * math.exp(-0.3 * layer)
        mods = _ada(cond, w_ada[layer], row(b_ada[layer]))
        mod_p = [m.reshape(nb, 1, d) for m in jnp.split(mods[:nb], 6, axis=-1)]
        mod_s = [jnp.repeat(m, dseq, axis=0) for m in jnp.split(mods[nb:n_cond], 6, axis=-1)]

        wi = w_in[layer]
        in_wts = (wi[:, :w3].astype(BF16),
                  jnp.pad(wi[:, w3:RWKV_PROJ], ((0, 0), (0, LORA_PAD - LORA_WIDTH))).astype(BF16),
                  wi[:, RWKV_PROJ:RWKV_PROJ + DIFF_QK_WIDTH].astype(BF16),
                  wi[:, RWKV_PROJ + DIFF_QK_WIDTH:RWKV_PROJ + 2 * DIFF_QK_WIDTH].astype(BF16),
                  wi[:, RWKV_PROJ + 2 * DIFF_QK_WIDTH:].astype(BF16),
                  row(jnp.tile(qn_g[layer], DIFF_QK_WIDTH // DIFF_DK)),
                  row(jnp.tile(kn_g[layer], DIFF_QK_WIDTH // DIFF_DK)),
                  seg512)
        mu = mu_shift[layer]
        rwkv_wts = (row(mu[:w3]), row(jnp.pad(mu[w3:], (0, LORA_PAD - LORA_WIDTH))), row(w0[layer]),
                    _pad_rows(w_lora_up[layer], 0, LORA_PAD).astype(BF16), row(a0[layer]),
                    _pad_rows(a_lora_up[layer], DECAY_LORA, LORA_PAD).astype(BF16),
                    _pad_rows(g_lora_up[layer], DECAY_LORA + ICLR_LORA, LORA_PAD).astype(BF16),
                    row(k_k[layer]), row(k_a[layer]), row(r_k[layer]), row(lnx_g[layer]), row(lnx_b[layer]), seg512)
        lam_vecs = jnp.stack([lam_q1[layer], lam_k1[layer], lam_q2[layer], lam_k2[layer]])
        sg = row(subln_g[layer])
        wr = w_router[layer].T
        wr_hi = wr.astype(BF16)
        out_wts = (w_out[layer][:RWKV_WIDTH].astype(BF16), w_out[layer][RWKV_WIDTH:].astype(BF16),
                   ws_gate[layer].astype(BF16), ws_up[layer].astype(BF16), ws_down[layer].astype(BF16),
                   wr_hi, (wr - wr_hi.astype(F32)).astype(BF16))

        def token_mix(x, mod, tm, tiles_per_seq, seq_len, chunk, shift_prev, wkv0):
            pm, plo, q, k, v, qb, kb, vb = _inproj(x, row(norm1_g[layer]), mod[0], mod[1], in_wts, tm, tiles_per_seq)
            n_seq = x.shape[0] // seq_len
            shift_m = shift_prev[:, :, :w3]
            shift_l = jnp.pad(shift_prev[:, :, w3:], ((0, 0), (0, 0), (0, LORA_PAD - LORA_WIDTH)))
            yr, wkv = _rwkv(pm, plo, shift_m, shift_l, wkv0, rwkv_wts, seq_len, chunk)
            last = jnp.concatenate([pm.reshape(n_seq, seq_len, w3)[:, -1:],
                                    plo.reshape(n_seq, seq_len, LORA_PAD)[:, -1:, :LORA_WIDTH]], axis=-1)
            return yr, wkv, last, (q, k, v, qb, kb, vb)

        yr_p, wkv_p, last_p, (q_p, k_p, v_p, qb_p, kb_p, vb_p) = token_mix(
            xp, mod_p, tm_p, seq // tm_p, seq, chunk_p,
            jnp.zeros((nb, 1, RWKV_PROJ), F32), jnp.zeros((nb, RWKV_HEADS, RWKV_HEAD_DIM, RWKV_HEAD_DIM), F32))
        o_p = _attn_prompt(qb_p, kb_p, vb_p, lam_vecs, sg, seq, lam_init)
        yr_s, wkv_s, last_s, (q_s, k_s, v_s, _, _, _) = token_mix(
            xs, mod_s, tm_s, 1, dseq, dseq, state_shift[layer], state_wkv[layer])
        o_s = _attn_sample(q_s, k_s, v_s, cache_k2, cache_v2, page_table, layer * n_pool, lam_vecs, sg, dseq,
                           lam_init)

        base_p, h2t_p, lg_p = _outproj(yr_p, o_p, xp, mod_p[2], row(norm2_g[layer]), mod_p[3], mod_p[4], mod_p[5],
                                       out_wts, tm_p, seq // tm_p)
        base_s, h2t_s, lg_s = _outproj(yr_s, o_s, xs, mod_s[2], row(norm2_g[layer]), mod_s[3], mod_s[4], mod_s[5],
                                       out_wts, tm_s, 1)

        t_all = tp + ts
        idx, gate, pos, cnt = _router(jnp.concatenate([lg_p, lg_s], axis=1), router_bias[layer].reshape(-1, 1))
        counts = cnt[:, 0].astype(I32)
        padded = (counts + EXPERT_BLOCK - 1) // EXPERT_BLOCK * EXPERT_BLOCK
        end_padded = jnp.cumsum(padded)
        dest = (end_padded - padded)[idx] + pos
        n_rows = (t_all * TOP_K + N_EXPERTS * (EXPERT_BLOCK - 1) + EXPERT_BLOCK - 1) // EXPERT_BLOCK * EXPERT_BLOCK
        n_blocks = n_rows // EXPERT_BLOCK
        block_expert = jnp.minimum(
            jnp.searchsorted(end_padded, jnp.arange(n_blocks, dtype=I32) * EXPERT_BLOCK, side="right"),
            N_EXPERTS - 1).astype(I32)
        n_used = (end_padded[-1:] // EXPERT_BLOCK).astype(I32)
        blk_ids = jnp.arange(n_blocks, dtype=I32)
        zero_flags = jnp.logical_or(blk_ids >= n_used[0],
                                    end_padded[block_expert] == (blk_ids + 1) * EXPERT_BLOCK).astype(I32)
        rows_in = _dispatch(jnp.concatenate([h2t_p, h2t_s], axis=0), dest, zero_flags, n_rows)
        rows_out = _experts(rows_in, block_expert, n_used, we_gate[layer], we_up[layer], we_down[layer])
        gates_t = gate.T
        xp = _combine(base_p, mod_p[5], gates_t, dest, rows_out, 0, seq // ROUTE_TILE)
        xs = _combine(base_s, mod_s[5], gates_t, dest, rows_out, tp, 1)

        outs["kp"].append(k_p.reshape(nb, seq, DIFF_HEADS, 2, DIFF_DK))
        outs["vp"].append(v_p.reshape(nb, seq, DIFF_HEADS, DIFF_DV))
        outs["wp"].append(wkv_p)
        outs["sp"].append(last_p)
        outs["ks"].append(k_s.reshape(db, dseq, DIFF_HEADS, 2, DIFF_DK))
        outs["vs"].append(v_s.reshape(db, dseq, DIFF_HEADS, DIFF_DV))
        outs["ws"].append(wkv_s)
        outs["ss"].append(last_s)

    st = {name: jnp.stack(v) for name, v in outs.items()}
    return (xp.reshape(nb, seq, d), xs.reshape(db, dseq, d), st["kp"], st["vp"], st["wp"], st["sp"],
            st["ks"], st["vs"], st["ws"], st["ss"])
```

```python
import functools
import math

import jax
import jax.numpy as jnp
from jax import lax
from jax.experimental import pallas as pl
from jax.experimental.pallas import tpu as pltpu

F32 = jnp.float32
BF16 = jnp.bfloat16
I32 = jnp.int32

RWKV_HEADS = 8
RWKV_HEAD_DIM = 64
RWKV_WIDTH = RWKV_HEADS * RWKV_HEAD_DIM
DECAY_LORA = 32
ICLR_LORA = 32
GATE_LORA = 96
LORA_WIDTH = DECAY_LORA + ICLR_LORA + GATE_LORA
LORA_PAD = 256
RWKV_PROJ = 3 * RWKV_WIDTH + LORA_WIDTH
GN_EPS = 64e-5
DIFF_HEADS = 4
DIFF_DK = 64
DIFF_DV = 2 * DIFF_DK
DIFF_QK_WIDTH = DIFF_HEADS * 2 * DIFF_DK
DIFF_V_WIDTH = DIFF_HEADS * DIFF_DV
DIFF_SCALE = DIFF_DK ** -0.5
N_EXPERTS = 256
N_GROUPS = 8
GROUP_SIZE = N_EXPERTS // N_GROUPS
TOPK_GROUPS = 4
TOP_K = 8
ROUTE_SCALE = 2.5
NORM_EPS = 1e-6

LANES = 128
SUBLANES = 8
ROW_TILE = 8
VMEM_LIMIT = 48 * 1024 * 1024

EXPERT_BLOCK = 128
ROUTE_TILE = 128
NEG_INF = float("-inf")


def _bdot(a, b):
    return jnp.dot(a.astype(BF16), b.astype(BF16), preferred_element_type=F32)


def _bdot_nt(a, b):
    return lax.dot_general(a.astype(BF16), b.astype(BF16), (((1,), (1,)), ((), ())), preferred_element_type=F32)


def _bdot_tn(a, b):
    return lax.dot_general(a.astype(BF16), b.astype(BF16), (((0,), (0,)), ((), ())), preferred_element_type=F32)


def _sigmoid(x):
    return 1.0 / (1.0 + jnp.exp(-x))


def _silu(x):
    return x * _sigmoid(x)


def _params(*sem, vmem=VMEM_LIMIT):
    return pltpu.CompilerParams(dimension_semantics=sem, vmem_limit_bytes=vmem)


def _ada_body(c_ref, w_ref, b_ref, o_ref):
    o_ref[...] = _bdot(_silu(c_ref[...]), w_ref[...]) + b_ref[...]


def _ada(c, w, b):
    rows, d = c.shape
    n = w.shape[1]
    tn = 512
    return pl.pallas_call(
        _ada_body,
        grid=(n // tn,),
        in_specs=[pl.BlockSpec((rows, d), lambda j: (0, 0)),
                  pl.BlockSpec((d, tn), lambda j: (0, j)),
                  pl.BlockSpec((1, tn), lambda j: (0, j))],
        out_specs=pl.BlockSpec((rows, tn), lambda j: (0, j)),
        out_shape=jax.ShapeDtypeStruct((rows, n), F32),
        compiler_params=_params("parallel"),
        name="ada",
    )(c, w, b)


def _mod_spec(mod, tm, tiles_per_seq):
    if mod.ndim == 3:
        return pl.BlockSpec((None, 1, mod.shape[-1]), lambda i: (i // tiles_per_seq, 0, 0))
    return pl.BlockSpec((tm, mod.shape[-1]), lambda i: (i, 0))


def _full_spec(a):
    nd = a.ndim
    return pl.BlockSpec(a.shape, lambda *_: (0,) * nd)


def _rms(x, g):
    return x * lax.rsqrt(jnp.mean(x * x, axis=-1, keepdims=True) + NORM_EPS) * g


def _inproj_body(x_ref, g_ref, sh_ref, sc_ref, wm_ref, wl_ref, wq_ref, wk_ref, wv_ref, qg_ref, kg_ref, seg_ref,
                 pm_ref, pl_ref, *out_refs, long_seq):
    tm = x_ref.shape[0]
    h = (_rms(x_ref[...], g_ref[...]) * (1.0 + sc_ref[...]) + sh_ref[...]).astype(BF16)
    pm_ref[...] = jnp.dot(h, wm_ref[...], preferred_element_type=F32)
    pl_ref[...] = jnp.dot(h, wl_ref[...], preferred_element_type=F32)
    seg = seg_ref[...]

    def head_norm(z, gain):
        ms = _bdot(z * z, seg) * (1.0 / DIFF_DK)
        return z * lax.rsqrt(ms + NORM_EPS) * gain

    q = head_norm(jnp.dot(h, wq_ref[...], preferred_element_type=F32), qg_ref[...])
    k = head_norm(jnp.dot(h, wk_ref[...], preferred_element_type=F32), kg_ref[...])
    v = jnp.dot(h, wv_ref[...], preferred_element_type=F32)
    if long_seq:
        kt_ref, v4_ref, qb_ref, vb_ref = out_refs
        kt_ref[...] = k.T
        for hd in range(DIFF_HEADS):
            v4_ref[pl.ds(hd, tm, stride=DIFF_HEADS), :] = v[:, hd * DIFF_DV:(hd + 1) * DIFF_DV]
        qb_ref[...] = (q * DIFF_SCALE).astype(BF16)
        vb_ref[...] = v.astype(BF16)
    else:
        q_ref, k_ref, v_ref = out_refs
        q_ref[...] = q
        k_ref[...] = k
        v_ref[...] = v


def _inproj(x, g, shift, scale, wts, tm, tiles_per_seq, long_seq):
    t, d = x.shape
    wm, wl, wq, wk, wv, qg, kg, seg = wts
    row = lambda n: pl.BlockSpec((tm, n), lambda i: (i, 0))
    out_specs = [row(3 * RWKV_WIDTH), row(LORA_PAD)]
    out_shape = [jax.ShapeDtypeStruct((t, 3 * RWKV_WIDTH), F32), jax.ShapeDtypeStruct((t, LORA_PAD), F32)]
    if long_seq:
        n_seq = t // (tm * tiles_per_seq)
        out_specs += [pl.BlockSpec((DIFF_QK_WIDTH, tm), lambda i: (i // tiles_per_seq, i % tiles_per_seq)),
                      pl.BlockSpec((tm * DIFF_HEADS, DIFF_DV), lambda i: (i, 0)),
                      row(DIFF_QK_WIDTH), row(DIFF_V_WIDTH)]
        out_shape += [jax.ShapeDtypeStruct((n_seq * DIFF_QK_WIDTH, tm * tiles_per_seq), F32),
                      jax.ShapeDtypeStruct((t * DIFF_HEADS, DIFF_DV), F32),
                      jax.ShapeDtypeStruct((t, DIFF_QK_WIDTH), BF16), jax.ShapeDtypeStruct((t, DIFF_V_WIDTH), BF16)]
    else:
        out_specs += [row(DIFF_QK_WIDTH), row(DIFF_QK_WIDTH), row(DIFF_V_WIDTH)]
        out_shape += [jax.ShapeDtypeStruct((t, n), F32) for n in (DIFF_QK_WIDTH, DIFF_QK_WIDTH, DIFF_V_WIDTH)]
    return pl.pallas_call(
        functools.partial(_inproj_body, long_seq=long_seq),
        grid=(t // tm,),
        in_specs=[row(d), _full_spec(g), _mod_spec(shift, tm, tiles_per_seq), _mod_spec(scale, tm, tiles_per_seq)]
                 + [_full_spec(a) for a in (wm, wl, wq, wk, wv, qg, kg, seg)],
        out_specs=out_specs,
        out_shape=out_shape,
        compiler_params=_params("parallel"),
        name="inproj",
    )(x, g, shift, scale, wm, wl, wq, wk, wv, qg, kg, seg)


def _split3(x):
    hi = x.astype(BF16)
    r1 = x - hi.astype(F32)
    mid = r1.astype(BF16)
    lo = (r1 - mid.astype(F32)).astype(BF16)
    return hi, mid, lo


def _rwkv_body(pm_ref, pl_ref, pm8_ref, pl8_ref, sm_ref, sl_ref, s0_ref,
               mum_ref, mul_ref, w0_ref, wupw_ref, a0_ref, wupa_ref, wupg_ref, kk_ref, ka_ref, rk_ref,
               lg_ref, lb_ref, seg_ref,
               y_ref, sout_ref, state_ref, *, chunk):
    c = pl.program_id(1)
    nc = pl.num_programs(1)
    C = chunk
    W = RWKV_WIDTH
    N = RWKV_HEAD_DIM

    @pl.when(c == 0)
    def _():
        state_ref[...] = s0_ref[...]

    first = c == 0
    prev_m = jnp.where(first, sm_ref[...], pm8_ref[SUBLANES - 1:SUBLANES, :])
    prev_l = jnp.where(first, sl_ref[...], pl8_ref[SUBLANES - 1:SUBLANES, :])

    def shifted(cur, prev_row):
        rows = lax.broadcasted_iota(I32, cur.shape, 0)
        return jnp.where(rows == 0, prev_row, pltpu.roll(cur, 1, 0))

    pm = pm_ref[...]
    plo = pl_ref[...]
    xm = pm + (shifted(pm, prev_m) - pm) * mum_ref[...]
    xl = plo + (shifted(plo, prev_l) - plo) * mul_ref[...]
    r = xm[:, 0:W]
    k = xm[:, W:2 * W]
    v = xm[:, 2 * W:3 * W]
    seg = seg_ref[...]

    z = -(w0_ref[...] + _bdot(jnp.tanh(xl), wupw_ref[...]))
    softplus = jnp.maximum(z, 0.0) + jnp.log(1.0 + jnp.exp(-jnp.abs(z)))
    w = -softplus - 0.5
    a = _sigmoid(a0_ref[...] + _bdot(xl, wupa_ref[...]))
    g = _bdot(_sigmoid(xl), wupg_ref[...])
    kk = k * kk_ref[...]
    kk = kk / jnp.maximum(jnp.sqrt(_bdot(kk * kk, seg)), 1e-12)
    k = k * (1.0 + (a - 1.0) * ka_ref[...])
    logdec = -jnp.exp(w)

    ti = lax.broadcasted_iota(I32, (C, C), 0)
    tj = lax.broadcasted_iota(I32, (C, C), 1)
    lower = (ti >= tj).astype(BF16)
    cum = sum(jnp.dot(lower, part, preferred_element_type=F32) for part in _split3(logdec))
    cum_end = cum[C - 1:C, :]
    a_t = -kk * jnp.exp(cum - logdec)
    r_t = r * jnp.exp(cum)
    inv = jnp.exp(-cum)
    b_t = kk * a * inv
    k_t = k * inv
    to_end = jnp.exp(cum_end - cum)
    b_e = kk * a * to_end
    k_e = k * to_end
    g_end = jnp.exp(cum_end)

    eye = (ti == tj).astype(F32)
    ri = lax.broadcasted_iota(I32, (2 * C, 2 * C), 0)
    ci = lax.broadcasted_iota(I32, (2 * C, 2 * C), 1)
    tr = jnp.where(ri >= C, ri - C, ri)
    tc = jnp.where(ci >= C, ci - C, ci)
    mask = jnp.logical_or(tr > tc, jnp.logical_and(ri >= C, tr == tc))
    zeros_cn = jnp.zeros((C, N), F32)
    levels = int(math.log2(C))
    heads = range(RWKV_HEADS)
    sls = [slice(h * N, (h + 1) * N) for h in heads]
    ah = [a_t[:, s] for s in sls]
    rh = [r_t[:, s] for s in sls]
    vh = [v[:, s] for s in sls]
    m_all = [jnp.where(mask, _bdot_nt(jnp.concatenate([ah[h], rh[h]], axis=0),
                                      jnp.concatenate([b_t[:, sls[h]], k_t[:, sls[h]]], axis=0)), 0.0)
             for h in heads]
    m_top = [m[0:C, :] for m in m_all]
    m_bot = [m[C:2 * C, :] for m in m_all]
    akv = [_bdot(m_top[h], jnp.concatenate([zeros_cn, vh[h]], axis=0)) for h in heads]
    power = [m[:, 0:C] for m in m_top]
    t_inv = [eye + p for p in power]
    for _ in range(levels - 1):
        power = [_bdot(p, p) for p in power]
        t_inv = [t + _bdot(t, p) for t, p in zip(t_inv, power)]
    w_mat = [_bdot(t_inv[h], ah[h]) for h in heads]
    u0 = [_bdot(t_inv[h], akv[h]) for h in heads]
    s0 = [state_ref[h] for h in heads]
    x = [_bdot_nt(jnp.concatenate([w_mat[h], rh[h]], axis=0), s0[h]) for h in heads]
    uv = [jnp.concatenate([x[h][0:C, :] + u0[h], vh[h]], axis=0) for h in heads]
    ys = [x[h][C:2 * C, :] + _bdot(m_bot[h], uv[h]) for h in heads]
    s_new = [s0[h] * g_end[:, sls[h]]
             + _bdot_tn(uv[h], jnp.concatenate([b_e[:, sls[h]], k_e[:, sls[h]]], axis=0)) for h in heads]
    for h in heads:
        state_ref[h] = s_new[h]
    y = jnp.concatenate(ys, axis=1)

    mean = _bdot(y, seg) * (1.0 / N)
    yc = y - mean
    var = _bdot(yc * yc, seg) * (1.0 / N)
    yn = yc * lax.rsqrt(var + GN_EPS) * lg_ref[...] + lb_ref[...]
    bonus = _bdot(r * k * rk_ref[...], seg) * v
    y_ref[...] = ((yn + bonus) * g).astype(y_ref.dtype)

    @pl.when(c == nc - 1)
    def _():
        sout_ref[...] = state_ref[...]


def _rwkv(pm, plo, shift_m, shift_l, s0, wts, seq_len, chunk):
    t_total = pm.shape[0]
    nb = t_total // seq_len
    ncnk = seq_len // chunk
    c8 = chunk // SUBLANES
    prev8 = lambda b, c: (jnp.maximum(b * (seq_len // SUBLANES) + c * c8 - 1, 0), 0)
    cur = lambda b, c: (b * ncnk + c, 0)
    per_seq3 = lambda n: pl.BlockSpec((None, 1, n), lambda b, c: (b, 0, 0))
    state_spec = pl.BlockSpec((None, RWKV_HEADS, RWKV_HEAD_DIM, RWKV_HEAD_DIM), lambda b, c: (b, 0, 0, 0))
    return pl.pallas_call(
        functools.partial(_rwkv_body, chunk=chunk),
        grid=(nb, ncnk),
        in_specs=[pl.BlockSpec((chunk, 3 * RWKV_WIDTH), cur), pl.BlockSpec((chunk, LORA_PAD), cur),
                  pl.BlockSpec((SUBLANES, 3 * RWKV_WIDTH), prev8), pl.BlockSpec((SUBLANES, LORA_PAD), prev8),
                  per_seq3(3 * RWKV_WIDTH), per_seq3(LORA_PAD), state_spec]
                 + [_full_spec(a) for a in wts],
        out_specs=[pl.BlockSpec((chunk, RWKV_WIDTH), cur), state_spec],
        out_shape=[jax.ShapeDtypeStruct((t_total, RWKV_WIDTH), BF16 if chunk % 16 == 0 else F32),
                   jax.ShapeDtypeStruct(s0.shape, F32)],
        scratch_shapes=[pltpu.VMEM((RWKV_HEADS, RWKV_HEAD_DIM, RWKV_HEAD_DIM), F32)],
        compiler_params=_params("parallel", "arbitrary"),
        name="rwkv",
    )(pm, plo, pm, plo, shift_m, shift_l, s0, *wts)


def _lambda(lam_ref, lam_init):
    lv = lam_ref[...]
    return (jnp.exp(jnp.sum(lv[0:1, :] * lv[1:2, :], axis=-1, keepdims=True))
            - jnp.exp(jnp.sum(lv[2:3, :] * lv[3:4, :], axis=-1, keepdims=True)) + lam_init)


def _subln(o, g, lam_init):
    return o * lax.rsqrt(jnp.mean(o * o, axis=-1, keepdims=True) + NORM_EPS) * g * (1.0 - lam_init)


def _attn_prompt_body(qi_ref, ki_ref, q_ref, kt_ref, v_ref, lam_ref, sg_ref, o_ref,
                      qm_ref, m_ref, acc_ref, *, lam_init):
    p = pl.program_id(2)
    qi = qi_ref[p]
    ki = ki_ref[p]
    tq = q_ref.shape[0]
    tk = kt_ref.shape[1]
    dv = v_ref.shape[1]

    @pl.when(ki == 0)
    def _():
        q = q_ref[...]
        lane = lax.broadcasted_iota(I32, q.shape, 1)
        qm_ref[0] = jnp.where(lane < DIFF_DK, q, jnp.zeros_like(q))
        qm_ref[1] = jnp.where(lane >= DIFF_DK, q, jnp.zeros_like(q))
        m_ref[...] = jnp.full(m_ref.shape, NEG_INF, F32)
        acc_ref[...] = jnp.zeros(acc_ref.shape, F32)

    def accumulate(masked):
        kt = kt_ref[...].astype(BF16)
        v1 = jnp.concatenate([v_ref[...], jnp.ones((tk, dv), BF16)], axis=1)
        for c in range(2):
            s = jnp.dot(qm_ref[c], kt, preferred_element_type=F32)
            if masked:
                row = lax.broadcasted_iota(I32, (tq, tk), 0)
                col = lax.broadcasted_iota(I32, (tq, tk), 1)
                s = jnp.where(col <= row, s, NEG_INF)
            m_old = m_ref[c]
            m_new = jnp.maximum(m_old, jnp.max(s, axis=-1, keepdims=True))
            alpha = jnp.exp(m_old - m_new)
            pr = jnp.exp(s - m_new)
            acc_ref[c] = alpha * acc_ref[c] + jnp.dot(pr.astype(BF16), v1, preferred_element_type=F32)
            m_ref[c] = m_new

    @pl.when(ki < qi)
    def _():
        accumulate(False)

    @pl.when(ki == qi)
    def _():
        accumulate(True)
        lam = _lambda(lam_ref, lam_init)
        a0 = acc_ref[0]
        a1 = acc_ref[1]
        o = a0[:, 0:dv] / a0[:, dv:2 * dv] - lam * (a1[:, 0:dv] / a1[:, dv:2 * dv])
        o_ref[...] = _subln(o, sg_ref[...], lam_init).astype(o_ref.dtype)


def _attn_prompt(qb, kt, vb, lam_vecs, subln_g, seq_len, lam_init):
    t_total = qb.shape[0]
    nb = t_total // seq_len
    tq = min(512, seq_len)
    nq = seq_len // tq
    pairs = [(i, j) for i in range(nq) for j in range(i + 1)]
    qi_tab = jnp.asarray([a for a, _ in pairs], I32)
    ki_tab = jnp.asarray([b for _, b in pairs], I32)
    qmap = lambda b, h, p, qi, ki: (b * nq + qi[p], h)
    vmap = lambda b, h, p, qi, ki: (b * nq + ki[p], h)
    ktmap = lambda b, h, p, qi, ki: (b * DIFF_HEADS + h, ki[p])
    grid_spec = pltpu.PrefetchScalarGridSpec(
        num_scalar_prefetch=2,
        grid=(nb, DIFF_HEADS, len(pairs)),
        in_specs=[pl.BlockSpec((tq, DIFF_DV), qmap), pl.BlockSpec((2 * DIFF_DK, tq), ktmap),
                  pl.BlockSpec((tq, DIFF_DV), vmap),
                  pl.BlockSpec(lam_vecs.shape, lambda *_: (0, 0)), pl.BlockSpec(subln_g.shape, lambda *_: (0, 0))],
        out_specs=pl.BlockSpec((tq, DIFF_DV), qmap),
        scratch_shapes=[pltpu.VMEM((2, tq, DIFF_DV), BF16), pltpu.VMEM((2, tq, 1), F32),
                        pltpu.VMEM((2, tq, 2 * DIFF_DV), F32)],
    )
    return pl.pallas_call(
        functools.partial(_attn_prompt_body, lam_init=lam_init),
        grid_spec=grid_spec,
        out_shape=jax.ShapeDtypeStruct((t_total, DIFF_V_WIDTH), BF16),
        compiler_params=_params("parallel", "parallel", "arbitrary"),
        name="attn_prompt",
    )(qi_tab, ki_tab, qb, kt, vb, lam_vecs, subln_g)


def _attn_sample_body(pt_ref, q_ref, k_ref, v_ref, lam_ref, sg_ref, *rest, n_pages, lam_init):
    kp_refs = rest[:n_pages]
    vp_refs = rest[n_pages:2 * n_pages]
    o_ref = rest[2 * n_pages]
    s_new = q_ref.shape[0]
    n_maps = DIFF_HEADS * 2
    page = kp_refs[0].shape[1]
    nrow = n_maps * s_new

    def value_page(vr):
        return jnp.concatenate([vr[pl.ds(h, page, stride=DIFF_HEADS), :] for h in range(DIFF_HEADS)], axis=1)

    qt = jnp.concatenate([q_ref[...] * DIFF_SCALE] * (DIFF_HEADS * 2), axis=0)
    row = lax.broadcasted_iota(I32, qt.shape, 0)
    col = lax.broadcasted_iota(I32, qt.shape, 1)
    qbd = jnp.where(col // DIFF_DK == row // s_new, qt, 0.0).astype(BF16)
    pad = jnp.zeros((page - s_new, k_ref.shape[1]), F32)
    k_new = jnp.concatenate([k_ref[...], pad], axis=0)
    v_new = jnp.concatenate([v_ref[...], pad], axis=0)
    scores = [_bdot(qbd, kr[...]) for kr in kp_refs]
    s_n = _bdot_nt(qbd, k_new)
    rn = lax.broadcasted_iota(I32, s_n.shape, 0)
    cn = lax.broadcasted_iota(I32, s_n.shape, 1)
    scores.append(jnp.where(cn <= rn % s_new, s_n, NEG_INF))
    m = functools.reduce(jnp.maximum, [jnp.max(s, axis=-1, keepdims=True) for s in scores])
    values = [value_page(vr) for vr in vp_refs] + [v_new]
    l = jnp.zeros_like(m)
    acc = jnp.zeros((nrow, v_new.shape[1]), F32)
    for s, val in zip(scores, values):
        pr = jnp.exp(s - m)
        l = l + jnp.sum(pr, axis=-1, keepdims=True)
        acc = acc + _bdot(pr, val)
    full = acc / l
    lam = _lambda(lam_ref, lam_init)
    outs = []
    for h in range(DIFF_HEADS):
        r0 = h * 2 * s_new
        cols = slice(h * DIFF_DV, (h + 1) * DIFF_DV)
        o = full[r0:r0 + s_new, cols] - lam * full[r0 + s_new:r0 + 2 * s_new, cols]
        outs.append(_subln(o, sg_ref[...], lam_init))
    o_ref[...] = jnp.concatenate(outs, axis=1)


def _attn_sample(q, k, v, cache_k, cache_v, n_pool_pages, page_table, page_offset, lam_vecs, subln_g, s_new,
                 lam_init):
    nseq, n_pages = page_table.shape
    width = q.shape[1]
    k_rows = cache_k.shape[0] // n_pool_pages
    v_rows = cache_v.shape[0] // n_pool_pages
    new_spec = pl.BlockSpec((s_new, width), lambda b, pt: (b, 0))
    page_spec = lambda rows, lanes, j: pl.BlockSpec((rows, lanes), lambda b, pt: (pt[b, j] + page_offset, 0))
    grid_spec = pltpu.PrefetchScalarGridSpec(
        num_scalar_prefetch=1,
        grid=(nseq,),
        in_specs=[new_spec, new_spec, new_spec,
                  pl.BlockSpec(lam_vecs.shape, lambda *_: (0, 0)), pl.BlockSpec(subln_g.shape, lambda *_: (0, 0))]
                 + [page_spec(k_rows, cache_k.shape[1], j) for j in range(n_pages)]
                 + [page_spec(v_rows, DIFF_DV, j) for j in range(n_pages)],
        out_specs=new_spec,
    )
    return pl.pallas_call(
        functools.partial(_attn_sample_body, n_pages=n_pages, lam_init=lam_init),
        grid_spec=grid_spec,
        out_shape=jax.ShapeDtypeStruct((nseq * s_new, width), F32),
        compiler_params=_params("parallel"),
        name="attn_sample",
    )(page_table, q, k, v, lam_vecs, subln_g, *([cache_k] * n_pages), *([cache_v] * n_pages))


def _outproj_body(yr_ref, o_ref, x_ref, g1_ref, g2n_ref, sh_ref, sc_ref, g2_ref, woa_ref, wob_ref,
                  wsg_ref, wsu_ref, wsd_ref, wrh_ref, wrl_ref,
                  base_ref, h2t_ref, lg_ref):
    tm = x_ref.shape[0]
    mixed = (jnp.dot(yr_ref[...].astype(BF16), woa_ref[...], preferred_element_type=F32)
             + jnp.dot(o_ref[...].astype(BF16), wob_ref[...], preferred_element_type=F32))
    x1 = x_ref[...] + g1_ref[...] * mixed
    h2 = _rms(x1, g2n_ref[...]) * (1.0 + sc_ref[...]) + sh_ref[...]
    h2b = h2.astype(BF16)
    hidden = _silu(jnp.dot(h2b, wsg_ref[...], preferred_element_type=F32)) * jnp.dot(
        h2b, wsu_ref[...], preferred_element_type=F32)
    shared = jnp.dot(hidden.astype(BF16), wsd_ref[...], preferred_element_type=F32)
    base_ref[...] = x1 + g2_ref[...] * shared
    h2l = (h2 - h2b.astype(F32)).astype(BF16)
    nt = lambda a, b: lax.dot_general(a, b, (((1,), (1,)), ((), ())), preferred_element_type=F32)
    lg_ref[...] = nt(wrh_ref[...], h2b) + nt(wrl_ref[...], h2b) + nt(wrh_ref[...], h2l)
    for c in range(ROW_TILE):
        h2t_ref[pl.ds(c, tm, stride=ROW_TILE), :] = h2[:, c * LANES:(c + 1) * LANES]


def _outproj(yr, o, x, gate1, norm2_g, shift2, scale2, gate2, wts, tm, tiles_per_seq):
    t, d = x.shape
    row = lambda n: pl.BlockSpec((tm, n), lambda i: (i, 0))
    mod = lambda m: _mod_spec(m, tm, tiles_per_seq)
    return pl.pallas_call(
        _outproj_body,
        grid=(t // tm,),
        in_specs=[row(RWKV_WIDTH), row(DIFF_V_WIDTH), row(d), mod(gate1), _full_spec(norm2_g), mod(shift2),
                  mod(scale2), mod(gate2)] + [_full_spec(a) for a in wts],
        out_specs=[row(d), pl.BlockSpec((tm * ROW_TILE, LANES), lambda i: (i, 0)),
                   pl.BlockSpec((N_EXPERTS, tm), lambda i: (0, i))],
        out_shape=[jax.ShapeDtypeStruct((t, d), F32),
                   jax.ShapeDtypeStruct((t * ROW_TILE, LANES), F32),
                   jax.ShapeDtypeStruct((N_EXPERTS, t), F32)],
        compiler_params=_params("parallel"),
        name="outproj",
    )(yr, o, x, gate1, norm2_g, shift2, scale2, gate2, *wts)


def _router_body(lg_ref, bias_ref, idx_ref, gate_ref, pos_ref, cnt_ref, run_ref):
    i = pl.program_id(0)
    tm = lg_ref.shape[1]

    @pl.when(i == 0)
    def _():
        run_ref[...] = jnp.zeros(run_ref.shape, F32)

    scores = _sigmoid(lg_ref[...])
    biased = scores + bias_ref[...]
    erow = lax.broadcasted_iota(I32, (N_EXPERTS, tm), 0)
    grow = lax.broadcasted_iota(I32, (GROUP_SIZE, tm), 0)

    def first_argmax(x, rows, limit):
        mx = jnp.max(x, axis=0, keepdims=True)
        return mx, jnp.min(jnp.where(x == mx, rows, limit), axis=0, keepdims=True)

    group_scores = []
    for gidx in range(N_GROUPS):
        xg = biased[gidx * GROUP_SIZE:(gidx + 1) * GROUP_SIZE, :]
        m1, i1 = first_argmax(xg, grow, GROUP_SIZE)
        m2 = jnp.max(jnp.where(grow == i1, NEG_INF, xg), axis=0, keepdims=True)
        group_scores.append(m1 + m2)
    gs = jnp.concatenate(group_scores, axis=0)
    g8 = lax.broadcasted_iota(I32, (N_GROUPS, tm), 0)
    chosen = jnp.zeros((N_GROUPS, tm), I32)
    for _ in range(TOPK_GROUPS):
        _, gi = first_argmax(gs, g8, N_GROUPS)
        hit = g8 == gi
        chosen = jnp.where(hit, 1, chosen)
        gs = jnp.where(hit, NEG_INF, gs)
    cand = jnp.concatenate(
        [jnp.where(chosen[gidx:gidx + 1, :] > 0, biased[gidx * GROUP_SIZE:(gidx + 1) * GROUP_SIZE, :], NEG_INF)
         for gidx in range(N_GROUPS)], axis=0)

    idxs, raws = [], []
    onehot = jnp.zeros((N_EXPERTS, tm), F32)
    for _ in range(TOP_K):
        _, ei = first_argmax(cand, erow, N_EXPERTS)
        hit = erow == ei
        idxs.append(ei)
        raws.append(jnp.sum(jnp.where(hit, scores, 0.0), axis=0, keepdims=True))
        onehot = onehot + hit.astype(F32)
        cand = jnp.where(hit, NEG_INF, cand)
    raw = jnp.concatenate(raws, axis=0)
    gate_ref[...] = raw / jnp.sum(raw, axis=0, keepdims=True) * ROUTE_SCALE
    idx_ref[...] = jnp.concatenate(idxs, axis=0)

    ti = lax.broadcasted_iota(I32, (tm, tm), 0)
    tj = lax.broadcasted_iota(I32, (tm, tm), 1)
    oh = onehot.astype(BF16)
    before = jnp.dot(oh, (ti < tj).astype(BF16), preferred_element_type=F32) + run_ref[...]
    run_ref[...] = run_ref[...] + jnp.dot(oh, jnp.ones((tm, tm), BF16), preferred_element_type=F32)
    pos_ref[...] = jnp.concatenate(
        [jnp.sum(jnp.where(erow == ei, before, 0.0), axis=0, keepdims=True) for ei in idxs], axis=0).astype(I32)
    cnt_ref[...] = run_ref[...]


def _router(logits_t, bias_col):
    t = logits_t.shape[1]
    tm = ROUTE_TILE
    tok = pl.BlockSpec((TOP_K, tm), lambda i: (0, i))
    return pl.pallas_call(
        _router_body,
        grid=(t // tm,),
        in_specs=[pl.BlockSpec((N_EXPERTS, tm), lambda i: (0, i)), _full_spec(bias_col)],
        out_specs=[tok, tok, tok, pl.BlockSpec((N_EXPERTS, tm), lambda i: (0, 0))],
        out_shape=[jax.ShapeDtypeStruct((TOP_K, t), I32), jax.ShapeDtypeStruct((TOP_K, t), F32),
                   jax.ShapeDtypeStruct((TOP_K, t), I32), jax.ShapeDtypeStruct((N_EXPERTS, tm), F32)],
        scratch_shapes=[pltpu.VMEM((N_EXPERTS, tm), F32)],
        compiler_params=_params("arbitrary"),
        name="router",
    )(logits_t, bias_col)


def _assign_body(idx_ref, pos_ref, start_ref, dest_ref):
    tm = idx_ref.shape[1]
    erow = lax.broadcasted_iota(I32, (N_EXPERTS, tm), 0)
    start = start_ref[...]
    idx = idx_ref[...]
    first = jnp.concatenate(
        [jnp.sum(jnp.where(erow == idx[j:j + 1, :], start, 0.0), axis=0, keepdims=True) for j in range(TOP_K)],
        axis=0)
    dest_ref[...] = first.astype(I32) + pos_ref[...]


def _assign(idx, pos, start_col):
    t = idx.shape[1]
    tm = ROUTE_TILE
    tok = pl.BlockSpec((TOP_K, tm), lambda i: (0, i))
    return pl.pallas_call(
        _assign_body,
        grid=(t // tm,),
        in_specs=[tok, tok, _full_spec(start_col)],
        out_specs=tok,
        out_shape=jax.ShapeDtypeStruct((TOP_K, t), I32),
        compiler_params=_params("parallel"),
        name="assign",
    )(idx, pos, start_col)


def _row_copy(src, src_row, dst, dst_row, sem):
    return pltpu.make_async_copy(src.at[pl.ds(pl.multiple_of(src_row * ROW_TILE, ROW_TILE), ROW_TILE), :],
                                 dst.at[pl.ds(pl.multiple_of(dst_row * ROW_TILE, ROW_TILE), ROW_TILE), :], sem)


def _dispatch_body(zflag_ref, h_ref, dest_ref, xs_ref, zero_ref, zsem, rsem):
    i = pl.program_id(0)
    tm = dest_ref.shape[1]
    blk_rows = EXPERT_BLOCK * ROW_TILE
    n_blocks = xs_ref.shape[0] // blk_rows

    def zero_copy(b):
        start = pl.multiple_of(b * blk_rows, blk_rows)
        return pltpu.make_async_copy(zero_ref, xs_ref.at[pl.ds(start, blk_rows), :], zsem)

    @pl.when(i == 0)
    def _():
        zero_ref[...] = jnp.zeros(zero_ref.shape, F32)

        def issue(b, carry):
            @pl.when(zflag_ref[b] > 0)
            def _():
                zero_copy(b).start()
            return carry

        def drain(b, carry):
            @pl.when(zflag_ref[b] > 0)
            def _():
                zero_copy(b).wait()
            return carry

        lax.fori_loop(0, n_blocks, issue, 0)
        lax.fori_loop(0, n_blocks, drain, 0)

    def issue_rows(t, carry):
        for j in range(TOP_K):
            _row_copy(h_ref, t, xs_ref, dest_ref[j, t], rsem).start()
        return carry

    lax.fori_loop(0, tm, issue_rows, 0)
    for _ in range(TOP_K):
        pltpu.make_async_copy(h_ref, xs_ref.at[pl.ds(0, tm * ROW_TILE), :], rsem).wait()


def _dispatch(h2t, dest, zero_flags, n_rows):
    t = dest.shape[1]
    tm = ROUTE_TILE
    grid_spec = pltpu.PrefetchScalarGridSpec(
        num_scalar_prefetch=1,
        grid=(t // tm,),
        in_specs=[pl.BlockSpec((tm * ROW_TILE, LANES), lambda i, *_: (i, 0)),
                  pl.BlockSpec((TOP_K, tm), lambda i, *_: (0, i), memory_space=pltpu.SMEM)],
        out_specs=pl.BlockSpec(memory_space=pl.ANY),
        scratch_shapes=[pltpu.VMEM((EXPERT_BLOCK * ROW_TILE, LANES), F32),
                        pltpu.SemaphoreType.DMA, pltpu.SemaphoreType.DMA],
    )
    return pl.pallas_call(
        _dispatch_body,
        grid_spec=grid_spec,
        out_shape=jax.ShapeDtypeStruct((n_rows * ROW_TILE, LANES), F32),
        compiler_params=_params("arbitrary"),
        name="dispatch",
    )(zero_flags, h2t, dest)


def _expert_body(be_ref, nu_ref, xs_ref, wg_ref, wu_ref, wd_ref, y_ref):
    i = pl.program_id(0)
    rows = EXPERT_BLOCK

    @pl.when(i < nu_ref[0])
    def _():
        x = jnp.concatenate([xs_ref[pl.ds(c, rows, stride=ROW_TILE), :] for c in range(ROW_TILE)], axis=1)
        xb = x.astype(BF16)
        hidden = _silu(_bdot(xb, wg_ref[...])) * _bdot(xb, wu_ref[...])
        y = _bdot(hidden, wd_ref[...])
        for c in range(ROW_TILE):
            y_ref[pl.ds(c, rows, stride=ROW_TILE), :] = y[:, c * LANES:(c + 1) * LANES]

    @pl.when(i >= nu_ref[0])
    def _():
        y_ref[...] = jnp.zeros(y_ref.shape, F32)


def _experts(xs, block_expert, n_used, we_gate, we_up, we_down):
    n_blocks = block_expert.shape[0]
    d, f = we_gate.shape[1], we_gate.shape[2]
    blk = (EXPERT_BLOCK * ROW_TILE, LANES)
    grid_spec = pltpu.PrefetchScalarGridSpec(
        num_scalar_prefetch=2,
        grid=(n_blocks,),
        in_specs=[pl.BlockSpec(blk, lambda i, be, nu: (jnp.minimum(i, nu[0] - 1), 0)),
                  pl.BlockSpec((None, d, f), lambda i, be, nu: (be[i], 0, 0)),
                  pl.BlockSpec((None, d, f), lambda i, be, nu: (be[i], 0, 0)),
                  pl.BlockSpec((None, f, d), lambda i, be, nu: (be[i], 0, 0))],
        out_specs=pl.BlockSpec(blk, lambda i, be, nu: (i, 0)),
    )
    return pl.pallas_call(
        _expert_body,
        grid_spec=grid_spec,
        out_shape=jax.ShapeDtypeStruct(xs.shape, F32),
        compiler_params=_params("arbitrary"),
        name="experts",
    )(block_expert, n_used, xs, we_gate, we_up, we_down)


def _combine_body(base_ref, g2_ref, gate_ref, dest_ref, y_ref, out_ref, buf_ref, sem):
    tm = base_ref.shape[0]
    slot_rows = tm * ROW_TILE

    def issue(t, carry):
        for j in range(TOP_K):
            _row_copy(y_ref, dest_ref[j, t], buf_ref, j * tm + t, sem).start()
        return carry

    lax.fori_loop(0, tm, issue, 0)
    pltpu.make_async_copy(y_ref.at[pl.ds(0, TOP_K * slot_rows), :], buf_ref, sem).wait()
    gates = gate_ref[...]
    gcols = [jnp.broadcast_to(gates[:, j:j + 1], (tm, LANES)) for j in range(TOP_K)]
    for c in range(ROW_TILE):
        cols = slice(c * LANES, (c + 1) * LANES)
        routed = sum(gcols[j] * buf_ref[pl.ds(j * slot_rows + c, tm, stride=ROW_TILE), :] for j in range(TOP_K))
        out_ref[:, cols] = base_ref[:, cols] + g2_ref[:, cols] * routed


def _combine(base, gate2, gates, dest, y_rows, tok_offset, tiles_per_seq):
    t, d = base.shape
    tm = ROUTE_TILE
    off = tok_offset // tm
    return pl.pallas_call(
        _combine_body,
        grid=(t // tm,),
        in_specs=[pl.BlockSpec((tm, d), lambda i: (i, 0)), _mod_spec(gate2, tm, tiles_per_seq),
                  pl.BlockSpec((tm, TOP_K), lambda i: (i + off, 0)),
                  pl.BlockSpec((TOP_K, tm), lambda i: (0, i + off), memory_space=pltpu.SMEM),
                  pl.BlockSpec(memory_space=pl.ANY)],
        out_specs=pl.BlockSpec((tm, d), lambda i: (i, 0)),
        out_shape=jax.ShapeDtypeStruct((t, d), F32),
        scratch_shapes=[pltpu.VMEM((TOP_K * tm * ROW_TILE, LANES), F32), pltpu.SemaphoreType.DMA],
        compiler_params=_params("arbitrary"),
        name="combine",
    )(base, gate2, gates, dest, y_rows)


def _segment_ones(width, seg):
    ids = jnp.arange(width) // seg
    return (ids[:, None] == ids[None, :]).astype(BF16)


def _pad_rows(w, start, total):
    return jnp.zeros((total, w.shape[1]), w.dtype).at[start:start + w.shape[0]].set(w)


def kernel(x_prompt, x_sample, cache_k, cache_v, state_wkv, state_shift, page_table, c_prompt, c_sample, w_ada, b_ada, norm1_g, norm2_g, w_in, mu_shift, w0, w_lora_up, a0, a_lora_up, g_lora_up, k_k, k_a, r_k, lnx_g, lnx_b, qn_g, kn_g, lam_q1, lam_k1, lam_q2, lam_k2, subln_g, w_out, w_router, router_bias, we_gate, we_up, we_down, ws_gate, ws_up, ws_down):
    depth = w_in.shape[0]
    nb, seq, d = x_prompt.shape
    db, dseq, _ = x_sample.shape
    n_pool, page = cache_k.shape[1], cache_k.shape[2]
    tp, ts = nb * seq, db * dseq
    w3 = 3 * RWKV_WIDTH
    row = lambda a: a.reshape(1, -1)
    seg512 = _segment_ones(RWKV_WIDTH, RWKV_HEAD_DIM)
    tm_p = min(256, seq)
    tm_s = min(256, ts)
    chunk_p = min(64, seq)

    xp = x_prompt.reshape(tp, d)
    xs = x_sample.reshape(ts, d)
    cache_k2 = jnp.transpose(cache_k, (0, 1, 3, 4, 5, 2)).reshape(depth * n_pool * DIFF_QK_WIDTH, page)
    cache_v2 = cache_v.reshape(depth * n_pool * page * DIFF_HEADS, DIFF_DV)
    n_cond = nb + db
    cond = jnp.concatenate([c_prompt, c_sample], axis=0)
    cond = jnp.pad(cond, ((0, -n_cond % SUBLANES), (0, 0)))

    outs = {name: [] for name in ("kp", "vp", "wp", "sp", "ks", "vs", "ws", "ss")}
    for layer in range(depth):
        lam_init = 0.8 - 0.6 * math.exp(-0.3 * layer)
        mods = _ada(cond, w_ada[layer], row(b_ada[layer]))
        mod_p = [m.reshape(nb, 1, d) for m in jnp.split(mods[:nb], 6, axis=-1)]
        mod_s = [jnp.repeat(m, dseq, axis=0) for m in jnp.split(mods[nb:n_cond], 6, axis=-1)]

        wi = w_in[layer]
        in_wts = (wi[:, :w3].astype(BF16),
                  jnp.pad(wi[:, w3:RWKV_PROJ], ((0, 0), (0, LORA_PAD - LORA_WIDTH))).astype(BF16),
                  wi[:, RWKV_PROJ:RWKV_PROJ + DIFF_QK_WIDTH].astype(BF16),
                  wi[:, RWKV_PROJ + DIFF_QK_WIDTH:RWKV_PROJ + 2 * DIFF_QK_WIDTH].astype(BF16),
                  wi[:, RWKV_PROJ + 2 * DIFF_QK_WIDTH:].astype(BF16),
                  row(jnp.tile(qn_g[layer], DIFF_QK_WIDTH // DIFF_DK)),
                  row(jnp.tile(kn_g[layer], DIFF_QK_WIDTH // DIFF_DK)),
                  seg512)
        mu = mu_shift[layer]
        rwkv_wts = (row(mu[:w3]), row(jnp.pad(mu[w3:], (0, LORA_PAD - LORA_WIDTH))), row(w0[layer]),
                    _pad_rows(w_lora_up[layer], 0, LORA_PAD).astype(BF16), row(a0[layer]),
                    _pad_rows(a_lora_up[layer], DECAY_LORA, LORA_PAD).astype(BF16),
                    _pad_rows(g_lora_up[layer], DECAY_LORA + ICLR_LORA, LORA_PAD).astype(BF16),
                    row(k_k[layer]), row(k_a[layer]), row(r_k[layer]), row(lnx_g[layer]), row(lnx_b[layer]), seg512)
        lam_vecs = jnp.stack([lam_q1[layer], lam_k1[layer], lam_q2[layer], lam_k2[layer]])
        sg = row(subln_g[layer])
        wr = w_router[layer].T
        wr_hi = wr.astype(BF16)
        out_wts = (w_out[layer][:RWKV_WIDTH].astype(BF16), w_out[layer][RWKV_WIDTH:].astype(BF16),
                   ws_gate[layer].astype(BF16), ws_up[layer].astype(BF16), ws_down[layer].astype(BF16),
                   wr_hi, (wr - wr_hi.astype(F32)).astype(BF16))

        def token_mix(x, mod, tm, tiles_per_seq, seq_len, chunk, shift_prev, wkv0, long_seq):
            pm, plo, *qkv = _inproj(x, row(norm1_g[layer]), mod[0], mod[1], in_wts, tm, tiles_per_seq, long_seq)
            n_seq = x.shape[0] // seq_len
            shift_m = shift_prev[:, :, :w3]
            shift_l = jnp.pad(shift_prev[:, :, w3:], ((0, 0), (0, 0), (0, LORA_PAD - LORA_WIDTH)))
            yr, wkv = _rwkv(pm, plo, shift_m, shift_l, wkv0, rwkv_wts, seq_len, chunk)
            last = jnp.concatenate([pm.reshape(n_seq, seq_len, w3)[:, -1:],
                                    plo.reshape(n_seq, seq_len, LORA_PAD)[:, -1:, :LORA_WIDTH]], axis=-1)
            return yr, wkv, last, qkv

        yr_p, wkv_p, last_p, (kt_p, v4_p, qb_p, vb_p) = token_mix(
            xp, mod_p, tm_p, seq // tm_p, seq, chunk_p,
            jnp.zeros((nb, 1, RWKV_PROJ), F32), jnp.zeros((nb, RWKV_HEADS, RWKV_HEAD_DIM, RWKV_HEAD_DIM), F32), True)
        o_p = _attn_prompt(qb_p, kt_p, vb_p, lam_vecs, sg, seq, lam_init)
        yr_s, wkv_s, last_s, (q_s, k_s, v_s) = token_mix(
            xs, mod_s, tm_s, 1, dseq, dseq, state_shift[layer], state_wkv[layer], False)
        o_s = _attn_sample(q_s, k_s, v_s, cache_k2, cache_v2, depth * n_pool, page_table, layer * n_pool, lam_vecs,
                           sg, dseq, lam_init)

        base_p, h2t_p, lg_p = _outproj(yr_p, o_p, xp, mod_p[2], row(norm2_g[layer]), mod_p[3], mod_p[4], mod_p[5],
                                       out_wts, tm_p, seq // tm_p)
        base_s, h2t_s, lg_s = _outproj(yr_s, o_s, xs, mod_s[2], row(norm2_g[layer]), mod_s[3], mod_s[4], mod_s[5],
                                       out_wts, tm_s, 1)

        t_all = tp + ts
        idx, gate, pos, cnt = _router(jnp.concatenate([lg_p, lg_s], axis=1), router_bias[layer].reshape(-1, 1))
        counts = cnt[:, 0].astype(I32)
        padded = (counts + EXPERT_BLOCK - 1) // EXPERT_BLOCK * EXPERT_BLOCK
        end_padded = jnp.cumsum(padded)
        dest = _assign(idx, pos, (end_padded - padded).astype(F32).reshape(-1, 1))
        n_rows = (t_all * TOP_K + N_EXPERTS * (EXPERT_BLOCK - 1) + EXPERT_BLOCK - 1) // EXPERT_BLOCK * EXPERT_BLOCK
        n_blocks = n_rows // EXPERT_BLOCK
        blk_ids = jnp.arange(n_blocks, dtype=I32)
        block_expert = jnp.minimum(
            jnp.sum((end_padded[None, :] <= blk_ids[:, None] * EXPERT_BLOCK).astype(I32), axis=1), N_EXPERTS - 1)
        n_used = (end_padded[-1:] // EXPERT_BLOCK).astype(I32)
        is_expert_end = jnp.any(end_padded[None, :] == (blk_ids[:, None] + 1) * EXPERT_BLOCK, axis=1)
        zero_flags = jnp.logical_or(blk_ids >= n_used[0], is_expert_end).astype(I32)
        rows_in = _dispatch(jnp.concatenate([h2t_p, h2t_s], axis=0), dest, zero_flags, n_rows)
        rows_out = _experts(rows_in, block_expert, n_used, we_gate[layer], we_up[layer], we_down[layer])
        gates_t = gate.T
        xp = _combine(base_p, mod_p[5], gates_t, dest, rows_out, 0, seq // ROUTE_TILE)
        xs = _combine(base_s, mod_s[5], gates_t, dest, rows_out, tp, 1)

        outs["kp"].append(jnp.transpose(kt_p.reshape(nb, DIFF_HEADS, 2, DIFF_DK, seq), (0, 4, 1, 2, 3)))
        outs["vp"].append(v4_p.reshape(nb, seq, DIFF_HEADS, DIFF_DV))
        outs["wp"].append(wkv_p)
        outs["sp"].append(last_p)
        outs["ks"].append(k_s.reshape(db, dseq, DIFF_HEADS, 2, DIFF_DK))
        outs["vs"].append(v_s.reshape(db, dseq, DIFF_HEADS, DIFF_DV))
        outs["ws"].append(wkv_s)
        outs["ss"].append(last_s)

    st = {name: jnp.stack(v) for name, v in outs.items()}
    return (xp.reshape(nb, seq, d), xs.reshape(db, dseq, d), st["kp"], st["vp"], st["wp"], st["sp"],
            st["ks"], st["vs"], st["ws"], st["ss"])
```

```python
import functools
import math

import jax
import jax.numpy as jnp
from jax import lax
from jax.experimental import pallas as pl
from jax.experimental.pallas import tpu as pltpu

F32 = jnp.float32
BF16 = jnp.bfloat16
I32 = jnp.int32

RWKV_HEADS = 8
RWKV_HEAD_DIM = 64
RWKV_WIDTH = RWKV_HEADS * RWKV_HEAD_DIM
DECAY_LORA = 32
ICLR_LORA = 32
GATE_LORA = 96
LORA_WIDTH = DECAY_LORA + ICLR_LORA + GATE_LORA
LORA_PAD = 256
RWKV_PROJ = 3 * RWKV_WIDTH + LORA_WIDTH
GN_EPS = 64e-5
DIFF_HEADS = 4
DIFF_DK = 64
DIFF_DV = 2 * DIFF_DK
DIFF_QK_WIDTH = DIFF_HEADS * 2 * DIFF_DK
DIFF_V_WIDTH = DIFF_HEADS * DIFF_DV
DIFF_SCALE = DIFF_DK ** -0.5
N_EXPERTS = 256
N_GROUPS = 8
GROUP_SIZE = N_EXPERTS // N_GROUPS
TOPK_GROUPS = 4
TOP_K = 8
ROUTE_SCALE = 2.5
NORM_EPS = 1e-6

LANES = 128
SUBLANES = 8
ROW_TILE = 8
VMEM_LIMIT = 48 * 1024 * 1024

EXPERT_BLOCK = 128
ROUTE_TILE = 128
ATTN_ROW_CHUNK = 256
NEG_INF = float("-inf")


def _bdot(a, b):
    return jnp.dot(a.astype(BF16), b.astype(BF16), preferred_element_type=F32)


def _bdot_nt(a, b):
    return lax.dot_general(a.astype(BF16), b.astype(BF16), (((1,), (1,)), ((), ())), preferred_element_type=F32)


def _bdot_tn(a, b):
    return lax.dot_general(a.astype(BF16), b.astype(BF16), (((0,), (0,)), ((), ())), preferred_element_type=F32)


def _sigmoid(x):
    return 1.0 / (1.0 + jnp.exp(-x))


def _silu(x):
    return x * _sigmoid(x)


def _params(*sem, vmem=VMEM_LIMIT):
    return pltpu.CompilerParams(dimension_semantics=sem, vmem_limit_bytes=vmem)


def _ada_body(c_ref, w_ref, b_ref, o_ref):
    o_ref[...] = _bdot(_silu(c_ref[...]), w_ref[...]) + b_ref[...]


def _ada(c, w, b):
    rows, d = c.shape
    n = w.shape[1]
    tn = 512
    return pl.pallas_call(
        _ada_body,
        grid=(n // tn,),
        in_specs=[pl.BlockSpec((rows, d), lambda j: (0, 0)),
                  pl.BlockSpec((d, tn), lambda j: (0, j)),
                  pl.BlockSpec((1, tn), lambda j: (0, j))],
        out_specs=pl.BlockSpec((rows, tn), lambda j: (0, j)),
        out_shape=jax.ShapeDtypeStruct((rows, n), F32),
        compiler_params=_params("parallel"),
        name="ada",
    )(c, w, b)


def _mod_spec(mod, tm, tiles_per_seq):
    if mod.ndim == 3:
        return pl.BlockSpec((None, 1, mod.shape[-1]), lambda i: (i // tiles_per_seq, 0, 0))
    return pl.BlockSpec((tm, mod.shape[-1]), lambda i: (i, 0))


def _full_spec(a):
    nd = a.ndim
    return pl.BlockSpec(a.shape, lambda *_: (0,) * nd)


def _rms(x, g):
    return x * lax.rsqrt(jnp.mean(x * x, axis=-1, keepdims=True) + NORM_EPS) * g


def _inproj_body(x_ref, g_ref, sh_ref, sc_ref, wm_ref, wl_ref, wq_ref, wk_ref, wv_ref, qg_ref, kg_ref, seg_ref,
                 pm_ref, pl_ref, *out_refs, long_seq):
    tm = x_ref.shape[0]
    h = (_rms(x_ref[...], g_ref[...]) * (1.0 + sc_ref[...]) + sh_ref[...]).astype(BF16)
    pm_ref[...] = jnp.dot(h, wm_ref[...], preferred_element_type=F32)
    pl_ref[...] = jnp.dot(h, wl_ref[...], preferred_element_type=F32)
    seg = seg_ref[...]

    def head_norm(z, gain):
        ms = _bdot(z * z, seg) * (1.0 / DIFF_DK)
        return z * lax.rsqrt(ms + NORM_EPS) * gain

    q = head_norm(jnp.dot(h, wq_ref[...], preferred_element_type=F32), qg_ref[...])
    k = head_norm(jnp.dot(h, wk_ref[...], preferred_element_type=F32), kg_ref[...])
    v = jnp.dot(h, wv_ref[...], preferred_element_type=F32)
    if long_seq:
        kt_ref, v4_ref, qb_ref, vb_ref = out_refs
        kt_ref[...] = k.T
        for hd in range(DIFF_HEADS):
            v4_ref[pl.ds(hd, tm, stride=DIFF_HEADS), :] = v[:, hd * DIFF_DV:(hd + 1) * DIFF_DV]
        qb_ref[...] = (q * DIFF_SCALE).astype(BF16)
        vb_ref[...] = v.astype(BF16)
    else:
        q_ref, k_ref, v_ref = out_refs
        q_ref[...] = q
        k_ref[...] = k
        v_ref[...] = v


def _inproj(x, g, shift, scale, wts, tm, tiles_per_seq, long_seq):
    t, d = x.shape
    wm, wl, wq, wk, wv, qg, kg, seg = wts
    row = lambda n: pl.BlockSpec((tm, n), lambda i: (i, 0))
    out_specs = [row(3 * RWKV_WIDTH), row(LORA_PAD)]
    out_shape = [jax.ShapeDtypeStruct((t, 3 * RWKV_WIDTH), F32), jax.ShapeDtypeStruct((t, LORA_PAD), F32)]
    if long_seq:
        n_seq = t // (tm * tiles_per_seq)
        out_specs += [pl.BlockSpec((DIFF_QK_WIDTH, tm), lambda i: (i // tiles_per_seq, i % tiles_per_seq)),
                      pl.BlockSpec((tm * DIFF_HEADS, DIFF_DV), lambda i: (i, 0)),
                      row(DIFF_QK_WIDTH), row(DIFF_V_WIDTH)]
        out_shape += [jax.ShapeDtypeStruct((n_seq * DIFF_QK_WIDTH, tm * tiles_per_seq), F32),
                      jax.ShapeDtypeStruct((t * DIFF_HEADS, DIFF_DV), F32),
                      jax.ShapeDtypeStruct((t, DIFF_QK_WIDTH), BF16), jax.ShapeDtypeStruct((t, DIFF_V_WIDTH), BF16)]
    else:
        out_specs += [row(DIFF_QK_WIDTH), row(DIFF_QK_WIDTH), row(DIFF_V_WIDTH)]
        out_shape += [jax.ShapeDtypeStruct((t, n), F32) for n in (DIFF_QK_WIDTH, DIFF_QK_WIDTH, DIFF_V_WIDTH)]
    return pl.pallas_call(
        functools.partial(_inproj_body, long_seq=long_seq),
        grid=(t // tm,),
        in_specs=[row(d), _full_spec(g), _mod_spec(shift, tm, tiles_per_seq), _mod_spec(scale, tm, tiles_per_seq)]
                 + [_full_spec(a) for a in (wm, wl, wq, wk, wv, qg, kg, seg)],
        out_specs=out_specs,
        out_shape=out_shape,
        compiler_params=_params("parallel"),
        name="inproj",
    )(x, g, shift, scale, wm, wl, wq, wk, wv, qg, kg, seg)


def _split3(x):
    hi = x.astype(BF16)
    r1 = x - hi.astype(F32)
    mid = r1.astype(BF16)
    lo = (r1 - mid.astype(F32)).astype(BF16)
    return hi, mid, lo


def _rwkv_body(pm_ref, pl_ref, pm8_ref, pl8_ref, sm_ref, sl_ref, s0_ref,
               mum_ref, mul_ref, w0_ref, wupw_ref, a0_ref, wupa_ref, wupg_ref, kk_ref, ka_ref, rk_ref,
               lg_ref, lb_ref, seg_ref,
               y_ref, sout_ref, state_ref, *, chunk):
    c = pl.program_id(1)
    nc = pl.num_programs(1)
    C = chunk
    W = RWKV_WIDTH
    N = RWKV_HEAD_DIM

    @pl.when(c == 0)
    def _():
        state_ref[...] = s0_ref[...]

    first = c == 0
    prev_m = jnp.where(first, sm_ref[...], pm8_ref[SUBLANES - 1:SUBLANES, :])
    prev_l = jnp.where(first, sl_ref[...], pl8_ref[SUBLANES - 1:SUBLANES, :])

    def shifted(cur, prev_row):
        rows = lax.broadcasted_iota(I32, cur.shape, 0)
        return jnp.where(rows == 0, prev_row, pltpu.roll(cur, 1, 0))

    pm = pm_ref[...]
    plo = pl_ref[...]
    xm = pm + (shifted(pm, prev_m) - pm) * mum_ref[...]
    xl = plo + (shifted(plo, prev_l) - plo) * mul_ref[...]
    r = xm[:, 0:W]
    k = xm[:, W:2 * W]
    v = xm[:, 2 * W:3 * W]
    seg = seg_ref[...]

    z = -(w0_ref[...] + _bdot(jnp.tanh(xl), wupw_ref[...]))
    softplus = jnp.maximum(z, 0.0) + jnp.log(1.0 + jnp.exp(-jnp.abs(z)))
    w = -softplus - 0.5
    a = _sigmoid(a0_ref[...] + _bdot(xl, wupa_ref[...]))
    g = _bdot(_sigmoid(xl), wupg_ref[...])
    kk = k * kk_ref[...]
    kk = kk / jnp.maximum(jnp.sqrt(_bdot(kk * kk, seg)), 1e-12)
    k = k * (1.0 + (a - 1.0) * ka_ref[...])
    logdec = -jnp.exp(w)

    ti = lax.broadcasted_iota(I32, (C, C), 0)
    tj = lax.broadcasted_iota(I32, (C, C), 1)
    lower = (ti >= tj).astype(BF16)
    cum = sum(jnp.dot(lower, part, preferred_element_type=F32) for part in _split3(logdec))
    cum_end = cum[C - 1:C, :]
    a_t = -kk * jnp.exp(cum - logdec)
    r_t = r * jnp.exp(cum)
    inv = jnp.exp(-cum)
    b_t = kk * a * inv
    k_t = k * inv
    to_end = jnp.exp(cum_end - cum)
    b_e = kk * a * to_end
    k_e = k * to_end
    g_end = jnp.exp(cum_end)

    eye = (ti == tj).astype(F32)
    ri = lax.broadcasted_iota(I32, (2 * C, 2 * C), 0)
    ci = lax.broadcasted_iota(I32, (2 * C, 2 * C), 1)
    tr = jnp.where(ri >= C, ri - C, ri)
    tc = jnp.where(ci >= C, ci - C, ci)
    mask = jnp.logical_or(tr > tc, jnp.logical_and(ri >= C, tr == tc))
    zeros_cn = jnp.zeros((C, N), F32)
    levels = int(math.log2(C))
    heads = range(RWKV_HEADS)
    sls = [slice(h * N, (h + 1) * N) for h in heads]
    ah = [a_t[:, s] for s in sls]
    rh = [r_t[:, s] for s in sls]
    vh = [v[:, s] for s in sls]
    m_all = [jnp.where(mask, _bdot_nt(jnp.concatenate([ah[h], rh[h]], axis=0),
                                      jnp.concatenate([b_t[:, sls[h]], k_t[:, sls[h]]], axis=0)), 0.0)
             for h in heads]
    m_top = [m[0:C, :] for m in m_all]
    m_bot = [m[C:2 * C, :] for m in m_all]
    akv = [_bdot(m_top[h], jnp.concatenate([zeros_cn, vh[h]], axis=0)) for h in heads]
    power = [m[:, 0:C] for m in m_top]
    t_inv = [eye + p for p in power]
    for _ in range(levels - 1):
        power = [_bdot(p, p) for p in power]
        t_inv = [t + _bdot(t, p) for t, p in zip(t_inv, power)]
    w_mat = [_bdot(t_inv[h], ah[h]) for h in heads]
    u0 = [_bdot(t_inv[h], akv[h]) for h in heads]
    s0 = [state_ref[h] for h in heads]
    x = [_bdot_nt(jnp.concatenate([w_mat[h], rh[h]], axis=0), s0[h]) for h in heads]
    uv = [jnp.concatenate([x[h][0:C, :] + u0[h], vh[h]], axis=0) for h in heads]
    ys = [x[h][C:2 * C, :] + _bdot(m_bot[h], uv[h]) for h in heads]
    s_new = [s0[h] * g_end[:, sls[h]]
             + _bdot_tn(uv[h], jnp.concatenate([b_e[:, sls[h]], k_e[:, sls[h]]], axis=0)) for h in heads]
    for h in heads:
        state_ref[h] = s_new[h]
    y = jnp.concatenate(ys, axis=1)

    mean = _bdot(y, seg) * (1.0 / N)
    yc = y - mean
    var = _bdot(yc * yc, seg) * (1.0 / N)
    yn = yc * lax.rsqrt(var + GN_EPS) * lg_ref[...] + lb_ref[...]
    bonus = _bdot(r * k * rk_ref[...], seg) * v
    y_ref[...] = ((yn + bonus) * g).astype(y_ref.dtype)

    @pl.when(c == nc - 1)
    def _():
        sout_ref[...] = state_ref[...]


def _rwkv(pm, plo, shift_m, shift_l, s0, wts, seq_len, chunk):
    t_total = pm.shape[0]
    nb = t_total // seq_len
    ncnk = seq_len // chunk
    c8 = chunk // SUBLANES
    prev8 = lambda b, c: (jnp.maximum(b * (seq_len // SUBLANES) + c * c8 - 1, 0), 0)
    cur = lambda b, c: (b * ncnk + c, 0)
    per_seq3 = lambda n: pl.BlockSpec((None, 1, n), lambda b, c: (b, 0, 0))
    state_spec = pl.BlockSpec((None, RWKV_HEADS, RWKV_HEAD_DIM, RWKV_HEAD_DIM), lambda b, c: (b, 0, 0, 0))
    return pl.pallas_call(
        functools.partial(_rwkv_body, chunk=chunk),
        grid=(nb, ncnk),
        in_specs=[pl.BlockSpec((chunk, 3 * RWKV_WIDTH), cur), pl.BlockSpec((chunk, LORA_PAD), cur),
                  pl.BlockSpec((SUBLANES, 3 * RWKV_WIDTH), prev8), pl.BlockSpec((SUBLANES, LORA_PAD), prev8),
                  per_seq3(3 * RWKV_WIDTH), per_seq3(LORA_PAD), state_spec]
                 + [_full_spec(a) for a in wts],
        out_specs=[pl.BlockSpec((chunk, RWKV_WIDTH), cur), state_spec],
        out_shape=[jax.ShapeDtypeStruct((t_total, RWKV_WIDTH), BF16 if chunk % 16 == 0 else F32),
                   jax.ShapeDtypeStruct(s0.shape, F32)],
        scratch_shapes=[pltpu.VMEM((RWKV_HEADS, RWKV_HEAD_DIM, RWKV_HEAD_DIM), F32)],
        compiler_params=_params("parallel", "arbitrary"),
        name="rwkv",
    )(pm, plo, pm, plo, shift_m, shift_l, s0, *wts)


def _lambda(lam_ref, lam_init):
    lv = lam_ref[...]
    return (jnp.exp(jnp.sum(lv[0:1, :] * lv[1:2, :], axis=-1, keepdims=True))
            - jnp.exp(jnp.sum(lv[2:3, :] * lv[3:4, :], axis=-1, keepdims=True)) + lam_init)


def _subln(o, g, lam_init):
    return o * lax.rsqrt(jnp.mean(o * o, axis=-1, keepdims=True) + NORM_EPS) * g * (1.0 - lam_init)


def _attn_prompt_body(qi_ref, ki_ref, q_ref, kt_ref, v_ref, lam_ref, sg_ref, o_ref,
                      qm_ref, m_ref, acc_ref, *, lam_init):
    p = pl.program_id(2)
    qi = qi_ref[p]
    ki = ki_ref[p]
    tq = q_ref.shape[0]
    tk = kt_ref.shape[1]
    dv = v_ref.shape[1]

    @pl.when(ki == 0)
    def _():
        q = q_ref[...]
        lane = lax.broadcasted_iota(I32, q.shape, 1)
        qm_ref[0] = jnp.where(lane < DIFF_DK, q, jnp.zeros_like(q))
        qm_ref[1] = jnp.where(lane >= DIFF_DK, q, jnp.zeros_like(q))
        m_ref[...] = jnp.full(m_ref.shape, NEG_INF, F32)
        acc_ref[...] = jnp.zeros(acc_ref.shape, F32)

    def accumulate(masked):
        kt = kt_ref[...].astype(BF16)
        v1 = jnp.concatenate([v_ref[...], jnp.ones((tk, dv), BF16)], axis=1)
        for c in range(2):
            for r0 in range(0, tq, ATTN_ROW_CHUNK):
                rows = slice(r0, r0 + ATTN_ROW_CHUNK)
                ncol = min(tk, r0 + ATTN_ROW_CHUNK) if masked else tk
                s = jnp.dot(qm_ref[c, rows, :], kt[:, 0:ncol], preferred_element_type=F32)
                if masked:
                    row = lax.broadcasted_iota(I32, s.shape, 0) + r0
                    col = lax.broadcasted_iota(I32, s.shape, 1)
                    s = jnp.where(col <= row, s, NEG_INF)
                m_old = m_ref[c, rows, :]
                m_new = jnp.maximum(m_old, jnp.max(s, axis=-1, keepdims=True))
                alpha = jnp.exp(m_old - m_new)
                pr = jnp.exp(s - jnp.concatenate([m_new] * (ncol // LANES), axis=1))
                acc_ref[c, rows, :] = (jnp.concatenate([alpha] * (2 * dv // LANES), axis=1) * acc_ref[c, rows, :]
                                       + jnp.dot(pr.astype(BF16), v1[0:ncol, :], preferred_element_type=F32))
                m_ref[c, rows, :] = m_new

    @pl.when(ki < qi)
    def _():
        accumulate(False)

    @pl.when(ki == qi)
    def _():
        accumulate(True)
        lam = _lambda(lam_ref, lam_init)
        a0 = acc_ref[0]
        a1 = acc_ref[1]
        o = a0[:, 0:dv] / a0[:, dv:2 * dv] - lam * (a1[:, 0:dv] / a1[:, dv:2 * dv])
        o_ref[...] = _subln(o, sg_ref[...], lam_init).astype(o_ref.dtype)


def _attn_prompt(qb, kt, vb, lam_vecs, subln_g, seq_len, lam_init):
    t_total = qb.shape[0]
    nb = t_total // seq_len
    tq = min(512, seq_len)
    nq = seq_len // tq
    pairs = [(i, j) for i in range(nq) for j in range(i + 1)]
    qi_tab = jnp.asarray([a for a, _ in pairs], I32)
    ki_tab = jnp.asarray([b for _, b in pairs], I32)
    qmap = lambda b, h, p, qi, ki: (b * nq + qi[p], h)
    vmap = lambda b, h, p, qi, ki: (b * nq + ki[p], h)
    ktmap = lambda b, h, p, qi, ki: (b * DIFF_HEADS + h, ki[p])
    grid_spec = pltpu.PrefetchScalarGridSpec(
        num_scalar_prefetch=2,
        grid=(nb, DIFF_HEADS, len(pairs)),
        in_specs=[pl.BlockSpec((tq, DIFF_DV), qmap), pl.BlockSpec((2 * DIFF_DK, tq), ktmap),
                  pl.BlockSpec((tq, DIFF_DV), vmap),
                  pl.BlockSpec(lam_vecs.shape, lambda *_: (0, 0)), pl.BlockSpec(subln_g.shape, lambda *_: (0, 0))],
        out_specs=pl.BlockSpec((tq, DIFF_DV), qmap),
        scratch_shapes=[pltpu.VMEM((2, tq, DIFF_DV), BF16), pltpu.VMEM((2, tq, LANES), F32),
                        pltpu.VMEM((2, tq, 2 * DIFF_DV), F32)],
    )
    return pl.pallas_call(
        functools.partial(_attn_prompt_body, lam_init=lam_init),
        grid_spec=grid_spec,
        out_shape=jax.ShapeDtypeStruct((t_total, DIFF_V_WIDTH), BF16),
        compiler_params=_params("parallel", "parallel", "arbitrary"),
        name="attn_prompt",
    )(qi_tab, ki_tab, qb, kt, vb, lam_vecs, subln_g)


def _attn_sample_body(pt_ref, q_ref, k_ref, v_ref, lam_ref, sg_ref, *rest, n_pages, lam_init):
    kp_refs = rest[:n_pages]
    vp_refs = rest[n_pages:2 * n_pages]
    o_ref = rest[2 * n_pages]
    s_new = q_ref.shape[0]
    n_maps = DIFF_HEADS * 2
    page = kp_refs[0].shape[1]
    nrow = n_maps * s_new

    def value_page(vr):
        return jnp.concatenate([vr[pl.ds(h, page, stride=DIFF_HEADS), :] for h in range(DIFF_HEADS)], axis=1)

    qt = jnp.concatenate([q_ref[...] * DIFF_SCALE] * (DIFF_HEADS * 2), axis=0)
    row = lax.broadcasted_iota(I32, qt.shape, 0)
    col = lax.broadcasted_iota(I32, qt.shape, 1)
    qbd = jnp.where(col // DIFF_DK == row // s_new, qt, 0.0).astype(BF16)
    pad = jnp.zeros((page - s_new, k_ref.shape[1]), F32)
    k_new = jnp.concatenate([k_ref[...], pad], axis=0)
    v_new = jnp.concatenate([v_ref[...], pad], axis=0)
    scores = [_bdot(qbd, kr[...]) for kr in kp_refs]
    s_n = _bdot_nt(qbd, k_new)
    rn = lax.broadcasted_iota(I32, s_n.shape, 0)
    cn = lax.broadcasted_iota(I32, s_n.shape, 1)
    scores.append(jnp.where(cn <= rn % s_new, s_n, NEG_INF))
    m = functools.reduce(jnp.maximum, [jnp.max(s, axis=-1, keepdims=True) for s in scores])
    values = [value_page(vr) for vr in vp_refs] + [v_new]
    l = jnp.zeros_like(m)
    acc = jnp.zeros((nrow, v_new.shape[1]), F32)
    for s, val in zip(scores, values):
        pr = jnp.exp(s - m)
        l = l + jnp.sum(pr, axis=-1, keepdims=True)
        acc = acc + _bdot(pr, val)
    full = acc / l
    lam = _lambda(lam_ref, lam_init)
    outs = []
    for h in range(DIFF_HEADS):
        r0 = h * 2 * s_new
        cols = slice(h * DIFF_DV, (h + 1) * DIFF_DV)
        o = full[r0:r0 + s_new, cols] - lam * full[r0 + s_new:r0 + 2 * s_new, cols]
        outs.append(_subln(o, sg_ref[...], lam_init))
    o_ref[...] = jnp.concatenate(outs, axis=1)


def _attn_sample(q, k, v, cache_k, cache_v, n_pool_pages, page_table, page_offset, lam_vecs, subln_g, s_new,
                 lam_init):
    nseq, n_pages = page_table.shape
    width = q.shape[1]
    k_rows = cache_k.shape[0] // n_pool_pages
    v_rows = cache_v.shape[0] // n_pool_pages
    new_spec = pl.BlockSpec((s_new, width), lambda b, pt: (b, 0))
    page_spec = lambda rows, lanes, j: pl.BlockSpec((rows, lanes), lambda b, pt: (pt[b, j] + page_offset, 0))
    grid_spec = pltpu.PrefetchScalarGridSpec(
        num_scalar_prefetch=1,
        grid=(nseq,),
        in_specs=[new_spec, new_spec, new_spec,
                  pl.BlockSpec(lam_vecs.shape, lambda *_: (0, 0)), pl.BlockSpec(subln_g.shape, lambda *_: (0, 0))]
                 + [page_spec(k_rows, cache_k.shape[1], j) for j in range(n_pages)]
                 + [page_spec(v_rows, DIFF_DV, j) for j in range(n_pages)],
        out_specs=new_spec,
    )
    return pl.pallas_call(
        functools.partial(_attn_sample_body, n_pages=n_pages, lam_init=lam_init),
        grid_spec=grid_spec,
        out_shape=jax.ShapeDtypeStruct((nseq * s_new, width), F32),
        compiler_params=_params("parallel"),
        name="attn_sample",
    )(page_table, q, k, v, lam_vecs, subln_g, *([cache_k] * n_pages), *([cache_v] * n_pages))


def _outproj_body(yr_ref, o_ref, x_ref, g1_ref, g2n_ref, sh_ref, sc_ref, g2_ref, woa_ref, wob_ref,
                  wsg_ref, wsu_ref, wsd_ref, wrh_ref, wrl_ref,
                  base_ref, h2t_ref, lg_ref):
    tm = x_ref.shape[0]
    mixed = (jnp.dot(yr_ref[...].astype(BF16), woa_ref[...], preferred_element_type=F32)
             + jnp.dot(o_ref[...].astype(BF16), wob_ref[...], preferred_element_type=F32))
    x1 = x_ref[...] + g1_ref[...] * mixed
    h2 = _rms(x1, g2n_ref[...]) * (1.0 + sc_ref[...]) + sh_ref[...]
    h2b = h2.astype(BF16)
    hidden = _silu(jnp.dot(h2b, wsg_ref[...], preferred_element_type=F32)) * jnp.dot(
        h2b, wsu_ref[...], preferred_element_type=F32)
    shared = jnp.dot(hidden.astype(BF16), wsd_ref[...], preferred_element_type=F32)
    base_ref[...] = x1 + g2_ref[...] * shared
    h2l = (h2 - h2b.astype(F32)).astype(BF16)
    nt = lambda a, b: lax.dot_general(a, b, (((1,), (1,)), ((), ())), preferred_element_type=F32)
    lg_ref[...] = nt(wrh_ref[...], h2b) + nt(wrl_ref[...], h2b) + nt(wrh_ref[...], h2l)
    for c in range(ROW_TILE):
        h2t_ref[pl.ds(c, tm, stride=ROW_TILE), :] = h2[:, c * LANES:(c + 1) * LANES]


def _outproj(yr, o, x, gate1, norm2_g, shift2, scale2, gate2, wts, tm, tiles_per_seq):
    t, d = x.shape
    row = lambda n: pl.BlockSpec((tm, n), lambda i: (i, 0))
    mod = lambda m: _mod_spec(m, tm, tiles_per_seq)
    return pl.pallas_call(
        _outproj_body,
        grid=(t // tm,),
        in_specs=[row(RWKV_WIDTH), row(DIFF_V_WIDTH), row(d), mod(gate1), _full_spec(norm2_g), mod(shift2),
                  mod(scale2), mod(gate2)] + [_full_spec(a) for a in wts],
        out_specs=[row(d), pl.BlockSpec((tm * ROW_TILE, LANES), lambda i: (i, 0)),
                   pl.BlockSpec((N_EXPERTS, tm), lambda i: (0, i))],
        out_shape=[jax.ShapeDtypeStruct((t, d), F32),
                   jax.ShapeDtypeStruct((t * ROW_TILE, LANES), F32),
                   jax.ShapeDtypeStruct((N_EXPERTS, t), F32)],
        compiler_params=_params("parallel"),
        name="outproj",
    )(yr, o, x, gate1, norm2_g, shift2, scale2, gate2, *wts)


def _router_body(lg_ref, bias_ref, idx_ref, gate_ref, pos_ref, cnt_ref, run_ref):
    i = pl.program_id(0)
    tm = lg_ref.shape[1]

    @pl.when(i == 0)
    def _():
        run_ref[...] = jnp.zeros(run_ref.shape, F32)

    scores = _sigmoid(lg_ref[...])
    biased = scores + bias_ref[...]
    erow = lax.broadcasted_iota(I32, (N_EXPERTS, tm), 0)
    grow = lax.broadcasted_iota(I32, (GROUP_SIZE, tm), 0)

    def first_argmax(x, rows, limit):
        mx = jnp.max(x, axis=0, keepdims=True)
        return mx, jnp.min(jnp.where(x == mx, rows, limit), axis=0, keepdims=True)

    group_scores = []
    for gidx in range(N_GROUPS):
        xg = biased[gidx * GROUP_SIZE:(gidx + 1) * GROUP_SIZE, :]
        m1, i1 = first_argmax(xg, grow, GROUP_SIZE)
        m2 = jnp.max(jnp.where(grow == i1, NEG_INF, xg), axis=0, keepdims=True)
        group_scores.append(m1 + m2)
    gs = jnp.concatenate(group_scores, axis=0)
    g8 = lax.broadcasted_iota(I32, (N_GROUPS, tm), 0)
    chosen = jnp.zeros((N_GROUPS, tm), I32)
    for _ in range(TOPK_GROUPS):
        _, gi = first_argmax(gs, g8, N_GROUPS)
        hit = g8 == gi
        chosen = jnp.where(hit, 1, chosen)
        gs = jnp.where(hit, NEG_INF, gs)
    cand = jnp.concatenate(
        [jnp.where(chosen[gidx:gidx + 1, :] > 0, biased[gidx * GROUP_SIZE:(gidx + 1) * GROUP_SIZE, :], NEG_INF)
         for gidx in range(N_GROUPS)], axis=0)

    idxs, raws = [], []
    onehot = jnp.zeros((N_EXPERTS, tm), F32)
    for _ in range(TOP_K):
        _, ei = first_argmax(cand, erow, N_EXPERTS)
        hit = erow == ei
        idxs.append(ei)
        raws.append(jnp.sum(jnp.where(hit, scores, 0.0), axis=0, keepdims=True))
        onehot = onehot + hit.astype(F32)
        cand = jnp.where(hit, NEG_INF, cand)
    raw = jnp.concatenate(raws, axis=0)
    gate_ref[...] = raw / jnp.sum(raw, axis=0, keepdims=True) * ROUTE_SCALE
    idx_ref[...] = jnp.concatenate(idxs, axis=0)

    ti = lax.broadcasted_iota(I32, (tm, tm), 0)
    tj = lax.broadcasted_iota(I32, (tm, tm), 1)
    oh = onehot.astype(BF16)
    before = jnp.dot(oh, (ti < tj).astype(BF16), preferred_element_type=F32) + run_ref[...]
    run_ref[...] = run_ref[...] + jnp.dot(oh, jnp.ones((tm, tm), BF16), preferred_element_type=F32)
    pos_ref[...] = jnp.concatenate(
        [jnp.sum(jnp.where(erow == ei, before, 0.0), axis=0, keepdims=True) for ei in idxs], axis=0).astype(I32)
    cnt_ref[...] = run_ref[...]


def _router(logits_t, bias_col):
    t = logits_t.shape[1]
    tm = ROUTE_TILE
    tok = pl.BlockSpec((TOP_K, tm), lambda i: (0, i))
    return pl.pallas_call(
        _router_body,
        grid=(t // tm,),
        in_specs=[pl.BlockSpec((N_EXPERTS, tm), lambda i: (0, i)), _full_spec(bias_col)],
        out_specs=[tok, tok, tok, pl.BlockSpec((N_EXPERTS, tm), lambda i: (0, 0))],
        out_shape=[jax.ShapeDtypeStruct((TOP_K, t), I32), jax.ShapeDtypeStruct((TOP_K, t), F32),
                   jax.ShapeDtypeStruct((TOP_K, t), I32), jax.ShapeDtypeStruct((N_EXPERTS, tm), F32)],
        scratch_shapes=[pltpu.VMEM((N_EXPERTS, tm), F32)],
        compiler_params=_params("arbitrary"),
        name="router",
    )(logits_t, bias_col)


def _assign_body(idx_ref, pos_ref, start_ref, dest_ref):
    tm = idx_ref.shape[1]
    erow = lax.broadcasted_iota(I32, (N_EXPERTS, tm), 0)
    start = start_ref[...]
    idx = idx_ref[...]
    first = jnp.concatenate(
        [jnp.sum(jnp.where(erow == idx[j:j + 1, :], start, 0.0), axis=0, keepdims=True) for j in range(TOP_K)],
        axis=0)
    dest_ref[...] = first.astype(I32) + pos_ref[...]


def _assign(idx, pos, start_col):
    t = idx.shape[1]
    tm = ROUTE_TILE
    tok = pl.BlockSpec((TOP_K, tm), lambda i: (0, i))
    return pl.pallas_call(
        _assign_body,
        grid=(t // tm,),
        in_specs=[tok, tok, _full_spec(start_col)],
        out_specs=tok,
        out_shape=jax.ShapeDtypeStruct((TOP_K, t), I32),
        compiler_params=_params("parallel"),
        name="assign",
    )(idx, pos, start_col)


def _row_copy(src, src_row, dst, dst_row, sem):
    return pltpu.make_async_copy(src.at[pl.ds(pl.multiple_of(src_row * ROW_TILE, ROW_TILE), ROW_TILE), :],
                                 dst.at[pl.ds(pl.multiple_of(dst_row * ROW_TILE, ROW_TILE), ROW_TILE), :], sem)


def _dispatch_body(zflag_ref, h_ref, dest_ref, xs_ref, zero_ref, zsem, rsem):
    i = pl.program_id(0)
    tm = dest_ref.shape[1]
    blk_rows = EXPERT_BLOCK * ROW_TILE
    n_blocks = xs_ref.shape[0] // blk_rows

    def zero_copy(b):
        start = pl.multiple_of(b * blk_rows, blk_rows)
        return pltpu.make_async_copy(zero_ref, xs_ref.at[pl.ds(start, blk_rows), :], zsem)

    @pl.when(i == 0)
    def _():
        zero_ref[...] = jnp.zeros(zero_ref.shape, F32)

        def issue(b, carry):
            @pl.when(zflag_ref[b] > 0)
            def _():
                zero_copy(b).start()
            return carry

        def drain(b, carry):
            @pl.when(zflag_ref[b] > 0)
            def _():
                zero_copy(b).wait()
            return carry

        lax.fori_loop(0, n_blocks, issue, 0)
        lax.fori_loop(0, n_blocks, drain, 0)

    def issue_rows(t, carry):
        for j in range(TOP_K):
            _row_copy(h_ref, t, xs_ref, dest_ref[j, t], rsem).start()
        return carry

    lax.fori_loop(0, tm, issue_rows, 0)
    for _ in range(TOP_K):
        pltpu.make_async_copy(h_ref, xs_ref.at[pl.ds(0, tm * ROW_TILE), :], rsem).wait()


def _dispatch(h2t, dest, zero_flags, n_rows):
    t = dest.shape[1]
    tm = ROUTE_TILE
    grid_spec = pltpu.PrefetchScalarGridSpec(
        num_scalar_prefetch=1,
        grid=(t // tm,),
        in_specs=[pl.BlockSpec((tm * ROW_TILE, LANES), lambda i, *_: (i, 0)),
                  pl.BlockSpec((TOP_K, tm), lambda i, *_: (0, i), memory_space=pltpu.SMEM)],
        out_specs=pl.BlockSpec(memory_space=pl.ANY),
        scratch_shapes=[pltpu.VMEM((EXPERT_BLOCK * ROW_TILE, LANES), F32),
                        pltpu.SemaphoreType.DMA, pltpu.SemaphoreType.DMA],
    )
    return pl.pallas_call(
        _dispatch_body,
        grid_spec=grid_spec,
        out_shape=jax.ShapeDtypeStruct((n_rows * ROW_TILE, LANES), F32),
        compiler_params=_params("arbitrary"),
        name="dispatch",
    )(zero_flags, h2t, dest)


def _expert_body(first_ref, count_ref, nu_ref, xs_ref, wg_ref, wu_ref, wd_ref, y_ref,
                 xin_ref, yout_ref, wgb_ref, wub_ref, wdb_ref, in_sem, out_sem):
    e = pl.program_id(0)
    rows = EXPERT_BLOCK
    blk_rows = EXPERT_BLOCK * ROW_TILE
    nb = count_ref[e]
    b0 = first_ref[e]

    def block_rows(b):
        return pl.ds(pl.multiple_of((b0 + b) * blk_rows, blk_rows), blk_rows)

    def in_copy(b, slot):
        return pltpu.make_async_copy(xs_ref.at[block_rows(b), :], xin_ref.at[slot], in_sem.at[slot])

    def out_copy(b, slot):
        return pltpu.make_async_copy(yout_ref.at[slot], y_ref.at[block_rows(b), :], out_sem.at[slot])

    @pl.when(nb > 0)
    def _():
        in_copy(0, 0).start()
        wgb_ref[...] = wg_ref[...].astype(BF16)
        wub_ref[...] = wu_ref[...].astype(BF16)
        wdb_ref[...] = wd_ref[...].astype(BF16)

        def step(b, carry):
            slot = b % 2
            in_copy(b, slot).wait()

            @pl.when(b + 1 < nb)
            def _():
                in_copy(b + 1, 1 - slot).start()

            @pl.when(b >= 2)
            def _():
                out_copy(b - 2, slot).wait()

            x = jnp.concatenate([xin_ref[slot, pl.ds(c, rows, stride=ROW_TILE), :] for c in range(ROW_TILE)],
                                axis=1).astype(BF16)
            hidden = _silu(jnp.dot(x, wgb_ref[...], preferred_element_type=F32)) * jnp.dot(
                x, wub_ref[...], preferred_element_type=F32)
            y = jnp.dot(hidden.astype(BF16), wdb_ref[...], preferred_element_type=F32)
            for c in range(ROW_TILE):
                yout_ref[slot, pl.ds(c, rows, stride=ROW_TILE), :] = y[:, c * LANES:(c + 1) * LANES]
            out_copy(b, slot).start()
            return carry

        lax.fori_loop(0, nb, step, 0)

        @pl.when(nb >= 2)
        def _():
            out_copy(nb - 2, nb % 2).wait()

        out_copy(nb - 1, (nb - 1) % 2).wait()

    @pl.when(e == pl.num_programs(0) - 1)
    def _():
        n_blocks = y_ref.shape[0] // blk_rows
        yout_ref[0] = jnp.zeros(yout_ref.shape[1:], F32)

        def tail(b, carry):
            cp = pltpu.make_async_copy(
                yout_ref.at[0], y_ref.at[pl.ds(pl.multiple_of(b * blk_rows, blk_rows), blk_rows), :], out_sem.at[0])
            cp.start()
            cp.wait()
            return carry

        lax.fori_loop(nu_ref[0], n_blocks, tail, 0)


def _experts(xs, first_block, block_count, n_used, we_gate, we_up, we_down):
    d, f = we_gate.shape[1], we_gate.shape[2]
    blk = (EXPERT_BLOCK * ROW_TILE, LANES)
    wspec = lambda a, b: pl.BlockSpec((None, a, b), lambda e, *_: (e, 0, 0))
    grid_spec = pltpu.PrefetchScalarGridSpec(
        num_scalar_prefetch=3,
        grid=(N_EXPERTS,),
        in_specs=[pl.BlockSpec(memory_space=pl.ANY), wspec(d, f), wspec(d, f), wspec(f, d)],
        out_specs=pl.BlockSpec(memory_space=pl.ANY),
        scratch_shapes=[pltpu.VMEM((2,) + blk, F32), pltpu.VMEM((2,) + blk, F32),
                        pltpu.VMEM((d, f), BF16), pltpu.VMEM((d, f), BF16), pltpu.VMEM((f, d), BF16),
                        pltpu.SemaphoreType.DMA((2,)), pltpu.SemaphoreType.DMA((2,))],
    )
    return pl.pallas_call(
        _expert_body,
        grid_spec=grid_spec,
        out_shape=jax.ShapeDtypeStruct(xs.shape, F32),
        compiler_params=_params("arbitrary"),
        name="experts",
    )(first_block, block_count, n_used, xs, we_gate, we_up, we_down)


def _combine_body(base_ref, g2_ref, gate_ref, dest_ref, y_ref, out_ref, buf_ref, sem):
    tm = base_ref.shape[0]
    slot_rows = tm * ROW_TILE

    def issue(t, carry):
        for j in range(TOP_K):
            _row_copy(y_ref, dest_ref[j, t], buf_ref, j * tm + t, sem).start()
        return carry

    lax.fori_loop(0, tm, issue, 0)
    pltpu.make_async_copy(y_ref.at[pl.ds(0, TOP_K * slot_rows), :], buf_ref, sem).wait()
    gates = gate_ref[...]
    gcols = [jnp.broadcast_to(gates[:, j:j + 1], (tm, LANES)) for j in range(TOP_K)]
    for c in range(ROW_TILE):
        cols = slice(c * LANES, (c + 1) * LANES)
        routed = sum(gcols[j] * buf_ref[pl.ds(j * slot_rows + c, tm, stride=ROW_TILE), :] for j in range(TOP_K))
        out_ref[:, cols] = base_ref[:, cols] + g2_ref[:, cols] * routed


def _combine(base, gate2, gates, dest, y_rows, tok_offset, tiles_per_seq):
    t, d = base.shape
    tm = ROUTE_TILE
    off = tok_offset // tm
    return pl.pallas_call(
        _combine_body,
        grid=(t // tm,),
        in_specs=[pl.BlockSpec((tm, d), lambda i: (i, 0)), _mod_spec(gate2, tm, tiles_per_seq),
                  pl.BlockSpec((tm, TOP_K), lambda i: (i + off, 0)),
                  pl.BlockSpec((TOP_K, tm), lambda i: (0, i + off), memory_space=pltpu.SMEM),
                  pl.BlockSpec(memory_space=pl.ANY)],
        out_specs=pl.BlockSpec((tm, d), lambda i: (i, 0)),
        out_shape=jax.ShapeDtypeStruct((t, d), F32),
        scratch_shapes=[pltpu.VMEM((TOP_K * tm * ROW_TILE, LANES), F32), pltpu.SemaphoreType.DMA],
        compiler_params=_params("arbitrary"),
        name="combine",
    )(base, gate2, gates, dest, y_rows)


def _segment_ones(width, seg):
    ids = jnp.arange(width) // seg
    return (ids[:, None] == ids[None, :]).astype(BF16)


def _pad_rows(w, start, total):
    return jnp.zeros((total, w.shape[1]), w.dtype).at[start:start + w.shape[0]].set(w)


def kernel(x_prompt, x_sample, cache_k, cache_v, state_wkv, state_shift, page_table, c_prompt, c_sample, w_ada, b_ada, norm1_g, norm2_g, w_in, mu_shift, w0, w_lora_up, a0, a_lora_up, g_lora_up, k_k, k_a, r_k, lnx_g, lnx_b, qn_g, kn_g, lam_q1, lam_k1, lam_q2, lam_k2, subln_g, w_out, w_router, router_bias, we_gate, we_up, we_down, ws_gate, ws_up, ws_down):
    depth = w_in.shape[0]
    nb, seq, d = x_prompt.shape
    db, dseq, _ = x_sample.shape
    n_pool, page = cache_k.shape[1], cache_k.shape[2]
    tp, ts = nb * seq, db * dseq
    w3 = 3 * RWKV_WIDTH
    row = lambda a: a.reshape(1, -1)
    seg512 = _segment_ones(RWKV_WIDTH, RWKV_HEAD_DIM)
    tm_p = min(256, seq)
    tm_s = min(256, ts)
    chunk_p = min(64, seq)

    xp = x_prompt.reshape(tp, d)
    xs = x_sample.reshape(ts, d)
    cache_k2 = jnp.transpose(cache_k, (0, 1, 3, 4, 5, 2)).reshape(depth * n_pool * DIFF_QK_WIDTH, page)
    cache_v2 = cache_v.reshape(depth * n_pool * page * DIFF_HEADS, DIFF_DV)
    n_cond = nb + db
    cond = jnp.concatenate([c_prompt, c_sample], axis=0)
    cond = jnp.pad(cond, ((0, -n_cond % SUBLANES), (0, 0)))

    outs = {name: [] for name in ("kp", "vp", "wp", "sp", "ks", "vs", "ws", "ss")}
    for layer in range(depth):
        lam_init = 0.8 - 0.6 * math.exp(-0.3 * layer)
        mods = _ada(cond, w_ada[layer], row(b_ada[layer]))
        mod_p = [m.reshape(nb, 1, d) for m in jnp.split(mods[:nb], 6, axis=-1)]
        mod_s = [jnp.repeat(m, dseq, axis=0) for m in jnp.split(mods[nb:n_cond], 6, axis=-1)]

        wi = w_in[layer]
        in_wts = (wi[:, :w3].astype(BF16),
                  jnp.pad(wi[:, w3:RWKV_PROJ], ((0, 0), (0, LORA_PAD - LORA_WIDTH))).astype(BF16),
                  wi[:, RWKV_PROJ:RWKV_PROJ + DIFF_QK_WIDTH].astype(BF16),
                  wi[:, RWKV_PROJ + DIFF_QK_WIDTH:RWKV_PROJ + 2 * DIFF_QK_WIDTH].astype(BF16),
                  wi[:, RWKV_PROJ + 2 * DIFF_QK_WIDTH:].astype(BF16),
                  row(jnp.tile(qn_g[layer], DIFF_QK_WIDTH // DIFF_DK)),
                  row(jnp.tile(kn_g[layer], DIFF_QK_WIDTH // DIFF_DK)),
                  seg512)
        mu = mu_shift[layer]
        rwkv_wts = (row(mu[:w3]), row(jnp.pad(mu[w3:], (0, LORA_PAD - LORA_WIDTH))), row(w0[layer]),
                    _pad_rows(w_lora_up[layer], 0, LORA_PAD).astype(BF16), row(a0[layer]),
                    _pad_rows(a_lora_up[layer], DECAY_LORA, LORA_PAD).astype(BF16),
                    _pad_rows(g_lora_up[layer], DECAY_LORA + ICLR_LORA, LORA_PAD).astype(BF16),
                    row(k_k[layer]), row(k_a[layer]), row(r_k[layer]), row(lnx_g[layer]), row(lnx_b[layer]), seg512)
        lam_vecs = jnp.stack([lam_q1[layer], lam_k1[layer], lam_q2[layer], lam_k2[layer]])
        sg = row(subln_g[layer])
        wr = w_router[layer].T
        wr_hi = wr.astype(BF16)
        out_wts = (w_out[layer][:RWKV_WIDTH].astype(BF16), w_out[layer][RWKV_WIDTH:].astype(BF16),
                   ws_gate[layer].astype(BF16), ws_up[layer].astype(BF16), ws_down[layer].astype(BF16),
                   wr_hi, (wr - wr_hi.astype(F32)).astype(BF16))

        def token_mix(x, mod, tm, tiles_per_seq, seq_len, chunk, shift_prev, wkv0, long_seq):
            pm, plo, *qkv = _inproj(x, row(norm1_g[layer]), mod[0], mod[1], in_wts, tm, tiles_per_seq, long_seq)
            n_seq = x.shape[0] // seq_len
            shift_m = shift_prev[:, :, :w3]
            shift_l = jnp.pad(shift_prev[:, :, w3:], ((0, 0), (0, 0), (0, LORA_PAD - LORA_WIDTH)))
            yr, wkv = _rwkv(pm, plo, shift_m, shift_l, wkv0, rwkv_wts, seq_len, chunk)
            last = jnp.concatenate([pm.reshape(n_seq, seq_len, w3)[:, -1:],
                                    plo.reshape(n_seq, seq_len, LORA_PAD)[:, -1:, :LORA_WIDTH]], axis=-1)
            return yr, wkv, last, qkv

        yr_p, wkv_p, last_p, (kt_p, v4_p, qb_p, vb_p) = token_mix(
            xp, mod_p, tm_p, seq // tm_p, seq, chunk_p,
            jnp.zeros((nb, 1, RWKV_PROJ), F32), jnp.zeros((nb, RWKV_HEADS, RWKV_HEAD_DIM, RWKV_HEAD_DIM), F32), True)
        o_p = _attn_prompt(qb_p, kt_p, vb_p, lam_vecs, sg, seq, lam_init)
        yr_s, wkv_s, last_s, (q_s, k_s, v_s) = token_mix(
            xs, mod_s, tm_s, 1, dseq, dseq, state_shift[layer], state_wkv[layer], False)
        o_s = _attn_sample(q_s, k_s, v_s, cache_k2, cache_v2, depth * n_pool, page_table, layer * n_pool, lam_vecs,
                           sg, dseq, lam_init)

        base_p, h2t_p, lg_p = _outproj(yr_p, o_p, xp, mod_p[2], row(norm2_g[layer]), mod_p[3], mod_p[4], mod_p[5],
                                       out_wts, tm_p, seq // tm_p)
        base_s, h2t_s, lg_s = _outproj(yr_s, o_s, xs, mod_s[2], row(norm2_g[layer]), mod_s[3], mod_s[4], mod_s[5],
                                       out_wts, tm_s, 1)

        t_all = tp + ts
        idx, gate, pos, cnt = _router(jnp.concatenate([lg_p, lg_s], axis=1), router_bias[layer].reshape(-1, 1))
        counts = cnt[:, 0].astype(I32)
        padded = (counts + EXPERT_BLOCK - 1) // EXPERT_BLOCK * EXPERT_BLOCK
        end_padded = jnp.cumsum(padded)
        dest = _assign(idx, pos, (end_padded - padded).astype(F32).reshape(-1, 1))
        n_rows = (t_all * TOP_K + N_EXPERTS * (EXPERT_BLOCK - 1) + EXPERT_BLOCK - 1) // EXPERT_BLOCK * EXPERT_BLOCK
        n_blocks = n_rows // EXPERT_BLOCK
        blk_ids = jnp.arange(n_blocks, dtype=I32)
        n_used = (end_padded[-1:] // EXPERT_BLOCK).astype(I32)
        is_expert_end = jnp.any(end_padded[None, :] == (blk_ids[:, None] + 1) * EXPERT_BLOCK, axis=1)
        zero_flags = jnp.logical_or(blk_ids >= n_used[0], is_expert_end).astype(I32)
        rows_in = _dispatch(jnp.concatenate([h2t_p, h2t_s], axis=0), dest, zero_flags, n_rows)
        rows_out = _experts(rows_in, ((end_padded - padded) // EXPERT_BLOCK).astype(I32),
                            (padded // EXPERT_BLOCK).astype(I32), n_used, we_gate[layer], we_up[layer],
                            we_down[layer])
        gates_t = gate.T
        xp = _combine(base_p, mod_p[5], gates_t, dest, rows_out, 0, seq // ROUTE_TILE)
        xs = _combine(base_s, mod_s[5], gates_t, dest, rows_out, tp, 1)

        outs["kp"].append(jnp.transpose(kt_p.reshape(nb, DIFF_HEADS, 2, DIFF_DK, seq), (0, 4, 1, 2, 3)))
        outs["vp"].append(v4_p.reshape(nb, seq, DIFF_HEADS, DIFF_DV))
        outs["wp"].append(wkv_p)
        outs["sp"].append(last_p)
        outs["ks"].append(k_s.reshape(db, dseq, DIFF_HEADS, 2, DIFF_DK))
        outs["vs"].append(v_s.reshape(db, dseq, DIFF_HEADS, DIFF_DV))
        outs["ws"].append(wkv_s)
        outs["ss"].append(last_s)

    st = {name: jnp.stack(v) for name, v in outs.items()}
    return (xp.reshape(nb, seq, d), xs.reshape(db, dseq, d), st["kp"], st["vp"], st["wp"], st["sp"],
            st["ks"], st["vs"], st["ws"], st["ss"])
```

```python
import functools
import math

import jax
import jax.numpy as jnp
from jax import lax
from jax.experimental import pallas as pl
from jax.experimental.pallas import tpu as pltpu

F32 = jnp.float32
BF16 = jnp.bfloat16
I32 = jnp.int32

RWKV_HEADS = 8
RWKV_HEAD_DIM = 64
RWKV_WIDTH = RWKV_HEADS * RWKV_HEAD_DIM
DECAY_LORA = 32
ICLR_LORA = 32
GATE_LORA = 96
LORA_WIDTH = DECAY_LORA + ICLR_LORA + GATE_LORA
LORA_PAD = 256
RWKV_PROJ = 3 * RWKV_WIDTH + LORA_WIDTH
GN_EPS = 64e-5
DIFF_HEADS = 4
DIFF_DK = 64
DIFF_DV = 2 * DIFF_DK
DIFF_QK_WIDTH = DIFF_HEADS * 2 * DIFF_DK
DIFF_V_WIDTH = DIFF_HEADS * DIFF_DV
DIFF_SCALE = DIFF_DK ** -0.5
N_EXPERTS = 256
N_GROUPS = 8
GROUP_SIZE = N_EXPERTS // N_GROUPS
TOPK_GROUPS = 4
TOP_K = 8
ROUTE_SCALE = 2.5
NORM_EPS = 1e-6

LANES = 128
SUBLANES = 8
ROW_TILE = 8
VMEM_LIMIT = 48 * 1024 * 1024

EXPERT_BLOCK = 128
EXPERT_IN_SLOTS = 4
EXPERT_OUT_SLOTS = 3
EXPERT_DMA_SPLIT = 4
ROUTE_TILE = 128
ATTN_ROW_CHUNK = 256
NEG_INF = float("-inf")


def _bdot(a, b):
    return jnp.dot(a.astype(BF16), b.astype(BF16), preferred_element_type=F32)


def _bdot_nt(a, b):
    return lax.dot_general(a.astype(BF16), b.astype(BF16), (((1,), (1,)), ((), ())), preferred_element_type=F32)


def _bdot_tn(a, b):
    return lax.dot_general(a.astype(BF16), b.astype(BF16), (((0,), (0,)), ((), ())), preferred_element_type=F32)


def _sigmoid(x):
    return 1.0 / (1.0 + jnp.exp(-x))


def _silu(x):
    return x * _sigmoid(x)


def _params(*sem, vmem=VMEM_LIMIT):
    return pltpu.CompilerParams(dimension_semantics=sem, vmem_limit_bytes=vmem)


def _ada_body(c_ref, w_ref, b_ref, o_ref):
    o_ref[...] = _bdot(_silu(c_ref[...]), w_ref[...]) + b_ref[...]


def _ada(c, w, b):
    rows, d = c.shape
    n = w.shape[1]
    tn = 512
    return pl.pallas_call(
        _ada_body,
        grid=(n // tn,),
        in_specs=[pl.BlockSpec((rows, d), lambda j: (0, 0)),
                  pl.BlockSpec((d, tn), lambda j: (0, j)),
                  pl.BlockSpec((1, tn), lambda j: (0, j))],
        out_specs=pl.BlockSpec((rows, tn), lambda j: (0, j)),
        out_shape=jax.ShapeDtypeStruct((rows, n), F32),
        compiler_params=_params("parallel"),
        name="ada",
    )(c, w, b)


def _mod_spec(mod, tm, tiles_per_seq):
    if mod.ndim == 3:
        return pl.BlockSpec((None, 1, mod.shape[-1]), lambda i: (i // tiles_per_seq, 0, 0))
    return pl.BlockSpec((tm, mod.shape[-1]), lambda i: (i, 0))


def _full_spec(a):
    nd = a.ndim
    return pl.BlockSpec(a.shape, lambda *_: (0,) * nd)


def _rms(x, g):
    return x * lax.rsqrt(jnp.mean(x * x, axis=-1, keepdims=True) + NORM_EPS) * g


def _inproj_body(x_ref, g_ref, sh_ref, sc_ref, wm_ref, wl_ref, wq_ref, wk_ref, wv_ref, qg_ref, kg_ref, seg_ref,
                 pm_ref, pl_ref, *out_refs, long_seq):
    tm = x_ref.shape[0]
    h = (_rms(x_ref[...], g_ref[...]) * (1.0 + sc_ref[...]) + sh_ref[...]).astype(BF16)
    pm_ref[...] = jnp.dot(h, wm_ref[...], preferred_element_type=F32)
    pl_ref[...] = jnp.dot(h, wl_ref[...], preferred_element_type=F32)
    seg = seg_ref[...]

    def head_norm(z, gain):
        ms = _bdot(z * z, seg) * (1.0 / DIFF_DK)
        return z * lax.rsqrt(ms + NORM_EPS) * gain

    q = head_norm(jnp.dot(h, wq_ref[...], preferred_element_type=F32), qg_ref[...])
    k = head_norm(jnp.dot(h, wk_ref[...], preferred_element_type=F32), kg_ref[...])
    v = jnp.dot(h, wv_ref[...], preferred_element_type=F32)
    if long_seq:
        kt_ref, v4_ref, qb_ref, vb_ref = out_refs
        kt_ref[...] = k.T
        for hd in range(DIFF_HEADS):
            v4_ref[pl.ds(hd, tm, stride=DIFF_HEADS), :] = v[:, hd * DIFF_DV:(hd + 1) * DIFF_DV]
        qb_ref[...] = (q * DIFF_SCALE).astype(BF16)
        vb_ref[...] = v.astype(BF16)
    else:
        q_ref, k_ref, v_ref = out_refs
        q_ref[...] = q
        k_ref[...] = k
        v_ref[...] = v


def _inproj(x, g, shift, scale, wts, tm, tiles_per_seq, long_seq):
    t, d = x.shape
    wm, wl, wq, wk, wv, qg, kg, seg = wts
    row = lambda n: pl.BlockSpec((tm, n), lambda i: (i, 0))
    out_specs = [row(3 * RWKV_WIDTH), row(LORA_PAD)]
    out_shape = [jax.ShapeDtypeStruct((t, 3 * RWKV_WIDTH), F32), jax.ShapeDtypeStruct((t, LORA_PAD), F32)]
    if long_seq:
        n_seq = t // (tm * tiles_per_seq)
        out_specs += [pl.BlockSpec((DIFF_QK_WIDTH, tm), lambda i: (i // tiles_per_seq, i % tiles_per_seq)),
                      pl.BlockSpec((tm * DIFF_HEADS, DIFF_DV), lambda i: (i, 0)),
                      row(DIFF_QK_WIDTH), row(DIFF_V_WIDTH)]
        out_shape += [jax.ShapeDtypeStruct((n_seq * DIFF_QK_WIDTH, tm * tiles_per_seq), F32),
                      jax.ShapeDtypeStruct((t * DIFF_HEADS, DIFF_DV), F32),
                      jax.ShapeDtypeStruct((t, DIFF_QK_WIDTH), BF16), jax.ShapeDtypeStruct((t, DIFF_V_WIDTH), BF16)]
    else:
        out_specs += [row(DIFF_QK_WIDTH), row(DIFF_QK_WIDTH), row(DIFF_V_WIDTH)]
        out_shape += [jax.ShapeDtypeStruct((t, n), F32) for n in (DIFF_QK_WIDTH, DIFF_QK_WIDTH, DIFF_V_WIDTH)]
    return pl.pallas_call(
        functools.partial(_inproj_body, long_seq=long_seq),
        grid=(t // tm,),
        in_specs=[row(d), _full_spec(g), _mod_spec(shift, tm, tiles_per_seq), _mod_spec(scale, tm, tiles_per_seq)]
                 + [_full_spec(a) for a in (wm, wl, wq, wk, wv, qg, kg, seg)],
        out_specs=out_specs,
        out_shape=out_shape,
        compiler_params=_params("parallel"),
        name="inproj",
    )(x, g, shift, scale, wm, wl, wq, wk, wv, qg, kg, seg)


def _split3(x):
    hi = x.astype(BF16)
    r1 = x - hi.astype(F32)
    mid = r1.astype(BF16)
    lo = (r1 - mid.astype(F32)).astype(BF16)
    return hi, mid, lo


def _rwkv_body(pm_ref, pl_ref, pm8_ref, pl8_ref, sm_ref, sl_ref, s0_ref,
               mum_ref, mul_ref, w0_ref, wupw_ref, a0_ref, wupa_ref, wupg_ref, kk_ref, ka_ref, rk_ref,
               lg_ref, lb_ref, seg_ref,
               y_ref, sout_ref, state_ref, *, chunk):
    c = pl.program_id(1)
    nc = pl.num_programs(1)
    C = chunk
    W = RWKV_WIDTH
    N = RWKV_HEAD_DIM

    @pl.when(c == 0)
    def _():
        state_ref[...] = s0_ref[...]

    first = c == 0
    prev_m = jnp.where(first, sm_ref[...], pm8_ref[SUBLANES - 1:SUBLANES, :])
    prev_l = jnp.where(first, sl_ref[...], pl8_ref[SUBLANES - 1:SUBLANES, :])

    def shifted(cur, prev_row):
        rows = lax.broadcasted_iota(I32, cur.shape, 0)
        return jnp.where(rows == 0, prev_row, pltpu.roll(cur, 1, 0))

    pm = pm_ref[...]
    plo = pl_ref[...]
    xm = pm + (shifted(pm, prev_m) - pm) * mum_ref[...]
    xl = plo + (shifted(plo, prev_l) - plo) * mul_ref[...]
    r = xm[:, 0:W]
    k = xm[:, W:2 * W]
    v = xm[:, 2 * W:3 * W]
    seg = seg_ref[...]

    z = -(w0_ref[...] + _bdot(jnp.tanh(xl), wupw_ref[...]))
    softplus = jnp.maximum(z, 0.0) + jnp.log(1.0 + jnp.exp(-jnp.abs(z)))
    w = -softplus - 0.5
    a = _sigmoid(a0_ref[...] + _bdot(xl, wupa_ref[...]))
    g = _bdot(_sigmoid(xl), wupg_ref[...])
    kk = k * kk_ref[...]
    kk = kk / jnp.maximum(jnp.sqrt(_bdot(kk * kk, seg)), 1e-12)
    k = k * (1.0 + (a - 1.0) * ka_ref[...])
    logdec = -jnp.exp(w)

    ti = lax.broadcasted_iota(I32, (C, C), 0)
    tj = lax.broadcasted_iota(I32, (C, C), 1)
    lower = (ti >= tj).astype(BF16)
    cum = sum(jnp.dot(lower, part, preferred_element_type=F32) for part in _split3(logdec))
    cum_end = cum[C - 1:C, :]
    a_t = -kk * jnp.exp(cum - logdec)
    r_t = r * jnp.exp(cum)
    inv = jnp.exp(-cum)
    b_t = kk * a * inv
    k_t = k * inv
    to_end = jnp.exp(cum_end - cum)
    b_e = kk * a * to_end
    k_e = k * to_end
    g_end = jnp.exp(cum_end)

    eye = (ti == tj).astype(F32)
    ri = lax.broadcasted_iota(I32, (2 * C, 2 * C), 0)
    ci = lax.broadcasted_iota(I32, (2 * C, 2 * C), 1)
    tr = jnp.where(ri >= C, ri - C, ri)
    tc = jnp.where(ci >= C, ci - C, ci)
    mask = jnp.logical_or(tr > tc, jnp.logical_and(ri >= C, tr == tc))
    zeros_cn = jnp.zeros((C, N), F32)
    levels = int(math.log2(C))
    heads = range(RWKV_HEADS)
    sls = [slice(h * N, (h + 1) * N) for h in heads]
    ah = [a_t[:, s] for s in sls]
    rh = [r_t[:, s] for s in sls]
    vh = [v[:, s] for s in sls]
    m_all = [jnp.where(mask, _bdot_nt(jnp.concatenate([ah[h], rh[h]], axis=0),
                                      jnp.concatenate([b_t[:, sls[h]], k_t[:, sls[h]]], axis=0)), 0.0)
             for h in heads]
    m_top = [m[0:C, :] for m in m_all]
    m_bot = [m[C:2 * C, :] for m in m_all]
    akv = [_bdot(m_top[h], jnp.concatenate([zeros_cn, vh[h]], axis=0)) for h in heads]
    power = [m[:, 0:C] for m in m_top]
    t_inv = [eye + p for p in power]
    for _ in range(levels - 1):
        power = [_bdot(p, p) for p in power]
        t_inv = [t + _bdot(t, p) for t, p in zip(t_inv, power)]
    w_mat = [_bdot(t_inv[h], ah[h]) for h in heads]
    u0 = [_bdot(t_inv[h], akv[h]) for h in heads]
    s0 = [state_ref[h] for h in heads]
    x = [_bdot_nt(jnp.concatenate([w_mat[h], rh[h]], axis=0), s0[h]) for h in heads]
    uv = [jnp.concatenate([x[h][0:C, :] + u0[h], vh[h]], axis=0) for h in heads]
    ys = [x[h][C:2 * C, :] + _bdot(m_bot[h], uv[h]) for h in heads]
    s_new = [s0[h] * g_end[:, sls[h]]
             + _bdot_tn(uv[h], jnp.concatenate([b_e[:, sls[h]], k_e[:, sls[h]]], axis=0)) for h in heads]
    for h in heads:
        state_ref[h] = s_new[h]
    y = jnp.concatenate(ys, axis=1)

    mean = _bdot(y, seg) * (1.0 / N)
    yc = y - mean
    var = _bdot(yc * yc, seg) * (1.0 / N)
    yn = yc * lax.rsqrt(var + GN_EPS) * lg_ref[...] + lb_ref[...]
    bonus = _bdot(r * k * rk_ref[...], seg) * v
    y_ref[...] = ((yn + bonus) * g).astype(y_ref.dtype)

    @pl.when(c == nc - 1)
    def _():
        sout_ref[...] = state_ref[...]


def _rwkv(pm, plo, shift_m, shift_l, s0, wts, seq_len, chunk):
    t_total = pm.shape[0]
    nb = t_total // seq_len
    ncnk = seq_len // chunk
    c8 = chunk // SUBLANES
    prev8 = lambda b, c: (jnp.maximum(b * (seq_len // SUBLANES) + c * c8 - 1, 0), 0)
    cur = lambda b, c: (b * ncnk + c, 0)
    per_seq3 = lambda n: pl.BlockSpec((None, 1, n), lambda b, c: (b, 0, 0))
    state_spec = pl.BlockSpec((None, RWKV_HEADS, RWKV_HEAD_DIM, RWKV_HEAD_DIM), lambda b, c: (b, 0, 0, 0))
    return pl.pallas_call(
        functools.partial(_rwkv_body, chunk=chunk),
        grid=(nb, ncnk),
        in_specs=[pl.BlockSpec((chunk, 3 * RWKV_WIDTH), cur), pl.BlockSpec((chunk, LORA_PAD), cur),
                  pl.BlockSpec((SUBLANES, 3 * RWKV_WIDTH), prev8), pl.BlockSpec((SUBLANES, LORA_PAD), prev8),
                  per_seq3(3 * RWKV_WIDTH), per_seq3(LORA_PAD), state_spec]
                 + [_full_spec(a) for a in wts],
        out_specs=[pl.BlockSpec((chunk, RWKV_WIDTH), cur), state_spec],
        out_shape=[jax.ShapeDtypeStruct((t_total, RWKV_WIDTH), BF16 if chunk % 16 == 0 else F32),
                   jax.ShapeDtypeStruct(s0.shape, F32)],
        scratch_shapes=[pltpu.VMEM((RWKV_HEADS, RWKV_HEAD_DIM, RWKV_HEAD_DIM), F32)],
        compiler_params=_params("parallel", "arbitrary"),
        name="rwkv",
    )(pm, plo, pm, plo, shift_m, shift_l, s0, *wts)


def _lambda(lam_ref, lam_init):
    lv = lam_ref[...]
    return (jnp.exp(jnp.sum(lv[0:1, :] * lv[1:2, :], axis=-1, keepdims=True))
            - jnp.exp(jnp.sum(lv[2:3, :] * lv[3:4, :], axis=-1, keepdims=True)) + lam_init)


def _subln(o, g, lam_init):
    return o * lax.rsqrt(jnp.mean(o * o, axis=-1, keepdims=True) + NORM_EPS) * g * (1.0 - lam_init)


def _attn_prompt_body(qi_ref, ki_ref, q_ref, kt_ref, v_ref, lam_ref, sg_ref, o_ref,
                      qm_ref, m_ref, acc_ref, *, lam_init):
    p = pl.program_id(2)
    qi = qi_ref[p]
    ki = ki_ref[p]
    tq = q_ref.shape[0]
    tk = kt_ref.shape[1]
    dv = v_ref.shape[1]

    @pl.when(ki == 0)
    def _():
        q = q_ref[...]
        lane = lax.broadcasted_iota(I32, q.shape, 1)
        qm_ref[0] = jnp.where(lane < DIFF_DK, q, jnp.zeros_like(q))
        qm_ref[1] = jnp.where(lane >= DIFF_DK, q, jnp.zeros_like(q))
        m_ref[...] = jnp.full(m_ref.shape, NEG_INF, F32)
        acc_ref[...] = jnp.zeros(acc_ref.shape, F32)

    def accumulate(masked):
        kt = kt_ref[...].astype(BF16)
        v1 = jnp.concatenate([v_ref[...], jnp.ones((tk, dv), BF16)], axis=1)
        for c in range(2):
            for r0 in range(0, tq, ATTN_ROW_CHUNK):
                rows = slice(r0, r0 + ATTN_ROW_CHUNK)
                ncol = min(tk, r0 + ATTN_ROW_CHUNK) if masked else tk
                s = jnp.dot(qm_ref[c, rows, :], kt[:, 0:ncol], preferred_element_type=F32)
                if masked:
                    row = lax.broadcasted_iota(I32, s.shape, 0) + r0
                    col = lax.broadcasted_iota(I32, s.shape, 1)
                    s = jnp.where(col <= row, s, NEG_INF)
                m_old = m_ref[c, rows, :]
                m_new = jnp.maximum(m_old, jnp.max(s, axis=-1, keepdims=True))
                alpha = jnp.exp(m_old - m_new)
                pr = jnp.exp(s - jnp.concatenate([m_new] * (ncol // LANES), axis=1))
                acc_ref[c, rows, :] = (jnp.concatenate([alpha] * (2 * dv // LANES), axis=1) * acc_ref[c, rows, :]
                                       + jnp.dot(pr.astype(BF16), v1[0:ncol, :], preferred_element_type=F32))
                m_ref[c, rows, :] = m_new

    @pl.when(ki < qi)
    def _():
        accumulate(False)

    @pl.when(ki == qi)
    def _():
        accumulate(True)
        lam = _lambda(lam_ref, lam_init)
        a0 = acc_ref[0]
        a1 = acc_ref[1]
        o = a0[:, 0:dv] / a0[:, dv:2 * dv] - lam * (a1[:, 0:dv] / a1[:, dv:2 * dv])
        o_ref[...] = _subln(o, sg_ref[...], lam_init).astype(o_ref.dtype)


def _attn_prompt(qb, kt, vb, lam_vecs, subln_g, seq_len, lam_init):
    t_total = qb.shape[0]
    nb = t_total // seq_len
    tq = min(512, seq_len)
    nq = seq_len // tq
    pairs = [(i, j) for i in range(nq) for j in range(i + 1)]
    qi_tab = jnp.asarray([a for a, _ in pairs], I32)
    ki_tab = jnp.asarray([b for _, b in pairs], I32)
    qmap = lambda b, h, p, qi, ki: (b * nq + qi[p], h)
    vmap = lambda b, h, p, qi, ki: (b * nq + ki[p], h)
    ktmap = lambda b, h, p, qi, ki: (b * DIFF_HEADS + h, ki[p])
    grid_spec = pltpu.PrefetchScalarGridSpec(
        num_scalar_prefetch=2,
        grid=(nb, DIFF_HEADS, len(pairs)),
        in_specs=[pl.BlockSpec((tq, DIFF_DV), qmap), pl.BlockSpec((2 * DIFF_DK, tq), ktmap),
                  pl.BlockSpec((tq, DIFF_DV), vmap),
                  pl.BlockSpec(lam_vecs.shape, lambda *_: (0, 0)), pl.BlockSpec(subln_g.shape, lambda *_: (0, 0))],
        out_specs=pl.BlockSpec((tq, DIFF_DV), qmap),
        scratch_shapes=[pltpu.VMEM((2, tq, DIFF_DV), BF16), pltpu.VMEM((2, tq, LANES), F32),
                        pltpu.VMEM((2, tq, 2 * DIFF_DV), F32)],
    )
    return pl.pallas_call(
        functools.partial(_attn_prompt_body, lam_init=lam_init),
        grid_spec=grid_spec,
        out_shape=jax.ShapeDtypeStruct((t_total, DIFF_V_WIDTH), BF16),
        compiler_params=_params("parallel", "parallel", "arbitrary"),
        name="attn_prompt",
    )(qi_tab, ki_tab, qb, kt, vb, lam_vecs, subln_g)


def _attn_sample_body(pt_ref, q_ref, k_ref, v_ref, lam_ref, sg_ref, *rest, n_pages, lam_init):
    kp_refs = rest[:n_pages]
    vp_refs = rest[n_pages:2 * n_pages]
    o_ref = rest[2 * n_pages]
    s_new = q_ref.shape[0]
    n_maps = DIFF_HEADS * 2
    page = kp_refs[0].shape[1]
    nrow = n_maps * s_new

    def value_page(vr):
        return jnp.concatenate([vr[pl.ds(h, page, stride=DIFF_HEADS), :] for h in range(DIFF_HEADS)], axis=1)

    qt = jnp.concatenate([q_ref[...] * DIFF_SCALE] * (DIFF_HEADS * 2), axis=0)
    row = lax.broadcasted_iota(I32, qt.shape, 0)
    col = lax.broadcasted_iota(I32, qt.shape, 1)
    qbd = jnp.where(col // DIFF_DK == row // s_new, qt, 0.0).astype(BF16)
    pad = jnp.zeros((page - s_new, k_ref.shape[1]), F32)
    k_new = jnp.concatenate([k_ref[...], pad], axis=0)
    v_new = jnp.concatenate([v_ref[...], pad], axis=0)
    scores = [_bdot(qbd, kr[...]) for kr in kp_refs]
    s_n = _bdot_nt(qbd, k_new)
    rn = lax.broadcasted_iota(I32, s_n.shape, 0)
    cn = lax.broadcasted_iota(I32, s_n.shape, 1)
    scores.append(jnp.where(cn <= rn % s_new, s_n, NEG_INF))
    m = functools.reduce(jnp.maximum, [jnp.max(s, axis=-1, keepdims=True) for s in scores])
    values = [value_page(vr) for vr in vp_refs] + [v_new]
    l = jnp.zeros_like(m)
    acc = jnp.zeros((nrow, v_new.shape[1]), F32)
    for s, val in zip(scores, values):
        pr = jnp.exp(s - m)
        l = l + jnp.sum(pr, axis=-1, keepdims=True)
        acc = acc + _bdot(pr, val)
    full = acc / l
    lam = _lambda(lam_ref, lam_init)
    outs = []
    for h in range(DIFF_HEADS):
        r0 = h * 2 * s_new
        cols = slice(h * DIFF_DV, (h + 1) * DIFF_DV)
        o = full[r0:r0 + s_new, cols] - lam * full[r0 + s_new:r0 + 2 * s_new, cols]
        outs.append(_subln(o, sg_ref[...], lam_init))
    o_ref[...] = jnp.concatenate(outs, axis=1)


def _attn_sample(q, k, v, cache_k, cache_v, n_pool_pages, page_table, page_offset, lam_vecs, subln_g, s_new,
                 lam_init):
    nseq, n_pages = page_table.shape
    width = q.shape[1]
    k_rows = cache_k.shape[0] // n_pool_pages
    v_rows = cache_v.shape[0] // n_pool_pages
    new_spec = pl.BlockSpec((s_new, width), lambda b, pt: (b, 0))
    page_spec = lambda rows, lanes, j: pl.BlockSpec((rows, lanes), lambda b, pt: (pt[b, j] + page_offset, 0))
    grid_spec = pltpu.PrefetchScalarGridSpec(
        num_scalar_prefetch=1,
        grid=(nseq,),
        in_specs=[new_spec, new_spec, new_spec,
                  pl.BlockSpec(lam_vecs.shape, lambda *_: (0, 0)), pl.BlockSpec(subln_g.shape, lambda *_: (0, 0))]
                 + [page_spec(k_rows, cache_k.shape[1], j) for j in range(n_pages)]
                 + [page_spec(v_rows, DIFF_DV, j) for j in range(n_pages)],
        out_specs=new_spec,
    )
    return pl.pallas_call(
        functools.partial(_attn_sample_body, n_pages=n_pages, lam_init=lam_init),
        grid_spec=grid_spec,
        out_shape=jax.ShapeDtypeStruct((nseq * s_new, width), F32),
        compiler_params=_params("parallel"),
        name="attn_sample",
    )(page_table, q, k, v, lam_vecs, subln_g, *([cache_k] * n_pages), *([cache_v] * n_pages))


def _outproj_body(yr_ref, o_ref, x_ref, g1_ref, g2n_ref, sh_ref, sc_ref, g2_ref, woa_ref, wob_ref,
                  wsg_ref, wsu_ref, wsd_ref, wrh_ref, wrl_ref,
                  base_ref, h2t_ref, lg_ref):
    tm = x_ref.shape[0]
    mixed = (jnp.dot(yr_ref[...].astype(BF16), woa_ref[...], preferred_element_type=F32)
             + jnp.dot(o_ref[...].astype(BF16), wob_ref[...], preferred_element_type=F32))
    x1 = x_ref[...] + g1_ref[...] * mixed
    h2 = _rms(x1, g2n_ref[...]) * (1.0 + sc_ref[...]) + sh_ref[...]
    h2b = h2.astype(BF16)
    hidden = _silu(jnp.dot(h2b, wsg_ref[...], preferred_element_type=F32)) * jnp.dot(
        h2b, wsu_ref[...], preferred_element_type=F32)
    shared = jnp.dot(hidden.astype(BF16), wsd_ref[...], preferred_element_type=F32)
    base_ref[...] = x1 + g2_ref[...] * shared
    h2l = (h2 - h2b.astype(F32)).astype(BF16)
    nt = lambda a, b: lax.dot_general(a, b, (((1,), (1,)), ((), ())), preferred_element_type=F32)
    lg_ref[...] = nt(wrh_ref[...], h2b) + nt(wrl_ref[...], h2b) + nt(wrh_ref[...], h2l)
    for c in range(ROW_TILE):
        h2t_ref[pl.ds(c, tm, stride=ROW_TILE), :] = h2[:, c * LANES:(c + 1) * LANES]


def _outproj(yr, o, x, gate1, norm2_g, shift2, scale2, gate2, wts, tm, tiles_per_seq):
    t, d = x.shape
    row = lambda n: pl.BlockSpec((tm, n), lambda i: (i, 0))
    mod = lambda m: _mod_spec(m, tm, tiles_per_seq)
    return pl.pallas_call(
        _outproj_body,
        grid=(t // tm,),
        in_specs=[row(RWKV_WIDTH), row(DIFF_V_WIDTH), row(d), mod(gate1), _full_spec(norm2_g), mod(shift2),
                  mod(scale2), mod(gate2)] + [_full_spec(a) for a in wts],
        out_specs=[row(d), pl.BlockSpec((tm * ROW_TILE, LANES), lambda i: (i, 0)),
                   pl.BlockSpec((N_EXPERTS, tm), lambda i: (0, i))],
        out_shape=[jax.ShapeDtypeStruct((t, d), F32),
                   jax.ShapeDtypeStruct((t * ROW_TILE, LANES), F32),
                   jax.ShapeDtypeStruct((N_EXPERTS, t), F32)],
        compiler_params=_params("parallel"),
        name="outproj",
    )(yr, o, x, gate1, norm2_g, shift2, scale2, gate2, *wts)


def _router_body(lg_ref, bias_ref, idx_ref, gate_ref, pos_ref, cnt_ref, run_ref):
    i = pl.program_id(0)
    tm = lg_ref.shape[1]

    @pl.when(i == 0)
    def _():
        run_ref[...] = jnp.zeros(run_ref.shape, F32)

    scores = _sigmoid(lg_ref[...])
    biased = scores + bias_ref[...]
    erow = lax.broadcasted_iota(I32, (N_EXPERTS, tm), 0)
    grow = lax.broadcasted_iota(I32, (GROUP_SIZE, tm), 0)

    def first_argmax(x, rows, limit):
        mx = jnp.max(x, axis=0, keepdims=True)
        return mx, jnp.min(jnp.where(x == mx, rows, limit), axis=0, keepdims=True)

    group_scores = []
    for gidx in range(N_GROUPS):
        xg = biased[gidx * GROUP_SIZE:(gidx + 1) * GROUP_SIZE, :]
        m1, i1 = first_argmax(xg, grow, GROUP_SIZE)
        m2 = jnp.max(jnp.where(grow == i1, NEG_INF, xg), axis=0, keepdims=True)
        group_scores.append(m1 + m2)
    gs = jnp.concatenate(group_scores, axis=0)
    g8 = lax.broadcasted_iota(I32, (N_GROUPS, tm), 0)
    chosen = jnp.zeros((N_GROUPS, tm), I32)
    for _ in range(TOPK_GROUPS):
        _, gi = first_argmax(gs, g8, N_GROUPS)
        hit = g8 == gi
        chosen = jnp.where(hit, 1, chosen)
        gs = jnp.where(hit, NEG_INF, gs)
    cand = jnp.concatenate(
        [jnp.where(chosen[gidx:gidx + 1, :] > 0, biased[gidx * GROUP_SIZE:(gidx + 1) * GROUP_SIZE, :], NEG_INF)
         for gidx in range(N_GROUPS)], axis=0)

    idxs, raws = [], []
    onehot = jnp.zeros((N_EXPERTS, tm), F32)
    for _ in range(TOP_K):
        _, ei = first_argmax(cand, erow, N_EXPERTS)
        hit = erow == ei
        idxs.append(ei)
        raws.append(jnp.sum(jnp.where(hit, scores, 0.0), axis=0, keepdims=True))
        onehot = onehot + hit.astype(F32)
        cand = jnp.where(hit, NEG_INF, cand)
    raw = jnp.concatenate(raws, axis=0)
    gate_ref[...] = raw / jnp.sum(raw, axis=0, keepdims=True) * ROUTE_SCALE
    idx_ref[...] = jnp.concatenate(idxs, axis=0)

    ti = lax.broadcasted_iota(I32, (tm, tm), 0)
    tj = lax.broadcasted_iota(I32, (tm, tm), 1)
    oh = onehot.astype(BF16)
    before = jnp.dot(oh, (ti < tj).astype(BF16), preferred_element_type=F32) + run_ref[...]
    run_ref[...] = run_ref[...] + jnp.dot(oh, jnp.ones((tm, tm), BF16), preferred_element_type=F32)
    pos_ref[...] = jnp.concatenate(
        [jnp.sum(jnp.where(erow == ei, before, 0.0), axis=0, keepdims=True) for ei in idxs], axis=0).astype(I32)
    cnt_ref[...] = run_ref[...]


def _router(logits_t, bias_col):
    t = logits_t.shape[1]
    tm = ROUTE_TILE
    tok = pl.BlockSpec((TOP_K, tm), lambda i: (0, i))
    return pl.pallas_call(
        _router_body,
        grid=(t // tm,),
        in_specs=[pl.BlockSpec((N_EXPERTS, tm), lambda i: (0, i)), _full_spec(bias_col)],
        out_specs=[tok, tok, tok, pl.BlockSpec((N_EXPERTS, tm), lambda i: (0, 0))],
        out_shape=[jax.ShapeDtypeStruct((TOP_K, t), I32), jax.ShapeDtypeStruct((TOP_K, t), F32),
                   jax.ShapeDtypeStruct((TOP_K, t), I32), jax.ShapeDtypeStruct((N_EXPERTS, tm), F32)],
        scratch_shapes=[pltpu.VMEM((N_EXPERTS, tm), F32)],
        compiler_params=_params("arbitrary"),
        name="router",
    )(logits_t, bias_col)


def _assign_body(idx_ref, pos_ref, start_ref, dest_ref):
    tm = idx_ref.shape[1]
    erow = lax.broadcasted_iota(I32, (N_EXPERTS, tm), 0)
    start = start_ref[...]
    idx = idx_ref[...]
    first = jnp.concatenate(
        [jnp.sum(jnp.where(erow == idx[j:j + 1, :], start, 0.0), axis=0, keepdims=True) for j in range(TOP_K)],
        axis=0)
    dest_ref[...] = first.astype(I32) + pos_ref[...]


def _assign(idx, pos, start_col):
    t = idx.shape[1]
    tm = ROUTE_TILE
    tok = pl.BlockSpec((TOP_K, tm), lambda i: (0, i))
    return pl.pallas_call(
        _assign_body,
        grid=(t // tm,),
        in_specs=[tok, tok, _full_spec(start_col)],
        out_specs=tok,
        out_shape=jax.ShapeDtypeStruct((TOP_K, t), I32),
        compiler_params=_params("parallel"),
        name="assign",
    )(idx, pos, start_col)


def _row_copy(src, src_row, dst, dst_row, sem):
    return pltpu.make_async_copy(src.at[pl.ds(pl.multiple_of(src_row * ROW_TILE, ROW_TILE), ROW_TILE), :],
                                 dst.at[pl.ds(pl.multiple_of(dst_row * ROW_TILE, ROW_TILE), ROW_TILE), :], sem)


def _dispatch_body(zflag_ref, h_ref, dest_ref, xs_ref, zero_ref, zsem, rsem):
    i = pl.program_id(0)
    tm = dest_ref.shape[1]
    blk_rows = EXPERT_BLOCK * ROW_TILE
    n_blocks = xs_ref.shape[0] // blk_rows

    def zero_copy(b):
        start = pl.multiple_of(b * blk_rows, blk_rows)
        return pltpu.make_async_copy(zero_ref, xs_ref.at[pl.ds(start, blk_rows), :], zsem)

    @pl.when(i == 0)
    def _():
        zero_ref[...] = jnp.zeros(zero_ref.shape, F32)

        def issue(b, carry):
            @pl.when(zflag_ref[b] > 0)
            def _():
                zero_copy(b).start()
            return carry

        def drain(b, carry):
            @pl.when(zflag_ref[b] > 0)
            def _():
                zero_copy(b).wait()
            return carry

        lax.fori_loop(0, n_blocks, issue, 0)
        lax.fori_loop(0, n_blocks, drain, 0)

    def issue_rows(t, carry):
        for j in range(TOP_K):
            _row_copy(h_ref, t, xs_ref, dest_ref[j, t], rsem).start()
        return carry

    lax.fori_loop(0, tm, issue_rows, 0)
    for _ in range(TOP_K):
        pltpu.make_async_copy(h_ref, xs_ref.at[pl.ds(0, tm * ROW_TILE), :], rsem).wait()


def _dispatch(h2t, dest, zero_flags, n_rows):
    t = dest.shape[1]
    tm = ROUTE_TILE
    grid_spec = pltpu.PrefetchScalarGridSpec(
        num_scalar_prefetch=1,
        grid=(t // tm,),
        in_specs=[pl.BlockSpec((tm * ROW_TILE, LANES), lambda i, *_: (i, 0)),
                  pl.BlockSpec((TOP_K, tm), lambda i, *_: (0, i), memory_space=pltpu.SMEM)],
        out_specs=pl.BlockSpec(memory_space=pl.ANY),
        scratch_shapes=[pltpu.VMEM((EXPERT_BLOCK * ROW_TILE, LANES), F32),
                        pltpu.SemaphoreType.DMA, pltpu.SemaphoreType.DMA],
    )
    return pl.pallas_call(
        _dispatch_body,
        grid_spec=grid_spec,
        out_shape=jax.ShapeDtypeStruct((n_rows * ROW_TILE, LANES), F32),
        compiler_params=_params("arbitrary"),
        name="dispatch",
    )(zero_flags, h2t, dest)


def _expert_body(first_ref, count_ref, nu_ref, xs_ref, wg_ref, wu_ref, wd_ref, y_ref,
                 xin_ref, yout_ref, wgb_ref, wub_ref, wdb_ref, in_sem, out_sem):
    e = pl.program_id(0)
    rows = EXPERT_BLOCK
    blk_rows = EXPERT_BLOCK * ROW_TILE
    nb = count_ref[e]
    b0 = first_ref[e]

    n_used = nu_ref[0]
    n_in = xin_ref.shape[0]
    n_out = yout_ref.shape[0]
    ahead = n_in - 1
    part = blk_rows // EXPERT_DMA_SPLIT

    def hbm_rows(g, k=0, n=blk_rows):
        return pl.ds(pl.multiple_of(g * blk_rows + k * part, part), n)

    def start_in(g):
        slot = g % n_in
        for k in range(EXPERT_DMA_SPLIT):
            pltpu.make_async_copy(xs_ref.at[hbm_rows(g, k, part), :], xin_ref.at[slot, pl.ds(k * part, part), :],
                                  in_sem.at[slot]).start()

    def wait_in(g):
        slot = g % n_in
        pltpu.make_async_copy(xs_ref.at[hbm_rows(g), :], xin_ref.at[slot], in_sem.at[slot]).wait()

    def start_out(g):
        slot = g % n_out
        for k in range(EXPERT_DMA_SPLIT):
            pltpu.make_async_copy(yout_ref.at[slot, pl.ds(k * part, part), :], y_ref.at[hbm_rows(g, k, part), :],
                                  out_sem.at[slot]).start()

    def wait_out(g):
        slot = g % n_out
        pltpu.make_async_copy(yout_ref.at[slot], y_ref.at[hbm_rows(g), :], out_sem.at[slot]).wait()

    @pl.when(e == 0)
    def _():
        for g in range(ahead):
            @pl.when(g < n_used)
            def _():
                start_in(g)

    @pl.when(nb > 0)
    def _():
        wgb_ref[...] = wg_ref[...].astype(BF16)
        wub_ref[...] = wu_ref[...].astype(BF16)
        wdb_ref[...] = wd_ref[...].astype(BF16)

        def step(b, carry):
            g = b0 + b
            wait_in(g)

            @pl.when(g + ahead < n_used)
            def _():
                start_in(g + ahead)

            @pl.when(g >= n_out)
            def _():
                wait_out(g - n_out)

            islot = g % n_in
            oslot = g % n_out
            x = jnp.concatenate([xin_ref[islot, pl.ds(c, rows, stride=ROW_TILE), :] for c in range(ROW_TILE)],
                                axis=1).astype(BF16)
            hidden = _silu(jnp.dot(x, wgb_ref[...], preferred_element_type=F32)) * jnp.dot(
                x, wub_ref[...], preferred_element_type=F32)
            y = jnp.dot(hidden.astype(BF16), wdb_ref[...], preferred_element_type=F32)
            for c in range(ROW_TILE):
                yout_ref[oslot, pl.ds(c, rows, stride=ROW_TILE), :] = y[:, c * LANES:(c + 1) * LANES]
            start_out(g)
            return carry

        lax.fori_loop(0, nb, step, 0)

    @pl.when(e == pl.num_programs(0) - 1)
    def _():
        def drain(g, carry):
            wait_out(g)
            return carry

        lax.fori_loop(jnp.maximum(n_used - n_out, 0), n_used, drain, 0)
        n_blocks = y_ref.shape[0] // blk_rows
        yout_ref[0] = jnp.zeros(yout_ref.shape[1:], F32)

        def tail(b, carry):
            cp = pltpu.make_async_copy(
                yout_ref.at[0], y_ref.at[pl.ds(pl.multiple_of(b * blk_rows, blk_rows), blk_rows), :], out_sem.at[0])
            cp.start()
            cp.wait()
            return carry

        lax.fori_loop(nu_ref[0], n_blocks, tail, 0)


def _experts(xs, first_block, block_count, n_used, we_gate, we_up, we_down):
    d, f = we_gate.shape[1], we_gate.shape[2]
    blk = (EXPERT_BLOCK * ROW_TILE, LANES)
    wspec = lambda a, b: pl.BlockSpec((None, a, b), lambda e, *_: (e, 0, 0))
    grid_spec = pltpu.PrefetchScalarGridSpec(
        num_scalar_prefetch=3,
        grid=(N_EXPERTS,),
        in_specs=[pl.BlockSpec(memory_space=pl.ANY), wspec(d, f), wspec(d, f), wspec(f, d)],
        out_specs=pl.BlockSpec(memory_space=pl.ANY),
        scratch_shapes=[pltpu.VMEM((EXPERT_IN_SLOTS,) + blk, F32), pltpu.VMEM((EXPERT_OUT_SLOTS,) + blk, F32),
                        pltpu.VMEM((d, f), BF16), pltpu.VMEM((d, f), BF16), pltpu.VMEM((f, d), BF16),
                        pltpu.SemaphoreType.DMA((EXPERT_IN_SLOTS,)), pltpu.SemaphoreType.DMA((EXPERT_OUT_SLOTS,))],
    )
    return pl.pallas_call(
        _expert_body,
        grid_spec=grid_spec,
        out_shape=jax.ShapeDtypeStruct(xs.shape, F32),
        compiler_params=_params("arbitrary"),
        name="experts",
    )(first_block, block_count, n_used, xs, we_gate, we_up, we_down)


def _combine_body(base_ref, g2_ref, gate_ref, dest_ref, y_ref, out_ref, buf_ref, sem):
    tm = base_ref.shape[0]
    slot_rows = tm * ROW_TILE

    def issue(t, carry):
        for j in range(TOP_K):
            _row_copy(y_ref, dest_ref[j, t], buf_ref, j * tm + t, sem).start()
        return carry

    lax.fori_loop(0, tm, issue, 0)
    pltpu.make_async_copy(y_ref.at[pl.ds(0, TOP_K * slot_rows), :], buf_ref, sem).wait()
    gates = gate_ref[...]
    gcols = [jnp.broadcast_to(gates[:, j:j + 1], (tm, LANES)) for j in range(TOP_K)]
    for c in range(ROW_TILE):
        cols = slice(c * LANES, (c + 1) * LANES)
        routed = sum(gcols[j] * buf_ref[pl.ds(j * slot_rows + c, tm, stride=ROW_TILE), :] for j in range(TOP_K))
        out_ref[:, cols] = base_ref[:, cols] + g2_ref[:, cols] * routed


def _combine(base, gate2, gates, dest, y_rows, tok_offset, tiles_per_seq):
    t, d = base.shape
    tm = ROUTE_TILE
    off = tok_offset // tm
    return pl.pallas_call(
        _combine_body,
        grid=(t // tm,),
        in_specs=[pl.BlockSpec((tm, d), lambda i: (i, 0)), _mod_spec(gate2, tm, tiles_per_seq),
                  pl.BlockSpec((tm, TOP_K), lambda i: (i + off, 0)),
                  pl.BlockSpec((TOP_K, tm), lambda i: (0, i + off), memory_space=pltpu.SMEM),
                  pl.BlockSpec(memory_space=pl.ANY)],
        out_specs=pl.BlockSpec((tm, d), lambda i: (i, 0)),
        out_shape=jax.ShapeDtypeStruct((t, d), F32),
        scratch_shapes=[pltpu.VMEM((TOP_K * tm * ROW_TILE, LANES), F32), pltpu.SemaphoreType.DMA],
        compiler_params=_params("arbitrary"),
        name="combine",
    )(base, gate2, gates, dest, y_rows)


def _segment_ones(width, seg):
    ids = jnp.arange(width) // seg
    return (ids[:, None] == ids[None, :]).astype(BF16)


def _pad_rows(w, start, total):
    return jnp.zeros((total, w.shape[1]), w.dtype).at[start:start + w.shape[0]].set(w)


def kernel(x_prompt, x_sample, cache_k, cache_v, state_wkv, state_shift, page_table, c_prompt, c_sample, w_ada, b_ada, norm1_g, norm2_g, w_in, mu_shift, w0, w_lora_up, a0, a_lora_up, g_lora_up, k_k, k_a, r_k, lnx_g, lnx_b, qn_g, kn_g, lam_q1, lam_k1, lam_q2, lam_k2, subln_g, w_out, w_router, router_bias, we_gate, we_up, we_down, ws_gate, ws_up, ws_down):
    depth = w_in.shape[0]
    nb, seq, d = x_prompt.shape
    db, dseq, _ = x_sample.shape
    n_pool, page = cache_k.shape[1], cache_k.shape[2]
    tp, ts = nb * seq, db * dseq
    w3 = 3 * RWKV_WIDTH
    row = lambda a: a.reshape(1, -1)
    seg512 = _segment_ones(RWKV_WIDTH, RWKV_HEAD_DIM)
    tm_p = min(256, seq)
    tm_s = min(256, ts)
    chunk_p = min(64, seq)

    xp = x_prompt.reshape(tp, d)
    xs = x_sample.reshape(ts, d)
    cache_k2 = jnp.transpose(cache_k, (0, 1, 3, 4, 5, 2)).reshape(depth * n_pool * DIFF_QK_WIDTH, page)
    cache_v2 = cache_v.reshape(depth * n_pool * page * DIFF_HEADS, DIFF_DV)
    n_cond = nb + db
    cond = jnp.concatenate([c_prompt, c_sample], axis=0)
    cond = jnp.pad(cond, ((0, -n_cond % SUBLANES), (0, 0)))

    outs = {name: [] for name in ("kp", "vp", "wp", "sp", "ks", "vs", "ws", "ss")}
    for layer in range(depth):
        lam_init = 0.8 - 0.6 * math.exp(-0.3 * layer)
        mods = _ada(cond, w_ada[layer], row(b_ada[layer]))
        mod_p = [m.reshape(nb, 1, d) for m in jnp.split(mods[:nb], 6, axis=-1)]
        mod_s = [jnp.repeat(m, dseq, axis=0) for m in jnp.split(mods[nb:n_cond], 6, axis=-1)]

        wi = w_in[layer]
        in_wts = (wi[:, :w3].astype(BF16),
                  jnp.pad(wi[:, w3:RWKV_PROJ], ((0, 0), (0, LORA_PAD - LORA_WIDTH))).astype(BF16),
                  wi[:, RWKV_PROJ:RWKV_PROJ + DIFF_QK_WIDTH].astype(BF16),
                  wi[:, RWKV_PROJ + DIFF_QK_WIDTH:RWKV_PROJ + 2 * DIFF_QK_WIDTH].astype(BF16),
                  wi[:, RWKV_PROJ + 2 * DIFF_QK_WIDTH:].astype(BF16),
                  row(jnp.tile(qn_g[layer], DIFF_QK_WIDTH // DIFF_DK)),
                  row(jnp.tile(kn_g[layer], DIFF_QK_WIDTH // DIFF_DK)),
                  seg512)
        mu = mu_shift[layer]
        rwkv_wts = (row(mu[:w3]), row(jnp.pad(mu[w3:], (0, LORA_PAD - LORA_WIDTH))), row(w0[layer]),
                    _pad_rows(w_lora_up[layer], 0, LORA_PAD).astype(BF16), row(a0[layer]),
                    _pad_rows(a_lora_up[layer], DECAY_LORA, LORA_PAD).astype(BF16),
                    _pad_rows(g_lora_up[layer], DECAY_LORA + ICLR_LORA, LORA_PAD).astype(BF16),
                    row(k_k[layer]), row(k_a[layer]), row(r_k[layer]), row(lnx_g[layer]), row(lnx_b[layer]), seg512)
        lam_vecs = jnp.stack([lam_q1[layer], lam_k1[layer], lam_q2[layer], lam_k2[layer]])
        sg = row(subln_g[layer])
        wr = w_router[layer].T
        wr_hi = wr.astype(BF16)
        out_wts = (w_out[layer][:RWKV_WIDTH].astype(BF16), w_out[layer][RWKV_WIDTH:].astype(BF16),
                   ws_gate[layer].astype(BF16), ws_up[layer].astype(BF16), ws_down[layer].astype(BF16),
                   wr_hi, (wr - wr_hi.astype(F32)).astype(BF16))

        def token_mix(x, mod, tm, tiles_per_seq, seq_len, chunk, shift_prev, wkv0, long_seq):
            pm, plo, *qkv = _inproj(x, row(norm1_g[layer]), mod[0], mod[1], in_wts, tm, tiles_per_seq, long_seq)
            n_seq = x.shape[0] // seq_len
            shift_m = shift_prev[:, :, :w3]
            shift_l = jnp.pad(shift_prev[:, :, w3:], ((0, 0), (0, 0), (0, LORA_PAD - LORA_WIDTH)))
            yr, wkv = _rwkv(pm, plo, shift_m, shift_l, wkv0, rwkv_wts, seq_len, chunk)
            last = jnp.concatenate([pm.reshape(n_seq, seq_len, w3)[:, -1:],
                                    plo.reshape(n_seq, seq_len, LORA_PAD)[:, -1:, :LORA_WIDTH]], axis=-1)
            return yr, wkv, last, qkv

        yr_p, wkv_p, last_p, (kt_p, v4_p, qb_p, vb_p) = token_mix(
            xp, mod_p, tm_p, seq // tm_p, seq, chunk_p,
            jnp.zeros((nb, 1, RWKV_PROJ), F32), jnp.zeros((nb, RWKV_HEADS, RWKV_HEAD_DIM, RWKV_HEAD_DIM), F32), True)
        o_p = _attn_prompt(qb_p, kt_p, vb_p, lam_vecs, sg, seq, lam_init)
        yr_s, wkv_s, last_s, (q_s, k_s, v_s) = token_mix(
            xs, mod_s, tm_s, 1, dseq, dseq, state_shift[layer], state_wkv[layer], False)
        o_s = _attn_sample(q_s, k_s, v_s, cache_k2, cache_v2, depth * n_pool, page_table, layer * n_pool, lam_vecs,
                           sg, dseq, lam_init)

        base_p, h2t_p, lg_p = _outproj(yr_p, o_p, xp, mod_p[2], row(norm2_g[layer]), mod_p[3], mod_p[4], mod_p[5],
                                       out_wts, tm_p, seq // tm_p)
        base_s, h2t_s, lg_s = _outproj(yr_s, o_s, xs, mod_s[2], row(norm2_g[layer]), mod_s[3], mod_s[4], mod_s[5],
                                       out_wts, tm_s, 1)

        t_all = tp + ts
        idx, gate, pos, cnt = _router(jnp.concatenate([lg_p, lg_s], axis=1), router_bias[layer].reshape(-1, 1))
        counts = cnt[:, 0].astype(I32)
        padded = (counts + EXPERT_BLOCK - 1) // EXPERT_BLOCK * EXPERT_BLOCK
        end_padded = jnp.cumsum(padded)
        dest = _assign(idx, pos, (end_padded - padded).astype(F32).reshape(-1, 1))
        n_rows = (t_all * TOP_K + N_EXPERTS * (EXPERT_BLOCK - 1) + EXPERT_BLOCK - 1) // EXPERT_BLOCK * EXPERT_BLOCK
        n_blocks = n_rows // EXPERT_BLOCK
        blk_ids = jnp.arange(n_blocks, dtype=I32)
        n_used = (end_padded[-1:] // EXPERT_BLOCK).astype(I32)
        is_expert_end = jnp.any(end_padded[None, :] == (blk_ids[:, None] + 1) * EXPERT_BLOCK, axis=1)
        zero_flags = jnp.logical_or(blk_ids >= n_used[0], is_expert_end).astype(I32)
        rows_in = _dispatch(jnp.concatenate([h2t_p, h2t_s], axis=0), dest, zero_flags, n_rows)
        rows_out = _experts(rows_in, ((end_padded - padded) // EXPERT_BLOCK).astype(I32),
                            (padded // EXPERT_BLOCK).astype(I32), n_used, we_gate[layer], we_up[layer],
                            we_down[layer])
        gates_t = gate.T
        xp = _combine(base_p, mod_p[5], gates_t, dest, rows_out, 0, seq // ROUTE_TILE)
        xs = _combine(base_s, mod_s[5], gates_t, dest, rows_out, tp, 1)

        outs["kp"].append(jnp.transpose(kt_p.reshape(nb, DIFF_HEADS, 2, DIFF_DK, seq), (0, 4, 1, 2, 3)))
        outs["vp"].append(v4_p.reshape(nb, seq, DIFF_HEADS, DIFF_DV))
        outs["wp"].append(wkv_p)
        outs["sp"].append(last_p)
        outs["ks"].append(k_s.reshape(db, dseq, DIFF_HEADS, 2, DIFF_DK))
        outs["vs"].append(v_s.reshape(db, dseq, DIFF_HEADS, DIFF_DV))
        outs["ws"].append(wkv_s)
        outs["ss"].append(last_s)

    st = {name: jnp.stack(v) for name, v in outs.items()}
    return (xp.reshape(nb, seq, d), xs.reshape(db, dseq, d), st["kp"], st["vp"], st["wp"], st["sp"],
            st["ks"], st["vs"], st["ws"], st["ss"])
```

```python
import functools
import math

import jax
import jax.numpy as jnp
from jax import lax
from jax.experimental import pallas as pl
from jax.experimental.pallas import tpu as pltpu

F32 = jnp.float32
BF16 = jnp.bfloat16
I32 = jnp.int32

RWKV_HEADS = 8
RWKV_HEAD_DIM = 64
RWKV_WIDTH = RWKV_HEADS * RWKV_HEAD_DIM
DECAY_LORA = 32
ICLR_LORA = 32
GATE_LORA = 96
LORA_WIDTH = DECAY_LORA + ICLR_LORA + GATE_LORA
LORA_PAD = 256
RWKV_PROJ = 3 * RWKV_WIDTH + LORA_WIDTH
GN_EPS = 64e-5
DIFF_HEADS = 4
DIFF_DK = 64
DIFF_DV = 2 * DIFF_DK
DIFF_QK_WIDTH = DIFF_HEADS * 2 * DIFF_DK
DIFF_V_WIDTH = DIFF_HEADS * DIFF_DV
DIFF_SCALE = DIFF_DK ** -0.5
N_EXPERTS = 256
N_GROUPS = 8
GROUP_SIZE = N_EXPERTS // N_GROUPS
TOPK_GROUPS = 4
TOP_K = 8
ROUTE_SCALE = 2.5
NORM_EPS = 1e-6

LANES = 128
SUBLANES = 8
ROW_TILE = 8
VMEM_LIMIT = 48 * 1024 * 1024

EXPERT_BLOCK = 128
EXPERT_IN_SLOTS = 4
EXPERT_OUT_SLOTS = 3
EXPERT_DMA_SPLIT = 4
ROUTE_TILE = 128
ATTN_TILE = 2048
ATTN_ROW_CHUNK = 256
NEG_INF = float("-inf")


def _bdot(a, b):
    return jnp.dot(a.astype(BF16), b.astype(BF16), preferred_element_type=F32)


def _bdot_nt(a, b):
    return lax.dot_general(a.astype(BF16), b.astype(BF16), (((1,), (1,)), ((), ())), preferred_element_type=F32)


def _bdot_tn(a, b):
    return lax.dot_general(a.astype(BF16), b.astype(BF16), (((0,), (0,)), ((), ())), preferred_element_type=F32)


def _sigmoid(x):
    return 1.0 / (1.0 + jnp.exp(-x))


def _silu(x):
    return x * _sigmoid(x)


def _params(*sem, vmem=VMEM_LIMIT):
    return pltpu.CompilerParams(dimension_semantics=sem, vmem_limit_bytes=vmem)


def _ada_body(c_ref, w_ref, b_ref, o_ref):
    o_ref[...] = _bdot(_silu(c_ref[...]), w_ref[...]) + b_ref[...]


def _ada(c, w, b):
    rows, d = c.shape
    n = w.shape[1]
    tn = 512
    return pl.pallas_call(
        _ada_body,
        grid=(n // tn,),
        in_specs=[pl.BlockSpec((rows, d), lambda j: (0, 0)),
                  pl.BlockSpec((d, tn), lambda j: (0, j)),
                  pl.BlockSpec((1, tn), lambda j: (0, j))],
        out_specs=pl.BlockSpec((rows, tn), lambda j: (0, j)),
        out_shape=jax.ShapeDtypeStruct((rows, n), F32),
        compiler_params=_params("parallel"),
        name="ada",
    )(c, w, b)


def _mod_spec(mod, tm, tiles_per_seq):
    if mod.ndim == 3:
        return pl.BlockSpec((None, 1, mod.shape[-1]), lambda i: (i // tiles_per_seq, 0, 0))
    return pl.BlockSpec((tm, mod.shape[-1]), lambda i: (i, 0))


def _full_spec(a):
    nd = a.ndim
    return pl.BlockSpec(a.shape, lambda *_: (0,) * nd)


def _rms(x, g):
    return x * lax.rsqrt(jnp.mean(x * x, axis=-1, keepdims=True) + NORM_EPS) * g


def _inproj_body(x_ref, g_ref, sh_ref, sc_ref, wm_ref, wl_ref, wq_ref, wk_ref, wv_ref, qg_ref, kg_ref, seg_ref,
                 pm_ref, pl_ref, *out_refs, long_seq):
    tm = x_ref.shape[0]
    h = (_rms(x_ref[...], g_ref[...]) * (1.0 + sc_ref[...]) + sh_ref[...]).astype(BF16)
    pm_ref[...] = jnp.dot(h, wm_ref[...], preferred_element_type=F32)
    pl_ref[...] = jnp.dot(h, wl_ref[...], preferred_element_type=F32)
    seg = seg_ref[...]

    def head_norm(z, gain):
        ms = _bdot(z * z, seg) * (1.0 / DIFF_DK)
        return z * lax.rsqrt(ms + NORM_EPS) * gain

    q = head_norm(jnp.dot(h, wq_ref[...], preferred_element_type=F32), qg_ref[...])
    k = head_norm(jnp.dot(h, wk_ref[...], preferred_element_type=F32), kg_ref[...])
    v = jnp.dot(h, wv_ref[...], preferred_element_type=F32)
    if long_seq:
        kt_ref, v4_ref, qb_ref, vb_ref = out_refs
        kt_ref[...] = k.T
        for hd in range(DIFF_HEADS):
            v4_ref[pl.ds(hd, tm, stride=DIFF_HEADS), :] = v[:, hd * DIFF_DV:(hd + 1) * DIFF_DV]
        qb_ref[...] = (q * DIFF_SCALE).astype(BF16)
        vb_ref[...] = v.astype(BF16)
    else:
        q_ref, k_ref, v_ref = out_refs
        q_ref[...] = q
        k_ref[...] = k
        v_ref[...] = v


def _inproj(x, g, shift, scale, wts, tm, tiles_per_seq, long_seq):
    t, d = x.shape
    wm, wl, wq, wk, wv, qg, kg, seg = wts
    row = lambda n: pl.BlockSpec((tm, n), lambda i: (i, 0))
    out_specs = [row(3 * RWKV_WIDTH), row(LORA_PAD)]
    out_shape = [jax.ShapeDtypeStruct((t, 3 * RWKV_WIDTH), F32), jax.ShapeDtypeStruct((t, LORA_PAD), F32)]
    if long_seq:
        n_seq = t // (tm * tiles_per_seq)
        out_specs += [pl.BlockSpec((DIFF_QK_WIDTH, tm), lambda i: (i // tiles_per_seq, i % tiles_per_seq)),
                      pl.BlockSpec((tm * DIFF_HEADS, DIFF_DV), lambda i: (i, 0)),
                      row(DIFF_QK_WIDTH), row(DIFF_V_WIDTH)]
        out_shape += [jax.ShapeDtypeStruct((n_seq * DIFF_QK_WIDTH, tm * tiles_per_seq), F32),
                      jax.ShapeDtypeStruct((t * DIFF_HEADS, DIFF_DV), F32),
                      jax.ShapeDtypeStruct((t, DIFF_QK_WIDTH), BF16), jax.ShapeDtypeStruct((t, DIFF_V_WIDTH), BF16)]
    else:
        out_specs += [row(DIFF_QK_WIDTH), row(DIFF_QK_WIDTH), row(DIFF_V_WIDTH)]
        out_shape += [jax.ShapeDtypeStruct((t, n), F32) for n in (DIFF_QK_WIDTH, DIFF_QK_WIDTH, DIFF_V_WIDTH)]
    return pl.pallas_call(
        functools.partial(_inproj_body, long_seq=long_seq),
        grid=(t // tm,),
        in_specs=[row(d), _full_spec(g), _mod_spec(shift, tm, tiles_per_seq), _mod_spec(scale, tm, tiles_per_seq)]
                 + [_full_spec(a) for a in (wm, wl, wq, wk, wv, qg, kg, seg)],
        out_specs=out_specs,
        out_shape=out_shape,
        compiler_params=_params("parallel"),
        name="inproj",
    )(x, g, shift, scale, wm, wl, wq, wk, wv, qg, kg, seg)


def _split3(x):
    hi = x.astype(BF16)
    r1 = x - hi.astype(F32)
    mid = r1.astype(BF16)
    lo = (r1 - mid.astype(F32)).astype(BF16)
    return hi, mid, lo


def _rwkv_body(pm_ref, pl_ref, pm8_ref, pl8_ref, sm_ref, sl_ref, s0_ref,
               mum_ref, mul_ref, w0_ref, wupw_ref, a0_ref, wupa_ref, wupg_ref, kk_ref, ka_ref, rk_ref,
               lg_ref, lb_ref, seg_ref,
               y_ref, sout_ref, state_ref, *, chunk):
    c = pl.program_id(1)
    nc = pl.num_programs(1)
    C = chunk
    W = RWKV_WIDTH
    N = RWKV_HEAD_DIM

    @pl.when(c == 0)
    def _():
        state_ref[...] = s0_ref[...]

    first = c == 0
    prev_m = jnp.where(first, sm_ref[...], pm8_ref[SUBLANES - 1:SUBLANES, :])
    prev_l = jnp.where(first, sl_ref[...], pl8_ref[SUBLANES - 1:SUBLANES, :])

    def shifted(cur, prev_row):
        rows = lax.broadcasted_iota(I32, cur.shape, 0)
        return jnp.where(rows == 0, prev_row, pltpu.roll(cur, 1, 0))

    pm = pm_ref[...]
    plo = pl_ref[...]
    xm = pm + (shifted(pm, prev_m) - pm) * mum_ref[...]
    xl = plo + (shifted(plo, prev_l) - plo) * mul_ref[...]
    r = xm[:, 0:W]
    k = xm[:, W:2 * W]
    v = xm[:, 2 * W:3 * W]
    seg = seg_ref[...]

    z = -(w0_ref[...] + _bdot(jnp.tanh(xl), wupw_ref[...]))
    softplus = jnp.maximum(z, 0.0) + jnp.log(1.0 + jnp.exp(-jnp.abs(z)))
    w = -softplus - 0.5
    a = _sigmoid(a0_ref[...] + _bdot(xl, wupa_ref[...]))
    g = _bdot(_sigmoid(xl), wupg_ref[...])
    kk = k * kk_ref[...]
    kk = kk / jnp.maximum(jnp.sqrt(_bdot(kk * kk, seg)), 1e-12)
    k = k * (1.0 + (a - 1.0) * ka_ref[...])
    logdec = -jnp.exp(w)

    ti = lax.broadcasted_iota(I32, (C, C), 0)
    tj = lax.broadcasted_iota(I32, (C, C), 1)
    lower = (ti >= tj).astype(BF16)
    cum = sum(jnp.dot(lower, part, preferred_element_type=F32) for part in _split3(logdec))
    cum_end = cum[C - 1:C, :]
    a_t = -kk * jnp.exp(cum - logdec)
    r_t = r * jnp.exp(cum)
    inv = jnp.exp(-cum)
    b_t = kk * a * inv
    k_t = k * inv
    to_end = jnp.exp(cum_end - cum)
    b_e = kk * a * to_end
    k_e = k * to_end
    g_end = jnp.exp(cum_end)

    eye = (ti == tj).astype(F32)
    ri = lax.broadcasted_iota(I32, (2 * C, 2 * C), 0)
    ci = lax.broadcasted_iota(I32, (2 * C, 2 * C), 1)
    tr = jnp.where(ri >= C, ri - C, ri)
    tc = jnp.where(ci >= C, ci - C, ci)
    mask = jnp.logical_or(tr > tc, jnp.logical_and(ri >= C, tr == tc))
    zeros_cn = jnp.zeros((C, N), F32)
    levels = int(math.log2(C))
    heads = range(RWKV_HEADS)
    sls = [slice(h * N, (h + 1) * N) for h in heads]
    ah = [a_t[:, s] for s in sls]
    rh = [r_t[:, s] for s in sls]
    vh = [v[:, s] for s in sls]
    m_all = [jnp.where(mask, _bdot_nt(jnp.concatenate([ah[h], rh[h]], axis=0),
                                      jnp.concatenate([b_t[:, sls[h]], k_t[:, sls[h]]], axis=0)), 0.0)
             for h in heads]
    m_top = [m[0:C, :] for m in m_all]
    m_bot = [m[C:2 * C, :] for m in m_all]
    akv = [_bdot(m_top[h], jnp.concatenate([zeros_cn, vh[h]], axis=0)) for h in heads]
    power = [m[:, 0:C] for m in m_top]
    t_inv = [eye + p for p in power]
    for _ in range(levels - 1):
        power = [_bdot(p, p) for p in power]
        t_inv = [t + _bdot(t, p) for t, p in zip(t_inv, power)]
    w_mat = [_bdot(t_inv[h], ah[h]) for h in heads]
    u0 = [_bdot(t_inv[h], akv[h]) for h in heads]
    s0 = [state_ref[h] for h in heads]
    x = [_bdot_nt(jnp.concatenate([w_mat[h], rh[h]], axis=0), s0[h]) for h in heads]
    uv = [jnp.concatenate([x[h][0:C, :] + u0[h], vh[h]], axis=0) for h in heads]
    ys = [x[h][C:2 * C, :] + _bdot(m_bot[h], uv[h]) for h in heads]
    s_new = [s0[h] * g_end[:, sls[h]]
             + _bdot_tn(uv[h], jnp.concatenate([b_e[:, sls[h]], k_e[:, sls[h]]], axis=0)) for h in heads]
    for h in heads:
        state_ref[h] = s_new[h]
    y = jnp.concatenate(ys, axis=1)

    mean = _bdot(y, seg) * (1.0 / N)
    yc = y - mean
    var = _bdot(yc * yc, seg) * (1.0 / N)
    yn = yc * lax.rsqrt(var + GN_EPS) * lg_ref[...] + lb_ref[...]
    bonus = _bdot(r * k * rk_ref[...], seg) * v
    y_ref[...] = ((yn + bonus) * g).astype(y_ref.dtype)

    @pl.when(c == nc - 1)
    def _():
        sout_ref[...] = state_ref[...]


def _rwkv(pm, plo, shift_m, shift_l, s0, wts, seq_len, chunk):
    t_total = pm.shape[0]
    nb = t_total // seq_len
    ncnk = seq_len // chunk
    c8 = chunk // SUBLANES
    prev8 = lambda b, c: (jnp.maximum(b * (seq_len // SUBLANES) + c * c8 - 1, 0), 0)
    cur = lambda b, c: (b * ncnk + c, 0)
    per_seq3 = lambda n: pl.BlockSpec((None, 1, n), lambda b, c: (b, 0, 0))
    state_spec = pl.BlockSpec((None, RWKV_HEADS, RWKV_HEAD_DIM, RWKV_HEAD_DIM), lambda b, c: (b, 0, 0, 0))
    return pl.pallas_call(
        functools.partial(_rwkv_body, chunk=chunk),
        grid=(nb, ncnk),
        in_specs=[pl.BlockSpec((chunk, 3 * RWKV_WIDTH), cur), pl.BlockSpec((chunk, LORA_PAD), cur),
                  pl.BlockSpec((SUBLANES, 3 * RWKV_WIDTH), prev8), pl.BlockSpec((SUBLANES, LORA_PAD), prev8),
                  per_seq3(3 * RWKV_WIDTH), per_seq3(LORA_PAD), state_spec]
                 + [_full_spec(a) for a in wts],
        out_specs=[pl.BlockSpec((chunk, RWKV_WIDTH), cur), state_spec],
        out_shape=[jax.ShapeDtypeStruct((t_total, RWKV_WIDTH), BF16 if chunk % 16 == 0 else F32),
                   jax.ShapeDtypeStruct(s0.shape, F32)],
        scratch_shapes=[pltpu.VMEM((RWKV_HEADS, RWKV_HEAD_DIM, RWKV_HEAD_DIM), F32)],
        compiler_params=_params("parallel", "arbitrary"),
        name="rwkv",
    )(pm, plo, pm, plo, shift_m, shift_l, s0, *wts)


def _lambda(lam_ref, lam_init):
    lv = lam_ref[...]
    return (jnp.exp(jnp.sum(lv[0:1, :] * lv[1:2, :], axis=-1, keepdims=True))
            - jnp.exp(jnp.sum(lv[2:3, :] * lv[3:4, :], axis=-1, keepdims=True)) + lam_init)


def _subln(o, g, lam_init):
    return o * lax.rsqrt(jnp.mean(o * o, axis=-1, keepdims=True) + NORM_EPS) * g * (1.0 - lam_init)


def _attn_prompt_body(qi_ref, ki_ref, q_ref, kt_ref, v_ref, lam_ref, sg_ref, o_ref,
                      qm_ref, m_ref, acc_ref, *, lam_init):
    p = pl.program_id(2)
    qi = qi_ref[p]
    ki = ki_ref[p]
    tq = q_ref.shape[0]
    tk = kt_ref.shape[1]
    dv = v_ref.shape[1]

    @pl.when(ki == 0)
    def _():
        q = q_ref[...]
        lane = lax.broadcasted_iota(I32, q.shape, 1)
        qm_ref[0] = jnp.where(lane < DIFF_DK, q, jnp.zeros_like(q))
        qm_ref[1] = jnp.where(lane >= DIFF_DK, q, jnp.zeros_like(q))
        m_ref[...] = jnp.full(m_ref.shape, NEG_INF, F32)
        acc_ref[...] = jnp.zeros(acc_ref.shape, F32)

    def accumulate(masked):
        kt = kt_ref[...].astype(BF16)
        v1 = jnp.concatenate([v_ref[...], jnp.ones((tk, dv), BF16)], axis=1)
        for c in range(2):
            for r0 in range(0, tq, ATTN_ROW_CHUNK):
                rows = slice(r0, r0 + ATTN_ROW_CHUNK)
                ncol = min(tk, r0 + ATTN_ROW_CHUNK) if masked else tk
                s = jnp.dot(qm_ref[c, rows, :], kt[:, 0:ncol], preferred_element_type=F32)
                if masked:
                    row = lax.broadcasted_iota(I32, s.shape, 0) + r0
                    col = lax.broadcasted_iota(I32, s.shape, 1)
                    s = jnp.where(col <= row, s, NEG_INF)
                m_old = m_ref[c, rows, :]
                m_new = jnp.maximum(m_old, jnp.max(s, axis=-1, keepdims=True))
                alpha = jnp.exp(m_old - m_new)
                pr = jnp.exp(s - jnp.concatenate([m_new] * (ncol // LANES), axis=1))
                acc_ref[c, rows, :] = (jnp.concatenate([alpha] * (2 * dv // LANES), axis=1) * acc_ref[c, rows, :]
                                       + jnp.dot(pr.astype(BF16), v1[0:ncol, :], preferred_element_type=F32))
                m_ref[c, rows, :] = m_new

    @pl.when(ki < qi)
    def _():
        accumulate(False)

    @pl.when(ki == qi)
    def _():
        accumulate(True)
        lam = _lambda(lam_ref, lam_init)
        a0 = acc_ref[0]
        a1 = acc_ref[1]
        o = a0[:, 0:dv] / a0[:, dv:2 * dv] - lam * (a1[:, 0:dv] / a1[:, dv:2 * dv])
        o_ref[...] = _subln(o, sg_ref[...], lam_init).astype(o_ref.dtype)


def _attn_prompt(qb, kt, vb, lam_vecs, subln_g, seq_len, lam_init):
    t_total = qb.shape[0]
    nb = t_total // seq_len
    tq = min(ATTN_TILE, seq_len)
    nq = seq_len // tq
    pairs = [(i, j) for i in range(nq) for j in range(i + 1)]
    qi_tab = jnp.asarray([a for a, _ in pairs], I32)
    ki_tab = jnp.asarray([b for _, b in pairs], I32)
    qmap = lambda b, h, p, qi, ki: (b * nq + qi[p], h)
    vmap = lambda b, h, p, qi, ki: (b * nq + ki[p], h)
    ktmap = lambda b, h, p, qi, ki: (b * DIFF_HEADS + h, ki[p])
    grid_spec = pltpu.PrefetchScalarGridSpec(
        num_scalar_prefetch=2,
        grid=(nb, DIFF_HEADS, len(pairs)),
        in_specs=[pl.BlockSpec((tq, DIFF_DV), qmap), pl.BlockSpec((2 * DIFF_DK, tq), ktmap),
                  pl.BlockSpec((tq, DIFF_DV), vmap),
                  pl.BlockSpec(lam_vecs.shape, lambda *_: (0, 0)), pl.BlockSpec(subln_g.shape, lambda *_: (0, 0))],
        out_specs=pl.BlockSpec((tq, DIFF_DV), qmap),
        scratch_shapes=[pltpu.VMEM((2, tq, DIFF_DV), BF16), pltpu.VMEM((2, tq, LANES), F32),
                        pltpu.VMEM((2, tq, 2 * DIFF_DV), F32)],
    )
    return pl.pallas_call(
        functools.partial(_attn_prompt_body, lam_init=lam_init),
        grid_spec=grid_spec,
        out_shape=jax.ShapeDtypeStruct((t_total, DIFF_V_WIDTH), BF16),
        compiler_params=_params("parallel", "parallel", "arbitrary"),
        name="attn_prompt",
    )(qi_tab, ki_tab, qb, kt, vb, lam_vecs, subln_g)


def _attn_sample_body(pt_ref, q_ref, k_ref, v_ref, lam_ref, sg_ref, *rest, n_pages, lam_init):
    kp_refs = rest[:n_pages]
    vp_refs = rest[n_pages:2 * n_pages]
    o_ref = rest[2 * n_pages]
    s_new = q_ref.shape[0]
    n_maps = DIFF_HEADS * 2
    page = kp_refs[0].shape[1]
    nrow = n_maps * s_new

    def value_page(vr):
        return jnp.concatenate([vr[pl.ds(h, page, stride=DIFF_HEADS), :] for h in range(DIFF_HEADS)], axis=1)

    qt = jnp.concatenate([q_ref[...] * DIFF_SCALE] * (DIFF_HEADS * 2), axis=0)
    row = lax.broadcasted_iota(I32, qt.shape, 0)
    col = lax.broadcasted_iota(I32, qt.shape, 1)
    qbd = jnp.where(col // DIFF_DK == row // s_new, qt, 0.0).astype(BF16)
    pad = jnp.zeros((page - s_new, k_ref.shape[1]), F32)
    k_new = jnp.concatenate([k_ref[...], pad], axis=0)
    v_new = jnp.concatenate([v_ref[...], pad], axis=0)
    scores = [_bdot(qbd, kr[...]) for kr in kp_refs]
    s_n = _bdot_nt(qbd, k_new)
    rn = lax.broadcasted_iota(I32, s_n.shape, 0)
    cn = lax.broadcasted_iota(I32, s_n.shape, 1)
    scores.append(jnp.where(cn <= rn % s_new, s_n, NEG_INF))
    m = functools.reduce(jnp.maximum, [jnp.max(s, axis=-1, keepdims=True) for s in scores])
    values = [value_page(vr) for vr in vp_refs] + [v_new]
    l = jnp.zeros_like(m)
    acc = jnp.zeros((nrow, v_new.shape[1]), F32)
    for s, val in zip(scores, values):
        pr = jnp.exp(s - m)
        l = l + jnp.sum(pr, axis=-1, keepdims=True)
        acc = acc + _bdot(pr, val)
    full = acc / l
    lam = _lambda(lam_ref, lam_init)
    outs = []
    for h in range(DIFF_HEADS):
        r0 = h * 2 * s_new
        cols = slice(h * DIFF_DV, (h + 1) * DIFF_DV)
        o = full[r0:r0 + s_new, cols] - lam * full[r0 + s_new:r0 + 2 * s_new, cols]
        outs.append(_subln(o, sg_ref[...], lam_init))
    o_ref[...] = jnp.concatenate(outs, axis=1)


def _attn_sample(q, k, v, cache_k, cache_v, n_pool_pages, page_table, page_offset, lam_vecs, subln_g, s_new,
                 lam_init):
    nseq, n_pages = page_table.shape
    width = q.shape[1]
    k_rows = cache_k.shape[0] // n_pool_pages
    v_rows = cache_v.shape[0] // n_pool_pages
    new_spec = pl.BlockSpec((s_new, width), lambda b, pt: (b, 0))
    page_spec = lambda rows, lanes, j: pl.BlockSpec((rows, lanes), lambda b, pt: (pt[b, j] + page_offset, 0))
    grid_spec = pltpu.PrefetchScalarGridSpec(
        num_scalar_prefetch=1,
        grid=(nseq,),
        in_specs=[new_spec, new_spec, new_spec,
                  pl.BlockSpec(lam_vecs.shape, lambda *_: (0, 0)), pl.BlockSpec(subln_g.shape, lambda *_: (0, 0))]
                 + [page_spec(k_rows, cache_k.shape[1], j) for j in range(n_pages)]
                 + [page_spec(v_rows, DIFF_DV, j) for j in range(n_pages)],
        out_specs=new_spec,
    )
    return pl.pallas_call(
        functools.partial(_attn_sample_body, n_pages=n_pages, lam_init=lam_init),
        grid_spec=grid_spec,
        out_shape=jax.ShapeDtypeStruct((nseq * s_new, width), F32),
        compiler_params=_params("parallel"),
        name="attn_sample",
    )(page_table, q, k, v, lam_vecs, subln_g, *([cache_k] * n_pages), *([cache_v] * n_pages))


def _outproj_body(yr_ref, o_ref, x_ref, g1_ref, g2n_ref, sh_ref, sc_ref, g2_ref, woa_ref, wob_ref,
                  wsg_ref, wsu_ref, wsd_ref, wrh_ref, wrl_ref,
                  base_ref, h2t_ref, lg_ref):
    tm = x_ref.shape[0]
    mixed = (jnp.dot(yr_ref[...].astype(BF16), woa_ref[...], preferred_element_type=F32)
             + jnp.dot(o_ref[...].astype(BF16), wob_ref[...], preferred_element_type=F32))
    x1 = x_ref[...] + g1_ref[...] * mixed
    h2 = _rms(x1, g2n_ref[...]) * (1.0 + sc_ref[...]) + sh_ref[...]
    h2b = h2.astype(BF16)
    hidden = _silu(jnp.dot(h2b, wsg_ref[...], preferred_element_type=F32)) * jnp.dot(
        h2b, wsu_ref[...], preferred_element_type=F32)
    shared = jnp.dot(hidden.astype(BF16), wsd_ref[...], preferred_element_type=F32)
    base_ref[...] = x1 + g2_ref[...] * shared
    h2l = (h2 - h2b.astype(F32)).astype(BF16)
    nt = lambda a, b: lax.dot_general(a, b, (((1,), (1,)), ((), ())), preferred_element_type=F32)
    lg_ref[...] = nt(wrh_ref[...], h2b) + nt(wrl_ref[...], h2b) + nt(wrh_ref[...], h2l)
    for c in range(ROW_TILE):
        h2t_ref[pl.ds(c, tm, stride=ROW_TILE), :] = h2[:, c * LANES:(c + 1) * LANES]


def _outproj(yr, o, x, gate1, norm2_g, shift2, scale2, gate2, wts, tm, tiles_per_seq):
    t, d = x.shape
    row = lambda n: pl.BlockSpec((tm, n), lambda i: (i, 0))
    mod = lambda m: _mod_spec(m, tm, tiles_per_seq)
    return pl.pallas_call(
        _outproj_body,
        grid=(t // tm,),
        in_specs=[row(RWKV_WIDTH), row(DIFF_V_WIDTH), row(d), mod(gate1), _full_spec(norm2_g), mod(shift2),
                  mod(scale2), mod(gate2)] + [_full_spec(a) for a in wts],
        out_specs=[row(d), pl.BlockSpec((tm * ROW_TILE, LANES), lambda i: (i, 0)),
                   pl.BlockSpec((N_EXPERTS, tm), lambda i: (0, i))],
        out_shape=[jax.ShapeDtypeStruct((t, d), F32),
                   jax.ShapeDtypeStruct((t * ROW_TILE, LANES), F32),
                   jax.ShapeDtypeStruct((N_EXPERTS, t), F32)],
        compiler_params=_params("parallel"),
        name="outproj",
    )(yr, o, x, gate1, norm2_g, shift2, scale2, gate2, *wts)


def _router_body(lg_ref, bias_ref, idx_ref, gate_ref, pos_ref, cnt_ref, run_ref):
    i = pl.program_id(0)
    tm = lg_ref.shape[1]

    @pl.when(i == 0)
    def _():
        run_ref[...] = jnp.zeros(run_ref.shape, F32)

    scores = _sigmoid(lg_ref[...])
    biased = scores + bias_ref[...]
    erow = lax.broadcasted_iota(I32, (N_EXPERTS, tm), 0)
    grow = lax.broadcasted_iota(I32, (GROUP_SIZE, tm), 0)

    def first_argmax(x, rows, limit):
        mx = jnp.max(x, axis=0, keepdims=True)
        return mx, jnp.min(jnp.where(x == mx, rows, limit), axis=0, keepdims=True)

    group_scores = []
    for gidx in range(N_GROUPS):
        xg = biased[gidx * GROUP_SIZE:(gidx + 1) * GROUP_SIZE, :]
        m1, i1 = first_argmax(xg, grow, GROUP_SIZE)
        m2 = jnp.max(jnp.where(grow == i1, NEG_INF, xg), axis=0, keepdims=True)
        group_scores.append(m1 + m2)
    gs = jnp.concatenate(group_scores, axis=0)
    g8 = lax.broadcasted_iota(I32, (N_GROUPS, tm), 0)
    chosen = jnp.zeros((N_GROUPS, tm), I32)
    for _ in range(TOPK_GROUPS):
        _, gi = first_argmax(gs, g8, N_GROUPS)
        hit = g8 == gi
        chosen = jnp.where(hit, 1, chosen)
        gs = jnp.where(hit, NEG_INF, gs)
    cand = jnp.concatenate(
        [jnp.where(chosen[gidx:gidx + 1, :] > 0, biased[gidx * GROUP_SIZE:(gidx + 1) * GROUP_SIZE, :], NEG_INF)
         for gidx in range(N_GROUPS)], axis=0)

    idxs, raws = [], []
    onehot = jnp.zeros((N_EXPERTS, tm), F32)
    for _ in range(TOP_K):
        _, ei = first_argmax(cand, erow, N_EXPERTS)
        hit = erow == ei
        idxs.append(ei)
        raws.append(jnp.sum(jnp.where(hit, scores, 0.0), axis=0, keepdims=True))
        onehot = onehot + hit.astype(F32)
        cand = jnp.where(hit, NEG_INF, cand)
    raw = jnp.concatenate(raws, axis=0)
    gate_ref[...] = raw / jnp.sum(raw, axis=0, keepdims=True) * ROUTE_SCALE
    idx_ref[...] = jnp.concatenate(idxs, axis=0)

    ti = lax.broadcasted_iota(I32, (tm, tm), 0)
    tj = lax.broadcasted_iota(I32, (tm, tm), 1)
    oh = onehot.astype(BF16)
    before = jnp.dot(oh, (ti < tj).astype(BF16), preferred_element_type=F32) + run_ref[...]
    run_ref[...] = run_ref[...] + jnp.dot(oh, jnp.ones((tm, tm), BF16), preferred_element_type=F32)
    pos_ref[...] = jnp.concatenate(
        [jnp.sum(jnp.where(erow == ei, before, 0.0), axis=0, keepdims=True) for ei in idxs], axis=0).astype(I32)
    cnt_ref[...] = run_ref[...]


def _router(logits_t, bias_col):
    t = logits_t.shape[1]
    tm = ROUTE_TILE
    tok = pl.BlockSpec((TOP_K, tm), lambda i: (0, i))
    return pl.pallas_call(
        _router_body,
        grid=(t // tm,),
        in_specs=[pl.BlockSpec((N_EXPERTS, tm), lambda i: (0, i)), _full_spec(bias_col)],
        out_specs=[tok, tok, tok, pl.BlockSpec((N_EXPERTS, tm), lambda i: (0, 0))],
        out_shape=[jax.ShapeDtypeStruct((TOP_K, t), I32), jax.ShapeDtypeStruct((TOP_K, t), F32),
                   jax.ShapeDtypeStruct((TOP_K, t), I32), jax.ShapeDtypeStruct((N_EXPERTS, tm), F32)],
        scratch_shapes=[pltpu.VMEM((N_EXPERTS, tm), F32)],
        compiler_params=_params("arbitrary"),
        name="router",
    )(logits_t, bias_col)


def _assign_body(idx_ref, pos_ref, start_ref, dest_ref):
    tm = idx_ref.shape[1]
    erow = lax.broadcasted_iota(I32, (N_EXPERTS, tm), 0)
    start = start_ref[...]
    idx = idx_ref[...]
    first = jnp.concatenate(
        [jnp.sum(jnp.where(erow == idx[j:j + 1, :], start, 0.0), axis=0, keepdims=True) for j in range(TOP_K)],
        axis=0)
    dest_ref[...] = first.astype(I32) + pos_ref[...]


def _assign(idx, pos, start_col):
    t = idx.shape[1]
    tm = ROUTE_TILE
    tok = pl.BlockSpec((TOP_K, tm), lambda i: (0, i))
    return pl.pallas_call(
        _assign_body,
        grid=(t // tm,),
        in_specs=[tok, tok, _full_spec(start_col)],
        out_specs=tok,
        out_shape=jax.ShapeDtypeStruct((TOP_K, t), I32),
        compiler_params=_params("parallel"),
        name="assign",
    )(idx, pos, start_col)


def _row_copy(src, src_row, dst, dst_row, sem):
    return pltpu.make_async_copy(src.at[pl.ds(pl.multiple_of(src_row * ROW_TILE, ROW_TILE), ROW_TILE), :],
                                 dst.at[pl.ds(pl.multiple_of(dst_row * ROW_TILE, ROW_TILE), ROW_TILE), :], sem)


def _invert_body(dest_ref, zeros_ref, src_ref, sem):
    i = pl.program_id(0)
    tm = dest_ref.shape[1]
    unroll = 4

    @pl.when(i == 0)
    def _():
        cp = pltpu.make_async_copy(zeros_ref, src_ref, sem)
        cp.start()
        cp.wait()

    def fill(t4, carry):
        for u in range(unroll):
            t = t4 * unroll + u
            for j in range(TOP_K):
                src_ref[dest_ref[j, t]] = i * tm + t
        return carry

    lax.fori_loop(0, tm // unroll, fill, 0)


def _invert(dest, n_rows):
    t = dest.shape[1]
    tm = ROUTE_TILE
    return pl.pallas_call(
        _invert_body,
        grid=(t // tm,),
        in_specs=[pl.BlockSpec((TOP_K, tm), lambda i: (0, i), memory_space=pltpu.SMEM),
                  pl.BlockSpec(memory_space=pl.ANY)],
        out_specs=pl.BlockSpec(memory_space=pltpu.SMEM),
        out_shape=jax.ShapeDtypeStruct((n_rows,), I32),
        scratch_shapes=[pltpu.SemaphoreType.DMA],
        compiler_params=_params("arbitrary"),
        name="invert",
    )(dest, jnp.zeros((n_rows,), I32))


def _expert_body(first_ref, count_ref, nu_ref, src_ref, xs_ref, wg_ref, wu_ref, wd_ref, y_ref,
                 xin_ref, yout_ref, wgb_ref, wub_ref, wdb_ref, in_sem, out_sem):
    e = pl.program_id(0)
    rows = EXPERT_BLOCK
    blk_rows = EXPERT_BLOCK * ROW_TILE
    nb = count_ref[e]
    b0 = first_ref[e]

    n_used = nu_ref[0]
    n_in = xin_ref.shape[0]
    n_out = yout_ref.shape[0]
    ahead = n_in - 1
    part = blk_rows // EXPERT_DMA_SPLIT

    def hbm_rows(g, k=0, n=blk_rows):
        return pl.ds(pl.multiple_of(g * blk_rows + k * part, part), n)

    def start_in(g):
        slot = g % n_in

        for r in range(rows):
            _row_copy(xs_ref, src_ref[g * rows + r], xin_ref.at[slot], r, in_sem.at[slot]).start()

    def wait_in(g):
        slot = g % n_in
        pltpu.make_async_copy(xs_ref.at[pl.ds(0, blk_rows), :], xin_ref.at[slot], in_sem.at[slot]).wait()

    def start_out(g):
        slot = g % n_out
        for k in range(EXPERT_DMA_SPLIT):
            pltpu.make_async_copy(yout_ref.at[slot, pl.ds(k * part, part), :], y_ref.at[hbm_rows(g, k, part), :],
                                  out_sem.at[slot]).start()

    def wait_out(g):
        slot = g % n_out
        pltpu.make_async_copy(yout_ref.at[slot], y_ref.at[hbm_rows(g), :], out_sem.at[slot]).wait()

    @pl.when(e == 0)
    def _():
        for g in range(ahead):
            @pl.when(g < n_used)
            def _():
                start_in(g)

    @pl.when(nb > 0)
    def _():
        wgb_ref[...] = wg_ref[...].astype(BF16)
        wub_ref[...] = wu_ref[...].astype(BF16)
        wdb_ref[...] = wd_ref[...].astype(BF16)

        def step(b, carry):
            g = b0 + b
            wait_in(g)

            @pl.when(g >= n_out)
            def _():
                wait_out(g - n_out)

            def compute():
                islot = g % n_in
                oslot = g % n_out
                x = jnp.concatenate([xin_ref[islot, pl.ds(c, rows, stride=ROW_TILE), :] for c in range(ROW_TILE)],
                                    axis=1).astype(BF16)
                hidden = _silu(jnp.dot(x, wgb_ref[...], preferred_element_type=F32)) * jnp.dot(
                    x, wub_ref[...], preferred_element_type=F32)
                y = jnp.dot(hidden.astype(BF16), wdb_ref[...], preferred_element_type=F32)
                for c in range(ROW_TILE):
                    yout_ref[oslot, pl.ds(c, rows, stride=ROW_TILE), :] = y[:, c * LANES:(c + 1) * LANES]
                start_out(g)

            @pl.when(g + ahead < n_used)
            def _():
                start_in(g + ahead)
                compute()

            @pl.when(g + ahead >= n_used)
            def _():
                compute()

            return carry

        lax.fori_loop(0, nb, step, 0)

    @pl.when(e == pl.num_programs(0) - 1)
    def _():
        def drain(g, carry):
            wait_out(g)
            return carry

        lax.fori_loop(jnp.maximum(n_used - n_out, 0), n_used, drain, 0)
        n_blocks = y_ref.shape[0] // blk_rows
        yout_ref[0] = jnp.zeros(yout_ref.shape[1:], F32)

        def tail(b, carry):
            cp = pltpu.make_async_copy(
                yout_ref.at[0], y_ref.at[pl.ds(pl.multiple_of(b * blk_rows, blk_rows), blk_rows), :], out_sem.at[0])
            cp.start()
            cp.wait()
            return carry

        lax.fori_loop(nu_ref[0], n_blocks, tail, 0)


def _experts(h2t, row_src, first_block, block_count, n_used, we_gate, we_up, we_down):
    d, f = we_gate.shape[1], we_gate.shape[2]
    blk = (EXPERT_BLOCK * ROW_TILE, LANES)
    wspec = lambda a, b: pl.BlockSpec((None, a, b), lambda e, *_: (e, 0, 0))
    grid_spec = pltpu.PrefetchScalarGridSpec(
        num_scalar_prefetch=3,
        grid=(N_EXPERTS,),
        in_specs=[pl.BlockSpec(memory_space=pltpu.SMEM), pl.BlockSpec(memory_space=pl.ANY),
                  wspec(d, f), wspec(d, f), wspec(f, d)],
        out_specs=pl.BlockSpec(memory_space=pl.ANY),
        scratch_shapes=[pltpu.VMEM((EXPERT_IN_SLOTS,) + blk, F32), pltpu.VMEM((EXPERT_OUT_SLOTS,) + blk, F32),
                        pltpu.VMEM((d, f), BF16), pltpu.VMEM((d, f), BF16), pltpu.VMEM((f, d), BF16),
                        pltpu.SemaphoreType.DMA((EXPERT_IN_SLOTS,)), pltpu.SemaphoreType.DMA((EXPERT_OUT_SLOTS,))],
    )
    return pl.pallas_call(
        _expert_body,
        grid_spec=grid_spec,
        out_shape=jax.ShapeDtypeStruct((row_src.shape[0] * ROW_TILE, LANES), F32),
        compiler_params=_params("arbitrary"),
        name="experts",
    )(first_block, block_count, n_used, row_src, h2t, we_gate, we_up, we_down)


def _combine_body(base_ref, g2_ref, gate_ref, dest_ref, y_ref, out_ref, buf_ref, sem):
    tm = base_ref.shape[0]
    slot_rows = tm * ROW_TILE

    def issue(t, carry):
        for j in range(TOP_K):
            _row_copy(y_ref, dest_ref[j, t], buf_ref, j * tm + t, sem).start()
        return carry

    lax.fori_loop(0, tm, issue, 0)
    pltpu.make_async_copy(y_ref.at[pl.ds(0, TOP_K * slot_rows), :], buf_ref, sem).wait()
    gates = gate_ref[...]
    gcols = [jnp.broadcast_to(gates[:, j:j + 1], (tm, LANES)) for j in range(TOP_K)]
    for c in range(ROW_TILE):
        cols = slice(c * LANES, (c + 1) * LANES)
        routed = sum(gcols[j] * buf_ref[pl.ds(j * slot_rows + c, tm, stride=ROW_TILE), :] for j in range(TOP_K))
        out_ref[:, cols] = base_ref[:, cols] + g2_ref[:, cols] * routed


def _combine(base, gate2, gates, dest, y_rows, tok_offset, tiles_per_seq):
    t, d = base.shape
    tm = ROUTE_TILE
    off = tok_offset // tm
    return pl.pallas_call(
        _combine_body,
        grid=(t // tm,),
        in_specs=[pl.BlockSpec((tm, d), lambda i: (i, 0)), _mod_spec(gate2, tm, tiles_per_seq),
                  pl.BlockSpec((tm, TOP_K), lambda i: (i + off, 0)),
                  pl.BlockSpec((TOP_K, tm), lambda i: (0, i + off), memory_space=pltpu.SMEM),
                  pl.BlockSpec(memory_space=pl.ANY)],
        out_specs=pl.BlockSpec((tm, d), lambda i: (i, 0)),
        out_shape=jax.ShapeDtypeStruct((t, d), F32),
        scratch_shapes=[pltpu.VMEM((TOP_K * tm * ROW_TILE, LANES), F32), pltpu.SemaphoreType.DMA],
        compiler_params=_params("arbitrary"),
        name="combine",
    )(base, gate2, gates, dest, y_rows)


def _segment_ones(width, seg):
    ids = jnp.arange(width) // seg
    return (ids[:, None] == ids[None, :]).astype(BF16)


def _pad_rows(w, start, total):
    return jnp.zeros((total, w.shape[1]), w.dtype).at[start:start + w.shape[0]].set(w)


def kernel(x_prompt, x_sample, cache_k, cache_v, state_wkv, state_shift, page_table, c_prompt, c_sample, w_ada, b_ada, norm1_g, norm2_g, w_in, mu_shift, w0, w_lora_up, a0, a_lora_up, g_lora_up, k_k, k_a, r_k, lnx_g, lnx_b, qn_g, kn_g, lam_q1, lam_k1, lam_q2, lam_k2, subln_g, w_out, w_router, router_bias, we_gate, we_up, we_down, ws_gate, ws_up, ws_down):
    depth = w_in.shape[0]
    nb, seq, d = x_prompt.shape
    db, dseq, _ = x_sample.shape
    n_pool, page = cache_k.shape[1], cache_k.shape[2]
    tp, ts = nb * seq, db * dseq
    w3 = 3 * RWKV_WIDTH
    row = lambda a: a.reshape(1, -1)
    seg512 = _segment_ones(RWKV_WIDTH, RWKV_HEAD_DIM)
    tm_p = min(256, seq)
    tm_s = min(256, ts)
    chunk_p = min(64, seq)

    xp = x_prompt.reshape(tp, d)
    xs = x_sample.reshape(ts, d)
    cache_k2 = jnp.transpose(cache_k, (0, 1, 3, 4, 5, 2)).reshape(depth * n_pool * DIFF_QK_WIDTH, page)
    cache_v2 = cache_v.reshape(depth * n_pool * page * DIFF_HEADS, DIFF_DV)
    n_cond = nb + db
    cond = jnp.concatenate([c_prompt, c_sample], axis=0)
    cond = jnp.pad(cond, ((0, -n_cond % SUBLANES), (0, 0)))

    outs = {name: [] for name in ("kp", "vp", "wp", "sp", "ks", "vs", "ws", "ss")}
    for layer in range(depth):
        lam_init = 0.8 - 0.6 * math.exp(-0.3 * layer)
        mods = _ada(cond, w_ada[layer], row(b_ada[layer]))
        mod_p = [m.reshape(nb, 1, d) for m in jnp.split(mods[:nb], 6, axis=-1)]
        mod_s = [jnp.repeat(m, dseq, axis=0) for m in jnp.split(mods[nb:n_cond], 6, axis=-1)]

        wi = w_in[layer]
        in_wts = (wi[:, :w3].astype(BF16),
                  jnp.pad(wi[:, w3:RWKV_PROJ], ((0, 0), (0, LORA_PAD - LORA_WIDTH))).astype(BF16),
                  wi[:, RWKV_PROJ:RWKV_PROJ + DIFF_QK_WIDTH].astype(BF16),
                  wi[:, RWKV_PROJ + DIFF_QK_WIDTH:RWKV_PROJ + 2 * DIFF_QK_WIDTH].astype(BF16),
                  wi[:, RWKV_PROJ + 2 * DIFF_QK_WIDTH:].astype(BF16),
                  row(jnp.tile(qn_g[layer], DIFF_QK_WIDTH // DIFF_DK)),
                  row(jnp.tile(kn_g[layer], DIFF_QK_WIDTH // DIFF_DK)),
                  seg512)
        mu = mu_shift[layer]
        rwkv_wts = (row(mu[:w3]), row(jnp.pad(mu[w3:], (0, LORA_PAD - LORA_WIDTH))), row(w0[layer]),
                    _pad_rows(w_lora_up[layer], 0, LORA_PAD).astype(BF16), row(a0[layer]),
                    _pad_rows(a_lora_up[layer], DECAY_LORA, LORA_PAD).astype(BF16),
                    _pad_rows(g_lora_up[layer], DECAY_LORA + ICLR_LORA, LORA_PAD).astype(BF16),
                    row(k_k[layer]), row(k_a[layer]), row(r_k[layer]), row(lnx_g[layer]), row(lnx_b[layer]), seg512)
        lam_vecs = jnp.stack([lam_q1[layer], lam_k1[layer], lam_q2[layer], lam_k2[layer]])
        sg = row(subln_g[layer])
        wr = w_router[layer].T
        wr_hi = wr.astype(BF16)
        out_wts = (w_out[layer][:RWKV_WIDTH].astype(BF16), w_out[layer][RWKV_WIDTH:].astype(BF16),
                   ws_gate[layer].astype(BF16), ws_up[layer].astype(BF16), ws_down[layer].astype(BF16),
                   wr_hi, (wr - wr_hi.astype(F32)).astype(BF16))

        def token_mix(x, mod, tm, tiles_per_seq, seq_len, chunk, shift_prev, wkv0, long_seq):
            pm, plo, *qkv = _inproj(x, row(norm1_g[layer]), mod[0], mod[1], in_wts, tm, tiles_per_seq, long_seq)
            n_seq = x.shape[0] // seq_len
            shift_m = shift_prev[:, :, :w3]
            shift_l = jnp.pad(shift_prev[:, :, w3:], ((0, 0), (0, 0), (0, LORA_PAD - LORA_WIDTH)))
            yr, wkv = _rwkv(pm, plo, shift_m, shift_l, wkv0, rwkv_wts, seq_len, chunk)
            last = jnp.concatenate([pm.reshape(n_seq, seq_len, w3)[:, -1:],
                                    plo.reshape(n_seq, seq_len, LORA_PAD)[:, -1:, :LORA_WIDTH]], axis=-1)
            return yr, wkv, last, qkv

        yr_p, wkv_p, last_p, (kt_p, v4_p, qb_p, vb_p) = token_mix(
            xp, mod_p, tm_p, seq // tm_p, seq, chunk_p,
            jnp.zeros((nb, 1, RWKV_PROJ), F32), jnp.zeros((nb, RWKV_HEADS, RWKV_HEAD_DIM, RWKV_HEAD_DIM), F32), True)
        o_p = _attn_prompt(qb_p, kt_p, vb_p, lam_vecs, sg, seq, lam_init)
        yr_s, wkv_s, last_s, (q_s, k_s, v_s) = token_mix(
            xs, mod_s, tm_s, 1, dseq, dseq, state_shift[layer], state_wkv[layer], False)
        o_s = _attn_sample(q_s, k_s, v_s, cache_k2, cache_v2, depth * n_pool, page_table, layer * n_pool, lam_vecs,
                           sg, dseq, lam_init)

        base_p, h2t_p, lg_p = _outproj(yr_p, o_p, xp, mod_p[2], row(norm2_g[layer]), mod_p[3], mod_p[4], mod_p[5],
                                       out_wts, tm_p, seq // tm_p)
        base_s, h2t_s, lg_s = _outproj(yr_s, o_s, xs, mod_s[2], row(norm2_g[layer]), mod_s[3], mod_s[4], mod_s[5],
                                       out_wts, tm_s, 1)

        t_all = tp + ts
        idx, gate, pos, cnt = _router(jnp.concatenate([lg_p, lg_s], axis=1), router_bias[layer].reshape(-1, 1))
        counts = cnt[:, 0].astype(I32)
        padded = (counts + EXPERT_BLOCK - 1) // EXPERT_BLOCK * EXPERT_BLOCK
        end_padded = jnp.cumsum(padded)
        dest = _assign(idx, pos, (end_padded - padded).astype(F32).reshape(-1, 1))
        n_rows = (t_all * TOP_K + N_EXPERTS * (EXPERT_BLOCK - 1) + EXPERT_BLOCK - 1) // EXPERT_BLOCK * EXPERT_BLOCK
        n_used = (end_padded[-1:] // EXPERT_BLOCK).astype(I32)
        rows_out = _experts(jnp.concatenate([h2t_p, h2t_s], axis=0), _invert(dest, n_rows),
                            ((end_padded - padded) // EXPERT_BLOCK).astype(I32),
                            (padded // EXPERT_BLOCK).astype(I32), n_used, we_gate[layer], we_up[layer],
                            we_down[layer])
        gates_t = gate.T
        xp = _combine(base_p, mod_p[5], gates_t, dest, rows_out, 0, seq // ROUTE_TILE)
        xs = _combine(base_s, mod_s[5], gates_t, dest, rows_out, tp, 1)

        outs["kp"].append(jnp.transpose(kt_p.reshape(nb, DIFF_HEADS, 2, DIFF_DK, seq), (0, 4, 1, 2, 3)))
        outs["vp"].append(v4_p.reshape(nb, seq, DIFF_HEADS, DIFF_DV))
        outs["wp"].append(wkv_p)
        outs["sp"].append(last_p)
        outs["ks"].append(k_s.reshape(db, dseq, DIFF_HEADS, 2, DIFF_DK))
        outs["vs"].append(v_s.reshape(db, dseq, DIFF_HEADS, DIFF_DV))
        outs["ws"].append(wkv_s)
        outs["ss"].append(last_s)

    st = {name: jnp.stack(v) for name, v in outs.items()}
    return (xp.reshape(nb, seq, d), xs.reshape(db, dseq, d), st["kp"], st["vp"], st["wp"], st["sp"],
            st["ks"], st["vs"], st["ws"], st["ss"])
```

```python
import functools
import math

import jax
import jax.numpy as jnp
from jax import lax
from jax.experimental import pallas as pl
from jax.experimental.pallas import tpu as pltpu

F32 = jnp.float32
BF16 = jnp.bfloat16
I32 = jnp.int32

RWKV_HEADS = 8
RWKV_HEAD_DIM = 64
RWKV_WIDTH = RWKV_HEADS * RWKV_HEAD_DIM
DECAY_LORA = 32
ICLR_LORA = 32
GATE_LORA = 96
LORA_WIDTH = DECAY_LORA + ICLR_LORA + GATE_LORA
LORA_PAD = 256
RWKV_PROJ = 3 * RWKV_WIDTH + LORA_WIDTH
GN_EPS = 64e-5
DIFF_HEADS = 4
DIFF_DK = 64
DIFF_DV = 2 * DIFF_DK
DIFF_QK_WIDTH = DIFF_HEADS * 2 * DIFF_DK
DIFF_V_WIDTH = DIFF_HEADS * DIFF_DV
DIFF_SCALE = DIFF_DK ** -0.5
N_EXPERTS = 256
N_GROUPS = 8
GROUP_SIZE = N_EXPERTS // N_GROUPS
TOPK_GROUPS = 4
TOP_K = 8
ROUTE_SCALE = 2.5
NORM_EPS = 1e-6

LANES = 128
SUBLANES = 8
ROW_TILE = 8
VMEM_LIMIT = 48 * 1024 * 1024

EXPERT_BLOCK = 128
EXPERT_IN_SLOTS = 4
EXPERT_OUT_SLOTS = 3
EXPERT_DMA_SPLIT = 4
ROUTE_TILE = 128
ATTN_TILE = 2048
ATTN_ROW_CHUNK = 256
NEG_INF = float("-inf")


def _bdot(a, b):
    return jnp.dot(a.astype(BF16), b.astype(BF16), preferred_element_type=F32)


def _bdot_nt(a, b):
    return lax.dot_general(a.astype(BF16), b.astype(BF16), (((1,), (1,)), ((), ())), preferred_element_type=F32)


def _bdot_tn(a, b):
    return lax.dot_general(a.astype(BF16), b.astype(BF16), (((0,), (0,)), ((), ())), preferred_element_type=F32)


def _sigmoid(x):
    return 1.0 / (1.0 + jnp.exp(-x))


def _silu(x):
    return x * _sigmoid(x)


def _params(*sem, vmem=VMEM_LIMIT):
    return pltpu.CompilerParams(dimension_semantics=sem, vmem_limit_bytes=vmem)


def _ada_body(c_ref, w_ref, b_ref, o_ref):
    o_ref[...] = _bdot(_silu(c_ref[...]), w_ref[...]) + b_ref[...]


def _ada(c, w, b):
    rows, d = c.shape
    n = w.shape[1]
    tn = 512
    return pl.pallas_call(
        _ada_body,
        grid=(n // tn,),
        in_specs=[pl.BlockSpec((rows, d), lambda j: (0, 0)),
                  pl.BlockSpec((d, tn), lambda j: (0, j)),
                  pl.BlockSpec((1, tn), lambda j: (0, j))],
        out_specs=pl.BlockSpec((rows, tn), lambda j: (0, j)),
        out_shape=jax.ShapeDtypeStruct((rows, n), F32),
        compiler_params=_params("parallel"),
        name="ada",
    )(c, w, b)


def _mod_spec(mod, tm, tiles_per_seq):
    if mod.ndim == 3:
        return pl.BlockSpec((None, 1, mod.shape[-1]), lambda i: (i // tiles_per_seq, 0, 0))
    return pl.BlockSpec((tm, mod.shape[-1]), lambda i: (i, 0))


def _full_spec(a):
    nd = a.ndim
    return pl.BlockSpec(a.shape, lambda *_: (0,) * nd)


def _rms(x, g):
    return x * lax.rsqrt(jnp.mean(x * x, axis=-1, keepdims=True) + NORM_EPS) * g


def _inproj_body(x_ref, g_ref, sh_ref, sc_ref, wm_ref, wl_ref, wq_ref, wk_ref, wv_ref, qg_ref, kg_ref, seg_ref,
                 pm_ref, pl_ref, *out_refs, long_seq):
    tm = x_ref.shape[0]
    h = (_rms(x_ref[...], g_ref[...]) * (1.0 + sc_ref[...]) + sh_ref[...]).astype(BF16)
    pm_ref[...] = jnp.dot(h, wm_ref[...], preferred_element_type=F32)
    pl_ref[...] = jnp.dot(h, wl_ref[...], preferred_element_type=F32)
    seg = seg_ref[...]

    def head_norm(z, gain):
        ms = _bdot(z * z, seg) * (1.0 / DIFF_DK)
        return z * lax.rsqrt(ms + NORM_EPS) * gain

    q = head_norm(jnp.dot(h, wq_ref[...], preferred_element_type=F32), qg_ref[...])
    k = head_norm(jnp.dot(h, wk_ref[...], preferred_element_type=F32), kg_ref[...])
    v = jnp.dot(h, wv_ref[...], preferred_element_type=F32)
    if long_seq:
        kt_ref, v4_ref, qb_ref, vb_ref = out_refs
        kt_ref[...] = k.T
        for hd in range(DIFF_HEADS):
            v4_ref[pl.ds(hd, tm, stride=DIFF_HEADS), :] = v[:, hd * DIFF_DV:(hd + 1) * DIFF_DV]
        qb_ref[...] = (q * DIFF_SCALE).astype(BF16)
        vb_ref[...] = v.astype(BF16)
    else:
        q_ref, k_ref, v_ref = out_refs
        q_ref[...] = q
        k_ref[...] = k
        v_ref[...] = v


def _inproj(x, g, shift, scale, wts, tm, tiles_per_seq, long_seq):
    t, d = x.shape
    wm, wl, wq, wk, wv, qg, kg, seg = wts
    row = lambda n: pl.BlockSpec((tm, n), lambda i: (i, 0))
    out_specs = [row(3 * RWKV_WIDTH), row(LORA_PAD)]
    out_shape = [jax.ShapeDtypeStruct((t, 3 * RWKV_WIDTH), F32), jax.ShapeDtypeStruct((t, LORA_PAD), F32)]
    if long_seq:
        n_seq = t // (tm * tiles_per_seq)
        out_specs += [pl.BlockSpec((DIFF_QK_WIDTH, tm), lambda i: (i // tiles_per_seq, i % tiles_per_seq)),
                      pl.BlockSpec((tm * DIFF_HEADS, DIFF_DV), lambda i: (i, 0)),
                      row(DIFF_QK_WIDTH), row(DIFF_V_WIDTH)]
        out_shape += [jax.ShapeDtypeStruct((n_seq * DIFF_QK_WIDTH, tm * tiles_per_seq), F32),
                      jax.ShapeDtypeStruct((t * DIFF_HEADS, DIFF_DV), F32),
                      jax.ShapeDtypeStruct((t, DIFF_QK_WIDTH), BF16), jax.ShapeDtypeStruct((t, DIFF_V_WIDTH), BF16)]
    else:
        out_specs += [row(DIFF_QK_WIDTH), row(DIFF_QK_WIDTH), row(DIFF_V_WIDTH)]
        out_shape += [jax.ShapeDtypeStruct((t, n), F32) for n in (DIFF_QK_WIDTH, DIFF_QK_WIDTH, DIFF_V_WIDTH)]
    return pl.pallas_call(
        functools.partial(_inproj_body, long_seq=long_seq),
        grid=(t // tm,),
        in_specs=[row(d), _full_spec(g), _mod_spec(shift, tm, tiles_per_seq), _mod_spec(scale, tm, tiles_per_seq)]
                 + [_full_spec(a) for a in (wm, wl, wq, wk, wv, qg, kg, seg)],
        out_specs=out_specs,
        out_shape=out_shape,
        compiler_params=_params("parallel"),
        name="inproj",
    )(x, g, shift, scale, wm, wl, wq, wk, wv, qg, kg, seg)


def _split3(x):
    hi = x.astype(BF16)
    r1 = x - hi.astype(F32)
    mid = r1.astype(BF16)
    lo = (r1 - mid.astype(F32)).astype(BF16)
    return hi, mid, lo


def _rwkv_body(pm_ref, pl_ref, pm8_ref, pl8_ref, sm_ref, sl_ref, s0_ref,
               mum_ref, mul_ref, w0_ref, wupw_ref, a0_ref, wupa_ref, wupg_ref, kk_ref, ka_ref, rk_ref,
               lg_ref, lb_ref, seg_ref,
               y_ref, sout_ref, state_ref, *, chunk):
    c = pl.program_id(1)
    nc = pl.num_programs(1)
    C = chunk
    W = RWKV_WIDTH
    N = RWKV_HEAD_DIM

    @pl.when(c == 0)
    def _():
        state_ref[...] = s0_ref[...]

    first = c == 0
    prev_m = jnp.where(first, sm_ref[...], pm8_ref[SUBLANES - 1:SUBLANES, :])
    prev_l = jnp.where(first, sl_ref[...], pl8_ref[SUBLANES - 1:SUBLANES, :])

    def shifted(cur, prev_row):
        rows = lax.broadcasted_iota(I32, cur.shape, 0)
        return jnp.where(rows == 0, prev_row, pltpu.roll(cur, 1, 0))

    pm = pm_ref[...]
    plo = pl_ref[...]
    xm = pm + (shifted(pm, prev_m) - pm) * mum_ref[...]
    xl = plo + (shifted(plo, prev_l) - plo) * mul_ref[...]
    r = xm[:, 0:W]
    k = xm[:, W:2 * W]
    v = xm[:, 2 * W:3 * W]
    seg = seg_ref[...]

    z = -(w0_ref[...] + _bdot(jnp.tanh(xl), wupw_ref[...]))
    softplus = jnp.maximum(z, 0.0) + jnp.log(1.0 + jnp.exp(-jnp.abs(z)))
    w = -softplus - 0.5
    a = _sigmoid(a0_ref[...] + _bdot(xl, wupa_ref[...]))
    g = _bdot(_sigmoid(xl), wupg_ref[...])
    kk = k * kk_ref[...]
    kk = kk / jnp.maximum(jnp.sqrt(_bdot(kk * kk, seg)), 1e-12)
    k = k * (1.0 + (a - 1.0) * ka_ref[...])
    logdec = -jnp.exp(w)

    ti = lax.broadcasted_iota(I32, (C, C), 0)
    tj = lax.broadcasted_iota(I32, (C, C), 1)
    lower = (ti >= tj).astype(BF16)
    cum = sum(jnp.dot(lower, part, preferred_element_type=F32) for part in _split3(logdec))
    cum_end = cum[C - 1:C, :]
    a_t = -kk * jnp.exp(cum - logdec)
    r_t = r * jnp.exp(cum)
    inv = jnp.exp(-cum)
    b_t = kk * a * inv
    k_t = k * inv
    to_end = jnp.exp(cum_end - cum)
    b_e = kk * a * to_end
    k_e = k * to_end
    g_end = jnp.exp(cum_end)

    eye = (ti == tj).astype(F32)
    ri = lax.broadcasted_iota(I32, (2 * C, 2 * C), 0)
    ci = lax.broadcasted_iota(I32, (2 * C, 2 * C), 1)
    tr = jnp.where(ri >= C, ri - C, ri)
    tc = jnp.where(ci >= C, ci - C, ci)
    mask = jnp.logical_or(tr > tc, jnp.logical_and(ri >= C, tr == tc))
    zeros_cn = jnp.zeros((C, N), F32)
    levels = int(math.log2(C))
    heads = range(RWKV_HEADS)
    sls = [slice(h * N, (h + 1) * N) for h in heads]
    ah = [a_t[:, s] for s in sls]
    rh = [r_t[:, s] for s in sls]
    vh = [v[:, s] for s in sls]
    m_all = [jnp.where(mask, _bdot_nt(jnp.concatenate([ah[h], rh[h]], axis=0),
                                      jnp.concatenate([b_t[:, sls[h]], k_t[:, sls[h]]], axis=0)), 0.0)
             for h in heads]
    m_top = [m[0:C, :] for m in m_all]
    m_bot = [m[C:2 * C, :] for m in m_all]
    akv = [_bdot(m_top[h], jnp.concatenate([zeros_cn, vh[h]], axis=0)) for h in heads]
    power = [m[:, 0:C] for m in m_top]
    t_inv = [eye + p for p in power]
    for _ in range(levels - 1):
        power = [_bdot(p, p) for p in power]
        t_inv = [t + _bdot(t, p) for t, p in zip(t_inv, power)]
    w_mat = [_bdot(t_inv[h], ah[h]) for h in heads]
    u0 = [_bdot(t_inv[h], akv[h]) for h in heads]
    s0 = [state_ref[h] for h in heads]
    x = [_bdot_nt(jnp.concatenate([w_mat[h], rh[h]], axis=0), s0[h]) for h in heads]
    uv = [jnp.concatenate([x[h][0:C, :] + u0[h], vh[h]], axis=0) for h in heads]
    ys = [x[h][C:2 * C, :] + _bdot(m_bot[h], uv[h]) for h in heads]
    s_new = [s0[h] * g_end[:, sls[h]]
             + _bdot_tn(uv[h], jnp.concatenate([b_e[:, sls[h]], k_e[:, sls[h]]], axis=0)) for h in heads]
    for h in heads:
        state_ref[h] = s_new[h]
    y = jnp.concatenate(ys, axis=1)

    mean = _bdot(y, seg) * (1.0 / N)
    yc = y - mean
    var = _bdot(yc * yc, seg) * (1.0 / N)
    yn = yc * lax.rsqrt(var + GN_EPS) * lg_ref[...] + lb_ref[...]
    bonus = _bdot(r * k * rk_ref[...], seg) * v
    y_ref[...] = ((yn + bonus) * g).astype(y_ref.dtype)

    @pl.when(c == nc - 1)
    def _():
        sout_ref[...] = state_ref[...]


def _rwkv(pm, plo, shift_m, shift_l, s0, wts, seq_len, chunk):
    t_total = pm.shape[0]
    nb = t_total // seq_len
    ncnk = seq_len // chunk
    c8 = chunk // SUBLANES
    prev8 = lambda b, c: (jnp.maximum(b * (seq_len // SUBLANES) + c * c8 - 1, 0), 0)
    cur = lambda b, c: (b * ncnk + c, 0)
    per_seq3 = lambda n: pl.BlockSpec((None, 1, n), lambda b, c: (b, 0, 0))
    state_spec = pl.BlockSpec((None, RWKV_HEADS, RWKV_HEAD_DIM, RWKV_HEAD_DIM), lambda b, c: (b, 0, 0, 0))
    return pl.pallas_call(
        functools.partial(_rwkv_body, chunk=chunk),
        grid=(nb, ncnk),
        in_specs=[pl.BlockSpec((chunk, 3 * RWKV_WIDTH), cur), pl.BlockSpec((chunk, LORA_PAD), cur),
                  pl.BlockSpec((SUBLANES, 3 * RWKV_WIDTH), prev8), pl.BlockSpec((SUBLANES, LORA_PAD), prev8),
                  per_seq3(3 * RWKV_WIDTH), per_seq3(LORA_PAD), state_spec]
                 + [_full_spec(a) for a in wts],
        out_specs=[pl.BlockSpec((chunk, RWKV_WIDTH), cur), state_spec],
        out_shape=[jax.ShapeDtypeStruct((t_total, RWKV_WIDTH), BF16 if chunk % 16 == 0 else F32),
                   jax.ShapeDtypeStruct(s0.shape, F32)],
        scratch_shapes=[pltpu.VMEM((RWKV_HEADS, RWKV_HEAD_DIM, RWKV_HEAD_DIM), F32)],
        compiler_params=_params("parallel", "arbitrary"),
        name="rwkv",
    )(pm, plo, pm, plo, shift_m, shift_l, s0, *wts)


def _lambda(lam_ref, lam_init):
    lv = lam_ref[...]
    return (jnp.exp(jnp.sum(lv[0:1, :] * lv[1:2, :], axis=-1, keepdims=True))
            - jnp.exp(jnp.sum(lv[2:3, :] * lv[3:4, :], axis=-1, keepdims=True)) + lam_init)


def _subln(o, g, lam_init):
    return o * lax.rsqrt(jnp.mean(o * o, axis=-1, keepdims=True) + NORM_EPS) * g * (1.0 - lam_init)


def _attn_prompt_body(qi_ref, ki_ref, q_ref, kt_ref, v_ref, lam_ref, sg_ref, o_ref,
                      qm_ref, m_ref, acc_ref, *, lam_init):
    p = pl.program_id(2)
    qi = qi_ref[p]
    ki = ki_ref[p]
    tq = q_ref.shape[0]
    tk = kt_ref.shape[1]
    dv = v_ref.shape[1]

    @pl.when(ki == 0)
    def _():
        q = q_ref[...]
        lane = lax.broadcasted_iota(I32, q.shape, 1)
        qm_ref[0] = jnp.where(lane < DIFF_DK, q, jnp.zeros_like(q))
        qm_ref[1] = jnp.where(lane >= DIFF_DK, q, jnp.zeros_like(q))
        m_ref[...] = jnp.full(m_ref.shape, NEG_INF, F32)
        acc_ref[...] = jnp.zeros(acc_ref.shape, F32)

    def accumulate(masked):
        kt = kt_ref[...].astype(BF16)
        v1 = jnp.concatenate([v_ref[...], jnp.ones((tk, dv), BF16)], axis=1)
        for c in range(2):
            for r0 in range(0, tq, ATTN_ROW_CHUNK):
                rows = slice(r0, r0 + ATTN_ROW_CHUNK)
                ncol = min(tk, r0 + ATTN_ROW_CHUNK) if masked else tk
                s = jnp.dot(qm_ref[c, rows, :], kt[:, 0:ncol], preferred_element_type=F32)
                if masked:
                    row = lax.broadcasted_iota(I32, s.shape, 0) + r0
                    col = lax.broadcasted_iota(I32, s.shape, 1)
                    s = jnp.where(col <= row, s, NEG_INF)
                m_old = m_ref[c, rows, :]
                m_new = jnp.maximum(m_old, jnp.max(s, axis=-1, keepdims=True))
                alpha = jnp.exp(m_old - m_new)
                pr = jnp.exp(s - jnp.concatenate([m_new] * (ncol // LANES), axis=1))
                acc_ref[c, rows, :] = (jnp.concatenate([alpha] * (2 * dv // LANES), axis=1) * acc_ref[c, rows, :]
                                       + jnp.dot(pr.astype(BF16), v1[0:ncol, :], preferred_element_type=F32))
                m_ref[c, rows, :] = m_new

    @pl.when(ki < qi)
    def _():
        accumulate(False)

    @pl.when(ki == qi)
    def _():
        accumulate(True)
        lam = _lambda(lam_ref, lam_init)
        a0 = acc_ref[0]
        a1 = acc_ref[1]
        o = a0[:, 0:dv] / a0[:, dv:2 * dv] - lam * (a1[:, 0:dv] / a1[:, dv:2 * dv])
        o_ref[...] = _subln(o, sg_ref[...], lam_init).astype(o_ref.dtype)


def _attn_prompt(qb, kt, vb, lam_vecs, subln_g, seq_len, lam_init):
    t_total = qb.shape[0]
    nb = t_total // seq_len
    tq = min(ATTN_TILE, seq_len)
    nq = seq_len // tq
    pairs = [(i, j) for i in range(nq) for j in range(i + 1)]
    qi_tab = jnp.asarray([a for a, _ in pairs], I32)
    ki_tab = jnp.asarray([b for _, b in pairs], I32)
    qmap = lambda b, h, p, qi, ki: (b * nq + qi[p], h)
    vmap = lambda b, h, p, qi, ki: (b * nq + ki[p], h)
    ktmap = lambda b, h, p, qi, ki: (b * DIFF_HEADS + h, ki[p])
    grid_spec = pltpu.PrefetchScalarGridSpec(
        num_scalar_prefetch=2,
        grid=(nb, DIFF_HEADS, len(pairs)),
        in_specs=[pl.BlockSpec((tq, DIFF_DV), qmap), pl.BlockSpec((2 * DIFF_DK, tq), ktmap),
                  pl.BlockSpec((tq, DIFF_DV), vmap),
                  pl.BlockSpec(lam_vecs.shape, lambda *_: (0, 0)), pl.BlockSpec(subln_g.shape, lambda *_: (0, 0))],
        out_specs=pl.BlockSpec((tq, DIFF_DV), qmap),
        scratch_shapes=[pltpu.VMEM((2, tq, DIFF_DV), BF16), pltpu.VMEM((2, tq, LANES), F32),
                        pltpu.VMEM((2, tq, 2 * DIFF_DV), F32)],
    )
    return pl.pallas_call(
        functools.partial(_attn_prompt_body, lam_init=lam_init),
        grid_spec=grid_spec,
        out_shape=jax.ShapeDtypeStruct((t_total, DIFF_V_WIDTH), BF16),
        compiler_params=_params("parallel", "parallel", "arbitrary"),
        name="attn_prompt",
    )(qi_tab, ki_tab, qb, kt, vb, lam_vecs, subln_g)


def _attn_sample_body(pt_ref, q_ref, k_ref, v_ref, lam_ref, sg_ref, *rest, n_pages, lam_init):
    kp_refs = rest[:n_pages]
    vp_refs = rest[n_pages:2 * n_pages]
    o_ref = rest[2 * n_pages]
    s_new = q_ref.shape[0]
    n_maps = DIFF_HEADS * 2
    page = kp_refs[0].shape[1]
    nrow = n_maps * s_new

    def value_page(vr):
        return jnp.concatenate([vr[pl.ds(h, page, stride=DIFF_HEADS), :] for h in range(DIFF_HEADS)], axis=1)

    qt = jnp.concatenate([q_ref[...] * DIFF_SCALE] * (DIFF_HEADS * 2), axis=0)
    row = lax.broadcasted_iota(I32, qt.shape, 0)
    col = lax.broadcasted_iota(I32, qt.shape, 1)
    qbd = jnp.where(col // DIFF_DK == row // s_new, qt, 0.0).astype(BF16)
    pad = jnp.zeros((page - s_new, k_ref.shape[1]), F32)
    k_new = jnp.concatenate([k_ref[...], pad], axis=0)
    v_new = jnp.concatenate([v_ref[...], pad], axis=0)
    scores = [_bdot(qbd, kr[...]) for kr in kp_refs]
    s_n = _bdot_nt(qbd, k_new)
    rn = lax.broadcasted_iota(I32, s_n.shape, 0)
    cn = lax.broadcasted_iota(I32, s_n.shape, 1)
    scores.append(jnp.where(cn <= rn % s_new, s_n, NEG_INF))
    m = functools.reduce(jnp.maximum, [jnp.max(s, axis=-1, keepdims=True) for s in scores])
    values = [value_page(vr) for vr in vp_refs] + [v_new]
    l = jnp.zeros_like(m)
    acc = jnp.zeros((nrow, v_new.shape[1]), F32)
    for s, val in zip(scores, values):
        pr = jnp.exp(s - m)
        l = l + jnp.sum(pr, axis=-1, keepdims=True)
        acc = acc + _bdot(pr, val)
    full = acc / l
    lam = _lambda(lam_ref, lam_init)
    outs = []
    for h in range(DIFF_HEADS):
        r0 = h * 2 * s_new
        cols = slice(h * DIFF_DV, (h + 1) * DIFF_DV)
        o = full[r0:r0 + s_new, cols] - lam * full[r0 + s_new:r0 + 2 * s_new, cols]
        outs.append(_subln(o, sg_ref[...], lam_init))
    o_ref[...] = jnp.concatenate(outs, axis=1)


def _attn_sample(q, k, v, cache_k, cache_v, n_pool_pages, page_table, page_offset, lam_vecs, subln_g, s_new,
                 lam_init):
    nseq, n_pages = page_table.shape
    width = q.shape[1]
    k_rows = cache_k.shape[0] // n_pool_pages
    v_rows = cache_v.shape[0] // n_pool_pages
    new_spec = pl.BlockSpec((s_new, width), lambda b, pt: (b, 0))
    page_spec = lambda rows, lanes, j: pl.BlockSpec((rows, lanes), lambda b, pt: (pt[b, j] + page_offset, 0))
    grid_spec = pltpu.PrefetchScalarGridSpec(
        num_scalar_prefetch=1,
        grid=(nseq,),
        in_specs=[new_spec, new_spec, new_spec,
                  pl.BlockSpec(lam_vecs.shape, lambda *_: (0, 0)), pl.BlockSpec(subln_g.shape, lambda *_: (0, 0))]
                 + [page_spec(k_rows, cache_k.shape[1], j) for j in range(n_pages)]
                 + [page_spec(v_rows, DIFF_DV, j) for j in range(n_pages)],
        out_specs=new_spec,
    )
    return pl.pallas_call(
        functools.partial(_attn_sample_body, n_pages=n_pages, lam_init=lam_init),
        grid_spec=grid_spec,
        out_shape=jax.ShapeDtypeStruct((nseq * s_new, width), F32),
        compiler_params=_params("parallel"),
        name="attn_sample",
    )(page_table, q, k, v, lam_vecs, subln_g, *([cache_k] * n_pages), *([cache_v] * n_pages))


def _outproj_body(yr_ref, o_ref, x_ref, g1_ref, g2n_ref, sh_ref, sc_ref, g2_ref, woa_ref, wob_ref,
                  wsg_ref, wsu_ref, wsd_ref, wrh_ref, wrl_ref,
                  base_ref, h2t_ref, lg_ref):
    tm = x_ref.shape[0]
    mixed = (jnp.dot(yr_ref[...].astype(BF16), woa_ref[...], preferred_element_type=F32)
             + jnp.dot(o_ref[...].astype(BF16), wob_ref[...], preferred_element_type=F32))
    x1 = x_ref[...] + g1_ref[...] * mixed
    h2 = _rms(x1, g2n_ref[...]) * (1.0 + sc_ref[...]) + sh_ref[...]
    h2b = h2.astype(BF16)
    hidden = _silu(jnp.dot(h2b, wsg_ref[...], preferred_element_type=F32)) * jnp.dot(
        h2b, wsu_ref[...], preferred_element_type=F32)
    shared = jnp.dot(hidden.astype(BF16), wsd_ref[...], preferred_element_type=F32)
    base_ref[...] = x1 + g2_ref[...] * shared
    h2l = (h2 - h2b.astype(F32)).astype(BF16)
    nt = lambda a, b: lax.dot_general(a, b, (((1,), (1,)), ((), ())), preferred_element_type=F32)
    lg_ref[...] = nt(wrh_ref[...], h2b) + nt(wrl_ref[...], h2b) + nt(wrh_ref[...], h2l)
    for c in range(ROW_TILE):
        h2t_ref[pl.ds(c, tm, stride=ROW_TILE), :] = h2[:, c * LANES:(c + 1) * LANES]


def _outproj(yr, o, x, gate1, norm2_g, shift2, scale2, gate2, wts, tm, tiles_per_seq):
    t, d = x.shape
    row = lambda n: pl.BlockSpec((tm, n), lambda i: (i, 0))
    mod = lambda m: _mod_spec(m, tm, tiles_per_seq)
    return pl.pallas_call(
        _outproj_body,
        grid=(t // tm,),
        in_specs=[row(RWKV_WIDTH), row(DIFF_V_WIDTH), row(d), mod(gate1), _full_spec(norm2_g), mod(shift2),
                  mod(scale2), mod(gate2)] + [_full_spec(a) for a in wts],
        out_specs=[row(d), pl.BlockSpec((tm * ROW_TILE, LANES), lambda i: (i, 0)),
                   pl.BlockSpec((N_EXPERTS, tm), lambda i: (0, i))],
        out_shape=[jax.ShapeDtypeStruct((t, d), F32),
                   jax.ShapeDtypeStruct((t * ROW_TILE, LANES), F32),
                   jax.ShapeDtypeStruct((N_EXPERTS, t), F32)],
        compiler_params=_params("parallel"),
        name="outproj",
    )(yr, o, x, gate1, norm2_g, shift2, scale2, gate2, *wts)


def _router_body(lg_ref, bias_ref, idx_ref, gate_ref, pos_ref, cnt_ref, run_ref):
    i = pl.program_id(0)
    tm = lg_ref.shape[1]

    @pl.when(i == 0)
    def _():
        run_ref[...] = jnp.zeros(run_ref.shape, F32)

    scores = _sigmoid(lg_ref[...])
    biased = scores + bias_ref[...]
    erow = lax.broadcasted_iota(I32, (N_EXPERTS, tm), 0)
    grow = lax.broadcasted_iota(I32, (GROUP_SIZE, tm), 0)

    def first_argmax(x, rows, limit):
        mx = jnp.max(x, axis=0, keepdims=True)
        return mx, jnp.min(jnp.where(x == mx, rows, limit), axis=0, keepdims=True)

    group_scores = []
    for gidx in range(N_GROUPS):
        xg = biased[gidx * GROUP_SIZE:(gidx + 1) * GROUP_SIZE, :]
        m1, i1 = first_argmax(xg, grow, GROUP_SIZE)
        m2 = jnp.max(jnp.where(grow == i1, NEG_INF, xg), axis=0, keepdims=True)
        group_scores.append(m1 + m2)
    gs = jnp.concatenate(group_scores, axis=0)
    g8 = lax.broadcasted_iota(I32, (N_GROUPS, tm), 0)
    chosen = jnp.zeros((N_GROUPS, tm), I32)
    for _ in range(TOPK_GROUPS):
        _, gi = first_argmax(gs, g8, N_GROUPS)
        hit = g8 == gi
        chosen = jnp.where(hit, 1, chosen)
        gs = jnp.where(hit, NEG_INF, gs)
    cand = jnp.concatenate(
        [jnp.where(chosen[gidx:gidx + 1, :] > 0, biased[gidx * GROUP_SIZE:(gidx + 1) * GROUP_SIZE, :], NEG_INF)
         for gidx in range(N_GROUPS)], axis=0)

    idxs, raws = [], []
    onehot = jnp.zeros((N_EXPERTS, tm), F32)
    for _ in range(TOP_K):
        _, ei = first_argmax(cand, erow, N_EXPERTS)
        hit = erow == ei
        idxs.append(ei)
        raws.append(jnp.sum(jnp.where(hit, scores, 0.0), axis=0, keepdims=True))
        onehot = onehot + hit.astype(F32)
        cand = jnp.where(hit, NEG_INF, cand)
    raw = jnp.concatenate(raws, axis=0)
    gate_ref[...] = raw / jnp.sum(raw, axis=0, keepdims=True) * ROUTE_SCALE
    idx_ref[...] = jnp.concatenate(idxs, axis=0)

    ti = lax.broadcasted_iota(I32, (tm, tm), 0)
    tj = lax.broadcasted_iota(I32, (tm, tm), 1)
    oh = onehot.astype(BF16)
    before = jnp.dot(oh, (ti < tj).astype(BF16), preferred_element_type=F32) + run_ref[...]
    run_ref[...] = run_ref[...] + jnp.dot(oh, jnp.ones((tm, tm), BF16), preferred_element_type=F32)
    pos_ref[...] = jnp.concatenate(
        [jnp.sum(jnp.where(erow == ei, before, 0.0), axis=0, keepdims=True) for ei in idxs], axis=0).astype(I32)
    cnt_ref[...] = run_ref[...]


def _router(logits_t, bias_col):
    t = logits_t.shape[1]
    tm = ROUTE_TILE
    tok = pl.BlockSpec((TOP_K, tm), lambda i: (0, i))
    return pl.pallas_call(
        _router_body,
        grid=(t // tm,),
        in_specs=[pl.BlockSpec((N_EXPERTS, tm), lambda i: (0, i)), _full_spec(bias_col)],
        out_specs=[tok, tok, tok, pl.BlockSpec((N_EXPERTS, tm), lambda i: (0, 0))],
        out_shape=[jax.ShapeDtypeStruct((TOP_K, t), I32), jax.ShapeDtypeStruct((TOP_K, t), F32),
                   jax.ShapeDtypeStruct((TOP_K, t), I32), jax.ShapeDtypeStruct((N_EXPERTS, tm), F32)],
        scratch_shapes=[pltpu.VMEM((N_EXPERTS, tm), F32)],
        compiler_params=_params("arbitrary"),
        name="router",
    )(logits_t, bias_col)


def _assign_body(idx_ref, pos_ref, start_ref, dest_ref):
    tm = idx_ref.shape[1]
    erow = lax.broadcasted_iota(I32, (N_EXPERTS, tm), 0)
    start = start_ref[...]
    idx = idx_ref[...]
    first = jnp.concatenate(
        [jnp.sum(jnp.where(erow == idx[j:j + 1, :], start, 0.0), axis=0, keepdims=True) for j in range(TOP_K)],
        axis=0)
    dest_ref[...] = first.astype(I32) + pos_ref[...]


def _assign(idx, pos, start_col):
    t = idx.shape[1]
    tm = ROUTE_TILE
    tok = pl.BlockSpec((TOP_K, tm), lambda i: (0, i))
    return pl.pallas_call(
        _assign_body,
        grid=(t // tm,),
        in_specs=[tok, tok, _full_spec(start_col)],
        out_specs=tok,
        out_shape=jax.ShapeDtypeStruct((TOP_K, t), I32),
        compiler_params=_params("parallel"),
        name="assign",
    )(idx, pos, start_col)


def _row_copy(src, src_row, dst, dst_row, sem):
    return pltpu.make_async_copy(src.at[pl.ds(pl.multiple_of(src_row * ROW_TILE, ROW_TILE), ROW_TILE), :],
                                 dst.at[pl.ds(pl.multiple_of(dst_row * ROW_TILE, ROW_TILE), ROW_TILE), :], sem)


def _invert_body(dest_ref, zeros_ref, src_ref, table_ref, sem):
    i = pl.program_id(0)
    tm = dest_ref.shape[1]
    unroll = 4

    @pl.when(i == 0)
    def _():
        cp = pltpu.make_async_copy(zeros_ref, table_ref, sem)
        cp.start()
        cp.wait()

    def fill(t4, carry):
        for u in range(unroll):
            t = t4 * unroll + u
            for j in range(TOP_K):
                table_ref[dest_ref[j, t]] = i * tm + t
        return carry

    lax.fori_loop(0, tm // unroll, fill, 0)

    @pl.when(i == pl.num_programs(0) - 1)
    def _():
        cp = pltpu.make_async_copy(table_ref, src_ref, sem)
        cp.start()
        cp.wait()


def _invert(dest, n_rows):
    t = dest.shape[1]
    tm = ROUTE_TILE
    return pl.pallas_call(
        _invert_body,
        grid=(t // tm,),
        in_specs=[pl.BlockSpec((TOP_K, tm), lambda i: (0, i), memory_space=pltpu.SMEM),
                  pl.BlockSpec(memory_space=pl.ANY)],
        out_specs=pl.BlockSpec(memory_space=pl.ANY),
        out_shape=jax.ShapeDtypeStruct((n_rows,), I32),
        scratch_shapes=[pltpu.SMEM((n_rows,), I32), pltpu.SemaphoreType.DMA],
        compiler_params=_params("arbitrary"),
        name="invert",
    )(dest, jnp.zeros((n_rows,), I32))


def _expert_body(first_ref, count_ref, nu_ref, src_ref, xs_ref, wg_ref, wu_ref, wd_ref, y_ref,
                 xin_ref, yout_ref, wgb_ref, wub_ref, wdb_ref, in_sem, out_sem):
    e = pl.program_id(0)
    rows = EXPERT_BLOCK
    blk_rows = EXPERT_BLOCK * ROW_TILE
    nb = count_ref[e]
    b0 = first_ref[e]

    n_used = nu_ref[0]
    n_in = xin_ref.shape[0]
    n_out = yout_ref.shape[0]
    ahead = n_in - 1
    part = blk_rows // EXPERT_DMA_SPLIT

    def hbm_rows(g, k=0, n=blk_rows):
        return pl.ds(pl.multiple_of(g * blk_rows + k * part, part), n)

    def start_in(g):
        slot = g % n_in

        for r in range(rows):
            _row_copy(xs_ref, src_ref[g * rows + r], xin_ref.at[slot], r, in_sem.at[slot]).start()

    def wait_in(g):
        slot = g % n_in
        pltpu.make_async_copy(xs_ref.at[pl.ds(0, blk_rows), :], xin_ref.at[slot], in_sem.at[slot]).wait()

    def start_out(g):
        slot = g % n_out
        for k in range(EXPERT_DMA_SPLIT):
            pltpu.make_async_copy(yout_ref.at[slot, pl.ds(k * part, part), :], y_ref.at[hbm_rows(g, k, part), :],
                                  out_sem.at[slot]).start()

    def wait_out(g):
        slot = g % n_out
        pltpu.make_async_copy(yout_ref.at[slot], y_ref.at[hbm_rows(g), :], out_sem.at[slot]).wait()

    @pl.when(e == 0)
    def _():
        for g in range(ahead):
            @pl.when(g < n_used)
            def _():
                start_in(g)

    @pl.when(nb > 0)
    def _():
        wgb_ref[...] = wg_ref[...].astype(BF16)
        wub_ref[...] = wu_ref[...].astype(BF16)
        wdb_ref[...] = wd_ref[...].astype(BF16)

        def step(b, carry):
            g = b0 + b
            wait_in(g)

            @pl.when(g >= n_out)
            def _():
                wait_out(g - n_out)

            def compute():
                islot = g % n_in
                oslot = g % n_out
                x = jnp.concatenate([xin_ref[islot, pl.ds(c, rows, stride=ROW_TILE), :] for c in range(ROW_TILE)],
                                    axis=1).astype(BF16)
                hidden = _silu(jnp.dot(x, wgb_ref[...], preferred_element_type=F32)) * jnp.dot(
                    x, wub_ref[...], preferred_element_type=F32)
                y = jnp.dot(hidden.astype(BF16), wdb_ref[...], preferred_element_type=F32)
                for c in range(ROW_TILE):
                    yout_ref[oslot, pl.ds(c, rows, stride=ROW_TILE), :] = y[:, c * LANES:(c + 1) * LANES]
                start_out(g)

            @pl.when(g + ahead < n_used)
            def _():
                start_in(g + ahead)
                compute()

            @pl.when(g + ahead >= n_used)
            def _():
                compute()

            return carry

        lax.fori_loop(0, nb, step, 0)

    @pl.when(e == pl.num_programs(0) - 1)
    def _():
        def drain(g, carry):
            wait_out(g)
            return carry

        lax.fori_loop(jnp.maximum(n_used - n_out, 0), n_used, drain, 0)
        n_blocks = y_ref.shape[0] // blk_rows
        yout_ref[0] = jnp.zeros(yout_ref.shape[1:], F32)

        def tail(b, carry):
            cp = pltpu.make_async_copy(
                yout_ref.at[0], y_ref.at[pl.ds(pl.multiple_of(b * blk_rows, blk_rows), blk_rows), :], out_sem.at[0])
            cp.start()
            cp.wait()
            return carry

        lax.fori_loop(nu_ref[0], n_blocks, tail, 0)


def _experts(h2t, row_src, first_block, block_count, n_used, we_gate, we_up, we_down):
    d, f = we_gate.shape[1], we_gate.shape[2]
    blk = (EXPERT_BLOCK * ROW_TILE, LANES)
    wspec = lambda a, b: pl.BlockSpec((None, a, b), lambda e, *_: (e, 0, 0))
    grid_spec = pltpu.PrefetchScalarGridSpec(
        num_scalar_prefetch=4,
        grid=(N_EXPERTS,),
        in_specs=[pl.BlockSpec(memory_space=pl.ANY), wspec(d, f), wspec(d, f), wspec(f, d)],
        out_specs=pl.BlockSpec(memory_space=pl.ANY),
        scratch_shapes=[pltpu.VMEM((EXPERT_IN_SLOTS,) + blk, F32), pltpu.VMEM((EXPERT_OUT_SLOTS,) + blk, F32),
                        pltpu.VMEM((d, f), BF16), pltpu.VMEM((d, f), BF16), pltpu.VMEM((f, d), BF16),
                        pltpu.SemaphoreType.DMA((EXPERT_IN_SLOTS,)), pltpu.SemaphoreType.DMA((EXPERT_OUT_SLOTS,))],
    )
    return pl.pallas_call(
        _expert_body,
        grid_spec=grid_spec,
        out_shape=jax.ShapeDtypeStruct((row_src.shape[0] * ROW_TILE, LANES), F32),
        compiler_params=_params("arbitrary"),
        name="experts",
    )(first_block, block_count, n_used, row_src, h2t, we_gate, we_up, we_down)


def _combine_body(base_ref, g2_ref, gate_ref, dest_ref, y_ref, out_ref, buf_ref, sem):
    tm = base_ref.shape[0]
    slot_rows = tm * ROW_TILE

    def issue(t, carry):
        for j in range(TOP_K):
            _row_copy(y_ref, dest_ref[j, t], buf_ref, j * tm + t, sem).start()
        return carry

    lax.fori_loop(0, tm, issue, 0)
    pltpu.make_async_copy(y_ref.at[pl.ds(0, TOP_K * slot_rows), :], buf_ref, sem).wait()
    gates = gate_ref[...]
    gcols = [jnp.broadcast_to(gates[:, j:j + 1], (tm, LANES)) for j in range(TOP_K)]
    for c in range(ROW_TILE):
        cols = slice(c * LANES, (c + 1) * LANES)
        routed = sum(gcols[j] * buf_ref[pl.ds(j * slot_rows + c, tm, stride=ROW_TILE), :] for j in range(TOP_K))
        out_ref[:, cols] = base_ref[:, cols] + g2_ref[:, cols] * routed


def _combine(base, gate2, gates, dest, y_rows, tok_offset, tiles_per_seq):
    t, d = base.shape
    tm = ROUTE_TILE
    off = tok_offset // tm
    return pl.pallas_call(
        _combine_body,
        grid=(t // tm,),
        in_specs=[pl.BlockSpec((tm, d), lambda i: (i, 0)), _mod_spec(gate2, tm, tiles_per_seq),
                  pl.BlockSpec((tm, TOP_K), lambda i: (i + off, 0)),
                  pl.BlockSpec((TOP_K, tm), lambda i: (0, i + off), memory_space=pltpu.SMEM),
                  pl.BlockSpec(memory_space=pl.ANY)],
        out_specs=pl.BlockSpec((tm, d), lambda i: (i, 0)),
        out_shape=jax.ShapeDtypeStruct((t, d), F32),
        scratch_shapes=[pltpu.VMEM((TOP_K * tm * ROW_TILE, LANES), F32), pltpu.SemaphoreType.DMA],
        compiler_params=_params("arbitrary"),
        name="combine",
    )(base, gate2, gates, dest, y_rows)


def _segment_ones(width, seg):
    ids = jnp.arange(width) // seg
    return (ids[:, None] == ids[None, :]).astype(BF16)


def _pad_rows(w, start, total):
    return jnp.zeros((total, w.shape[1]), w.dtype).at[start:start + w.shape[0]].set(w)


def kernel(x_prompt, x_sample, cache_k, cache_v, state_wkv, state_shift, page_table, c_prompt, c_sample, w_ada, b_ada, norm1_g, norm2_g, w_in, mu_shift, w0, w_lora_up, a0, a_lora_up, g_lora_up, k_k, k_a, r_k, lnx_g, lnx_b, qn_g, kn_g, lam_q1, lam_k1, lam_q2, lam_k2, subln_g, w_out, w_router, router_bias, we_gate, we_up, we_down, ws_gate, ws_up, ws_down):
    depth = w_in.shape[0]
    nb, seq, d = x_prompt.shape
    db, dseq, _ = x_sample.shape
    n_pool, page = cache_k.shape[1], cache_k.shape[2]
    tp, ts = nb * seq, db * dseq
    w3 = 3 * RWKV_WIDTH
    row = lambda a: a.reshape(1, -1)
    seg512 = _segment_ones(RWKV_WIDTH, RWKV_HEAD_DIM)
    tm_p = min(256, seq)
    tm_s = min(256, ts)
    chunk_p = min(64, seq)

    xp = x_prompt.reshape(tp, d)
    xs = x_sample.reshape(ts, d)
    cache_k2 = jnp.transpose(cache_k, (0, 1, 3, 4, 5, 2)).reshape(depth * n_pool * DIFF_QK_WIDTH, page)
    cache_v2 = cache_v.reshape(depth * n_pool * page * DIFF_HEADS, DIFF_DV)
    n_cond = nb + db
    cond = jnp.concatenate([c_prompt, c_sample], axis=0)
    cond = jnp.pad(cond, ((0, -n_cond % SUBLANES), (0, 0)))

    outs = {name: [] for name in ("kp", "vp", "wp", "sp", "ks", "vs", "ws", "ss")}
    for layer in range(depth):
        lam_init = 0.8 - 0.6 * math.exp(-0.3 * layer)
        mods = _ada(cond, w_ada[layer], row(b_ada[layer]))
        mod_p = [m.reshape(nb, 1, d) for m in jnp.split(mods[:nb], 6, axis=-1)]
        mod_s = [jnp.repeat(m, dseq, axis=0) for m in jnp.split(mods[nb:n_cond], 6, axis=-1)]

        wi = w_in[layer]
        in_wts = (wi[:, :w3].astype(BF16),
                  jnp.pad(wi[:, w3:RWKV_PROJ], ((0, 0), (0, LORA_PAD - LORA_WIDTH))).astype(BF16),
                  wi[:, RWKV_PROJ:RWKV_PROJ + DIFF_QK_WIDTH].astype(BF16),
                  wi[:, RWKV_PROJ + DIFF_QK_WIDTH:RWKV_PROJ + 2 * DIFF_QK_WIDTH].astype(BF16),
                  wi[:, RWKV_PROJ + 2 * DIFF_QK_WIDTH:].astype(BF16),
                  row(jnp.tile(qn_g[layer], DIFF_QK_WIDTH // DIFF_DK)),
                  row(jnp.tile(kn_g[layer], DIFF_QK_WIDTH // DIFF_DK)),
                  seg512)
        mu = mu_shift[layer]
        rwkv_wts = (row(mu[:w3]), row(jnp.pad(mu[w3:], (0, LORA_PAD - LORA_WIDTH))), row(w0[layer]),
                    _pad_rows(w_lora_up[layer], 0, LORA_PAD).astype(BF16), row(a0[layer]),
                    _pad_rows(a_lora_up[layer], DECAY_LORA, LORA_PAD).astype(BF16),
                    _pad_rows(g_lora_up[layer], DECAY_LORA + ICLR_LORA, LORA_PAD).astype(BF16),
                    row(k_k[layer]), row(k_a[layer]), row(r_k[layer]), row(lnx_g[layer]), row(lnx_b[layer]), seg512)
        lam_vecs = jnp.stack([lam_q1[layer], lam_k1[layer], lam_q2[layer], lam_k2[layer]])
        sg = row(subln_g[layer])
        wr = w_router[layer].T
        wr_hi = wr.astype(BF16)
        out_wts = (w_out[layer][:RWKV_WIDTH].astype(BF16), w_out[layer][RWKV_WIDTH:].astype(BF16),
                   ws_gate[layer].astype(BF16), ws_up[layer].astype(BF16), ws_down[layer].astype(BF16),
                   wr_hi, (wr - wr_hi.astype(F32)).astype(BF16))

        def token_mix(x, mod, tm, tiles_per_seq, seq_len, chunk, shift_prev, wkv0, long_seq):
            pm, plo, *qkv = _inproj(x, row(norm1_g[layer]), mod[0], mod[1], in_wts, tm, tiles_per_seq, long_seq)
            n_seq = x.shape[0] // seq_len
            shift_m = shift_prev[:, :, :w3]
            shift_l = jnp.pad(shift_prev[:, :, w3:], ((0, 0), (0, 0), (0, LORA_PAD - LORA_WIDTH)))
            yr, wkv = _rwkv(pm, plo, shift_m, shift_l, wkv0, rwkv_wts, seq_len, chunk)
            last = jnp.concatenate([pm.reshape(n_seq, seq_len, w3)[:, -1:],
                                    plo.reshape(n_seq, seq_len, LORA_PAD)[:, -1:, :LORA_WIDTH]], axis=-1)
            return yr, wkv, last, qkv

        yr_p, wkv_p, last_p, (kt_p, v4_p, qb_p, vb_p) = token_mix(
            xp, mod_p, tm_p, seq // tm_p, seq, chunk_p,
            jnp.zeros((nb, 1, RWKV_PROJ), F32), jnp.zeros((nb, RWKV_HEADS, RWKV_HEAD_DIM, RWKV_HEAD_DIM), F32), True)
        o_p = _attn_prompt(qb_p, kt_p, vb_p, lam_vecs, sg, seq, lam_init)
        yr_s, wkv_s, last_s, (q_s, k_s, v_s) = token_mix(
            xs, mod_s, tm_s, 1, dseq, dseq, state_shift[layer], state_wkv[layer], False)
        o_s = _attn_sample(q_s, k_s, v_s, cache_k2, cache_v2, depth * n_pool, page_table, layer * n_pool, lam_vecs,
                           sg, dseq, lam_init)

        base_p, h2t_p, lg_p = _outproj(yr_p, o_p, xp, mod_p[2], row(norm2_g[layer]), mod_p[3], mod_p[4], mod_p[5],
                                       out_wts, tm_p, seq // tm_p)
        base_s, h2t_s, lg_s = _outproj(yr_s, o_s, xs, mod_s[2], row(norm2_g[layer]), mod_s[3], mod_s[4], mod_s[5],
                                       out_wts, tm_s, 1)

        t_all = tp + ts
        idx, gate, pos, cnt = _router(jnp.concatenate([lg_p, lg_s], axis=1), router_bias[layer].reshape(-1, 1))
        counts = cnt[:, 0].astype(I32)
        padded = (counts + EXPERT_BLOCK - 1) // EXPERT_BLOCK * EXPERT_BLOCK
        end_padded = jnp.cumsum(padded)
        dest = _assign(idx, pos, (end_padded - padded).astype(F32).reshape(-1, 1))
        n_rows = (t_all * TOP_K + N_EXPERTS * (EXPERT_BLOCK - 1) + EXPERT_BLOCK - 1) // EXPERT_BLOCK * EXPERT_BLOCK
        n_used = (end_padded[-1:] // EXPERT_BLOCK).astype(I32)
        rows_out = _experts(jnp.concatenate([h2t_p, h2t_s], axis=0), _invert(dest, n_rows),
                            ((end_padded - padded) // EXPERT_BLOCK).astype(I32),
                            (padded // EXPERT_BLOCK).astype(I32), n_used, we_gate[layer], we_up[layer],
                            we_down[layer])
        gates_t = gate.T
        xp = _combine(base_p, mod_p[5], gates_t, dest, rows_out, 0, seq // ROUTE_TILE)
        xs = _combine(base_s, mod_s[5], gates_t, dest, rows_out, tp, 1)

        outs["kp"].append(jnp.transpose(kt_p.reshape(nb, DIFF_HEADS, 2, DIFF_DK, seq), (0, 4, 1, 2, 3)))
        outs["vp"].append(v4_p.reshape(nb, seq, DIFF_HEADS, DIFF_DV))
        outs["wp"].append(wkv_p)
        outs["sp"].append(last_p)
        outs["ks"].append(k_s.reshape(db, dseq, DIFF_HEADS, 2, DIFF_DK))
        outs["vs"].append(v_s.reshape(db, dseq, DIFF_HEADS, DIFF_DV))
        outs["ws"].append(wkv_s)
        outs["ss"].append(last_s)

    st = {name: jnp.stack(v) for name, v in outs.items()}
    return (xp.reshape(nb, seq, d), xs.reshape(db, dseq, d), st["kp"], st["vp"], st["wp"], st["sp"],
            st["ks"], st["vs"], st["ws"], st["ss"])
```

```python
import functools
import math

import jax
import jax.numpy as jnp
from jax import lax
from jax.experimental import pallas as pl
from jax.experimental.pallas import tpu as pltpu

F32 = jnp.float32
BF16 = jnp.bfloat16
I32 = jnp.int32

RWKV_HEADS = 8
RWKV_HEAD_DIM = 64
RWKV_WIDTH = RWKV_HEADS * RWKV_HEAD_DIM
DECAY_LORA = 32
ICLR_LORA = 32
GATE_LORA = 96
LORA_WIDTH = DECAY_LORA + ICLR_LORA + GATE_LORA
LORA_PAD = 256
RWKV_PROJ = 3 * RWKV_WIDTH + LORA_WIDTH
GN_EPS = 64e-5
DIFF_HEADS = 4
DIFF_DK = 64
DIFF_DV = 2 * DIFF_DK
DIFF_QK_WIDTH = DIFF_HEADS * 2 * DIFF_DK
DIFF_V_WIDTH = DIFF_HEADS * DIFF_DV
DIFF_SCALE = DIFF_DK ** -0.5
N_EXPERTS = 256
N_GROUPS = 8
GROUP_SIZE = N_EXPERTS // N_GROUPS
TOPK_GROUPS = 4
TOP_K = 8
ROUTE_SCALE = 2.5
NORM_EPS = 1e-6

LANES = 128
SUBLANES = 8
ROW_TILE = 8
VMEM_LIMIT = 48 * 1024 * 1024

EXPERT_BLOCK = 128
EXPERT_IN_SLOTS = 4
EXPERT_OUT_SLOTS = 3
EXPERT_DMA_SPLIT = 4
ROUTE_TILE = 128
RWKV_CHUNKS_PER_STEP = 4
ATTN_TILE = 2048
ATTN_ROW_CHUNK = 256
NEG_INF = float("-inf")


def _bdot(a, b):
    return jnp.dot(a.astype(BF16), b.astype(BF16), preferred_element_type=F32)


def _bdot_nt(a, b):
    return lax.dot_general(a.astype(BF16), b.astype(BF16), (((1,), (1,)), ((), ())), preferred_element_type=F32)


def _bdot_tn(a, b):
    return lax.dot_general(a.astype(BF16), b.astype(BF16), (((0,), (0,)), ((), ())), preferred_element_type=F32)


def _sigmoid(x):
    return 1.0 / (1.0 + jnp.exp(-x))


def _silu(x):
    return x * _sigmoid(x)


def _params(*sem, vmem=VMEM_LIMIT):
    return pltpu.CompilerParams(dimension_semantics=sem, vmem_limit_bytes=vmem)


def _ada_body(c_ref, w_ref, b_ref, o_ref):
    o_ref[...] = _bdot(_silu(c_ref[...]), w_ref[...]) + b_ref[...]


def _ada(c, w, b):
    rows, d = c.shape
    n = w.shape[1]
    tn = 512
    return pl.pallas_call(
        _ada_body,
        grid=(n // tn,),
        in_specs=[pl.BlockSpec((rows, d), lambda j: (0, 0)),
                  pl.BlockSpec((d, tn), lambda j: (0, j)),
                  pl.BlockSpec((1, tn), lambda j: (0, j))],
        out_specs=pl.BlockSpec((rows, tn), lambda j: (0, j)),
        out_shape=jax.ShapeDtypeStruct((rows, n), F32),
        compiler_params=_params("parallel"),
        name="ada",
    )(c, w, b)


def _mod_spec(mod, tm, tiles_per_seq):
    if mod.ndim == 3:
        return pl.BlockSpec((None, 1, mod.shape[-1]), lambda i: (i // tiles_per_seq, 0, 0))
    return pl.BlockSpec((tm, mod.shape[-1]), lambda i: (i, 0))


def _full_spec(a):
    nd = a.ndim
    return pl.BlockSpec(a.shape, lambda *_: (0,) * nd)


def _rms(x, g):
    return x * lax.rsqrt(jnp.mean(x * x, axis=-1, keepdims=True) + NORM_EPS) * g


def _inproj_body(x_ref, g_ref, sh_ref, sc_ref, wm_ref, wl_ref, wq_ref, wk_ref, wv_ref, qg_ref, kg_ref, seg_ref,
                 pm_ref, pl_ref, *out_refs, long_seq):
    tm = x_ref.shape[0]
    h = (_rms(x_ref[...], g_ref[...]) * (1.0 + sc_ref[...]) + sh_ref[...]).astype(BF16)
    pm_ref[...] = jnp.dot(h, wm_ref[...], preferred_element_type=F32)
    pl_ref[...] = jnp.dot(h, wl_ref[...], preferred_element_type=F32)
    seg = seg_ref[...]

    def head_norm(z, gain):
        ms = _bdot(z * z, seg) * (1.0 / DIFF_DK)
        return z * lax.rsqrt(ms + NORM_EPS) * gain

    q = head_norm(jnp.dot(h, wq_ref[...], preferred_element_type=F32), qg_ref[...])
    k = head_norm(jnp.dot(h, wk_ref[...], preferred_element_type=F32), kg_ref[...])
    v = jnp.dot(h, wv_ref[...], preferred_element_type=F32)
    if long_seq:
        kt_ref, v4_ref, qb_ref, vb_ref = out_refs
        kt_ref[...] = k.T
        for hd in range(DIFF_HEADS):
            v4_ref[pl.ds(hd, tm, stride=DIFF_HEADS), :] = v[:, hd * DIFF_DV:(hd + 1) * DIFF_DV]
        qb_ref[...] = (q * DIFF_SCALE).astype(BF16)
        vb_ref[...] = v.astype(BF16)
    else:
        q_ref, k_ref, v_ref = out_refs
        q_ref[...] = q
        k_ref[...] = k
        v_ref[...] = v


def _inproj(x, g, shift, scale, wts, tm, tiles_per_seq, long_seq):
    t, d = x.shape
    wm, wl, wq, wk, wv, qg, kg, seg = wts
    row = lambda n: pl.BlockSpec((tm, n), lambda i: (i, 0))
    out_specs = [row(3 * RWKV_WIDTH), row(LORA_PAD)]
    out_shape = [jax.ShapeDtypeStruct((t, 3 * RWKV_WIDTH), F32), jax.ShapeDtypeStruct((t, LORA_PAD), F32)]
    if long_seq:
        n_seq = t // (tm * tiles_per_seq)
        out_specs += [pl.BlockSpec((DIFF_QK_WIDTH, tm), lambda i: (i // tiles_per_seq, i % tiles_per_seq)),
                      pl.BlockSpec((tm * DIFF_HEADS, DIFF_DV), lambda i: (i, 0)),
                      row(DIFF_QK_WIDTH), row(DIFF_V_WIDTH)]
        out_shape += [jax.ShapeDtypeStruct((n_seq * DIFF_QK_WIDTH, tm * tiles_per_seq), F32),
                      jax.ShapeDtypeStruct((t * DIFF_HEADS, DIFF_DV), F32),
                      jax.ShapeDtypeStruct((t, DIFF_QK_WIDTH), BF16), jax.ShapeDtypeStruct((t, DIFF_V_WIDTH), BF16)]
    else:
        out_specs += [row(DIFF_QK_WIDTH), row(DIFF_QK_WIDTH), row(DIFF_V_WIDTH)]
        out_shape += [jax.ShapeDtypeStruct((t, n), F32) for n in (DIFF_QK_WIDTH, DIFF_QK_WIDTH, DIFF_V_WIDTH)]
    return pl.pallas_call(
        functools.partial(_inproj_body, long_seq=long_seq),
        grid=(t // tm,),
        in_specs=[row(d), _full_spec(g), _mod_spec(shift, tm, tiles_per_seq), _mod_spec(scale, tm, tiles_per_seq)]
                 + [_full_spec(a) for a in (wm, wl, wq, wk, wv, qg, kg, seg)],
        out_specs=out_specs,
        out_shape=out_shape,
        compiler_params=_params("parallel"),
        name="inproj",
    )(x, g, shift, scale, wm, wl, wq, wk, wv, qg, kg, seg)


def _split3(x):
    hi = x.astype(BF16)
    r1 = x - hi.astype(F32)
    mid = r1.astype(BF16)
    lo = (r1 - mid.astype(F32)).astype(BF16)
    return hi, mid, lo


def _rwkv_body(pm_ref, pl_ref, pm8_ref, pl8_ref, sm_ref, sl_ref, s0_ref,
               mum_ref, mul_ref, w0_ref, wupw_ref, a0_ref, wupa_ref, wupg_ref, kk_ref, ka_ref, rk_ref,
               lg_ref, lb_ref, seg_ref,
               y_ref, sout_ref, state_ref, *, chunk):
    c = pl.program_id(1)
    nc = pl.num_programs(1)
    C = chunk
    W = RWKV_WIDTH
    N = RWKV_HEAD_DIM

    @pl.when(c == 0)
    def _():
        state_ref[...] = s0_ref[...]

    first = c == 0
    prev_m = jnp.where(first, sm_ref[...], pm8_ref[SUBLANES - 1:SUBLANES, :])
    prev_l = jnp.where(first, sl_ref[...], pl8_ref[SUBLANES - 1:SUBLANES, :])

    def shifted(cur, prev_row):
        rows = lax.broadcasted_iota(I32, cur.shape, 0)
        return jnp.where(rows == 0, prev_row, pltpu.roll(cur, 1, 0))

    pm = pm_ref[...]
    plo = pl_ref[...]
    xm = pm + (shifted(pm, prev_m) - pm) * mum_ref[...]
    xl = plo + (shifted(plo, prev_l) - plo) * mul_ref[...]
    r = xm[:, 0:W]
    k = xm[:, W:2 * W]
    v = xm[:, 2 * W:3 * W]
    seg = seg_ref[...]

    z = -(w0_ref[...] + _bdot(jnp.tanh(xl), wupw_ref[...]))
    softplus = jnp.maximum(z, 0.0) + jnp.log(1.0 + jnp.exp(-jnp.abs(z)))
    w = -softplus - 0.5
    a = _sigmoid(a0_ref[...] + _bdot(xl, wupa_ref[...]))
    g = _bdot(_sigmoid(xl), wupg_ref[...])
    kk = k * kk_ref[...]
    kk = kk / jnp.maximum(jnp.sqrt(_bdot(kk * kk, seg)), 1e-12)
    k = k * (1.0 + (a - 1.0) * ka_ref[...])
    logdec = -jnp.exp(w)

    rows_blk = pm.shape[0]
    n_sub = rows_blk // C
    bi = lax.broadcasted_iota(I32, (rows_blk, rows_blk), 0)
    bj = lax.broadcasted_iota(I32, (rows_blk, rows_blk), 1)
    lower = jnp.logical_and(bi >= bj, bi // C == bj // C).astype(BF16)
    cum = sum(jnp.dot(lower, part, preferred_element_type=F32) for part in _split3(logdec))
    cum_end = jnp.concatenate(
        [jnp.broadcast_to(cum[(s + 1) * C - 1:(s + 1) * C, :], (C, W)) for s in range(n_sub)], axis=0)
    a_t = -kk * jnp.exp(cum - logdec)
    r_t = r * jnp.exp(cum)
    inv = jnp.exp(-cum)
    b_t = kk * a * inv
    k_t = k * inv
    to_end = jnp.exp(cum_end - cum)
    b_e = kk * a * to_end
    k_e = k * to_end
    g_end = jnp.exp(cum_end)

    ti = lax.broadcasted_iota(I32, (C, C), 0)
    tj = lax.broadcasted_iota(I32, (C, C), 1)
    eye = (ti == tj).astype(F32)
    ri = lax.broadcasted_iota(I32, (2 * C, 2 * C), 0)
    ci = lax.broadcasted_iota(I32, (2 * C, 2 * C), 1)
    tr = jnp.where(ri >= C, ri - C, ri)
    tc = jnp.where(ci >= C, ci - C, ci)
    mask = jnp.logical_or(tr > tc, jnp.logical_and(ri >= C, tr == tc))
    zeros_cn = jnp.zeros((C, N), F32)
    levels = int(math.log2(C))
    heads = range(RWKV_HEADS)
    pairs = [(s, h) for s in range(n_sub) for h in heads]
    tile = lambda z, s, h: z[s * C:(s + 1) * C, h * N:(h + 1) * N]
    ah = [tile(a_t, s, h) for s, h in pairs]
    rh = [tile(r_t, s, h) for s, h in pairs]
    vh = [tile(v, s, h) for s, h in pairs]
    m_all = [jnp.where(mask, _bdot_nt(jnp.concatenate([ah[i], rh[i]], axis=0),
                                      jnp.concatenate([tile(b_t, s, h), tile(k_t, s, h)], axis=0)), 0.0)
             for i, (s, h) in enumerate(pairs)]
    m_top = [m[0:C, :] for m in m_all]
    m_bot = [m[C:2 * C, :] for m in m_all]
    akv = [_bdot(m_top[i], jnp.concatenate([zeros_cn, vh[i]], axis=0)) for i in range(len(pairs))]
    power = [m[:, 0:C] for m in m_top]
    t_inv = [eye + p for p in power]
    for _ in range(levels - 1):
        power = [_bdot(p, p) for p in power]
        t_inv = [t + _bdot(t, p) for t, p in zip(t_inv, power)]
    w_mat = [_bdot(t_inv[i], ah[i]) for i in range(len(pairs))]
    u0 = [_bdot(t_inv[i], akv[i]) for i in range(len(pairs))]
    state = [state_ref[h] for h in heads]
    y_rows = []
    for s in range(n_sub):
        at = lambda h: s * RWKV_HEADS + h
        x = [_bdot_nt(jnp.concatenate([w_mat[at(h)], rh[at(h)]], axis=0), state[h]) for h in heads]
        uv = [jnp.concatenate([x[h][0:C, :] + u0[at(h)], vh[at(h)]], axis=0) for h in heads]
        y_rows.append(jnp.concatenate([x[h][C:2 * C, :] + _bdot(m_bot[at(h)], uv[h]) for h in heads], axis=1))
        state = [state[h] * tile(g_end, s, h)[0:1, :]
                 + _bdot_tn(uv[h], jnp.concatenate([tile(b_e, s, h), tile(k_e, s, h)], axis=0)) for h in heads]
    for h in heads:
        state_ref[h] = state[h]
    y = jnp.concatenate(y_rows, axis=0)

    mean = _bdot(y, seg) * (1.0 / N)
    yc = y - mean
    var = _bdot(yc * yc, seg) * (1.0 / N)
    yn = yc * lax.rsqrt(var + GN_EPS) * lg_ref[...] + lb_ref[...]
    bonus = _bdot(r * k * rk_ref[...], seg) * v
    y_ref[...] = ((yn + bonus) * g).astype(y_ref.dtype)

    @pl.when(c == nc - 1)
    def _():
        sout_ref[...] = state_ref[...]


def _rwkv(pm, plo, shift_m, shift_l, s0, wts, seq_len, chunk, chunks_per_step):
    t_total = pm.shape[0]
    nb = t_total // seq_len
    rows = chunk * chunks_per_step
    ncnk = seq_len // rows
    c8 = rows // SUBLANES
    prev8 = lambda b, c: (jnp.maximum(b * (seq_len // SUBLANES) + c * c8 - 1, 0), 0)
    cur = lambda b, c: (b * ncnk + c, 0)
    per_seq3 = lambda n: pl.BlockSpec((None, 1, n), lambda b, c: (b, 0, 0))
    state_spec = pl.BlockSpec((None, RWKV_HEADS, RWKV_HEAD_DIM, RWKV_HEAD_DIM), lambda b, c: (b, 0, 0, 0))
    return pl.pallas_call(
        functools.partial(_rwkv_body, chunk=chunk),
        grid=(nb, ncnk),
        in_specs=[pl.BlockSpec((rows, 3 * RWKV_WIDTH), cur), pl.BlockSpec((rows, LORA_PAD), cur),
                  pl.BlockSpec((SUBLANES, 3 * RWKV_WIDTH), prev8), pl.BlockSpec((SUBLANES, LORA_PAD), prev8),
                  per_seq3(3 * RWKV_WIDTH), per_seq3(LORA_PAD), state_spec]
                 + [_full_spec(a) for a in wts],
        out_specs=[pl.BlockSpec((rows, RWKV_WIDTH), cur), state_spec],
        out_shape=[jax.ShapeDtypeStruct((t_total, RWKV_WIDTH), BF16 if chunk % 16 == 0 else F32),
                   jax.ShapeDtypeStruct(s0.shape, F32)],
        scratch_shapes=[pltpu.VMEM((RWKV_HEADS, RWKV_HEAD_DIM, RWKV_HEAD_DIM), F32)],
        compiler_params=_params("parallel", "arbitrary"),
        name="rwkv",
    )(pm, plo, pm, plo, shift_m, shift_l, s0, *wts)


def _lambda(lam_ref, lam_init):
    lv = lam_ref[...]
    return (jnp.exp(jnp.sum(lv[0:1, :] * lv[1:2, :], axis=-1, keepdims=True))
            - jnp.exp(jnp.sum(lv[2:3, :] * lv[3:4, :], axis=-1, keepdims=True)) + lam_init)


def _subln(o, g, lam_init):
    return o * lax.rsqrt(jnp.mean(o * o, axis=-1, keepdims=True) + NORM_EPS) * g * (1.0 - lam_init)


def _attn_prompt_body(qi_ref, ki_ref, q_ref, kt_ref, v_ref, lam_ref, sg_ref, o_ref,
                      qm_ref, m_ref, acc_ref, *, lam_init):
    p = pl.program_id(2)
    qi = qi_ref[p]
    ki = ki_ref[p]
    tq = q_ref.shape[0]
    tk = kt_ref.shape[1]
    dv = v_ref.shape[1]

    @pl.when(ki == 0)
    def _():
        q = q_ref[...]
        lane = lax.broadcasted_iota(I32, q.shape, 1)
        qm_ref[0] = jnp.where(lane < DIFF_DK, q, jnp.zeros_like(q))
        qm_ref[1] = jnp.where(lane >= DIFF_DK, q, jnp.zeros_like(q))
        m_ref[...] = jnp.full(m_ref.shape, NEG_INF, F32)
        acc_ref[...] = jnp.zeros(acc_ref.shape, F32)

    def accumulate(masked):
        kt = kt_ref[...].astype(BF16)
        v1 = jnp.concatenate([v_ref[...], jnp.ones((tk, dv), BF16)], axis=1)
        for c in range(2):
            for r0 in range(0, tq, ATTN_ROW_CHUNK):
                rows = slice(r0, r0 + ATTN_ROW_CHUNK)
                ncol = min(tk, r0 + ATTN_ROW_CHUNK) if masked else tk
                s = jnp.dot(qm_ref[c, rows, :], kt[:, 0:ncol], preferred_element_type=F32)
                if masked:
                    row = lax.broadcasted_iota(I32, s.shape, 0) + r0
                    col = lax.broadcasted_iota(I32, s.shape, 1)
                    s = jnp.where(col <= row, s, NEG_INF)
                m_old = m_ref[c, rows, :]
                m_new = jnp.maximum(m_old, jnp.max(s, axis=-1, keepdims=True))
                alpha = jnp.exp(m_old - m_new)
                pr = jnp.exp(s - jnp.concatenate([m_new] * (ncol // LANES), axis=1))
                acc_ref[c, rows, :] = (jnp.concatenate([alpha] * (2 * dv // LANES), axis=1) * acc_ref[c, rows, :]
                                       + jnp.dot(pr.astype(BF16), v1[0:ncol, :], preferred_element_type=F32))
                m_ref[c, rows, :] = m_new

    @pl.when(ki < qi)
    def _():
        accumulate(False)

    @pl.when(ki == qi)
    def _():
        accumulate(True)
        lam = _lambda(lam_ref, lam_init)
        a0 = acc_ref[0]
        a1 = acc_ref[1]
        o = a0[:, 0:dv] / a0[:, dv:2 * dv] - lam * (a1[:, 0:dv] / a1[:, dv:2 * dv])
        o_ref[...] = _subln(o, sg_ref[...], lam_init).astype(o_ref.dtype)


def _attn_prompt(qb, kt, vb, lam_vecs, subln_g, seq_len, lam_init):
    t_total = qb.shape[0]
    nb = t_total // seq_len
    tq = min(ATTN_TILE, seq_len)
    nq = seq_len // tq
    pairs = [(i, j) for i in range(nq) for j in range(i + 1)]
    qi_tab = jnp.asarray([a for a, _ in pairs], I32)
    ki_tab = jnp.asarray([b for _, b in pairs], I32)
    qmap = lambda b, h, p, qi, ki: (b * nq + qi[p], h)
    vmap = lambda b, h, p, qi, ki: (b * nq + ki[p], h)
    ktmap = lambda b, h, p, qi, ki: (b * DIFF_HEADS + h, ki[p])
    grid_spec = pltpu.PrefetchScalarGridSpec(
        num_scalar_prefetch=2,
        grid=(nb, DIFF_HEADS, len(pairs)),
        in_specs=[pl.BlockSpec((tq, DIFF_DV), qmap), pl.BlockSpec((2 * DIFF_DK, tq), ktmap),
                  pl.BlockSpec((tq, DIFF_DV), vmap),
                  pl.BlockSpec(lam_vecs.shape, lambda *_: (0, 0)), pl.BlockSpec(subln_g.shape, lambda *_: (0, 0))],
        out_specs=pl.BlockSpec((tq, DIFF_DV), qmap),
        scratch_shapes=[pltpu.VMEM((2, tq, DIFF_DV), BF16), pltpu.VMEM((2, tq, LANES), F32),
                        pltpu.VMEM((2, tq, 2 * DIFF_DV), F32)],
    )
    return pl.pallas_call(
        functools.partial(_attn_prompt_body, lam_init=lam_init),
        grid_spec=grid_spec,
        out_shape=jax.ShapeDtypeStruct((t_total, DIFF_V_WIDTH), BF16),
        compiler_params=_params("parallel", "parallel", "arbitrary"),
        name="attn_prompt",
    )(qi_tab, ki_tab, qb, kt, vb, lam_vecs, subln_g)


def _attn_sample_body(pt_ref, q_ref, k_ref, v_ref, lam_ref, sg_ref, *rest, n_pages, lam_init):
    kp_refs = rest[:n_pages]
    vp_refs = rest[n_pages:2 * n_pages]
    o_ref = rest[2 * n_pages]
    s_new = q_ref.shape[0]
    n_maps = DIFF_HEADS * 2
    page = kp_refs[0].shape[1]
    nrow = n_maps * s_new

    def value_page(vr):
        return jnp.concatenate([vr[pl.ds(h, page, stride=DIFF_HEADS), :] for h in range(DIFF_HEADS)], axis=1)

    qt = jnp.concatenate([q_ref[...] * DIFF_SCALE] * (DIFF_HEADS * 2), axis=0)
    row = lax.broadcasted_iota(I32, qt.shape, 0)
    col = lax.broadcasted_iota(I32, qt.shape, 1)
    qbd = jnp.where(col // DIFF_DK == row // s_new, qt, 0.0).astype(BF16)
    pad = jnp.zeros((page - s_new, k_ref.shape[1]), F32)
    k_new = jnp.concatenate([k_ref[...], pad], axis=0)
    v_new = jnp.concatenate([v_ref[...], pad], axis=0)
    scores = [_bdot(qbd, kr[...]) for kr in kp_refs]
    s_n = _bdot_nt(qbd, k_new)
    rn = lax.broadcasted_iota(I32, s_n.shape, 0)
    cn = lax.broadcasted_iota(I32, s_n.shape, 1)
    scores.append(jnp.where(cn <= rn % s_new, s_n, NEG_INF))
    m = functools.reduce(jnp.maximum, [jnp.max(s, axis=-1, keepdims=True) for s in scores])
    values = [value_page(vr) for vr in vp_refs] + [v_new]
    l = jnp.zeros_like(m)
    acc = jnp.zeros((nrow, v_new.shape[1]), F32)
    for s, val in zip(scores, values):
        pr = jnp.exp(s - m)
        l = l + jnp.sum(pr, axis=-1, keepdims=True)
        acc = acc + _bdot(pr, val)
    full = acc / l
    lam = _lambda(lam_ref, lam_init)
    outs = []
    for h in range(DIFF_HEADS):
        r0 = h * 2 * s_new
        cols = slice(h * DIFF_DV, (h + 1) * DIFF_DV)
        o = full[r0:r0 + s_new, cols] - lam * full[r0 + s_new:r0 + 2 * s_new, cols]
        outs.append(_subln(o, sg_ref[...], lam_init))
    o_ref[...] = jnp.concatenate(outs, axis=1)


def _attn_sample(q, k, v, cache_k, cache_v, n_pool_pages, page_table, page_offset, lam_vecs, subln_g, s_new,
                 lam_init):
    nseq, n_pages = page_table.shape
    width = q.shape[1]
    k_rows = cache_k.shape[0] // n_pool_pages
    v_rows = cache_v.shape[0] // n_pool_pages
    new_spec = pl.BlockSpec((s_new, width), lambda b, pt: (b, 0))
    page_spec = lambda rows, lanes, j: pl.BlockSpec((rows, lanes), lambda b, pt: (pt[b, j] + page_offset, 0))
    grid_spec = pltpu.PrefetchScalarGridSpec(
        num_scalar_prefetch=1,
        grid=(nseq,),
        in_specs=[new_spec, new_spec, new_spec,
                  pl.BlockSpec(lam_vecs.shape, lambda *_: (0, 0)), pl.BlockSpec(subln_g.shape, lambda *_: (0, 0))]
                 + [page_spec(k_rows, cache_k.shape[1], j) for j in range(n_pages)]
                 + [page_spec(v_rows, DIFF_DV, j) for j in range(n_pages)],
        out_specs=new_spec,
    )
    return pl.pallas_call(
        functools.partial(_attn_sample_body, n_pages=n_pages, lam_init=lam_init),
        grid_spec=grid_spec,
        out_shape=jax.ShapeDtypeStruct((nseq * s_new, width), F32),
        compiler_params=_params("parallel"),
        name="attn_sample",
    )(page_table, q, k, v, lam_vecs, subln_g, *([cache_k] * n_pages), *([cache_v] * n_pages))


def _outproj_body(yr_ref, o_ref, x_ref, g1_ref, g2n_ref, sh_ref, sc_ref, g2_ref, woa_ref, wob_ref,
                  wsg_ref, wsu_ref, wsd_ref, wrh_ref, wrl_ref,
                  base_ref, h2t_ref, lg_ref):
    tm = x_ref.shape[0]
    mixed = (jnp.dot(yr_ref[...].astype(BF16), woa_ref[...], preferred_element_type=F32)
             + jnp.dot(o_ref[...].astype(BF16), wob_ref[...], preferred_element_type=F32))
    x1 = x_ref[...] + g1_ref[...] * mixed
    h2 = _rms(x1, g2n_ref[...]) * (1.0 + sc_ref[...]) + sh_ref[...]
    h2b = h2.astype(BF16)
    hidden = _silu(jnp.dot(h2b, wsg_ref[...], preferred_element_type=F32)) * jnp.dot(
        h2b, wsu_ref[...], preferred_element_type=F32)
    shared = jnp.dot(hidden.astype(BF16), wsd_ref[...], preferred_element_type=F32)
    base_ref[...] = x1 + g2_ref[...] * shared
    h2l = (h2 - h2b.astype(F32)).astype(BF16)
    nt = lambda a, b: lax.dot_general(a, b, (((1,), (1,)), ((), ())), preferred_element_type=F32)
    lg_ref[...] = nt(wrh_ref[...], h2b) + nt(wrl_ref[...], h2b) + nt(wrh_ref[...], h2l)
    for c in range(ROW_TILE):
        h2t_ref[pl.ds(c, tm, stride=ROW_TILE), :] = h2[:, c * LANES:(c + 1) * LANES]


def _outproj(yr, o, x, gate1, norm2_g, shift2, scale2, gate2, wts, tm, tiles_per_seq):
    t, d = x.shape
    row = lambda n: pl.BlockSpec((tm, n), lambda i: (i, 0))
    mod = lambda m: _mod_spec(m, tm, tiles_per_seq)
    return pl.pallas_call(
        _outproj_body,
        grid=(t // tm,),
        in_specs=[row(RWKV_WIDTH), row(DIFF_V_WIDTH), row(d), mod(gate1), _full_spec(norm2_g), mod(shift2),
                  mod(scale2), mod(gate2)] + [_full_spec(a) for a in wts],
        out_specs=[row(d), pl.BlockSpec((tm * ROW_TILE, LANES), lambda i: (i, 0)),
                   pl.BlockSpec((N_EXPERTS, tm), lambda i: (0, i))],
        out_shape=[jax.ShapeDtypeStruct((t, d), F32),
                   jax.ShapeDtypeStruct((t * ROW_TILE, LANES), F32),
                   jax.ShapeDtypeStruct((N_EXPERTS, t), F32)],
        compiler_params=_params("parallel"),
        name="outproj",
    )(yr, o, x, gate1, norm2_g, shift2, scale2, gate2, *wts)


def _router_body(lg_ref, bias_ref, idx_ref, gate_ref, pos_ref, cnt_ref, run_ref):
    i = pl.program_id(0)
    tm = lg_ref.shape[1]

    @pl.when(i == 0)
    def _():
        run_ref[...] = jnp.zeros(run_ref.shape, F32)

    scores = _sigmoid(lg_ref[...])
    biased = scores + bias_ref[...]
    erow = lax.broadcasted_iota(I32, (N_EXPERTS, tm), 0)
    grow = lax.broadcasted_iota(I32, (GROUP_SIZE, tm), 0)

    def first_argmax(x, rows, limit):
        mx = jnp.max(x, axis=0, keepdims=True)
        return mx, jnp.min(jnp.where(x == mx, rows, limit), axis=0, keepdims=True)

    group_scores = []
    for gidx in range(N_GROUPS):
        xg = biased[gidx * GROUP_SIZE:(gidx + 1) * GROUP_SIZE, :]
        m1, i1 = first_argmax(xg, grow, GROUP_SIZE)
        m2 = jnp.max(jnp.where(grow == i1, NEG_INF, xg), axis=0, keepdims=True)
        group_scores.append(m1 + m2)
    gs = jnp.concatenate(group_scores, axis=0)
    g8 = lax.broadcasted_iota(I32, (N_GROUPS, tm), 0)
    chosen = jnp.zeros((N_GROUPS, tm), I32)
    for _ in range(TOPK_GROUPS):
        _, gi = first_argmax(gs, g8, N_GROUPS)
        hit = g8 == gi
        chosen = jnp.where(hit, 1, chosen)
        gs = jnp.where(hit, NEG_INF, gs)
    cand = jnp.concatenate(
        [jnp.where(chosen[gidx:gidx + 1, :] > 0, biased[gidx * GROUP_SIZE:(gidx + 1) * GROUP_SIZE, :], NEG_INF)
         for gidx in range(N_GROUPS)], axis=0)

    idxs, raws = [], []
    onehot = jnp.zeros((N_EXPERTS, tm), F32)
    for _ in range(TOP_K):
        _, ei = first_argmax(cand, erow, N_EXPERTS)
        hit = erow == ei
        idxs.append(ei)
        raws.append(jnp.sum(jnp.where(hit, scores, 0.0), axis=0, keepdims=True))
        onehot = onehot + hit.astype(F32)
        cand = jnp.where(hit, NEG_INF, cand)
    raw = jnp.concatenate(raws, axis=0)
    gate_ref[...] = raw / jnp.sum(raw, axis=0, keepdims=True) * ROUTE_SCALE
    idx_ref[...] = jnp.concatenate(idxs, axis=0)

    ti = lax.broadcasted_iota(I32, (tm, tm), 0)
    tj = lax.broadcasted_iota(I32, (tm, tm), 1)
    oh = onehot.astype(BF16)
    before = jnp.dot(oh, (ti < tj).astype(BF16), preferred_element_type=F32) + run_ref[...]
    run_ref[...] = run_ref[...] + jnp.dot(oh, jnp.ones((tm, tm), BF16), preferred_element_type=F32)
    pos_ref[...] = jnp.concatenate(
        [jnp.sum(jnp.where(erow == ei, before, 0.0), axis=0, keepdims=True) for ei in idxs], axis=0).astype(I32)
    cnt_ref[...] = run_ref[...]


def _router(logits_t, bias_col):
    t = logits_t.shape[1]
    tm = ROUTE_TILE
    tok = pl.BlockSpec((TOP_K, tm), lambda i: (0, i))
    return pl.pallas_call(
        _router_body,
        grid=(t // tm,),
        in_specs=[pl.BlockSpec((N_EXPERTS, tm), lambda i: (0, i)), _full_spec(bias_col)],
        out_specs=[tok, tok, tok, pl.BlockSpec((N_EXPERTS, tm), lambda i: (0, 0))],
        out_shape=[jax.ShapeDtypeStruct((TOP_K, t), I32), jax.ShapeDtypeStruct((TOP_K, t), F32),
                   jax.ShapeDtypeStruct((TOP_K, t), I32), jax.ShapeDtypeStruct((N_EXPERTS, tm), F32)],
        scratch_shapes=[pltpu.VMEM((N_EXPERTS, tm), F32)],
        compiler_params=_params("arbitrary"),
        name="router",
    )(logits_t, bias_col)


def _assign_body(idx_ref, pos_ref, start_ref, dest_ref):
    tm = idx_ref.shape[1]
    erow = lax.broadcasted_iota(I32, (N_EXPERTS, tm), 0)
    start = start_ref[...]
    idx = idx_ref[...]
    first = jnp.concatenate(
        [jnp.sum(jnp.where(erow == idx[j:j + 1, :], start, 0.0), axis=0, keepdims=True) for j in range(TOP_K)],
        axis=0)
    dest_ref[...] = first.astype(I32) + pos_ref[...]


def _assign(idx, pos, start_col):
    t = idx.shape[1]
    tm = ROUTE_TILE
    tok = pl.BlockSpec((TOP_K, tm), lambda i: (0, i))
    return pl.pallas_call(
        _assign_body,
        grid=(t // tm,),
        in_specs=[tok, tok, _full_spec(start_col)],
        out_specs=tok,
        out_shape=jax.ShapeDtypeStruct((TOP_K, t), I32),
        compiler_params=_params("parallel"),
        name="assign",
    )(idx, pos, start_col)


def _row_copy(src, src_row, dst, dst_row, sem):
    return pltpu.make_async_copy(src.at[pl.ds(pl.multiple_of(src_row * ROW_TILE, ROW_TILE), ROW_TILE), :],
                                 dst.at[pl.ds(pl.multiple_of(dst_row * ROW_TILE, ROW_TILE), ROW_TILE), :], sem)


def _dispatch_body(zflag_ref, h_ref, dest_ref, xs_ref, zero_ref, zsem, rsem):
    i = pl.program_id(0)
    tm = dest_ref.shape[1]
    blk_rows = EXPERT_BLOCK * ROW_TILE
    n_blocks = xs_ref.shape[0] // blk_rows

    def zero_copy(b):
        start = pl.multiple_of(b * blk_rows, blk_rows)
        return pltpu.make_async_copy(zero_ref, xs_ref.at[pl.ds(start, blk_rows), :], zsem)

    @pl.when(i == 0)
    def _():
        zero_ref[...] = jnp.zeros(zero_ref.shape, F32)

        def issue(b, carry):
            @pl.when(zflag_ref[b] > 0)
            def _():
                zero_copy(b).start()
            return carry

        def drain(b, carry):
            @pl.when(zflag_ref[b] > 0)
            def _():
                zero_copy(b).wait()
            return carry

        lax.fori_loop(0, n_blocks, issue, 0)
        lax.fori_loop(0, n_blocks, drain, 0)

    def issue_rows(t, carry):
        for j in range(TOP_K):
            _row_copy(h_ref, t, xs_ref, dest_ref[j, t], rsem).start()
        return carry

    lax.fori_loop(0, tm, issue_rows, 0)
    for _ in range(TOP_K):
        pltpu.make_async_copy(h_ref, xs_ref.at[pl.ds(0, tm * ROW_TILE), :], rsem).wait()


def _dispatch(h2t, dest, zero_flags, n_rows):
    t = dest.shape[1]
    tm = ROUTE_TILE
    grid_spec = pltpu.PrefetchScalarGridSpec(
        num_scalar_prefetch=1,
        grid=(t // tm,),
        in_specs=[pl.BlockSpec((tm * ROW_TILE, LANES), lambda i, *_: (i, 0)),
                  pl.BlockSpec((TOP_K, tm), lambda i, *_: (0, i), memory_space=pltpu.SMEM)],
        out_specs=pl.BlockSpec(memory_space=pl.ANY),
        scratch_shapes=[pltpu.VMEM((EXPERT_BLOCK * ROW_TILE, LANES), F32),
                        pltpu.SemaphoreType.DMA, pltpu.SemaphoreType.DMA],
    )
    return pl.pallas_call(
        _dispatch_body,
        grid_spec=grid_spec,
        out_shape=jax.ShapeDtypeStruct((n_rows * ROW_TILE, LANES), F32),
        compiler_params=_params("arbitrary"),
        name="dispatch",
    )(zero_flags, h2t, dest)


def _expert_body(first_ref, count_ref, nu_ref, xs_ref, wg_ref, wu_ref, wd_ref, y_ref,
                 xin_ref, yout_ref, wgb_ref, wub_ref, wdb_ref, in_sem, out_sem):
    e = pl.program_id(0)
    rows = EXPERT_BLOCK
    blk_rows = EXPERT_BLOCK * ROW_TILE
    nb = count_ref[e]
    b0 = first_ref[e]

    n_used = nu_ref[0]
    n_in = xin_ref.shape[0]
    n_out = yout_ref.shape[0]
    ahead = n_in - 1
    part = blk_rows // EXPERT_DMA_SPLIT

    def hbm_rows(g, k=0, n=blk_rows):
        return pl.ds(pl.multiple_of(g * blk_rows + k * part, part), n)

    def start_in(g):
        slot = g % n_in
        for k in range(EXPERT_DMA_SPLIT):
            pltpu.make_async_copy(xs_ref.at[hbm_rows(g, k, part), :], xin_ref.at[slot, pl.ds(k * part, part), :],
                                  in_sem.at[slot]).start()

    def wait_in(g):
        slot = g % n_in
        pltpu.make_async_copy(xs_ref.at[hbm_rows(g), :], xin_ref.at[slot], in_sem.at[slot]).wait()

    def start_out(g):
        slot = g % n_out
        for k in range(EXPERT_DMA_SPLIT):
            pltpu.make_async_copy(yout_ref.at[slot, pl.ds(k * part, part), :], y_ref.at[hbm_rows(g, k, part), :],
                                  out_sem.at[slot]).start()

    def wait_out(g):
        slot = g % n_out
        pltpu.make_async_copy(yout_ref.at[slot], y_ref.at[hbm_rows(g), :], out_sem.at[slot]).wait()

    @pl.when(e == 0)
    def _():
        for g in range(ahead):
            @pl.when(g < n_used)
            def _():
                start_in(g)

    @pl.when(nb > 0)
    def _():
        wgb_ref[...] = wg_ref[...].astype(BF16)
        wub_ref[...] = wu_ref[...].astype(BF16)
        wdb_ref[...] = wd_ref[...].astype(BF16)

        def step(b, carry):
            g = b0 + b
            wait_in(g)

            @pl.when(g + ahead < n_used)
            def _():
                start_in(g + ahead)

            @pl.when(g >= n_out)
            def _():
                wait_out(g - n_out)

            islot = g % n_in
            oslot = g % n_out
            x = jnp.concatenate([xin_ref[islot, pl.ds(c, rows, stride=ROW_TILE), :] for c in range(ROW_TILE)],
                                axis=1).astype(BF16)
            hidden = _silu(jnp.dot(x, wgb_ref[...], preferred_element_type=F32)) * jnp.dot(
                x, wub_ref[...], preferred_element_type=F32)
            y = jnp.dot(hidden.astype(BF16), wdb_ref[...], preferred_element_type=F32)
            for c in range(ROW_TILE):
                yout_ref[oslot, pl.ds(c, rows, stride=ROW_TILE), :] = y[:, c * LANES:(c + 1) * LANES]
            start_out(g)
            return carry

        lax.fori_loop(0, nb, step, 0)

    @pl.when(e == pl.num_programs(0) - 1)
    def _():
        def drain(g, carry):
            wait_out(g)
            return carry

        lax.fori_loop(jnp.maximum(n_used - n_out, 0), n_used, drain, 0)
        n_blocks = y_ref.shape[0] // blk_rows
        yout_ref[0] = jnp.zeros(yout_ref.shape[1:], F32)

        def tail(b, carry):
            cp = pltpu.make_async_copy(
                yout_ref.at[0], y_ref.at[pl.ds(pl.multiple_of(b * blk_rows, blk_rows), blk_rows), :], out_sem.at[0])
            cp.start()
            cp.wait()
            return carry

        lax.fori_loop(nu_ref[0], n_blocks, tail, 0)


def _experts(xs, first_block, block_count, n_used, we_gate, we_up, we_down):
    d, f = we_gate.shape[1], we_gate.shape[2]
    blk = (EXPERT_BLOCK * ROW_TILE, LANES)
    wspec = lambda a, b: pl.BlockSpec((None, a, b), lambda e, *_: (e, 0, 0))
    grid_spec = pltpu.PrefetchScalarGridSpec(
        num_scalar_prefetch=3,
        grid=(N_EXPERTS,),
        in_specs=[pl.BlockSpec(memory_space=pl.ANY), wspec(d, f), wspec(d, f), wspec(f, d)],
        out_specs=pl.BlockSpec(memory_space=pl.ANY),
        scratch_shapes=[pltpu.VMEM((EXPERT_IN_SLOTS,) + blk, F32), pltpu.VMEM((EXPERT_OUT_SLOTS,) + blk, F32),
                        pltpu.VMEM((d, f), BF16), pltpu.VMEM((d, f), BF16), pltpu.VMEM((f, d), BF16),
                        pltpu.SemaphoreType.DMA((EXPERT_IN_SLOTS,)), pltpu.SemaphoreType.DMA((EXPERT_OUT_SLOTS,))],
    )
    return pl.pallas_call(
        _expert_body,
        grid_spec=grid_spec,
        out_shape=jax.ShapeDtypeStruct(xs.shape, F32),
        compiler_params=_params("arbitrary"),
        name="experts",
    )(first_block, block_count, n_used, xs, we_gate, we_up, we_down)


def _combine_body(base_ref, g2_ref, gate_ref, dest_ref, y_ref, out_ref, buf_ref, sem):
    tm = base_ref.shape[0]
    slot_rows = tm * ROW_TILE

    def issue(t, carry):
        for j in range(TOP_K):
            _row_copy(y_ref, dest_ref[j, t], buf_ref, j * tm + t, sem).start()
        return carry

    lax.fori_loop(0, tm, issue, 0)
    pltpu.make_async_copy(y_ref.at[pl.ds(0, TOP_K * slot_rows), :], buf_ref, sem).wait()
    gates = gate_ref[...]
    gcols = [jnp.broadcast_to(gates[:, j:j + 1], (tm, LANES)) for j in range(TOP_K)]
    for c in range(ROW_TILE):
        cols = slice(c * LANES, (c + 1) * LANES)
        routed = sum(gcols[j] * buf_ref[pl.ds(j * slot_rows + c, tm, stride=ROW_TILE), :] for j in range(TOP_K))
        out_ref[:, cols] = base_ref[:, cols] + g2_ref[:, cols] * routed


def _combine(base, gate2, gates, dest, y_rows, tok_offset, tiles_per_seq):
    t, d = base.shape
    tm = ROUTE_TILE
    off = tok_offset // tm
    return pl.pallas_call(
        _combine_body,
        grid=(t // tm,),
        in_specs=[pl.BlockSpec((tm, d), lambda i: (i, 0)), _mod_spec(gate2, tm, tiles_per_seq),
                  pl.BlockSpec((tm, TOP_K), lambda i: (i + off, 0)),
                  pl.BlockSpec((TOP_K, tm), lambda i: (0, i + off), memory_space=pltpu.SMEM),
                  pl.BlockSpec(memory_space=pl.ANY)],
        out_specs=pl.BlockSpec((tm, d), lambda i: (i, 0)),
        out_shape=jax.ShapeDtypeStruct((t, d), F32),
        scratch_shapes=[pltpu.VMEM((TOP_K * tm * ROW_TILE, LANES), F32), pltpu.SemaphoreType.DMA],
        compiler_params=_params("arbitrary"),
        name="combine",
    )(base, gate2, gates, dest, y_rows)


def _segment_ones(width, seg):
    ids = jnp.arange(width) // seg
    return (ids[:, None] == ids[None, :]).astype(BF16)


def _pad_rows(w, start, total):
    return jnp.zeros((total, w.shape[1]), w.dtype).at[start:start + w.shape[0]].set(w)


def kernel(x_prompt, x_sample, cache_k, cache_v, state_wkv, state_shift, page_table, c_prompt, c_sample, w_ada, b_ada, norm1_g, norm2_g, w_in, mu_shift, w0, w_lora_up, a0, a_lora_up, g_lora_up, k_k, k_a, r_k, lnx_g, lnx_b, qn_g, kn_g, lam_q1, lam_k1, lam_q2, lam_k2, subln_g, w_out, w_router, router_bias, we_gate, we_up, we_down, ws_gate, ws_up, ws_down):
    depth = w_in.shape[0]
    nb, seq, d = x_prompt.shape
    db, dseq, _ = x_sample.shape
    n_pool, page = cache_k.shape[1], cache_k.shape[2]
    tp, ts = nb * seq, db * dseq
    w3 = 3 * RWKV_WIDTH
    row = lambda a: a.reshape(1, -1)
    seg512 = _segment_ones(RWKV_WIDTH, RWKV_HEAD_DIM)
    tm_p = min(256, seq)
    tm_s = min(256, ts)
    chunk_p = min(64, seq)

    xp = x_prompt.reshape(tp, d)
    xs = x_sample.reshape(ts, d)
    cache_k2 = jnp.transpose(cache_k, (0, 1, 3, 4, 5, 2)).reshape(depth * n_pool * DIFF_QK_WIDTH, page)
    cache_v2 = cache_v.reshape(depth * n_pool * page * DIFF_HEADS, DIFF_DV)
    n_cond = nb + db
    cond = jnp.concatenate([c_prompt, c_sample], axis=0)
    cond = jnp.pad(cond, ((0, -n_cond % SUBLANES), (0, 0)))

    outs = {name: [] for name in ("kp", "vp", "wp", "sp", "ks", "vs", "ws", "ss")}
    for layer in range(depth):
        lam_init = 0.8 - 0.6 * math.exp(-0.3 * layer)
        mods = _ada(cond, w_ada[layer], row(b_ada[layer]))
        mod_p = [m.reshape(nb, 1, d) for m in jnp.split(mods[:nb], 6, axis=-1)]
        mod_s = [jnp.repeat(m, dseq, axis=0) for m in jnp.split(mods[nb:n_cond], 6, axis=-1)]

        wi = w_in[layer]
        in_wts = (wi[:, :w3].astype(BF16),
                  jnp.pad(wi[:, w3:RWKV_PROJ], ((0, 0), (0, LORA_PAD - LORA_WIDTH))).astype(BF16),
                  wi[:, RWKV_PROJ:RWKV_PROJ + DIFF_QK_WIDTH].astype(BF16),
                  wi[:, RWKV_PROJ + DIFF_QK_WIDTH:RWKV_PROJ + 2 * DIFF_QK_WIDTH].astype(BF16),
                  wi[:, RWKV_PROJ + 2 * DIFF_QK_WIDTH:].astype(BF16),
                  row(jnp.tile(qn_g[layer], DIFF_QK_WIDTH // DIFF_DK)),
                  row(jnp.tile(kn_g[layer], DIFF_QK_WIDTH // DIFF_DK)),
                  seg512)
        mu = mu_shift[layer]
        rwkv_wts = (row(mu[:w3]), row(jnp.pad(mu[w3:], (0, LORA_PAD - LORA_WIDTH))), row(w0[layer]),
                    _pad_rows(w_lora_up[layer], 0, LORA_PAD).astype(BF16), row(a0[layer]),
                    _pad_rows(a_lora_up[layer], DECAY_LORA, LORA_PAD).astype(BF16),
                    _pad_rows(g_lora_up[layer], DECAY_LORA + ICLR_LORA, LORA_PAD).astype(BF16),
                    row(k_k[layer]), row(k_a[layer]), row(r_k[layer]), row(lnx_g[layer]), row(lnx_b[layer]), seg512)
        lam_vecs = jnp.stack([lam_q1[layer], lam_k1[layer], lam_q2[layer], lam_k2[layer]])
        sg = row(subln_g[layer])
        wr = w_router[layer].T
        wr_hi = wr.astype(BF16)
        out_wts = (w_out[layer][:RWKV_WIDTH].astype(BF16), w_out[layer][RWKV_WIDTH:].astype(BF16),
                   ws_gate[layer].astype(BF16), ws_up[layer].astype(BF16), ws_down[layer].astype(BF16),
                   wr_hi, (wr - wr_hi.astype(F32)).astype(BF16))

        def token_mix(x, mod, tm, tiles_per_seq, seq_len, chunk, shift_prev, wkv0, long_seq):
            pm, plo, *qkv = _inproj(x, row(norm1_g[layer]), mod[0], mod[1], in_wts, tm, tiles_per_seq, long_seq)
            n_seq = x.shape[0] // seq_len
            shift_m = shift_prev[:, :, :w3]
            shift_l = jnp.pad(shift_prev[:, :, w3:], ((0, 0), (0, 0), (0, LORA_PAD - LORA_WIDTH)))
            per_step = RWKV_CHUNKS_PER_STEP if seq_len % (chunk * RWKV_CHUNKS_PER_STEP) == 0 else 1
            yr, wkv = _rwkv(pm, plo, shift_m, shift_l, wkv0, rwkv_wts, seq_len, chunk, per_step)
            last = jnp.concatenate([pm.reshape(n_seq, seq_len, w3)[:, -1:],
                                    plo.reshape(n_seq, seq_len, LORA_PAD)[:, -1:, :LORA_WIDTH]], axis=-1)
            return yr, wkv, last, qkv

        yr_p, wkv_p, last_p, (kt_p, v4_p, qb_p, vb_p) = token_mix(
            xp, mod_p, tm_p, seq // tm_p, seq, chunk_p,
            jnp.zeros((nb, 1, RWKV_PROJ), F32), jnp.zeros((nb, RWKV_HEADS, RWKV_HEAD_DIM, RWKV_HEAD_DIM), F32), True)
        o_p = _attn_prompt(qb_p, kt_p, vb_p, lam_vecs, sg, seq, lam_init)
        yr_s, wkv_s, last_s, (q_s, k_s, v_s) = token_mix(
            xs, mod_s, tm_s, 1, dseq, dseq, state_shift[layer], state_wkv[layer], False)
        o_s = _attn_sample(q_s, k_s, v_s, cache_k2, cache_v2, depth * n_pool, page_table, layer * n_pool, lam_vecs,
                           sg, dseq, lam_init)

        base_p, h2t_p, lg_p = _outproj(yr_p, o_p, xp, mod_p[2], row(norm2_g[layer]), mod_p[3], mod_p[4], mod_p[5],
                                       out_wts, tm_p, seq // tm_p)
        base_s, h2t_s, lg_s = _outproj(yr_s, o_s, xs, mod_s[2], row(norm2_g[layer]), mod_s[3], mod_s[4], mod_s[5],
                                       out_wts, tm_s, 1)

        t_all = tp + ts
        idx, gate, pos, cnt = _router(jnp.concatenate([lg_p, lg_s], axis=1), router_bias[layer].reshape(-1, 1))
        counts = cnt[:, 0].astype(I32)
        padded = (counts + EXPERT_BLOCK - 1) // EXPERT_BLOCK * EXPERT_BLOCK
        end_padded = jnp.cumsum(padded)
        dest = _assign(idx, pos, (end_padded - padded).astype(F32).reshape(-1, 1))
        n_rows = (t_all * TOP_K + N_EXPERTS * (EXPERT_BLOCK - 1) + EXPERT_BLOCK - 1) // EXPERT_BLOCK * EXPERT_BLOCK
        n_blocks = n_rows // EXPERT_BLOCK
        blk_ids = jnp.arange(n_blocks, dtype=I32)
        n_used = (end_padded[-1:] // EXPERT_BLOCK).astype(I32)
        is_expert_end = jnp.any(end_padded[None, :] == (blk_ids[:, None] + 1) * EXPERT_BLOCK, axis=1)
        zero_flags = jnp.logical_or(blk_ids >= n_used[0], is_expert_end).astype(I32)
        rows_in = _dispatch(jnp.concatenate([h2t_p, h2t_s], axis=0), dest, zero_flags, n_rows)
        rows_out = _experts(rows_in, ((end_padded - padded) // EXPERT_BLOCK).astype(I32),
                            (padded // EXPERT_BLOCK).astype(I32), n_used, we_gate[layer], we_up[layer],
                            we_down[layer])
        gates_t = gate.T
        xp = _combine(base_p, mod_p[5], gates_t, dest, rows_out, 0, seq // ROUTE_TILE)
        xs = _combine(base_s, mod_s[5], gates_t, dest, rows_out, tp, 1)

        outs["kp"].append(jnp.transpose(kt_p.reshape(nb, DIFF_HEADS, 2, DIFF_DK, seq), (0, 4, 1, 2, 3)))
        outs["vp"].append(v4_p.reshape(nb, seq, DIFF_HEADS, DIFF_DV))
        outs["wp"].append(wkv_p)
        outs["sp"].append(last_p)
        outs["ks"].append(k_s.reshape(db, dseq, DIFF_HEADS, 2, DIFF_DK))
        outs["vs"].append(v_s.reshape(db, dseq, DIFF_HEADS, DIFF_DV))
        outs["ws"].append(wkv_s)
        outs["ss"].append(last_s)

    st = {name: jnp.stack(v) for name, v in outs.items()}
    return (xp.reshape(nb, seq, d), xs.reshape(db, dseq, d), st["kp"], st["vp"], st["wp"], st["sp"],
            st["ks"], st["vs"], st["ws"], st["ss"])
```

```python
import functools
import math

import jax
import jax.numpy as jnp
from jax import lax
from jax.experimental import pallas as pl
from jax.experimental.pallas import tpu as pltpu

F32 = jnp.float32
BF16 = jnp.bfloat16
I32 = jnp.int32

RWKV_HEADS = 8
RWKV_HEAD_DIM = 64
RWKV_WIDTH = RWKV_HEADS * RWKV_HEAD_DIM
DECAY_LORA = 32
ICLR_LORA = 32
GATE_LORA = 96
LORA_WIDTH = DECAY_LORA + ICLR_LORA + GATE_LORA
LORA_PAD = 256
RWKV_PROJ = 3 * RWKV_WIDTH + LORA_WIDTH
GN_EPS = 64e-5
DIFF_HEADS = 4
DIFF_DK = 64
DIFF_DV = 2 * DIFF_DK
DIFF_QK_WIDTH = DIFF_HEADS * 2 * DIFF_DK
DIFF_V_WIDTH = DIFF_HEADS * DIFF_DV
DIFF_SCALE = DIFF_DK ** -0.5
N_EXPERTS = 256
N_GROUPS = 8
GROUP_SIZE = N_EXPERTS // N_GROUPS
TOPK_GROUPS = 4
TOP_K = 8
ROUTE_SCALE = 2.5
NORM_EPS = 1e-6

LANES = 128
SUBLANES = 8
ROW_TILE = 8
VMEM_LIMIT = 48 * 1024 * 1024

EXPERT_BLOCK = 128
EXPERT_IN_SLOTS = 4
EXPERT_OUT_SLOTS = 3
EXPERT_DMA_SPLIT = 4
ROUTE_TILE = 128
RWKV_CHUNKS_PER_STEP = 4
RWKV_SEQS_PER_STEP = 8
ATTN_TILE = 2048
ATTN_ROW_CHUNK = 256
NEG_INF = float("-inf")


def _bdot(a, b):
    return jnp.dot(a.astype(BF16), b.astype(BF16), preferred_element_type=F32)


def _bdot_nt(a, b):
    return lax.dot_general(a.astype(BF16), b.astype(BF16), (((1,), (1,)), ((), ())), preferred_element_type=F32)


def _bdot_tn(a, b):
    return lax.dot_general(a.astype(BF16), b.astype(BF16), (((0,), (0,)), ((), ())), preferred_element_type=F32)


def _sigmoid(x):
    return 1.0 / (1.0 + jnp.exp(-x))


def _silu(x):
    return x * _sigmoid(x)


def _params(*sem, vmem=VMEM_LIMIT):
    return pltpu.CompilerParams(dimension_semantics=sem, vmem_limit_bytes=vmem)


def _ada_body(c_ref, w_ref, b_ref, o_ref):
    o_ref[...] = _bdot(_silu(c_ref[...]), w_ref[...]) + b_ref[...]


def _ada(c, w, b):
    rows, d = c.shape
    n = w.shape[1]
    tn = 512
    return pl.pallas_call(
        _ada_body,
        grid=(n // tn,),
        in_specs=[pl.BlockSpec((rows, d), lambda j: (0, 0)),
                  pl.BlockSpec((d, tn), lambda j: (0, j)),
                  pl.BlockSpec((1, tn), lambda j: (0, j))],
        out_specs=pl.BlockSpec((rows, tn), lambda j: (0, j)),
        out_shape=jax.ShapeDtypeStruct((rows, n), F32),
        compiler_params=_params("parallel"),
        name="ada",
    )(c, w, b)


def _mod_spec(mod, tm, tiles_per_seq):
    if mod.ndim == 3:
        return pl.BlockSpec((None, 1, mod.shape[-1]), lambda i: (i // tiles_per_seq, 0, 0))
    return pl.BlockSpec((tm, mod.shape[-1]), lambda i: (i, 0))


def _full_spec(a):
    nd = a.ndim
    return pl.BlockSpec(a.shape, lambda *_: (0,) * nd)


def _rms(x, g):
    return x * lax.rsqrt(jnp.mean(x * x, axis=-1, keepdims=True) + NORM_EPS) * g


def _inproj_body(x_ref, g_ref, sh_ref, sc_ref, wm_ref, wl_ref, wq_ref, wk_ref, wv_ref, qg_ref, kg_ref, seg_ref,
                 pm_ref, pl_ref, *out_refs, long_seq):
    tm = x_ref.shape[0]
    h = (_rms(x_ref[...], g_ref[...]) * (1.0 + sc_ref[...]) + sh_ref[...]).astype(BF16)
    pm_ref[...] = jnp.dot(h, wm_ref[...], preferred_element_type=F32)
    pl_ref[...] = jnp.dot(h, wl_ref[...], preferred_element_type=F32)
    seg = seg_ref[...]

    def head_norm(z, gain):
        ms = _bdot(z * z, seg) * (1.0 / DIFF_DK)
        return z * lax.rsqrt(ms + NORM_EPS) * gain

    q = head_norm(jnp.dot(h, wq_ref[...], preferred_element_type=F32), qg_ref[...])
    k = head_norm(jnp.dot(h, wk_ref[...], preferred_element_type=F32), kg_ref[...])
    v = jnp.dot(h, wv_ref[...], preferred_element_type=F32)
    if long_seq:
        kt_ref, v4_ref, qb_ref, vb_ref = out_refs
        kt_ref[...] = k.T
        for hd in range(DIFF_HEADS):
            v4_ref[pl.ds(hd, tm, stride=DIFF_HEADS), :] = v[:, hd * DIFF_DV:(hd + 1) * DIFF_DV]
        qb_ref[...] = (q * DIFF_SCALE).astype(BF16)
        vb_ref[...] = v.astype(BF16)
    else:
        q_ref, k_ref, v_ref = out_refs
        q_ref[...] = q
        k_ref[...] = k
        v_ref[...] = v


def _inproj(x, g, shift, scale, wts, tm, tiles_per_seq, long_seq):
    t, d = x.shape
    wm, wl, wq, wk, wv, qg, kg, seg = wts
    row = lambda n: pl.BlockSpec((tm, n), lambda i: (i, 0))
    out_specs = [row(3 * RWKV_WIDTH), row(LORA_PAD)]
    out_shape = [jax.ShapeDtypeStruct((t, 3 * RWKV_WIDTH), F32), jax.ShapeDtypeStruct((t, LORA_PAD), F32)]
    if long_seq:
        n_seq = t // (tm * tiles_per_seq)
        out_specs += [pl.BlockSpec((DIFF_QK_WIDTH, tm), lambda i: (i // tiles_per_seq, i % tiles_per_seq)),
                      pl.BlockSpec((tm * DIFF_HEADS, DIFF_DV), lambda i: (i, 0)),
                      row(DIFF_QK_WIDTH), row(DIFF_V_WIDTH)]
        out_shape += [jax.ShapeDtypeStruct((n_seq * DIFF_QK_WIDTH, tm * tiles_per_seq), F32),
                      jax.ShapeDtypeStruct((t * DIFF_HEADS, DIFF_DV), F32),
                      jax.ShapeDtypeStruct((t, DIFF_QK_WIDTH), BF16), jax.ShapeDtypeStruct((t, DIFF_V_WIDTH), BF16)]
    else:
        out_specs += [row(DIFF_QK_WIDTH), row(DIFF_QK_WIDTH), row(DIFF_V_WIDTH)]
        out_shape += [jax.ShapeDtypeStruct((t, n), F32) for n in (DIFF_QK_WIDTH, DIFF_QK_WIDTH, DIFF_V_WIDTH)]
    return pl.pallas_call(
        functools.partial(_inproj_body, long_seq=long_seq),
        grid=(t // tm,),
        in_specs=[row(d), _full_spec(g), _mod_spec(shift, tm, tiles_per_seq), _mod_spec(scale, tm, tiles_per_seq)]
                 + [_full_spec(a) for a in (wm, wl, wq, wk, wv, qg, kg, seg)],
        out_specs=out_specs,
        out_shape=out_shape,
        compiler_params=_params("parallel"),
        name="inproj",
    )(x, g, shift, scale, wm, wl, wq, wk, wv, qg, kg, seg)


def _split3(x):
    hi = x.astype(BF16)
    r1 = x - hi.astype(F32)
    mid = r1.astype(BF16)
    lo = (r1 - mid.astype(F32)).astype(BF16)
    return hi, mid, lo


def _rwkv_body(pm_ref, pl_ref, pm8_ref, pl8_ref, sm_ref, sl_ref, s0_ref,
               mum_ref, mul_ref, w0_ref, wupw_ref, a0_ref, wupa_ref, wupg_ref, kk_ref, ka_ref, rk_ref,
               lg_ref, lb_ref, seg_ref,
               y_ref, sout_ref, state_ref, *, chunk, whole_seqs):
    c = pl.program_id(1)
    nc = pl.num_programs(1)
    C = chunk
    W = RWKV_WIDTH
    N = RWKV_HEAD_DIM
    pm = pm_ref[...]
    plo = pl_ref[...]

    if whole_seqs:
        n_seq = pm.shape[0] // C
        rows_of = lambda ref: jnp.concatenate(
            [jnp.broadcast_to(ref[s], (C, ref.shape[-1])) for s in range(n_seq)], axis=0)
        prev_m, prev_l = rows_of(sm_ref), rows_of(sl_ref)

        def shifted(cur, prev_rows):
            rows = lax.broadcasted_iota(I32, cur.shape, 0)
            return jnp.where(rows % C == 0, prev_rows, pltpu.roll(cur, 1, 0))
    else:
        @pl.when(c == 0)
        def _():
            state_ref[...] = s0_ref[0]

        first = c == 0
        prev_m = jnp.where(first, sm_ref[0], pm8_ref[SUBLANES - 1:SUBLANES, :])
        prev_l = jnp.where(first, sl_ref[0], pl8_ref[SUBLANES - 1:SUBLANES, :])

        def shifted(cur, prev_row):
            rows = lax.broadcasted_iota(I32, cur.shape, 0)
            return jnp.where(rows == 0, prev_row, pltpu.roll(cur, 1, 0))

    xm = pm + (shifted(pm, prev_m) - pm) * mum_ref[...]
    xl = plo + (shifted(plo, prev_l) - plo) * mul_ref[...]
    r = xm[:, 0:W]
    k = xm[:, W:2 * W]
    v = xm[:, 2 * W:3 * W]
    seg = seg_ref[...]

    z = -(w0_ref[...] + _bdot(jnp.tanh(xl), wupw_ref[...]))
    softplus = jnp.maximum(z, 0.0) + jnp.log(1.0 + jnp.exp(-jnp.abs(z)))
    w = -softplus - 0.5
    a = _sigmoid(a0_ref[...] + _bdot(xl, wupa_ref[...]))
    g = _bdot(_sigmoid(xl), wupg_ref[...])
    kk = k * kk_ref[...]
    kk = kk / jnp.maximum(jnp.sqrt(_bdot(kk * kk, seg)), 1e-12)
    k = k * (1.0 + (a - 1.0) * ka_ref[...])
    logdec = -jnp.exp(w)

    rows_blk = pm.shape[0]
    n_sub = rows_blk // C
    bi = lax.broadcasted_iota(I32, (rows_blk, rows_blk), 0)
    bj = lax.broadcasted_iota(I32, (rows_blk, rows_blk), 1)
    lower = jnp.logical_and(bi >= bj, bi // C == bj // C).astype(BF16)
    cum = sum(jnp.dot(lower, part, preferred_element_type=F32) for part in _split3(logdec))
    cum_end = jnp.concatenate(
        [jnp.broadcast_to(cum[(s + 1) * C - 1:(s + 1) * C, :], (C, W)) for s in range(n_sub)], axis=0)
    a_t = -kk * jnp.exp(cum - logdec)
    r_t = r * jnp.exp(cum)
    inv = jnp.exp(-cum)
    b_t = kk * a * inv
    k_t = k * inv
    to_end = jnp.exp(cum_end - cum)
    b_e = kk * a * to_end
    k_e = k * to_end
    g_end = jnp.exp(cum_end)

    ti = lax.broadcasted_iota(I32, (C, C), 0)
    tj = lax.broadcasted_iota(I32, (C, C), 1)
    eye = (ti == tj).astype(F32)
    ri = lax.broadcasted_iota(I32, (2 * C, 2 * C), 0)
    ci = lax.broadcasted_iota(I32, (2 * C, 2 * C), 1)
    tr = jnp.where(ri >= C, ri - C, ri)
    tc = jnp.where(ci >= C, ci - C, ci)
    mask = jnp.logical_or(tr > tc, jnp.logical_and(ri >= C, tr == tc))
    zeros_cn = jnp.zeros((C, N), F32)
    levels = int(math.log2(C))
    heads = range(RWKV_HEADS)
    pairs = [(s, h) for s in range(n_sub) for h in heads]
    tile = lambda z, s, h: z[s * C:(s + 1) * C, h * N:(h + 1) * N]
    ah = [tile(a_t, s, h) for s, h in pairs]
    rh = [tile(r_t, s, h) for s, h in pairs]
    vh = [tile(v, s, h) for s, h in pairs]
    m_all = [jnp.where(mask, _bdot_nt(jnp.concatenate([ah[i], rh[i]], axis=0),
                                      jnp.concatenate([tile(b_t, s, h), tile(k_t, s, h)], axis=0)), 0.0)
             for i, (s, h) in enumerate(pairs)]
    m_top = [m[0:C, :] for m in m_all]
    m_bot = [m[C:2 * C, :] for m in m_all]
    akv = [_bdot(m_top[i], jnp.concatenate([zeros_cn, vh[i]], axis=0)) for i in range(len(pairs))]
    power = [m[:, 0:C] for m in m_top]
    t_inv = [eye + p for p in power]
    for _ in range(levels - 1):
        power = [_bdot(p, p) for p in power]
        t_inv = [t + _bdot(t, p) for t, p in zip(t_inv, power)]
    w_mat = [_bdot(t_inv[i], ah[i]) for i in range(len(pairs))]
    u0 = [_bdot(t_inv[i], akv[i]) for i in range(len(pairs))]
    state = None if whole_seqs else [state_ref[h] for h in heads]
    y_rows = []
    for s in range(n_sub):
        at = lambda h: s * RWKV_HEADS + h
        if whole_seqs:
            state = [s0_ref[s, h] for h in heads]
        x = [_bdot_nt(jnp.concatenate([w_mat[at(h)], rh[at(h)]], axis=0), state[h]) for h in heads]
        uv = [jnp.concatenate([x[h][0:C, :] + u0[at(h)], vh[at(h)]], axis=0) for h in heads]
        y_rows.append(jnp.concatenate([x[h][C:2 * C, :] + _bdot(m_bot[at(h)], uv[h]) for h in heads], axis=1))
        state = [state[h] * tile(g_end, s, h)[0:1, :]
                 + _bdot_tn(uv[h], jnp.concatenate([tile(b_e, s, h), tile(k_e, s, h)], axis=0)) for h in heads]
        if whole_seqs:
            for h in heads:
                sout_ref[s, h] = state[h]
    if not whole_seqs:
        for h in heads:
            state_ref[h] = state[h]
    y = jnp.concatenate(y_rows, axis=0)

    mean = _bdot(y, seg) * (1.0 / N)
    yc = y - mean
    var = _bdot(yc * yc, seg) * (1.0 / N)
    yn = yc * lax.rsqrt(var + GN_EPS) * lg_ref[...] + lb_ref[...]
    bonus = _bdot(r * k * rk_ref[...], seg) * v
    y_ref[...] = ((yn + bonus) * g).astype(y_ref.dtype)

    if not whole_seqs:
        @pl.when(c == nc - 1)
        def _():
            sout_ref[0] = state_ref[...]


def _rwkv(pm, plo, shift_m, shift_l, s0, wts, seq_len, chunk, chunks_per_step):
    t_total = pm.shape[0]
    n_seq = t_total // seq_len
    whole_seqs = chunk == seq_len
    rows = chunk * chunks_per_step
    seqs = chunks_per_step if whole_seqs else 1
    ncnk = 1 if whole_seqs else seq_len // rows
    c8 = rows // SUBLANES
    prev8 = lambda b, c: (jnp.maximum(b * (rows if whole_seqs else seq_len) // SUBLANES + c * c8 - 1, 0), 0)
    cur = lambda b, c: (b * ncnk + c, 0)
    per_seq3 = lambda n: pl.BlockSpec((seqs, 1, n), lambda b, c: (b, 0, 0))
    state_spec = pl.BlockSpec((seqs, RWKV_HEADS, RWKV_HEAD_DIM, RWKV_HEAD_DIM), lambda b, c: (b, 0, 0, 0))
    return pl.pallas_call(
        functools.partial(_rwkv_body, chunk=chunk, whole_seqs=whole_seqs),
        grid=(n_seq // seqs, ncnk),
        in_specs=[pl.BlockSpec((rows, 3 * RWKV_WIDTH), cur), pl.BlockSpec((rows, LORA_PAD), cur),
                  pl.BlockSpec((SUBLANES, 3 * RWKV_WIDTH), prev8), pl.BlockSpec((SUBLANES, LORA_PAD), prev8),
                  per_seq3(3 * RWKV_WIDTH), per_seq3(LORA_PAD), state_spec]
                 + [_full_spec(a) for a in wts],
        out_specs=[pl.BlockSpec((rows, RWKV_WIDTH), cur), state_spec],
        out_shape=[jax.ShapeDtypeStruct((t_total, RWKV_WIDTH), BF16 if chunk % 16 == 0 else F32),
                   jax.ShapeDtypeStruct(s0.shape, F32)],
        scratch_shapes=[pltpu.VMEM((RWKV_HEADS, RWKV_HEAD_DIM, RWKV_HEAD_DIM), F32)],
        compiler_params=_params("parallel", "arbitrary"),
        name="rwkv",
    )(pm, plo, pm, plo, shift_m, shift_l, s0, *wts)


def _lambda(lam_ref, lam_init):
    lv = lam_ref[...]
    return (jnp.exp(jnp.sum(lv[0:1, :] * lv[1:2, :], axis=-1, keepdims=True))
            - jnp.exp(jnp.sum(lv[2:3, :] * lv[3:4, :], axis=-1, keepdims=True)) + lam_init)


def _subln(o, g, lam_init):
    return o * lax.rsqrt(jnp.mean(o * o, axis=-1, keepdims=True) + NORM_EPS) * g * (1.0 - lam_init)


def _attn_prompt_body(qi_ref, ki_ref, q_ref, kt_ref, v_ref, lam_ref, sg_ref, o_ref,
                      qm_ref, m_ref, acc_ref, *, lam_init):
    p = pl.program_id(2)
    qi = qi_ref[p]
    ki = ki_ref[p]
    tq = q_ref.shape[0]
    tk = kt_ref.shape[1]
    dv = v_ref.shape[1]

    @pl.when(ki == 0)
    def _():
        q = q_ref[...]
        lane = lax.broadcasted_iota(I32, q.shape, 1)
        qm_ref[0] = jnp.where(lane < DIFF_DK, q, jnp.zeros_like(q))
        qm_ref[1] = jnp.where(lane >= DIFF_DK, q, jnp.zeros_like(q))
        m_ref[...] = jnp.full(m_ref.shape, NEG_INF, F32)
        acc_ref[...] = jnp.zeros(acc_ref.shape, F32)

    def accumulate(masked):
        kt = kt_ref[...].astype(BF16)
        v1 = jnp.concatenate([v_ref[...], jnp.ones((tk, dv), BF16)], axis=1)
        for c in range(2):
            for r0 in range(0, tq, ATTN_ROW_CHUNK):
                rows = slice(r0, r0 + ATTN_ROW_CHUNK)
                ncol = min(tk, r0 + ATTN_ROW_CHUNK) if masked else tk
                s = jnp.dot(qm_ref[c, rows, :], kt[:, 0:ncol], preferred_element_type=F32)
                if masked:
                    row = lax.broadcasted_iota(I32, s.shape, 0) + r0
                    col = lax.broadcasted_iota(I32, s.shape, 1)
                    s = jnp.where(col <= row, s, NEG_INF)
                m_old = m_ref[c, rows, :]
                m_new = jnp.maximum(m_old, jnp.max(s, axis=-1, keepdims=True))
                alpha = jnp.exp(m_old - m_new)
                pr = jnp.exp(s - jnp.concatenate([m_new] * (ncol // LANES), axis=1))
                acc_ref[c, rows, :] = (jnp.concatenate([alpha] * (2 * dv // LANES), axis=1) * acc_ref[c, rows, :]
                                       + jnp.dot(pr.astype(BF16), v1[0:ncol, :], preferred_element_type=F32))
                m_ref[c, rows, :] = m_new

    @pl.when(ki < qi)
    def _():
        accumulate(False)

    @pl.when(ki == qi)
    def _():
        accumulate(True)
        lam = _lambda(lam_ref, lam_init)
        a0 = acc_ref[0]
        a1 = acc_ref[1]
        o = a0[:, 0:dv] / a0[:, dv:2 * dv] - lam * (a1[:, 0:dv] / a1[:, dv:2 * dv])
        o_ref[...] = _subln(o, sg_ref[...], lam_init).astype(o_ref.dtype)


def _attn_prompt(qb, kt, vb, lam_vecs, subln_g, seq_len, lam_init):
    t_total = qb.shape[0]
    nb = t_total // seq_len
    tq = min(ATTN_TILE, seq_len)
    nq = seq_len // tq
    pairs = [(i, j) for i in range(nq) for j in range(i + 1)]
    qi_tab = jnp.asarray([a for a, _ in pairs], I32)
    ki_tab = jnp.asarray([b for _, b in pairs], I32)
    qmap = lambda b, h, p, qi, ki: (b * nq + qi[p], h)
    vmap = lambda b, h, p, qi, ki: (b * nq + ki[p], h)
    ktmap = lambda b, h, p, qi, ki: (b * DIFF_HEADS + h, ki[p])
    grid_spec = pltpu.PrefetchScalarGridSpec(
        num_scalar_prefetch=2,
        grid=(nb, DIFF_HEADS, len(pairs)),
        in_specs=[pl.BlockSpec((tq, DIFF_DV), qmap), pl.BlockSpec((2 * DIFF_DK, tq), ktmap),
                  pl.BlockSpec((tq, DIFF_DV), vmap),
                  pl.BlockSpec(lam_vecs.shape, lambda *_: (0, 0)), pl.BlockSpec(subln_g.shape, lambda *_: (0, 0))],
        out_specs=pl.BlockSpec((tq, DIFF_DV), qmap),
        scratch_shapes=[pltpu.VMEM((2, tq, DIFF_DV), BF16), pltpu.VMEM((2, tq, LANES), F32),
                        pltpu.VMEM((2, tq, 2 * DIFF_DV), F32)],
    )
    return pl.pallas_call(
        functools.partial(_attn_prompt_body, lam_init=lam_init),
        grid_spec=grid_spec,
        out_shape=jax.ShapeDtypeStruct((t_total, DIFF_V_WIDTH), BF16),
        compiler_params=_params("parallel", "parallel", "arbitrary"),
        name="attn_prompt",
    )(qi_tab, ki_tab, qb, kt, vb, lam_vecs, subln_g)


def _attn_sample_body(pt_ref, q_ref, k_ref, v_ref, lam_ref, sg_ref, *rest, n_pages, lam_init):
    kp_refs = rest[:n_pages]
    vp_refs = rest[n_pages:2 * n_pages]
    o_ref = rest[2 * n_pages]
    s_new = q_ref.shape[0]
    n_maps = DIFF_HEADS * 2
    page = kp_refs[0].shape[1]
    nrow = n_maps * s_new

    def value_page(vr):
        return jnp.concatenate([vr[pl.ds(h, page, stride=DIFF_HEADS), :] for h in range(DIFF_HEADS)], axis=1)

    qt = jnp.concatenate([q_ref[...] * DIFF_SCALE] * (DIFF_HEADS * 2), axis=0)
    row = lax.broadcasted_iota(I32, qt.shape, 0)
    col = lax.broadcasted_iota(I32, qt.shape, 1)
    qbd = jnp.where(col // DIFF_DK == row // s_new, qt, 0.0).astype(BF16)
    pad = jnp.zeros((page - s_new, k_ref.shape[1]), F32)
    k_new = jnp.concatenate([k_ref[...], pad], axis=0)
    v_new = jnp.concatenate([v_ref[...], pad], axis=0)
    scores = [_bdot(qbd, kr[...]) for kr in kp_refs]
    s_n = _bdot_nt(qbd, k_new)
    rn = lax.broadcasted_iota(I32, s_n.shape, 0)
    cn = lax.broadcasted_iota(I32, s_n.shape, 1)
    scores.append(jnp.where(cn <= rn % s_new, s_n, NEG_INF))
    m = functools.reduce(jnp.maximum, [jnp.max(s, axis=-1, keepdims=True) for s in scores])
    values = [value_page(vr) for vr in vp_refs] + [v_new]
    l = jnp.zeros_like(m)
    acc = jnp.zeros((nrow, v_new.shape[1]), F32)
    for s, val in zip(scores, values):
        pr = jnp.exp(s - m)
        l = l + jnp.sum(pr, axis=-1, keepdims=True)
        acc = acc + _bdot(pr, val)
    full = acc / l
    lam = _lambda(lam_ref, lam_init)
    outs = []
    for h in range(DIFF_HEADS):
        r0 = h * 2 * s_new
        cols = slice(h * DIFF_DV, (h + 1) * DIFF_DV)
        o = full[r0:r0 + s_new, cols] - lam * full[r0 + s_new:r0 + 2 * s_new, cols]
        outs.append(_subln(o, sg_ref[...], lam_init))
    o_ref[...] = jnp.concatenate(outs, axis=1)


def _attn_sample(q, k, v, cache_k, cache_v, n_pool_pages, page_table, page_offset, lam_vecs, subln_g, s_new,
                 lam_init):
    nseq, n_pages = page_table.shape
    width = q.shape[1]
    k_rows = cache_k.shape[0] // n_pool_pages
    v_rows = cache_v.shape[0] // n_pool_pages
    new_spec = pl.BlockSpec((s_new, width), lambda b, pt: (b, 0))
    page_spec = lambda rows, lanes, j: pl.BlockSpec((rows, lanes), lambda b, pt: (pt[b, j] + page_offset, 0))
    grid_spec = pltpu.PrefetchScalarGridSpec(
        num_scalar_prefetch=1,
        grid=(nseq,),
        in_specs=[new_spec, new_spec, new_spec,
                  pl.BlockSpec(lam_vecs.shape, lambda *_: (0, 0)), pl.BlockSpec(subln_g.shape, lambda *_: (0, 0))]
                 + [page_spec(k_rows, cache_k.shape[1], j) for j in range(n_pages)]
                 + [page_spec(v_rows, DIFF_DV, j) for j in range(n_pages)],
        out_specs=new_spec,
    )
    return pl.pallas_call(
        functools.partial(_attn_sample_body, n_pages=n_pages, lam_init=lam_init),
        grid_spec=grid_spec,
        out_shape=jax.ShapeDtypeStruct((nseq * s_new, width), F32),
        compiler_params=_params("parallel"),
        name="attn_sample",
    )(page_table, q, k, v, lam_vecs, subln_g, *([cache_k] * n_pages), *([cache_v] * n_pages))


def _outproj_body(yr_ref, o_ref, x_ref, g1_ref, g2n_ref, sh_ref, sc_ref, g2_ref, woa_ref, wob_ref,
                  wsg_ref, wsu_ref, wsd_ref, wrh_ref, wrl_ref,
                  base_ref, h2t_ref, lg_ref):
    tm = x_ref.shape[0]
    mixed = (jnp.dot(yr_ref[...].astype(BF16), woa_ref[...], preferred_element_type=F32)
             + jnp.dot(o_ref[...].astype(BF16), wob_ref[...], preferred_element_type=F32))
    x1 = x_ref[...] + g1_ref[...] * mixed
    h2 = _rms(x1, g2n_ref[...]) * (1.0 + sc_ref[...]) + sh_ref[...]
    h2b = h2.astype(BF16)
    hidden = _silu(jnp.dot(h2b, wsg_ref[...], preferred_element_type=F32)) * jnp.dot(
        h2b, wsu_ref[...], preferred_element_type=F32)
    shared = jnp.dot(hidden.astype(BF16), wsd_ref[...], preferred_element_type=F32)
    base_ref[...] = x1 + g2_ref[...] * shared
    h2l = (h2 - h2b.astype(F32)).astype(BF16)
    nt = lambda a, b: lax.dot_general(a, b, (((1,), (1,)), ((), ())), preferred_element_type=F32)
    lg_ref[...] = nt(wrh_ref[...], h2b) + nt(wrl_ref[...], h2b) + nt(wrh_ref[...], h2l)
    for c in range(ROW_TILE):
        h2t_ref[pl.ds(c, tm, stride=ROW_TILE), :] = h2[:, c * LANES:(c + 1) * LANES]


def _outproj(yr, o, x, gate1, norm2_g, shift2, scale2, gate2, wts, tm, tiles_per_seq):
    t, d = x.shape
    row = lambda n: pl.BlockSpec((tm, n), lambda i: (i, 0))
    mod = lambda m: _mod_spec(m, tm, tiles_per_seq)
    return pl.pallas_call(
        _outproj_body,
        grid=(t // tm,),
        in_specs=[row(RWKV_WIDTH), row(DIFF_V_WIDTH), row(d), mod(gate1), _full_spec(norm2_g), mod(shift2),
                  mod(scale2), mod(gate2)] + [_full_spec(a) for a in wts],
        out_specs=[row(d), pl.BlockSpec((tm * ROW_TILE, LANES), lambda i: (i, 0)),
                   pl.BlockSpec((N_EXPERTS, tm), lambda i: (0, i))],
        out_shape=[jax.ShapeDtypeStruct((t, d), F32),
                   jax.ShapeDtypeStruct((t * ROW_TILE, LANES), F32),
                   jax.ShapeDtypeStruct((N_EXPERTS, t), F32)],
        compiler_params=_params("parallel"),
        name="outproj",
    )(yr, o, x, gate1, norm2_g, shift2, scale2, gate2, *wts)


def _router_body(lg_ref, bias_ref, idx_ref, gate_ref, pos_ref, cnt_ref, run_ref):
    i = pl.program_id(0)
    tm = lg_ref.shape[1]

    @pl.when(i == 0)
    def _():
        run_ref[...] = jnp.zeros(run_ref.shape, F32)

    scores = _sigmoid(lg_ref[...])
    biased = scores + bias_ref[...]
    erow = lax.broadcasted_iota(I32, (N_EXPERTS, tm), 0)
    grow = lax.broadcasted_iota(I32, (GROUP_SIZE, tm), 0)

    def first_argmax(x, rows, limit):
        mx = jnp.max(x, axis=0, keepdims=True)
        return mx, jnp.min(jnp.where(x == mx, rows, limit), axis=0, keepdims=True)

    group_scores = []
    for gidx in range(N_GROUPS):
        xg = biased[gidx * GROUP_SIZE:(gidx + 1) * GROUP_SIZE, :]
        m1, i1 = first_argmax(xg, grow, GROUP_SIZE)
        m2 = jnp.max(jnp.where(grow == i1, NEG_INF, xg), axis=0, keepdims=True)
        group_scores.append(m1 + m2)
    gs = jnp.concatenate(group_scores, axis=0)
    g8 = lax.broadcasted_iota(I32, (N_GROUPS, tm), 0)
    chosen = jnp.zeros((N_GROUPS, tm), I32)
    for _ in range(TOPK_GROUPS):
        _, gi = first_argmax(gs, g8, N_GROUPS)
        hit = g8 == gi
        chosen = jnp.where(hit, 1, chosen)
        gs = jnp.where(hit, NEG_INF, gs)
    cand = jnp.concatenate(
        [jnp.where(chosen[gidx:gidx + 1, :] > 0, biased[gidx * GROUP_SIZE:(gidx + 1) * GROUP_SIZE, :], NEG_INF)
         for gidx in range(N_GROUPS)], axis=0)

    idxs, raws = [], []
    onehot = jnp.zeros((N_EXPERTS, tm), F32)
    for _ in range(TOP_K):
        _, ei = first_argmax(cand, erow, N_EXPERTS)
        hit = erow == ei
        idxs.append(ei)
        raws.append(jnp.sum(jnp.where(hit, scores, 0.0), axis=0, keepdims=True))
        onehot = onehot + hit.astype(F32)
        cand = jnp.where(hit, NEG_INF, cand)
    raw = jnp.concatenate(raws, axis=0)
    gate_ref[...] = raw / jnp.sum(raw, axis=0, keepdims=True) * ROUTE_SCALE
    idx_ref[...] = jnp.concatenate(idxs, axis=0)

    ti = lax.broadcasted_iota(I32, (tm, tm), 0)
    tj = lax.broadcasted_iota(I32, (tm, tm), 1)
    oh = onehot.astype(BF16)
    before = jnp.dot(oh, (ti < tj).astype(BF16), preferred_element_type=F32) + run_ref[...]
    run_ref[...] = run_ref[...] + jnp.dot(oh, jnp.ones((tm, tm), BF16), preferred_element_type=F32)
    pos_ref[...] = jnp.concatenate(
        [jnp.sum(jnp.where(erow == ei, before, 0.0), axis=0, keepdims=True) for ei in idxs], axis=0).astype(I32)
    cnt_ref[...] = run_ref[...]


def _router(logits_t, bias_col):
    t = logits_t.shape[1]
    tm = ROUTE_TILE
    tok = pl.BlockSpec((TOP_K, tm), lambda i: (0, i))
    return pl.pallas_call(
        _router_body,
        grid=(t // tm,),
        in_specs=[pl.BlockSpec((N_EXPERTS, tm), lambda i: (0, i)), _full_spec(bias_col)],
        out_specs=[tok, tok, tok, pl.BlockSpec((N_EXPERTS, tm), lambda i: (0, 0))],
        out_shape=[jax.ShapeDtypeStruct((TOP_K, t), I32), jax.ShapeDtypeStruct((TOP_K, t), F32),
                   jax.ShapeDtypeStruct((TOP_K, t), I32), jax.ShapeDtypeStruct((N_EXPERTS, tm), F32)],
        scratch_shapes=[pltpu.VMEM((N_EXPERTS, tm), F32)],
        compiler_params=_params("arbitrary"),
        name="router",
    )(logits_t, bias_col)


def _assign_body(idx_ref, pos_ref, start_ref, dest_ref):
    tm = idx_ref.shape[1]
    erow = lax.broadcasted_iota(I32, (N_EXPERTS, tm), 0)
    start = start_ref[...]
    idx = idx_ref[...]
    first = jnp.concatenate(
        [jnp.sum(jnp.where(erow == idx[j:j + 1, :], start, 0.0), axis=0, keepdims=True) for j in range(TOP_K)],
        axis=0)
    dest_ref[...] = first.astype(I32) + pos_ref[...]


def _assign(idx, pos, start_col):
    t = idx.shape[1]
    tm = ROUTE_TILE
    tok = pl.BlockSpec((TOP_K, tm), lambda i: (0, i))
    return pl.pallas_call(
        _assign_body,
        grid=(t // tm,),
        in_specs=[tok, tok, _full_spec(start_col)],
        out_specs=tok,
        out_shape=jax.ShapeDtypeStruct((TOP_K, t), I32),
        compiler_params=_params("parallel"),
        name="assign",
    )(idx, pos, start_col)


def _row_copy(src, src_row, dst, dst_row, sem):
    return pltpu.make_async_copy(src.at[pl.ds(pl.multiple_of(src_row * ROW_TILE, ROW_TILE), ROW_TILE), :],
                                 dst.at[pl.ds(pl.multiple_of(dst_row * ROW_TILE, ROW_TILE), ROW_TILE), :], sem)


def _dispatch_body(zflag_ref, h_ref, dest_ref, xs_ref, zero_ref, zsem, rsem):
    i = pl.program_id(0)
    tm = dest_ref.shape[1]
    blk_rows = EXPERT_BLOCK * ROW_TILE
    n_blocks = xs_ref.shape[0] // blk_rows

    def zero_copy(b):
        start = pl.multiple_of(b * blk_rows, blk_rows)
        return pltpu.make_async_copy(zero_ref, xs_ref.at[pl.ds(start, blk_rows), :], zsem)

    @pl.when(i == 0)
    def _():
        zero_ref[...] = jnp.zeros(zero_ref.shape, F32)

        def issue(b, carry):
            @pl.when(zflag_ref[b] > 0)
            def _():
                zero_copy(b).start()
            return carry

        def drain(b, carry):
            @pl.when(zflag_ref[b] > 0)
            def _():
                zero_copy(b).wait()
            return carry

        lax.fori_loop(0, n_blocks, issue, 0)
        lax.fori_loop(0, n_blocks, drain, 0)

    def issue_rows(t, carry):
        for j in range(TOP_K):
            _row_copy(h_ref, t, xs_ref, dest_ref[j, t], rsem).start()
        return carry

    lax.fori_loop(0, tm, issue_rows, 0)
    for _ in range(TOP_K):
        pltpu.make_async_copy(h_ref, xs_ref.at[pl.ds(0, tm * ROW_TILE), :], rsem).wait()


def _dispatch(h2t, dest, zero_flags, n_rows):
    t = dest.shape[1]
    tm = ROUTE_TILE
    grid_spec = pltpu.PrefetchScalarGridSpec(
        num_scalar_prefetch=1,
        grid=(t // tm,),
        in_specs=[pl.BlockSpec((tm * ROW_TILE, LANES), lambda i, *_: (i, 0)),
                  pl.BlockSpec((TOP_K, tm), lambda i, *_: (0, i), memory_space=pltpu.SMEM)],
        out_specs=pl.BlockSpec(memory_space=pl.ANY),
        scratch_shapes=[pltpu.VMEM((EXPERT_BLOCK * ROW_TILE, LANES), F32),
                        pltpu.SemaphoreType.DMA, pltpu.SemaphoreType.DMA],
    )
    return pl.pallas_call(
        _dispatch_body,
        grid_spec=grid_spec,
        out_shape=jax.ShapeDtypeStruct((n_rows * ROW_TILE, LANES), F32),
        compiler_params=_params("arbitrary"),
        name="dispatch",
    )(zero_flags, h2t, dest)


def _expert_body(first_ref, count_ref, nu_ref, xs_ref, wg_ref, wu_ref, wd_ref, y_ref,
                 xin_ref, yout_ref, wgb_ref, wub_ref, wdb_ref, in_sem, out_sem):
    e = pl.program_id(0)
    rows = EXPERT_BLOCK
    blk_rows = EXPERT_BLOCK * ROW_TILE
    nb = count_ref[e]
    b0 = first_ref[e]

    n_used = nu_ref[0]
    n_in = xin_ref.shape[0]
    n_out = yout_ref.shape[0]
    ahead = n_in - 1
    part = blk_rows // EXPERT_DMA_SPLIT

    def hbm_rows(g, k=0, n=blk_rows):
        return pl.ds(pl.multiple_of(g * blk_rows + k * part, part), n)

    def start_in(g):
        slot = g % n_in
        for k in range(EXPERT_DMA_SPLIT):
            pltpu.make_async_copy(xs_ref.at[hbm_rows(g, k, part), :], xin_ref.at[slot, pl.ds(k * part, part), :],
                                  in_sem.at[slot]).start()

    def wait_in(g):
        slot = g % n_in
        pltpu.make_async_copy(xs_ref.at[hbm_rows(g), :], xin_ref.at[slot], in_sem.at[slot]).wait()

    def start_out(g):
        slot = g % n_out
        for k in range(EXPERT_DMA_SPLIT):
            pltpu.make_async_copy(yout_ref.at[slot, pl.ds(k * part, part), :], y_ref.at[hbm_rows(g, k, part), :],
                                  out_sem.at[slot]).start()

    def wait_out(g):
        slot = g % n_out
        pltpu.make_async_copy(yout_ref.at[slot], y_ref.at[hbm_rows(g), :], out_sem.at[slot]).wait()

    @pl.when(e == 0)
    def _():
        for g in range(ahead):
            @pl.when(g < n_used)
            def _():
                start_in(g)

    @pl.when(nb > 0)
    def _():
        wgb_ref[...] = wg_ref[...].astype(BF16)
        wub_ref[...] = wu_ref[...].astype(BF16)
        wdb_ref[...] = wd_ref[...].astype(BF16)

        def step(b, carry):
            g = b0 + b
            wait_in(g)

            @pl.when(g + ahead < n_used)
            def _():
                start_in(g + ahead)

            @pl.when(g >= n_out)
            def _():
                wait_out(g - n_out)

            islot = g % n_in
            oslot = g % n_out
            x = jnp.concatenate([xin_ref[islot, pl.ds(c, rows, stride=ROW_TILE), :] for c in range(ROW_TILE)],
                                axis=1).astype(BF16)
            hidden = _silu(jnp.dot(x, wgb_ref[...], preferred_element_type=F32)) * jnp.dot(
                x, wub_ref[...], preferred_element_type=F32)
            y = jnp.dot(hidden.astype(BF16), wdb_ref[...], preferred_element_type=F32)
            for c in range(ROW_TILE):
                yout_ref[oslot, pl.ds(c, rows, stride=ROW_TILE), :] = y[:, c * LANES:(c + 1) * LANES]
            start_out(g)
            return carry

        lax.fori_loop(0, nb, step, 0)

    @pl.when(e == pl.num_programs(0) - 1)
    def _():
        def drain(g, carry):
            wait_out(g)
            return carry

        lax.fori_loop(jnp.maximum(n_used - n_out, 0), n_used, drain, 0)
        n_blocks = y_ref.shape[0] // blk_rows
        yout_ref[0] = jnp.zeros(yout_ref.shape[1:], F32)

        def tail(b, carry):
            cp = pltpu.make_async_copy(
                yout_ref.at[0], y_ref.at[pl.ds(pl.multiple_of(b * blk_rows, blk_rows), blk_rows), :], out_sem.at[0])
            cp.start()
            cp.wait()
            return carry

        lax.fori_loop(nu_ref[0], n_blocks, tail, 0)


def _experts(xs, first_block, block_count, n_used, we_gate, we_up, we_down):
    d, f = we_gate.shape[1], we_gate.shape[2]
    blk = (EXPERT_BLOCK * ROW_TILE, LANES)
    wspec = lambda a, b: pl.BlockSpec((None, a, b), lambda e, *_: (e, 0, 0))
    grid_spec = pltpu.PrefetchScalarGridSpec(
        num_scalar_prefetch=3,
        grid=(N_EXPERTS,),
        in_specs=[pl.BlockSpec(memory_space=pl.ANY), wspec(d, f), wspec(d, f), wspec(f, d)],
        out_specs=pl.BlockSpec(memory_space=pl.ANY),
        scratch_shapes=[pltpu.VMEM((EXPERT_IN_SLOTS,) + blk, F32), pltpu.VMEM((EXPERT_OUT_SLOTS,) + blk, F32),
                        pltpu.VMEM((d, f), BF16), pltpu.VMEM((d, f), BF16), pltpu.VMEM((f, d), BF16),
                        pltpu.SemaphoreType.DMA((EXPERT_IN_SLOTS,)), pltpu.SemaphoreType.DMA((EXPERT_OUT_SLOTS,))],
    )
    return pl.pallas_call(
        _expert_body,
        grid_spec=grid_spec,
        out_shape=jax.ShapeDtypeStruct(xs.shape, F32),
        compiler_params=_params("arbitrary"),
        name="experts",
    )(first_block, block_count, n_used, xs, we_gate, we_up, we_down)


def _combine_body(base_ref, g2_ref, gate_ref, dest_ref, y_ref, out_ref, buf_ref, sem):
    tm = base_ref.shape[0]
    slot_rows = tm * ROW_TILE

    def issue(t, carry):
        for j in range(TOP_K):
            _row_copy(y_ref, dest_ref[j, t], buf_ref, j * tm + t, sem).start()
        return carry

    lax.fori_loop(0, tm, issue, 0)
    pltpu.make_async_copy(y_ref.at[pl.ds(0, TOP_K * slot_rows), :], buf_ref, sem).wait()
    gates = gate_ref[...]
    gcols = [jnp.broadcast_to(gates[:, j:j + 1], (tm, LANES)) for j in range(TOP_K)]
    for c in range(ROW_TILE):
        cols = slice(c * LANES, (c + 1) * LANES)
        routed = sum(gcols[j] * buf_ref[pl.ds(j * slot_rows + c, tm, stride=ROW_TILE), :] for j in range(TOP_K))
        out_ref[:, cols] = base_ref[:, cols] + g2_ref[:, cols] * routed


def _combine(base, gate2, gates, dest, y_rows, tok_offset, tiles_per_seq):
    t, d = base.shape
    tm = ROUTE_TILE
    off = tok_offset // tm
    return pl.pallas_call(
        _combine_body,
        grid=(t // tm,),
        in_specs=[pl.BlockSpec((tm, d), lambda i: (i, 0)), _mod_spec(gate2, tm, tiles_per_seq),
                  pl.BlockSpec((tm, TOP_K), lambda i: (i + off, 0)),
                  pl.BlockSpec((TOP_K, tm), lambda i: (0, i + off), memory_space=pltpu.SMEM),
                  pl.BlockSpec(memory_space=pl.ANY)],
        out_specs=pl.BlockSpec((tm, d), lambda i: (i, 0)),
        out_shape=jax.ShapeDtypeStruct((t, d), F32),
        scratch_shapes=[pltpu.VMEM((TOP_K * tm * ROW_TILE, LANES), F32), pltpu.SemaphoreType.DMA],
        compiler_params=_params("arbitrary"),
        name="combine",
    )(base, gate2, gates, dest, y_rows)


def _segment_ones(width, seg):
    ids = jnp.arange(width) // seg
    return (ids[:, None] == ids[None, :]).astype(BF16)


def _pad_rows(w, start, total):
    return jnp.zeros((total, w.shape[1]), w.dtype).at[start:start + w.shape[0]].set(w)


def kernel(x_prompt, x_sample, cache_k, cache_v, state_wkv, state_shift, page_table, c_prompt, c_sample, w_ada, b_ada, norm1_g, norm2_g, w_in, mu_shift, w0, w_lora_up, a0, a_lora_up, g_lora_up, k_k, k_a, r_k, lnx_g, lnx_b, qn_g, kn_g, lam_q1, lam_k1, lam_q2, lam_k2, subln_g, w_out, w_router, router_bias, we_gate, we_up, we_down, ws_gate, ws_up, ws_down):
    depth = w_in.shape[0]
    nb, seq, d = x_prompt.shape
    db, dseq, _ = x_sample.shape
    n_pool, page = cache_k.shape[1], cache_k.shape[2]
    tp, ts = nb * seq, db * dseq
    w3 = 3 * RWKV_WIDTH
    row = lambda a: a.reshape(1, -1)
    seg512 = _segment_ones(RWKV_WIDTH, RWKV_HEAD_DIM)
    tm_p = min(512, seq)
    tm_s = min(256, ts)
    chunk_p = min(64, seq)

    xp = x_prompt.reshape(tp, d)
    xs = x_sample.reshape(ts, d)
    cache_k2 = jnp.transpose(cache_k, (0, 1, 3, 4, 5, 2)).reshape(depth * n_pool * DIFF_QK_WIDTH, page)
    cache_v2 = cache_v.reshape(depth * n_pool * page * DIFF_HEADS, DIFF_DV)
    n_cond = nb + db
    cond = jnp.concatenate([c_prompt, c_sample], axis=0)
    cond = jnp.pad(cond, ((0, -n_cond % SUBLANES), (0, 0)))

    outs = {name: [] for name in ("kp", "vp", "wp", "sp", "ks", "vs", "ws", "ss")}
    for layer in range(depth):
        lam_init = 0.8 - 0.6 * math.exp(-0.3 * layer)
        mods = _ada(cond, w_ada[layer], row(b_ada[layer]))
        mod_p = [m.reshape(nb, 1, d) for m in jnp.split(mods[:nb], 6, axis=-1)]
        mod_s = [jnp.repeat(m, dseq, axis=0) for m in jnp.split(mods[nb:n_cond], 6, axis=-1)]

        wi = w_in[layer]
        in_wts = (wi[:, :w3].astype(BF16),
                  jnp.pad(wi[:, w3:RWKV_PROJ], ((0, 0), (0, LORA_PAD - LORA_WIDTH))).astype(BF16),
                  wi[:, RWKV_PROJ:RWKV_PROJ + DIFF_QK_WIDTH].astype(BF16),
                  wi[:, RWKV_PROJ + DIFF_QK_WIDTH:RWKV_PROJ + 2 * DIFF_QK_WIDTH].astype(BF16),
                  wi[:, RWKV_PROJ + 2 * DIFF_QK_WIDTH:].astype(BF16),
                  row(jnp.tile(qn_g[layer], DIFF_QK_WIDTH // DIFF_DK)),
                  row(jnp.tile(kn_g[layer], DIFF_QK_WIDTH // DIFF_DK)),
                  seg512)
        mu = mu_shift[layer]
        rwkv_wts = (row(mu[:w3]), row(jnp.pad(mu[w3:], (0, LORA_PAD - LORA_WIDTH))), row(w0[layer]),
                    _pad_rows(w_lora_up[layer], 0, LORA_PAD).astype(BF16), row(a0[layer]),
                    _pad_rows(a_lora_up[layer], DECAY_LORA, LORA_PAD).astype(BF16),
                    _pad_rows(g_lora_up[layer], DECAY_LORA + ICLR_LORA, LORA_PAD).astype(BF16),
                    row(k_k[layer]), row(k_a[layer]), row(r_k[layer]), row(lnx_g[layer]), row(lnx_b[layer]), seg512)
        lam_vecs = jnp.stack([lam_q1[layer], lam_k1[layer], lam_q2[layer], lam_k2[layer]])
        sg = row(subln_g[layer])
        wr = w_router[layer].T
        wr_hi = wr.astype(BF16)
        out_wts = (w_out[layer][:RWKV_WIDTH].astype(BF16), w_out[layer][RWKV_WIDTH:].astype(BF16),
                   ws_gate[layer].astype(BF16), ws_up[layer].astype(BF16), ws_down[layer].astype(BF16),
                   wr_hi, (wr - wr_hi.astype(F32)).astype(BF16))

        def token_mix(x, mod, tm, tiles_per_seq, seq_len, chunk, shift_prev, wkv0, long_seq):
            pm, plo, *qkv = _inproj(x, row(norm1_g[layer]), mod[0], mod[1], in_wts, tm, tiles_per_seq, long_seq)
            n_seq = x.shape[0] // seq_len
            shift_m = shift_prev[:, :, :w3]
            shift_l = jnp.pad(shift_prev[:, :, w3:], ((0, 0), (0, 0), (0, LORA_PAD - LORA_WIDTH)))
            if chunk == seq_len:
                per_step = RWKV_SEQS_PER_STEP if n_seq % RWKV_SEQS_PER_STEP == 0 else 1
            else:
                per_step = RWKV_CHUNKS_PER_STEP if seq_len % (chunk * RWKV_CHUNKS_PER_STEP) == 0 else 1
            yr, wkv = _rwkv(pm, plo, shift_m, shift_l, wkv0, rwkv_wts, seq_len, chunk, per_step)
            last = jnp.concatenate([pm.reshape(n_seq, seq_len, w3)[:, -1:],
                                    plo.reshape(n_seq, seq_len, LORA_PAD)[:, -1:, :LORA_WIDTH]], axis=-1)
            return yr, wkv, last, qkv

        yr_p, wkv_p, last_p, (kt_p, v4_p, qb_p, vb_p) = token_mix(
            xp, mod_p, tm_p, seq // tm_p, seq, chunk_p,
            jnp.zeros((nb, 1, RWKV_PROJ), F32), jnp.zeros((nb, RWKV_HEADS, RWKV_HEAD_DIM, RWKV_HEAD_DIM), F32), True)
        o_p = _attn_prompt(qb_p, kt_p, vb_p, lam_vecs, sg, seq, lam_init)
        yr_s, wkv_s, last_s, (q_s, k_s, v_s) = token_mix(
            xs, mod_s, tm_s, 1, dseq, dseq, state_shift[layer], state_wkv[layer], False)
        o_s = _attn_sample(q_s, k_s, v_s, cache_k2, cache_v2, depth * n_pool, page_table, layer * n_pool, lam_vecs,
                           sg, dseq, lam_init)

        base_p, h2t_p, lg_p = _outproj(yr_p, o_p, xp, mod_p[2], row(norm2_g[layer]), mod_p[3], mod_p[4], mod_p[5],
                                       out_wts, tm_p, seq // tm_p)
        base_s, h2t_s, lg_s = _outproj(yr_s, o_s, xs, mod_s[2], row(norm2_g[layer]), mod_s[3], mod_s[4], mod_s[5],
                                       out_wts, tm_s, 1)

        t_all = tp + ts
        idx, gate, pos, cnt = _router(jnp.concatenate([lg_p, lg_s], axis=1), router_bias[layer].reshape(-1, 1))
        counts = cnt[:, 0].astype(I32)
        padded = (counts + EXPERT_BLOCK - 1) // EXPERT_BLOCK * EXPERT_BLOCK
        end_padded = jnp.cumsum(padded)
        dest = _assign(idx, pos, (end_padded - padded).astype(F32).reshape(-1, 1))
        n_rows = (t_all * TOP_K + N_EXPERTS * (EXPERT_BLOCK - 1) + EXPERT_BLOCK - 1) // EXPERT_BLOCK * EXPERT_BLOCK
        n_blocks = n_rows // EXPERT_BLOCK
        blk_ids = jnp.arange(n_blocks, dtype=I32)
        n_used = (end_padded[-1:] // EXPERT_BLOCK).astype(I32)
        is_expert_end = jnp.any(end_padded[None, :] == (blk_ids[:, None] + 1) * EXPERT_BLOCK, axis=1)
        zero_flags = jnp.logical_or(blk_ids >= n_used[0], is_expert_end).astype(I32)
        rows_in = _dispatch(jnp.concatenate([h2t_p, h2t_s], axis=0), dest, zero_flags, n_rows)
        rows_out = _experts(rows_in, ((end_padded - padded) // EXPERT_BLOCK).astype(I32),
                            (padded // EXPERT_BLOCK).astype(I32), n_used, we_gate[layer], we_up[layer],
                            we_down[layer])
        gates_t = gate.T
        xp = _combine(base_p, mod_p[5], gates_t, dest, rows_out, 0, seq // ROUTE_TILE)
        xs = _combine(base_s, mod_s[5], gates_t, dest, rows_out, tp, 1)

        outs["kp"].append(jnp.transpose(kt_p.reshape(nb, DIFF_HEADS, 2, DIFF_DK, seq), (0, 4, 1, 2, 3)))
        outs["vp"].append(v4_p.reshape(nb, seq, DIFF_HEADS, DIFF_DV))
        outs["wp"].append(wkv_p)
        outs["sp"].append(last_p)
        outs["ks"].append(k_s.reshape(db, dseq, DIFF_HEADS, 2, DIFF_DK))
        outs["vs"].append(v_s.reshape(db, dseq, DIFF_HEADS, DIFF_DV))
        outs["ws"].append(wkv_s)
        outs["ss"].append(last_s)

    st = {name: jnp.stack(v) for name, v in outs.items()}
    return (xp.reshape(nb, seq, d), xs.reshape(db, dseq, d), st["kp"], st["vp"], st["wp"], st["sp"],
            st["ks"], st["vs"], st["ws"], st["ss"])
```

```python
import functools
import math

import jax
import jax.numpy as jnp
from jax import lax
from jax.experimental import pallas as pl
from jax.experimental.pallas import tpu as pltpu

F32 = jnp.float32
BF16 = jnp.bfloat16
I32 = jnp.int32

RWKV_HEADS = 8
RWKV_HEAD_DIM = 64
RWKV_WIDTH = RWKV_HEADS * RWKV_HEAD_DIM
DECAY_LORA = 32
ICLR_LORA = 32
GATE_LORA = 96
LORA_WIDTH = DECAY_LORA + ICLR_LORA + GATE_LORA
LORA_PAD = 256
RWKV_PROJ = 3 * RWKV_WIDTH + LORA_WIDTH
GN_EPS = 64e-5
DIFF_HEADS = 4
DIFF_DK = 64
DIFF_DV = 2 * DIFF_DK
DIFF_QK_WIDTH = DIFF_HEADS * 2 * DIFF_DK
DIFF_V_WIDTH = DIFF_HEADS * DIFF_DV
DIFF_SCALE = DIFF_DK ** -0.5
N_EXPERTS = 256
N_GROUPS = 8
GROUP_SIZE = N_EXPERTS // N_GROUPS
TOPK_GROUPS = 4
TOP_K = 8
ROUTE_SCALE = 2.5
NORM_EPS = 1e-6

LANES = 128
SUBLANES = 8
ROW_TILE = 8
VMEM_LIMIT = 48 * 1024 * 1024

EXPERT_BLOCK = 128
EXPERT_IN_SLOTS = 4
EXPERT_OUT_SLOTS = 3
EXPERT_DMA_SPLIT = 4
ROUTE_TILE = 128
RWKV_CHUNKS_PER_STEP = 4
RWKV_SEQS_PER_STEP = 8
ATTN_TILE = 2048
ATTN_ROW_CHUNK = 256
NEG_INF = float("-inf")


def _bdot(a, b):
    return jnp.dot(a.astype(BF16), b.astype(BF16), preferred_element_type=F32)


def _bdot_nt(a, b):
    return lax.dot_general(a.astype(BF16), b.astype(BF16), (((1,), (1,)), ((), ())), preferred_element_type=F32)


def _bdot_tn(a, b):
    return lax.dot_general(a.astype(BF16), b.astype(BF16), (((0,), (0,)), ((), ())), preferred_element_type=F32)


def _sigmoid(x):
    return 1.0 / (1.0 + jnp.exp(-x))


def _silu(x):
    return x * _sigmoid(x)


def _params(*sem, vmem=VMEM_LIMIT):
    return pltpu.CompilerParams(dimension_semantics=sem, vmem_limit_bytes=vmem)


def _ada_body(c_ref, w_ref, b_ref, o_ref):
    o_ref[...] = _bdot(_silu(c_ref[...]), w_ref[...]) + b_ref[...]


def _ada(c, w, b):
    rows, d = c.shape
    n = w.shape[1]
    tn = 512
    return pl.pallas_call(
        _ada_body,
        grid=(n // tn,),
        in_specs=[pl.BlockSpec((rows, d), lambda j: (0, 0)),
                  pl.BlockSpec((d, tn), lambda j: (0, j)),
                  pl.BlockSpec((1, tn), lambda j: (0, j))],
        out_specs=pl.BlockSpec((rows, tn), lambda j: (0, j)),
        out_shape=jax.ShapeDtypeStruct((rows, n), F32),
        compiler_params=_params("parallel"),
        name="ada",
    )(c, w, b)


def _mod_spec(mod, tm, tiles_per_seq):
    if mod.ndim == 3:
        return pl.BlockSpec((None, 1, mod.shape[-1]), lambda i: (i // tiles_per_seq, 0, 0))
    return pl.BlockSpec((tm, mod.shape[-1]), lambda i: (i, 0))


def _full_spec(a):
    nd = a.ndim
    return pl.BlockSpec(a.shape, lambda *_: (0,) * nd)


def _rms(x, g):
    return x * lax.rsqrt(jnp.mean(x * x, axis=-1, keepdims=True) + NORM_EPS) * g


def _inproj_body(x_ref, g_ref, sh_ref, sc_ref, wm_ref, wl_ref, wq_ref, wk_ref, wv_ref, qg_ref, kg_ref, seg_ref,
                 pm_ref, pl_ref, *out_refs, long_seq):
    tm = x_ref.shape[0]
    h = (_rms(x_ref[...], g_ref[...]) * (1.0 + sc_ref[...]) + sh_ref[...]).astype(BF16)
    pm_ref[...] = jnp.dot(h, wm_ref[...], preferred_element_type=F32)
    pl_ref[...] = jnp.dot(h, wl_ref[...], preferred_element_type=F32)
    seg = seg_ref[...]

    def head_norm(z, gain):
        ms = _bdot(z * z, seg) * (1.0 / DIFF_DK)
        return z * lax.rsqrt(ms + NORM_EPS) * gain

    q = head_norm(jnp.dot(h, wq_ref[...], preferred_element_type=F32), qg_ref[...])
    k = head_norm(jnp.dot(h, wk_ref[...], preferred_element_type=F32), kg_ref[...])
    v = jnp.dot(h, wv_ref[...], preferred_element_type=F32)
    if long_seq:
        kt_ref, v4_ref, qb_ref, vb_ref = out_refs
        kt_ref[...] = k.T
        for hd in range(DIFF_HEADS):
            v4_ref[pl.ds(hd, tm, stride=DIFF_HEADS), :] = v[:, hd * DIFF_DV:(hd + 1) * DIFF_DV]
        qb_ref[...] = (q * DIFF_SCALE).astype(BF16)
        vb_ref[...] = v.astype(BF16)
    else:
        q_ref, k_ref, v_ref = out_refs
        q_ref[...] = q
        k_ref[...] = k
        v_ref[...] = v


def _inproj(x, g, shift, scale, wts, tm, tiles_per_seq, long_seq):
    t, d = x.shape
    wm, wl, wq, wk, wv, qg, kg, seg = wts
    row = lambda n: pl.BlockSpec((tm, n), lambda i: (i, 0))
    out_specs = [row(3 * RWKV_WIDTH), row(LORA_PAD)]
    out_shape = [jax.ShapeDtypeStruct((t, 3 * RWKV_WIDTH), F32), jax.ShapeDtypeStruct((t, LORA_PAD), F32)]
    if long_seq:
        n_seq = t // (tm * tiles_per_seq)
        out_specs += [pl.BlockSpec((DIFF_QK_WIDTH, tm), lambda i: (i // tiles_per_seq, i % tiles_per_seq)),
                      pl.BlockSpec((tm * DIFF_HEADS, DIFF_DV), lambda i: (i, 0)),
                      row(DIFF_QK_WIDTH), row(DIFF_V_WIDTH)]
        out_shape += [jax.ShapeDtypeStruct((n_seq * DIFF_QK_WIDTH, tm * tiles_per_seq), F32),
                      jax.ShapeDtypeStruct((t * DIFF_HEADS, DIFF_DV), F32),
                      jax.ShapeDtypeStruct((t, DIFF_QK_WIDTH), BF16), jax.ShapeDtypeStruct((t, DIFF_V_WIDTH), BF16)]
    else:
        out_specs += [row(DIFF_QK_WIDTH), row(DIFF_QK_WIDTH), row(DIFF_V_WIDTH)]
        out_shape += [jax.ShapeDtypeStruct((t, n), F32) for n in (DIFF_QK_WIDTH, DIFF_QK_WIDTH, DIFF_V_WIDTH)]
    return pl.pallas_call(
        functools.partial(_inproj_body, long_seq=long_seq),
        grid=(t // tm,),
        in_specs=[row(d), _full_spec(g), _mod_spec(shift, tm, tiles_per_seq), _mod_spec(scale, tm, tiles_per_seq)]
                 + [_full_spec(a) for a in (wm, wl, wq, wk, wv, qg, kg, seg)],
        out_specs=out_specs,
        out_shape=out_shape,
        compiler_params=_params("parallel"),
        name="inproj",
    )(x, g, shift, scale, wm, wl, wq, wk, wv, qg, kg, seg)


def _split3(x):
    hi = x.astype(BF16)
    r1 = x - hi.astype(F32)
    mid = r1.astype(BF16)
    lo = (r1 - mid.astype(F32)).astype(BF16)
    return hi, mid, lo


def _rwkv_body(pm_ref, pl_ref, pm8_ref, pl8_ref, sm_ref, sl_ref, s0_ref,
               mum_ref, mul_ref, w0_ref, wupw_ref, a0_ref, wupa_ref, wupg_ref, kk_ref, ka_ref, rk_ref,
               lg_ref, lb_ref, seg_ref,
               y_ref, sout_ref, state_ref, *, chunk, whole_seqs):
    c = pl.program_id(1)
    nc = pl.num_programs(1)
    C = chunk
    W = RWKV_WIDTH
    N = RWKV_HEAD_DIM
    pm = pm_ref[...]
    plo = pl_ref[...]

    if whole_seqs:
        n_seq = pm.shape[0] // C
        rows_of = lambda ref: jnp.concatenate(
            [jnp.broadcast_to(ref[s], (C, ref.shape[-1])) for s in range(n_seq)], axis=0)
        prev_m, prev_l = rows_of(sm_ref), rows_of(sl_ref)

        def shifted(cur, prev_rows):
            rows = lax.broadcasted_iota(I32, cur.shape, 0)
            return jnp.where(rows % C == 0, prev_rows, pltpu.roll(cur, 1, 0))
    else:
        @pl.when(c == 0)
        def _():
            state_ref[...] = s0_ref[0]

        first = c == 0
        prev_m = jnp.where(first, sm_ref[0], pm8_ref[SUBLANES - 1:SUBLANES, :])
        prev_l = jnp.where(first, sl_ref[0], pl8_ref[SUBLANES - 1:SUBLANES, :])

        def shifted(cur, prev_row):
            rows = lax.broadcasted_iota(I32, cur.shape, 0)
            return jnp.where(rows == 0, prev_row, pltpu.roll(cur, 1, 0))

    xm = pm + (shifted(pm, prev_m) - pm) * mum_ref[...]
    xl = plo + (shifted(plo, prev_l) - plo) * mul_ref[...]
    r = xm[:, 0:W]
    k = xm[:, W:2 * W]
    v = xm[:, 2 * W:3 * W]
    seg = seg_ref[...]

    z = -(w0_ref[...] + _bdot(jnp.tanh(xl), wupw_ref[...]))
    softplus = jnp.maximum(z, 0.0) + jnp.log(1.0 + jnp.exp(-jnp.abs(z)))
    w = -softplus - 0.5
    a = _sigmoid(a0_ref[...] + _bdot(xl, wupa_ref[...]))
    g = _bdot(_sigmoid(xl), wupg_ref[...])
    kk = k * kk_ref[...]
    kk = kk / jnp.maximum(jnp.sqrt(_bdot(kk * kk, seg)), 1e-12)
    k = k * (1.0 + (a - 1.0) * ka_ref[...])
    logdec = -jnp.exp(w)

    rows_blk = pm.shape[0]
    n_sub = rows_blk // C
    bi = lax.broadcasted_iota(I32, (rows_blk, rows_blk), 0)
    bj = lax.broadcasted_iota(I32, (rows_blk, rows_blk), 1)
    lower = jnp.logical_and(bi >= bj, bi // C == bj // C).astype(BF16)
    cum = sum(jnp.dot(lower, part, preferred_element_type=F32) for part in _split3(logdec))
    cum_end = jnp.concatenate(
        [jnp.broadcast_to(cum[(s + 1) * C - 1:(s + 1) * C, :], (C, W)) for s in range(n_sub)], axis=0)
    a_t = -kk * jnp.exp(cum - logdec)
    r_t = r * jnp.exp(cum)
    inv = jnp.exp(-cum)
    b_t = kk * a * inv
    k_t = k * inv
    to_end = jnp.exp(cum_end - cum)
    b_e = kk * a * to_end
    k_e = k * to_end
    g_end = jnp.exp(cum_end)

    ti = lax.broadcasted_iota(I32, (C, C), 0)
    tj = lax.broadcasted_iota(I32, (C, C), 1)
    eye = (ti == tj).astype(F32)
    ri = lax.broadcasted_iota(I32, (2 * C, 2 * C), 0)
    ci = lax.broadcasted_iota(I32, (2 * C, 2 * C), 1)
    tr = jnp.where(ri >= C, ri - C, ri)
    tc = jnp.where(ci >= C, ci - C, ci)
    mask = jnp.logical_or(tr > tc, jnp.logical_and(ri >= C, tr == tc))
    zeros_cn = jnp.zeros((C, N), F32)
    levels = int(math.log2(C))
    heads = range(RWKV_HEADS)
    pairs = [(s, h) for s in range(n_sub) for h in heads]
    tile = lambda z, s, h: z[s * C:(s + 1) * C, h * N:(h + 1) * N]
    ah = [tile(a_t, s, h) for s, h in pairs]
    rh = [tile(r_t, s, h) for s, h in pairs]
    vh = [tile(v, s, h) for s, h in pairs]
    m_all = [jnp.where(mask, _bdot_nt(jnp.concatenate([ah[i], rh[i]], axis=0),
                                      jnp.concatenate([tile(b_t, s, h), tile(k_t, s, h)], axis=0)), 0.0)
             for i, (s, h) in enumerate(pairs)]
    m_top = [m[0:C, :] for m in m_all]
    m_bot = [m[C:2 * C, :] for m in m_all]
    akv = [_bdot(m_top[i], jnp.concatenate([zeros_cn, vh[i]], axis=0)) for i in range(len(pairs))]
    power = [m[:, 0:C] for m in m_top]
    t_inv = [eye + p for p in power]
    for _ in range(levels - 1):
        power = [_bdot(p, p) for p in power]
        t_inv = [t + _bdot(t, p) for t, p in zip(t_inv, power)]
    w_mat = [_bdot(t_inv[i], ah[i]) for i in range(len(pairs))]
    u0 = [_bdot(t_inv[i], akv[i]) for i in range(len(pairs))]
    state = None if whole_seqs else [state_ref[h] for h in heads]
    y_rows = []
    for s in range(n_sub):
        at = lambda h: s * RWKV_HEADS + h
        if whole_seqs:
            state = [s0_ref[s, h] for h in heads]
        x = [_bdot_nt(jnp.concatenate([w_mat[at(h)], rh[at(h)]], axis=0), state[h]) for h in heads]
        uv = [jnp.concatenate([x[h][0:C, :] + u0[at(h)], vh[at(h)]], axis=0) for h in heads]
        y_rows.append(jnp.concatenate([x[h][C:2 * C, :] + _bdot(m_bot[at(h)], uv[h]) for h in heads], axis=1))
        state = [state[h] * tile(g_end, s, h)[0:1, :]
                 + _bdot_tn(uv[h], jnp.concatenate([tile(b_e, s, h), tile(k_e, s, h)], axis=0)) for h in heads]
        if whole_seqs:
            for h in heads:
                sout_ref[s, h] = state[h]
    if not whole_seqs:
        for h in heads:
            state_ref[h] = state[h]
    y = jnp.concatenate(y_rows, axis=0)

    mean = _bdot(y, seg) * (1.0 / N)
    yc = y - mean
    var = _bdot(yc * yc, seg) * (1.0 / N)
    yn = yc * lax.rsqrt(var + GN_EPS) * lg_ref[...] + lb_ref[...]
    bonus = _bdot(r * k * rk_ref[...], seg) * v
    y_ref[...] = ((yn + bonus) * g).astype(y_ref.dtype)

    if not whole_seqs:
        @pl.when(c == nc - 1)
        def _():
            sout_ref[0] = state_ref[...]


def _rwkv(pm, plo, shift_m, shift_l, s0, wts, seq_len, chunk, chunks_per_step):
    t_total = pm.shape[0]
    n_seq = t_total // seq_len
    whole_seqs = chunk == seq_len
    rows = chunk * chunks_per_step
    seqs = chunks_per_step if whole_seqs else 1
    ncnk = 1 if whole_seqs else seq_len // rows
    c8 = rows // SUBLANES
    prev8 = lambda b, c: (jnp.maximum(b * (rows if whole_seqs else seq_len) // SUBLANES + c * c8 - 1, 0), 0)
    cur = lambda b, c: (b * ncnk + c, 0)
    per_seq3 = lambda n: pl.BlockSpec((seqs, 1, n), lambda b, c: (b, 0, 0))
    state_spec = pl.BlockSpec((seqs, RWKV_HEADS, RWKV_HEAD_DIM, RWKV_HEAD_DIM), lambda b, c: (b, 0, 0, 0))
    return pl.pallas_call(
        functools.partial(_rwkv_body, chunk=chunk, whole_seqs=whole_seqs),
        grid=(n_seq // seqs, ncnk),
        in_specs=[pl.BlockSpec((rows, 3 * RWKV_WIDTH), cur), pl.BlockSpec((rows, LORA_PAD), cur),
                  pl.BlockSpec((SUBLANES, 3 * RWKV_WIDTH), prev8), pl.BlockSpec((SUBLANES, LORA_PAD), prev8),
                  per_seq3(3 * RWKV_WIDTH), per_seq3(LORA_PAD), state_spec]
                 + [_full_spec(a) for a in wts],
        out_specs=[pl.BlockSpec((rows, RWKV_WIDTH), cur), state_spec],
        out_shape=[jax.ShapeDtypeStruct((t_total, RWKV_WIDTH), BF16 if chunk % 16 == 0 else F32),
                   jax.ShapeDtypeStruct(s0.shape, F32)],
        scratch_shapes=[pltpu.VMEM((RWKV_HEADS, RWKV_HEAD_DIM, RWKV_HEAD_DIM), F32)],
        compiler_params=_params("parallel", "arbitrary"),
        name="rwkv",
    )(pm, plo, pm, plo, shift_m, shift_l, s0, *wts)


def _lambda(lam_ref, lam_init):
    lv = lam_ref[...]
    return (jnp.exp(jnp.sum(lv[0:1, :] * lv[1:2, :], axis=-1, keepdims=True))
            - jnp.exp(jnp.sum(lv[2:3, :] * lv[3:4, :], axis=-1, keepdims=True)) + lam_init)


def _subln(o, g, lam_init):
    return o * lax.rsqrt(jnp.mean(o * o, axis=-1, keepdims=True) + NORM_EPS) * g * (1.0 - lam_init)


def _attn_prompt_body(qi_ref, ki_ref, q_ref, kt_ref, v_ref, lam_ref, sg_ref, o_ref,
                      qm_ref, m_ref, acc_ref, *, lam_init):
    p = pl.program_id(2)
    qi = qi_ref[p]
    ki = ki_ref[p]
    tq = q_ref.shape[0]
    tk = kt_ref.shape[1]
    dv = v_ref.shape[1]

    @pl.when(ki == 0)
    def _():
        q = q_ref[...]
        lane = lax.broadcasted_iota(I32, q.shape, 1)
        qm_ref[0] = jnp.where(lane < DIFF_DK, q, jnp.zeros_like(q))
        qm_ref[1] = jnp.where(lane >= DIFF_DK, q, jnp.zeros_like(q))
        m_ref[...] = jnp.full(m_ref.shape, NEG_INF, F32)
        acc_ref[...] = jnp.zeros(acc_ref.shape, F32)

    def accumulate(masked):
        kt = kt_ref[...].astype(BF16)
        v1 = jnp.concatenate([v_ref[...], jnp.ones((tk, dv), BF16)], axis=1)
        for c in range(2):
            for r0 in range(0, tq, ATTN_ROW_CHUNK):
                rows = slice(r0, r0 + ATTN_ROW_CHUNK)
                ncol = min(tk, r0 + ATTN_ROW_CHUNK) if masked else tk
                s = jnp.dot(qm_ref[c, rows, :], kt[:, 0:ncol], preferred_element_type=F32)
                if masked:
                    row = lax.broadcasted_iota(I32, s.shape, 0) + r0
                    col = lax.broadcasted_iota(I32, s.shape, 1)
                    s = jnp.where(col <= row, s, NEG_INF)
                m_old = m_ref[c, rows, :]
                m_new = jnp.maximum(m_old, jnp.max(s, axis=-1, keepdims=True))
                alpha = jnp.exp(m_old - m_new)
                pr = jnp.exp(s - jnp.concatenate([m_new] * (ncol // LANES), axis=1))
                acc_ref[c, rows, :] = (jnp.concatenate([alpha] * (2 * dv // LANES), axis=1) * acc_ref[c, rows, :]
                                       + jnp.dot(pr.astype(BF16), v1[0:ncol, :], preferred_element_type=F32))
                m_ref[c, rows, :] = m_new

    @pl.when(ki < qi)
    def _():
        accumulate(False)

    @pl.when(ki == qi)
    def _():
        accumulate(True)
        lam = _lambda(lam_ref, lam_init)
        a0 = acc_ref[0]
        a1 = acc_ref[1]
        o = a0[:, 0:dv] / a0[:, dv:2 * dv] - lam * (a1[:, 0:dv] / a1[:, dv:2 * dv])
        o_ref[...] = _subln(o, sg_ref[...], lam_init).astype(o_ref.dtype)


def _attn_prompt(qb, kt, vb, lam_vecs, subln_g, seq_len, lam_init):
    t_total = qb.shape[0]
    nb = t_total // seq_len
    tq = min(ATTN_TILE, seq_len)
    nq = seq_len // tq
    pairs = [(i, j) for i in range(nq) for j in range(i + 1)]
    qi_tab = jnp.asarray([a for a, _ in pairs], I32)
    ki_tab = jnp.asarray([b for _, b in pairs], I32)
    qmap = lambda b, h, p, qi, ki: (b * nq + qi[p], h)
    vmap = lambda b, h, p, qi, ki: (b * nq + ki[p], h)
    ktmap = lambda b, h, p, qi, ki: (b * DIFF_HEADS + h, ki[p])
    grid_spec = pltpu.PrefetchScalarGridSpec(
        num_scalar_prefetch=2,
        grid=(nb, DIFF_HEADS, len(pairs)),
        in_specs=[pl.BlockSpec((tq, DIFF_DV), qmap), pl.BlockSpec((2 * DIFF_DK, tq), ktmap),
                  pl.BlockSpec((tq, DIFF_DV), vmap),
                  pl.BlockSpec(lam_vecs.shape, lambda *_: (0, 0)), pl.BlockSpec(subln_g.shape, lambda *_: (0, 0))],
        out_specs=pl.BlockSpec((tq, DIFF_DV), qmap),
        scratch_shapes=[pltpu.VMEM((2, tq, DIFF_DV), BF16), pltpu.VMEM((2, tq, LANES), F32),
                        pltpu.VMEM((2, tq, 2 * DIFF_DV), F32)],
    )
    return pl.pallas_call(
        functools.partial(_attn_prompt_body, lam_init=lam_init),
        grid_spec=grid_spec,
        out_shape=jax.ShapeDtypeStruct((t_total, DIFF_V_WIDTH), BF16),
        compiler_params=_params("parallel", "parallel", "arbitrary"),
        name="attn_prompt",
    )(qi_tab, ki_tab, qb, kt, vb, lam_vecs, subln_g)


def _attn_sample_body(pt_ref, q_ref, k_ref, v_ref, lam_ref, sg_ref, *rest, n_pages, lam_init):
    kp_refs = rest[:n_pages]
    vp_refs = rest[n_pages:2 * n_pages]
    o_ref = rest[2 * n_pages]
    s_new = q_ref.shape[0]
    n_maps = DIFF_HEADS * 2
    page = kp_refs[0].shape[1]
    nrow = n_maps * s_new

    def value_page(vr):
        return jnp.concatenate([vr[pl.ds(h, page, stride=DIFF_HEADS), :] for h in range(DIFF_HEADS)], axis=1)

    qt = jnp.concatenate([q_ref[...] * DIFF_SCALE] * (DIFF_HEADS * 2), axis=0)
    row = lax.broadcasted_iota(I32, qt.shape, 0)
    col = lax.broadcasted_iota(I32, qt.shape, 1)
    qbd = jnp.where(col // DIFF_DK == row // s_new, qt, 0.0).astype(BF16)
    pad = jnp.zeros((page - s_new, k_ref.shape[1]), F32)
    k_new = jnp.concatenate([k_ref[...], pad], axis=0)
    v_new = jnp.concatenate([v_ref[...], pad], axis=0)
    scores = [_bdot(qbd, kr[...]) for kr in kp_refs]
    s_n = _bdot_nt(qbd, k_new)
    rn = lax.broadcasted_iota(I32, s_n.shape, 0)
    cn = lax.broadcasted_iota(I32, s_n.shape, 1)
    scores.append(jnp.where(cn <= rn % s_new, s_n, NEG_INF))
    m = functools.reduce(jnp.maximum, [jnp.max(s, axis=-1, keepdims=True) for s in scores])
    values = [value_page(vr) for vr in vp_refs] + [v_new]
    l = jnp.zeros_like(m)
    acc = jnp.zeros((nrow, v_new.shape[1]), F32)
    for s, val in zip(scores, values):
        pr = jnp.exp(s - m)
        l = l + jnp.sum(pr, axis=-1, keepdims=True)
        acc = acc + _bdot(pr, val)
    full = acc / l
    lam = _lambda(lam_ref, lam_init)
    outs = []
    for h in range(DIFF_HEADS):
        r0 = h * 2 * s_new
        cols = slice(h * DIFF_DV, (h + 1) * DIFF_DV)
        o = full[r0:r0 + s_new, cols] - lam * full[r0 + s_new:r0 + 2 * s_new, cols]
        outs.append(_subln(o, sg_ref[...], lam_init))
    o_ref[...] = jnp.concatenate(outs, axis=1)


def _attn_sample(q, k, v, cache_k, cache_v, n_pool_pages, page_table, page_offset, lam_vecs, subln_g, s_new,
                 lam_init):
    nseq, n_pages = page_table.shape
    width = q.shape[1]
    k_rows = cache_k.shape[0] // n_pool_pages
    v_rows = cache_v.shape[0] // n_pool_pages
    new_spec = pl.BlockSpec((s_new, width), lambda b, pt: (b, 0))
    page_spec = lambda rows, lanes, j: pl.BlockSpec((rows, lanes), lambda b, pt: (pt[b, j] + page_offset, 0))
    grid_spec = pltpu.PrefetchScalarGridSpec(
        num_scalar_prefetch=1,
        grid=(nseq,),
        in_specs=[new_spec, new_spec, new_spec,
                  pl.BlockSpec(lam_vecs.shape, lambda *_: (0, 0)), pl.BlockSpec(subln_g.shape, lambda *_: (0, 0))]
                 + [page_spec(k_rows, cache_k.shape[1], j) for j in range(n_pages)]
                 + [page_spec(v_rows, DIFF_DV, j) for j in range(n_pages)],
        out_specs=new_spec,
    )
    return pl.pallas_call(
        functools.partial(_attn_sample_body, n_pages=n_pages, lam_init=lam_init),
        grid_spec=grid_spec,
        out_shape=jax.ShapeDtypeStruct((nseq * s_new, width), F32),
        compiler_params=_params("parallel"),
        name="attn_sample",
    )(page_table, q, k, v, lam_vecs, subln_g, *([cache_k] * n_pages), *([cache_v] * n_pages))


def _outproj_body(yr_ref, o_ref, x_ref, g1_ref, g2n_ref, sh_ref, sc_ref, g2_ref, woa_ref, wob_ref,
                  wsg_ref, wsu_ref, wsd_ref, wrh_ref, wrl_ref,
                  base_ref, h2t_ref, lg_ref):
    tm = x_ref.shape[0]
    mixed = (jnp.dot(yr_ref[...].astype(BF16), woa_ref[...], preferred_element_type=F32)
             + jnp.dot(o_ref[...].astype(BF16), wob_ref[...], preferred_element_type=F32))
    x1 = x_ref[...] + g1_ref[...] * mixed
    h2 = _rms(x1, g2n_ref[...]) * (1.0 + sc_ref[...]) + sh_ref[...]
    h2b = h2.astype(BF16)
    hidden = _silu(jnp.dot(h2b, wsg_ref[...], preferred_element_type=F32)) * jnp.dot(
        h2b, wsu_ref[...], preferred_element_type=F32)
    shared = jnp.dot(hidden.astype(BF16), wsd_ref[...], preferred_element_type=F32)
    base_ref[...] = x1 + g2_ref[...] * shared
    h2l = (h2 - h2b.astype(F32)).astype(BF16)
    nt = lambda a, b: lax.dot_general(a, b, (((1,), (1,)), ((), ())), preferred_element_type=F32)
    lg_ref[...] = nt(wrh_ref[...], h2b) + nt(wrl_ref[...], h2b) + nt(wrh_ref[...], h2l)
    for c in range(ROW_TILE):
        h2t_ref[pl.ds(c, tm, stride=ROW_TILE), :] = h2[:, c * LANES:(c + 1) * LANES]


def _outproj(yr, o, x, gate1, norm2_g, shift2, scale2, gate2, wts, tm, tiles_per_seq):
    t, d = x.shape
    row = lambda n: pl.BlockSpec((tm, n), lambda i: (i, 0))
    mod = lambda m: _mod_spec(m, tm, tiles_per_seq)
    return pl.pallas_call(
        _outproj_body,
        grid=(t // tm,),
        in_specs=[row(RWKV_WIDTH), row(DIFF_V_WIDTH), row(d), mod(gate1), _full_spec(norm2_g), mod(shift2),
                  mod(scale2), mod(gate2)] + [_full_spec(a) for a in wts],
        out_specs=[row(d), pl.BlockSpec((tm * ROW_TILE, LANES), lambda i: (i, 0)),
                   pl.BlockSpec((N_EXPERTS, tm), lambda i: (0, i))],
        out_shape=[jax.ShapeDtypeStruct((t, d), F32),
                   jax.ShapeDtypeStruct((t * ROW_TILE, LANES), F32),
                   jax.ShapeDtypeStruct((N_EXPERTS, t), F32)],
        compiler_params=_params("parallel"),
        name="outproj",
    )(yr, o, x, gate1, norm2_g, shift2, scale2, gate2, *wts)


def _router_body(lg_ref, bias_ref, idx_ref, gate_ref, pos_ref, cnt_ref, run_ref):
    i = pl.program_id(0)
    tm = lg_ref.shape[1]

    @pl.when(i == 0)
    def _():
        run_ref[...] = jnp.zeros(run_ref.shape, F32)

    scores = _sigmoid(lg_ref[...])
    biased = scores + bias_ref[...]
    erow = lax.broadcasted_iota(I32, (N_EXPERTS, tm), 0)
    grow = lax.broadcasted_iota(I32, (GROUP_SIZE, tm), 0)

    def first_argmax(x, rows, limit):
        mx = jnp.max(x, axis=0, keepdims=True)
        return mx, jnp.min(jnp.where(x == mx, rows, limit), axis=0, keepdims=True)

    group_scores = []
    for gidx in range(N_GROUPS):
        xg = biased[gidx * GROUP_SIZE:(gidx + 1) * GROUP_SIZE, :]
        m1, i1 = first_argmax(xg, grow, GROUP_SIZE)
        m2 = jnp.max(jnp.where(grow == i1, NEG_INF, xg), axis=0, keepdims=True)
        group_scores.append(m1 + m2)
    gs = jnp.concatenate(group_scores, axis=0)
    g8 = lax.broadcasted_iota(I32, (N_GROUPS, tm), 0)
    chosen = jnp.zeros((N_GROUPS, tm), I32)
    for _ in range(TOPK_GROUPS):
        _, gi = first_argmax(gs, g8, N_GROUPS)
        hit = g8 == gi
        chosen = jnp.where(hit, 1, chosen)
        gs = jnp.where(hit, NEG_INF, gs)
    cand = jnp.concatenate(
        [jnp.where(chosen[gidx:gidx + 1, :] > 0, biased[gidx * GROUP_SIZE:(gidx + 1) * GROUP_SIZE, :], NEG_INF)
         for gidx in range(N_GROUPS)], axis=0)

    idxs, raws = [], []
    onehot = jnp.zeros((N_EXPERTS, tm), F32)
    for _ in range(TOP_K):
        _, ei = first_argmax(cand, erow, N_EXPERTS)
        hit = erow == ei
        idxs.append(ei)
        raws.append(jnp.sum(jnp.where(hit, scores, 0.0), axis=0, keepdims=True))
        onehot = onehot + hit.astype(F32)
        cand = jnp.where(hit, NEG_INF, cand)
    raw = jnp.concatenate(raws, axis=0)
    gate_ref[...] = raw / jnp.sum(raw, axis=0, keepdims=True) * ROUTE_SCALE
    idx_ref[...] = jnp.concatenate(idxs, axis=0)

    ti = lax.broadcasted_iota(I32, (tm, tm), 0)
    tj = lax.broadcasted_iota(I32, (tm, tm), 1)
    oh = onehot.astype(BF16)
    before = jnp.dot(oh, (ti < tj).astype(BF16), preferred_element_type=F32) + run_ref[...]
    run_ref[...] = run_ref[...] + jnp.dot(oh, jnp.ones((tm, tm), BF16), preferred_element_type=F32)
    pos_ref[...] = jnp.concatenate(
        [jnp.sum(jnp.where(erow == ei, before, 0.0), axis=0, keepdims=True) for ei in idxs], axis=0).astype(I32)
    cnt_ref[...] = run_ref[...]


def _router(logits_t, bias_col):
    t = logits_t.shape[1]
    tm = ROUTE_TILE
    tok = pl.BlockSpec((TOP_K, tm), lambda i: (0, i))
    return pl.pallas_call(
        _router_body,
        grid=(t // tm,),
        in_specs=[pl.BlockSpec((N_EXPERTS, tm), lambda i: (0, i)), _full_spec(bias_col)],
        out_specs=[tok, tok, tok, pl.BlockSpec((N_EXPERTS, tm), lambda i: (0, 0))],
        out_shape=[jax.ShapeDtypeStruct((TOP_K, t), I32), jax.ShapeDtypeStruct((TOP_K, t), F32),
                   jax.ShapeDtypeStruct((TOP_K, t), I32), jax.ShapeDtypeStruct((N_EXPERTS, tm), F32)],
        scratch_shapes=[pltpu.VMEM((N_EXPERTS, tm), F32)],
        compiler_params=_params("arbitrary"),
        name="router",
    )(logits_t, bias_col)


def _assign_body(idx_ref, pos_ref, start_ref, dest_ref):
    tm = idx_ref.shape[1]
    erow = lax.broadcasted_iota(I32, (N_EXPERTS, tm), 0)
    start = start_ref[...]
    idx = idx_ref[...]
    first = jnp.concatenate(
        [jnp.sum(jnp.where(erow == idx[j:j + 1, :], start, 0.0), axis=0, keepdims=True) for j in range(TOP_K)],
        axis=0)
    dest_ref[...] = first.astype(I32) + pos_ref[...]


def _assign(idx, pos, start_col):
    t = idx.shape[1]
    tm = ROUTE_TILE
    tok = pl.BlockSpec((TOP_K, tm), lambda i: (0, i))
    return pl.pallas_call(
        _assign_body,
        grid=(t // tm,),
        in_specs=[tok, tok, _full_spec(start_col)],
        out_specs=tok,
        out_shape=jax.ShapeDtypeStruct((TOP_K, t), I32),
        compiler_params=_params("parallel"),
        name="assign",
    )(idx, pos, start_col)


def _row_copy(src, src_row, dst, dst_row, sem):
    return pltpu.make_async_copy(src.at[pl.ds(pl.multiple_of(src_row * ROW_TILE, ROW_TILE), ROW_TILE), :],
                                 dst.at[pl.ds(pl.multiple_of(dst_row * ROW_TILE, ROW_TILE), ROW_TILE), :], sem)


def _dispatch_body(zflag_ref, h_ref, dest_ref, xs_ref, zero_ref, zsem, rsem):
    i = pl.program_id(0)
    tm = dest_ref.shape[1]
    blk_rows = EXPERT_BLOCK * ROW_TILE
    n_blocks = xs_ref.shape[0] // blk_rows

    def zero_copy(b):
        start = pl.multiple_of(b * blk_rows, blk_rows)
        return pltpu.make_async_copy(zero_ref, xs_ref.at[pl.ds(start, blk_rows), :], zsem)

    @pl.when(i == 0)
    def _():
        zero_ref[...] = jnp.zeros(zero_ref.shape, F32)

        def issue(b, carry):
            @pl.when(zflag_ref[b] > 0)
            def _():
                zero_copy(b).start()
            return carry

        def drain(b, carry):
            @pl.when(zflag_ref[b] > 0)
            def _():
                zero_copy(b).wait()
            return carry

        lax.fori_loop(0, n_blocks, issue, 0)
        lax.fori_loop(0, n_blocks, drain, 0)

    def issue_rows(t, carry):
        for j in range(TOP_K):
            _row_copy(h_ref, t, xs_ref, dest_ref[j, t], rsem).start()
        return carry

    lax.fori_loop(0, tm, issue_rows, 0)
    for _ in range(TOP_K):
        pltpu.make_async_copy(h_ref, xs_ref.at[pl.ds(0, tm * ROW_TILE), :], rsem).wait()


def _dispatch(h2t, dest, zero_flags, n_rows):
    t = dest.shape[1]
    tm = ROUTE_TILE
    grid_spec = pltpu.PrefetchScalarGridSpec(
        num_scalar_prefetch=1,
        grid=(t // tm,),
        in_specs=[pl.BlockSpec((tm * ROW_TILE, LANES), lambda i, *_: (i, 0)),
                  pl.BlockSpec((TOP_K, tm), lambda i, *_: (0, i), memory_space=pltpu.SMEM)],
        out_specs=pl.BlockSpec(memory_space=pl.ANY),
        scratch_shapes=[pltpu.VMEM((EXPERT_BLOCK * ROW_TILE, LANES), F32),
                        pltpu.SemaphoreType.DMA, pltpu.SemaphoreType.DMA],
    )
    return pl.pallas_call(
        _dispatch_body,
        grid_spec=grid_spec,
        out_shape=jax.ShapeDtypeStruct((n_rows * ROW_TILE, LANES), F32),
        compiler_params=_params("arbitrary"),
        name="dispatch",
    )(zero_flags, h2t, dest)


def _expert_body(first_ref, count_ref, nu_ref, xs_ref, wg_ref, wu_ref, wd_ref, y_ref,
                 xin_ref, yout_ref, wgb_ref, wub_ref, wdb_ref, in_sem, out_sem):
    e = pl.program_id(0)
    rows = EXPERT_BLOCK
    blk_rows = EXPERT_BLOCK * ROW_TILE
    nb = count_ref[e]
    b0 = first_ref[e]

    n_used = nu_ref[0]
    n_in = xin_ref.shape[0]
    n_out = yout_ref.shape[0]
    ahead = n_in - 1
    part = blk_rows // EXPERT_DMA_SPLIT

    def hbm_rows(g, k=0, n=blk_rows):
        return pl.ds(pl.multiple_of(g * blk_rows + k * part, part), n)

    def start_in(g):
        slot = g % n_in
        for k in range(EXPERT_DMA_SPLIT):
            pltpu.make_async_copy(xs_ref.at[hbm_rows(g, k, part), :], xin_ref.at[slot, pl.ds(k * part, part), :],
                                  in_sem.at[slot]).start()

    def wait_in(g):
        slot = g % n_in
        pltpu.make_async_copy(xs_ref.at[hbm_rows(g), :], xin_ref.at[slot], in_sem.at[slot]).wait()

    def start_out(g):
        slot = g % n_out
        for k in range(EXPERT_DMA_SPLIT):
            pltpu.make_async_copy(yout_ref.at[slot, pl.ds(k * part, part), :], y_ref.at[hbm_rows(g, k, part), :],
                                  out_sem.at[slot]).start()

    def wait_out(g):
        slot = g % n_out
        pltpu.make_async_copy(yout_ref.at[slot], y_ref.at[hbm_rows(g), :], out_sem.at[slot]).wait()

    @pl.when(e == 0)
    def _():
        for g in range(ahead):
            @pl.when(g < n_used)
            def _():
                start_in(g)

    @pl.when(nb > 0)
    def _():
        wgb_ref[...] = wg_ref[...].astype(BF16)
        wub_ref[...] = wu_ref[...].astype(BF16)
        wdb_ref[...] = wd_ref[...].astype(BF16)

        def step(b, carry):
            g = b0 + b
            wait_in(g)

            @pl.when(g + ahead < n_used)
            def _():
                start_in(g + ahead)

            @pl.when(g >= n_out)
            def _():
                wait_out(g - n_out)

            islot = g % n_in
            oslot = g % n_out
            x = jnp.concatenate([xin_ref[islot, pl.ds(c, rows, stride=ROW_TILE), :] for c in range(ROW_TILE)],
                                axis=1).astype(BF16)
            hidden = _silu(jnp.dot(x, wgb_ref[...], preferred_element_type=F32)) * jnp.dot(
                x, wub_ref[...], preferred_element_type=F32)
            y = jnp.dot(hidden.astype(BF16), wdb_ref[...], preferred_element_type=F32)
            for c in range(ROW_TILE):
                yout_ref[oslot, pl.ds(c, rows, stride=ROW_TILE), :] = y[:, c * LANES:(c + 1) * LANES]
            start_out(g)
            return carry

        lax.fori_loop(0, nb, step, 0)

    @pl.when(e == pl.num_programs(0) - 1)
    def _():
        def drain(g, carry):
            wait_out(g)
            return carry

        lax.fori_loop(jnp.maximum(n_used - n_out, 0), n_used, drain, 0)
        n_blocks = y_ref.shape[0] // blk_rows
        yout_ref[0] = jnp.zeros(yout_ref.shape[1:], F32)

        def tail(b, carry):
            cp = pltpu.make_async_copy(
                yout_ref.at[0], y_ref.at[pl.ds(pl.multiple_of(b * blk_rows, blk_rows), blk_rows), :], out_sem.at[0])
            cp.start()
            cp.wait()
            return carry

        lax.fori_loop(nu_ref[0], n_blocks, tail, 0)


def _experts(xs, first_block, block_count, n_used, we_gate, we_up, we_down):
    d, f = we_gate.shape[1], we_gate.shape[2]
    blk = (EXPERT_BLOCK * ROW_TILE, LANES)
    wspec = lambda a, b: pl.BlockSpec((None, a, b), lambda e, *_: (e, 0, 0))
    grid_spec = pltpu.PrefetchScalarGridSpec(
        num_scalar_prefetch=3,
        grid=(N_EXPERTS,),
        in_specs=[pl.BlockSpec(memory_space=pl.ANY), wspec(d, f), wspec(d, f), wspec(f, d)],
        out_specs=pl.BlockSpec(memory_space=pl.ANY),
        scratch_shapes=[pltpu.VMEM((EXPERT_IN_SLOTS,) + blk, F32), pltpu.VMEM((EXPERT_OUT_SLOTS,) + blk, F32),
                        pltpu.VMEM((d, f), BF16), pltpu.VMEM((d, f), BF16), pltpu.VMEM((f, d), BF16),
                        pltpu.SemaphoreType.DMA((EXPERT_IN_SLOTS,)), pltpu.SemaphoreType.DMA((EXPERT_OUT_SLOTS,))],
    )
    return pl.pallas_call(
        _expert_body,
        grid_spec=grid_spec,
        out_shape=jax.ShapeDtypeStruct(xs.shape, F32),
        compiler_params=_params("arbitrary"),
        name="experts",
    )(first_block, block_count, n_used, xs, we_gate, we_up, we_down)


def _combine_body(base_ref, g2_ref, gate_ref, dest_ref, next_ref, y_ref, out_ref, buf_ref, sem):
    i = pl.program_id(0)
    n = pl.num_programs(0)
    tm = base_ref.shape[0]
    slot_rows = tm * ROW_TILE

    def issue_tile(idx_ref, slot):
        def issue(t, carry):
            for j in range(TOP_K):
                _row_copy(y_ref, idx_ref[j, t], buf_ref.at[slot], j * tm + t, sem.at[slot]).start()
            return carry

        lax.fori_loop(0, tm, issue, 0)

    def finish_tile(slot):
        pltpu.make_async_copy(y_ref.at[pl.ds(0, TOP_K * slot_rows), :], buf_ref.at[slot], sem.at[slot]).wait()
        gates = gate_ref[...]
        gcols = [jnp.broadcast_to(gates[:, j:j + 1], (tm, LANES)) for j in range(TOP_K)]
        for c in range(ROW_TILE):
            cols = slice(c * LANES, (c + 1) * LANES)
            routed = sum(gcols[j] * buf_ref[slot, pl.ds(j * slot_rows + c, tm, stride=ROW_TILE), :]
                         for j in range(TOP_K))
            out_ref[:, cols] = base_ref[:, cols] + g2_ref[:, cols] * routed

    @pl.when(i == 0)
    def _():
        issue_tile(dest_ref, 0)

    for parity in range(2):
        @pl.when(i % 2 == parity)
        def _():
            @pl.when(i + 1 < n)
            def _():
                issue_tile(next_ref, 1 - parity)

            finish_tile(parity)


def _combine(base, gate2, gates, dest, y_rows, tok_offset, tiles_per_seq):
    t, d = base.shape
    tm = ROUTE_TILE
    off = tok_offset // tm
    steps = t // tm
    return pl.pallas_call(
        _combine_body,
        grid=(steps,),
        in_specs=[pl.BlockSpec((tm, d), lambda i: (i, 0)), _mod_spec(gate2, tm, tiles_per_seq),
                  pl.BlockSpec((tm, TOP_K), lambda i: (i + off, 0)),
                  pl.BlockSpec((TOP_K, tm), lambda i: (0, i + off), memory_space=pltpu.SMEM),
                  pl.BlockSpec((TOP_K, tm), lambda i: (0, jnp.minimum(i + 1, steps - 1) + off),
                               memory_space=pltpu.SMEM),
                  pl.BlockSpec(memory_space=pl.ANY)],
        out_specs=pl.BlockSpec((tm, d), lambda i: (i, 0)),
        out_shape=jax.ShapeDtypeStruct((t, d), F32),
        scratch_shapes=[pltpu.VMEM((2, TOP_K * tm * ROW_TILE, LANES), F32), pltpu.SemaphoreType.DMA((2,))],
        compiler_params=_params("arbitrary"),
        name="combine",
    )(base, gate2, gates, dest, dest, y_rows)


def _segment_ones(width, seg):
    ids = jnp.arange(width) // seg
    return (ids[:, None] == ids[None, :]).astype(BF16)


def _pad_rows(w, start, total):
    return jnp.zeros((total, w.shape[1]), w.dtype).at[start:start + w.shape[0]].set(w)


def kernel(x_prompt, x_sample, cache_k, cache_v, state_wkv, state_shift, page_table, c_prompt, c_sample, w_ada, b_ada, norm1_g, norm2_g, w_in, mu_shift, w0, w_lora_up, a0, a_lora_up, g_lora_up, k_k, k_a, r_k, lnx_g, lnx_b, qn_g, kn_g, lam_q1, lam_k1, lam_q2, lam_k2, subln_g, w_out, w_router, router_bias, we_gate, we_up, we_down, ws_gate, ws_up, ws_down):
    depth = w_in.shape[0]
    nb, seq, d = x_prompt.shape
    db, dseq, _ = x_sample.shape
    n_pool, page = cache_k.shape[1], cache_k.shape[2]
    tp, ts = nb * seq, db * dseq
    w3 = 3 * RWKV_WIDTH
    row = lambda a: a.reshape(1, -1)
    seg512 = _segment_ones(RWKV_WIDTH, RWKV_HEAD_DIM)
    tm_p = min(512, seq)
    tm_s = min(256, ts)
    chunk_p = min(64, seq)

    xp = x_prompt.reshape(tp, d)
    xs = x_sample.reshape(ts, d)
    cache_k2 = jnp.transpose(cache_k, (0, 1, 3, 4, 5, 2)).reshape(depth * n_pool * DIFF_QK_WIDTH, page)
    cache_v2 = cache_v.reshape(depth * n_pool * page * DIFF_HEADS, DIFF_DV)
    n_cond = nb + db
    cond = jnp.concatenate([c_prompt, c_sample], axis=0)
    cond = jnp.pad(cond, ((0, -n_cond % SUBLANES), (0, 0)))

    outs = {name: [] for name in ("kp", "vp", "wp", "sp", "ks", "vs", "ws", "ss")}
    for layer in range(depth):
        lam_init = 0.8 - 0.6 * math.exp(-0.3 * layer)
        mods = _ada(cond, w_ada[layer], row(b_ada[layer]))
        mod_p = [m.reshape(nb, 1, d) for m in jnp.split(mods[:nb], 6, axis=-1)]
        mod_s = [jnp.repeat(m, dseq, axis=0) for m in jnp.split(mods[nb:n_cond], 6, axis=-1)]

        wi = w_in[layer]
        in_wts = (wi[:, :w3].astype(BF16),
                  jnp.pad(wi[:, w3:RWKV_PROJ], ((0, 0), (0, LORA_PAD - LORA_WIDTH))).astype(BF16),
                  wi[:, RWKV_PROJ:RWKV_PROJ + DIFF_QK_WIDTH].astype(BF16),
                  wi[:, RWKV_PROJ + DIFF_QK_WIDTH:RWKV_PROJ + 2 * DIFF_QK_WIDTH].astype(BF16),
                  wi[:, RWKV_PROJ + 2 * DIFF_QK_WIDTH:].astype(BF16),
                  row(jnp.tile(qn_g[layer], DIFF_QK_WIDTH // DIFF_DK)),
                  row(jnp.tile(kn_g[layer], DIFF_QK_WIDTH // DIFF_DK)),
                  seg512)
        mu = mu_shift[layer]
        rwkv_wts = (row(mu[:w3]), row(jnp.pad(mu[w3:], (0, LORA_PAD - LORA_WIDTH))), row(w0[layer]),
                    _pad_rows(w_lora_up[layer], 0, LORA_PAD).astype(BF16), row(a0[layer]),
                    _pad_rows(a_lora_up[layer], DECAY_LORA, LORA_PAD).astype(BF16),
                    _pad_rows(g_lora_up[layer], DECAY_LORA + ICLR_LORA, LORA_PAD).astype(BF16),
                    row(k_k[layer]), row(k_a[layer]), row(r_k[layer]), row(lnx_g[layer]), row(lnx_b[layer]), seg512)
        lam_vecs = jnp.stack([lam_q1[layer], lam_k1[layer], lam_q2[layer], lam_k2[layer]])
        sg = row(subln_g[layer])
        wr = w_router[layer].T
        wr_hi = wr.astype(BF16)
        out_wts = (w_out[layer][:RWKV_WIDTH].astype(BF16), w_out[layer][RWKV_WIDTH:].astype(BF16),
                   ws_gate[layer].astype(BF16), ws_up[layer].astype(BF16), ws_down[layer].astype(BF16),
                   wr_hi, (wr - wr_hi.astype(F32)).astype(BF16))

        def token_mix(x, mod, tm, tiles_per_seq, seq_len, chunk, shift_prev, wkv0, long_seq):
            pm, plo, *qkv = _inproj(x, row(norm1_g[layer]), mod[0], mod[1], in_wts, tm, tiles_per_seq, long_seq)
            n_seq = x.shape[0] // seq_len
            shift_m = shift_prev[:, :, :w3]
            shift_l = jnp.pad(shift_prev[:, :, w3:], ((0, 0), (0, 0), (0, LORA_PAD - LORA_WIDTH)))
            if chunk == seq_len:
                per_step = RWKV_SEQS_PER_STEP if n_seq % RWKV_SEQS_PER_STEP == 0 else 1
            else:
                per_step = RWKV_CHUNKS_PER_STEP if seq_len % (chunk * RWKV_CHUNKS_PER_STEP) == 0 else 1
            yr, wkv = _rwkv(pm, plo, shift_m, shift_l, wkv0, rwkv_wts, seq_len, chunk, per_step)
            last = jnp.concatenate([pm.reshape(n_seq, seq_len, w3)[:, -1:],
                                    plo.reshape(n_seq, seq_len, LORA_PAD)[:, -1:, :LORA_WIDTH]], axis=-1)
            return yr, wkv, last, qkv

        yr_p, wkv_p, last_p, (kt_p, v4_p, qb_p, vb_p) = token_mix(
            xp, mod_p, tm_p, seq // tm_p, seq, chunk_p,
            jnp.zeros((nb, 1, RWKV_PROJ), F32), jnp.zeros((nb, RWKV_HEADS, RWKV_HEAD_DIM, RWKV_HEAD_DIM), F32), True)
        o_p = _attn_prompt(qb_p, kt_p, vb_p, lam_vecs, sg, seq, lam_init)
        yr_s, wkv_s, last_s, (q_s, k_s, v_s) = token_mix(
            xs, mod_s, tm_s, 1, dseq, dseq, state_shift[layer], state_wkv[layer], False)
        o_s = _attn_sample(q_s, k_s, v_s, cache_k2, cache_v2, depth * n_pool, page_table, layer * n_pool, lam_vecs,
                           sg, dseq, lam_init)

        base_p, h2t_p, lg_p = _outproj(yr_p, o_p, xp, mod_p[2], row(norm2_g[layer]), mod_p[3], mod_p[4], mod_p[5],
                                       out_wts, tm_p, seq // tm_p)
        base_s, h2t_s, lg_s = _outproj(yr_s, o_s, xs, mod_s[2], row(norm2_g[layer]), mod_s[3], mod_s[4], mod_s[5],
                                       out_wts, tm_s, 1)

        t_all = tp + ts
        idx, gate, pos, cnt = _router(jnp.concatenate([lg_p, lg_s], axis=1), router_bias[layer].reshape(-1, 1))
        counts = cnt[:, 0].astype(I32)
        padded = (counts + EXPERT_BLOCK - 1) // EXPERT_BLOCK * EXPERT_BLOCK
        end_padded = jnp.cumsum(padded)
        dest = _assign(idx, pos, (end_padded - padded).astype(F32).reshape(-1, 1))
        n_rows = (t_all * TOP_K + N_EXPERTS * (EXPERT_BLOCK - 1) + EXPERT_BLOCK - 1) // EXPERT_BLOCK * EXPERT_BLOCK
        n_blocks = n_rows // EXPERT_BLOCK
        blk_ids = jnp.arange(n_blocks, dtype=I32)
        n_used = (end_padded[-1:] // EXPERT_BLOCK).astype(I32)
        is_expert_end = jnp.any(end_padded[None, :] == (blk_ids[:, None] + 1) * EXPERT_BLOCK, axis=1)
        zero_flags = jnp.logical_or(blk_ids >= n_used[0], is_expert_end).astype(I32)
        rows_in = _dispatch(jnp.concatenate([h2t_p, h2t_s], axis=0), dest, zero_flags, n_rows)
        rows_out = _experts(rows_in, ((end_padded - padded) // EXPERT_BLOCK).astype(I32),
                            (padded // EXPERT_BLOCK).astype(I32), n_used, we_gate[layer], we_up[layer],
                            we_down[layer])
        gates_t = gate.T
        xp = _combine(base_p, mod_p[5], gates_t, dest, rows_out, 0, seq // ROUTE_TILE)
        xs = _combine(base_s, mod_s[5], gates_t, dest, rows_out, tp, 1)

        outs["kp"].append(jnp.transpose(kt_p.reshape(nb, DIFF_HEADS, 2, DIFF_DK, seq), (0, 4, 1, 2, 3)))
        outs["vp"].append(v4_p.reshape(nb, seq, DIFF_HEADS, DIFF_DV))
        outs["wp"].append(wkv_p)
        outs["sp"].append(last_p)
        outs["ks"].append(k_s.reshape(db, dseq, DIFF_HEADS, 2, DIFF_DK))
        outs["vs"].append(v_s.reshape(db, dseq, DIFF_HEADS, DIFF_DV))
        outs["ws"].append(wkv_s)
        outs["ss"].append(last_s)

    st = {name: jnp.stack(v) for name, v in outs.items()}
    return (xp.reshape(nb, seq, d), xs.reshape(db, dseq, d), st["kp"], st["vp"], st["wp"], st["sp"],
            st["ks"], st["vs"], st["ws"], st["ss"])
```

```python
import functools
import math

import jax
import jax.numpy as jnp
from jax import lax
from jax.experimental import pallas as pl
from jax.experimental.pallas import tpu as pltpu

F32 = jnp.float32
BF16 = jnp.bfloat16
I32 = jnp.int32

RWKV_HEADS = 8
RWKV_HEAD_DIM = 64
RWKV_WIDTH = RWKV_HEADS * RWKV_HEAD_DIM
DECAY_LORA = 32
ICLR_LORA = 32
GATE_LORA = 96
LORA_WIDTH = DECAY_LORA + ICLR_LORA + GATE_LORA
LORA_PAD = 256
RWKV_PROJ = 3 * RWKV_WIDTH + LORA_WIDTH
GN_EPS = 64e-5
DIFF_HEADS = 4
DIFF_DK = 64
DIFF_DV = 2 * DIFF_DK
DIFF_QK_WIDTH = DIFF_HEADS * 2 * DIFF_DK
DIFF_V_WIDTH = DIFF_HEADS * DIFF_DV
DIFF_SCALE = DIFF_DK ** -0.5
N_EXPERTS = 256
N_GROUPS = 8
GROUP_SIZE = N_EXPERTS // N_GROUPS
TOPK_GROUPS = 4
TOP_K = 8
ROUTE_SCALE = 2.5
NORM_EPS = 1e-6

LANES = 128
SUBLANES = 8
ROW_TILE = 8
VMEM_LIMIT = 48 * 1024 * 1024

EXPERT_BLOCK = 128
EXPERT_IN_SLOTS = 4
EXPERT_OUT_SLOTS = 3
EXPERT_DMA_SPLIT = 4
ROUTE_TILE = 128
RWKV_CHUNKS_PER_STEP = 4
RWKV_SEQS_PER_STEP = 8
ATTN_TILE = 2048
ATTN_ROW_CHUNK = 256
NEG_INF = float("-inf")


def _bdot(a, b):
    return jnp.dot(a.astype(BF16), b.astype(BF16), preferred_element_type=F32)


def _bdot_nt(a, b):
    return lax.dot_general(a.astype(BF16), b.astype(BF16), (((1,), (1,)), ((), ())), preferred_element_type=F32)


def _bdot_tn(a, b):
    return lax.dot_general(a.astype(BF16), b.astype(BF16), (((0,), (0,)), ((), ())), preferred_element_type=F32)


def _sigmoid(x):
    return 1.0 / (1.0 + jnp.exp(-x))


def _silu(x):
    return x * _sigmoid(x)


def _params(*sem, vmem=VMEM_LIMIT):
    return pltpu.CompilerParams(dimension_semantics=sem, vmem_limit_bytes=vmem)


def _ada_body(c_ref, w_ref, b_ref, o_ref):
    o_ref[...] = _bdot(_silu(c_ref[...]), w_ref[...]) + b_ref[...]


def _ada(c, w, b):
    rows, d = c.shape
    n = w.shape[1]
    tn = 512
    return pl.pallas_call(
        _ada_body,
        grid=(n // tn,),
        in_specs=[pl.BlockSpec((rows, d), lambda j: (0, 0)),
                  pl.BlockSpec((d, tn), lambda j: (0, j)),
                  pl.BlockSpec((1, tn), lambda j: (0, j))],
        out_specs=pl.BlockSpec((rows, tn), lambda j: (0, j)),
        out_shape=jax.ShapeDtypeStruct((rows, n), F32),
        compiler_params=_params("parallel"),
        name="ada",
    )(c, w, b)


def _mod_spec(mod, tm, tiles_per_seq):
    if mod.ndim == 3:
        return pl.BlockSpec((None, 1, mod.shape[-1]), lambda i: (i // tiles_per_seq, 0, 0))
    return pl.BlockSpec((tm, mod.shape[-1]), lambda i: (i, 0))


def _full_spec(a):
    nd = a.ndim
    return pl.BlockSpec(a.shape, lambda *_: (0,) * nd)


def _rms(x, g):
    return x * lax.rsqrt(jnp.mean(x * x, axis=-1, keepdims=True) + NORM_EPS) * g


def _inproj_body(x_ref, g_ref, sh_ref, sc_ref, wm_ref, wl_ref, wq_ref, wk_ref, wv_ref, qg_ref, kg_ref, seg_ref,
                 pm_ref, pl_ref, *out_refs, long_seq):
    tm = x_ref.shape[0]
    h = (_rms(x_ref[...], g_ref[...]) * (1.0 + sc_ref[...]) + sh_ref[...]).astype(BF16)
    pm_ref[...] = jnp.dot(h, wm_ref[...], preferred_element_type=F32)
    pl_ref[...] = jnp.dot(h, wl_ref[...], preferred_element_type=F32)
    seg = seg_ref[...]

    def head_norm(z, gain):
        ms = _bdot(z * z, seg) * (1.0 / DIFF_DK)
        return z * lax.rsqrt(ms + NORM_EPS) * gain

    q = head_norm(jnp.dot(h, wq_ref[...], preferred_element_type=F32), qg_ref[...])
    k = head_norm(jnp.dot(h, wk_ref[...], preferred_element_type=F32), kg_ref[...])
    v = jnp.dot(h, wv_ref[...], preferred_element_type=F32)
    if long_seq:
        kt_ref, v4_ref, qb_ref, vb_ref = out_refs
        kt_ref[...] = k.T
        for hd in range(DIFF_HEADS):
            v4_ref[pl.ds(hd, tm, stride=DIFF_HEADS), :] = v[:, hd * DIFF_DV:(hd + 1) * DIFF_DV]
        qb_ref[...] = (q * DIFF_SCALE).astype(BF16)
        vb_ref[...] = v.astype(BF16)
    else:
        q_ref, k_ref, v_ref = out_refs
        q_ref[...] = q
        k_ref[...] = k
        v_ref[...] = v


def _inproj(x, g, shift, scale, wts, tm, tiles_per_seq, long_seq):
    t, d = x.shape
    wm, wl, wq, wk, wv, qg, kg, seg = wts
    row = lambda n: pl.BlockSpec((tm, n), lambda i: (i, 0))
    out_specs = [row(3 * RWKV_WIDTH), row(LORA_PAD)]
    out_shape = [jax.ShapeDtypeStruct((t, 3 * RWKV_WIDTH), F32), jax.ShapeDtypeStruct((t, LORA_PAD), F32)]
    if long_seq:
        n_seq = t // (tm * tiles_per_seq)
        out_specs += [pl.BlockSpec((DIFF_QK_WIDTH, tm), lambda i: (i // tiles_per_seq, i % tiles_per_seq)),
                      pl.BlockSpec((tm * DIFF_HEADS, DIFF_DV), lambda i: (i, 0)),
                      row(DIFF_QK_WIDTH), row(DIFF_V_WIDTH)]
        out_shape += [jax.ShapeDtypeStruct((n_seq * DIFF_QK_WIDTH, tm * tiles_per_seq), F32),
                      jax.ShapeDtypeStruct((t * DIFF_HEADS, DIFF_DV), F32),
                      jax.ShapeDtypeStruct((t, DIFF_QK_WIDTH), BF16), jax.ShapeDtypeStruct((t, DIFF_V_WIDTH), BF16)]
    else:
        out_specs += [row(DIFF_QK_WIDTH), row(DIFF_QK_WIDTH), row(DIFF_V_WIDTH)]
        out_shape += [jax.ShapeDtypeStruct((t, n), F32) for n in (DIFF_QK_WIDTH, DIFF_QK_WIDTH, DIFF_V_WIDTH)]
    return pl.pallas_call(
        functools.partial(_inproj_body, long_seq=long_seq),
        grid=(t // tm,),
        in_specs=[row(d), _full_spec(g), _mod_spec(shift, tm, tiles_per_seq), _mod_spec(scale, tm, tiles_per_seq)]
                 + [_full_spec(a) for a in (wm, wl, wq, wk, wv, qg, kg, seg)],
        out_specs=out_specs,
        out_shape=out_shape,
        compiler_params=_params("parallel"),
        name="inproj",
    )(x, g, shift, scale, wm, wl, wq, wk, wv, qg, kg, seg)


def _split3(x):
    hi = x.astype(BF16)
    r1 = x - hi.astype(F32)
    mid = r1.astype(BF16)
    lo = (r1 - mid.astype(F32)).astype(BF16)
    return hi, mid, lo


def _rwkv_body(pm_ref, pl_ref, pm8_ref, pl8_ref, sm_ref, sl_ref, s0_ref,
               mum_ref, mul_ref, w0_ref, wupw_ref, a0_ref, wupa_ref, wupg_ref, kk_ref, ka_ref, rk_ref,
               lg_ref, lb_ref, seg_ref,
               y_ref, sout_ref, state_ref, *, chunk, whole_seqs):
    c = pl.program_id(1)
    nc = pl.num_programs(1)
    C = chunk
    W = RWKV_WIDTH
    N = RWKV_HEAD_DIM
    pm = pm_ref[...]
    plo = pl_ref[...]

    if whole_seqs:
        n_seq = pm.shape[0] // C
        rows_of = lambda ref: jnp.concatenate(
            [jnp.broadcast_to(ref[s], (C, ref.shape[-1])) for s in range(n_seq)], axis=0)
        prev_m, prev_l = rows_of(sm_ref), rows_of(sl_ref)

        def shifted(cur, prev_rows):
            rows = lax.broadcasted_iota(I32, cur.shape, 0)
            return jnp.where(rows % C == 0, prev_rows, pltpu.roll(cur, 1, 0))
    else:
        @pl.when(c == 0)
        def _():
            state_ref[...] = s0_ref[0]

        first = c == 0
        prev_m = jnp.where(first, sm_ref[0], pm8_ref[SUBLANES - 1:SUBLANES, :])
        prev_l = jnp.where(first, sl_ref[0], pl8_ref[SUBLANES - 1:SUBLANES, :])

        def shifted(cur, prev_row):
            rows = lax.broadcasted_iota(I32, cur.shape, 0)
            return jnp.where(rows == 0, prev_row, pltpu.roll(cur, 1, 0))

    xm = pm + (shifted(pm, prev_m) - pm) * mum_ref[...]
    xl = plo + (shifted(plo, prev_l) - plo) * mul_ref[...]
    r = xm[:, 0:W]
    k = xm[:, W:2 * W]
    v = xm[:, 2 * W:3 * W]
    seg = seg_ref[...]

    z = -(w0_ref[...] + _bdot(jnp.tanh(xl), wupw_ref[...]))
    softplus = jnp.maximum(z, 0.0) + jnp.log(1.0 + jnp.exp(-jnp.abs(z)))
    w = -softplus - 0.5
    a = _sigmoid(a0_ref[...] + _bdot(xl, wupa_ref[...]))
    g = _bdot(_sigmoid(xl), wupg_ref[...])
    kk = k * kk_ref[...]
    kk = kk / jnp.maximum(jnp.sqrt(_bdot(kk * kk, seg)), 1e-12)
    k = k * (1.0 + (a - 1.0) * ka_ref[...])
    logdec = -jnp.exp(w)

    rows_blk = pm.shape[0]
    n_sub = rows_blk // C
    bi = lax.broadcasted_iota(I32, (rows_blk, rows_blk), 0)
    bj = lax.broadcasted_iota(I32, (rows_blk, rows_blk), 1)
    lower = jnp.logical_and(bi >= bj, bi // C == bj // C).astype(BF16)
    cum = sum(jnp.dot(lower, part, preferred_element_type=F32) for part in _split3(logdec))
    cum_end = jnp.concatenate(
        [jnp.broadcast_to(cum[(s + 1) * C - 1:(s + 1) * C, :], (C, W)) for s in range(n_sub)], axis=0)
    a_t = -kk * jnp.exp(cum - logdec)
    r_t = r * jnp.exp(cum)
    inv = jnp.exp(-cum)
    b_t = kk * a * inv
    k_t = k * inv
    to_end = jnp.exp(cum_end - cum)
    b_e = kk * a * to_end
    k_e = k * to_end
    g_end = jnp.exp(cum_end)

    ti = lax.broadcasted_iota(I32, (C, C), 0)
    tj = lax.broadcasted_iota(I32, (C, C), 1)
    eye = (ti == tj).astype(F32)
    ri = lax.broadcasted_iota(I32, (2 * C, 2 * C), 0)
    ci = lax.broadcasted_iota(I32, (2 * C, 2 * C), 1)
    tr = jnp.where(ri >= C, ri - C, ri)
    tc = jnp.where(ci >= C, ci - C, ci)
    mask = jnp.logical_or(tr > tc, jnp.logical_and(ri >= C, tr == tc))
    zeros_cn = jnp.zeros((C, N), F32)
    levels = int(math.log2(C))
    heads = range(RWKV_HEADS)
    pairs = [(s, h) for s in range(n_sub) for h in heads]
    tile = lambda z, s, h: z[s * C:(s + 1) * C, h * N:(h + 1) * N]
    ah = [tile(a_t, s, h) for s, h in pairs]
    rh = [tile(r_t, s, h) for s, h in pairs]
    vh = [tile(v, s, h) for s, h in pairs]
    m_all = [jnp.where(mask, _bdot_nt(jnp.concatenate([ah[i], rh[i]], axis=0),
                                      jnp.concatenate([tile(b_t, s, h), tile(k_t, s, h)], axis=0)), 0.0)
             for i, (s, h) in enumerate(pairs)]
    m_top = [m[0:C, :] for m in m_all]
    m_bot = [m[C:2 * C, :] for m in m_all]
    akv = [_bdot(m_top[i], jnp.concatenate([zeros_cn, vh[i]], axis=0)) for i in range(len(pairs))]
    power = [m[:, 0:C] for m in m_top]
    t_inv = [eye + p for p in power]
    for _ in range(levels - 1):
        power = [_bdot(p, p) for p in power]
        t_inv = [t + _bdot(t, p) for t, p in zip(t_inv, power)]
    w_mat = [_bdot(t_inv[i], ah[i]) for i in range(len(pairs))]
    u0 = [_bdot(t_inv[i], akv[i]) for i in range(len(pairs))]
    state = None if whole_seqs else [state_ref[h] for h in heads]
    y_rows = []
    for s in range(n_sub):
        at = lambda h: s * RWKV_HEADS + h
        if whole_seqs:
            state = [s0_ref[s, h] for h in heads]
        x = [_bdot_nt(jnp.concatenate([w_mat[at(h)], rh[at(h)]], axis=0), state[h]) for h in heads]
        uv = [jnp.concatenate([x[h][0:C, :] + u0[at(h)], vh[at(h)]], axis=0) for h in heads]
        y_rows.append(jnp.concatenate([x[h][C:2 * C, :] + _bdot(m_bot[at(h)], uv[h]) for h in heads], axis=1))
        state = [state[h] * tile(g_end, s, h)[0:1, :]
                 + _bdot_tn(uv[h], jnp.concatenate([tile(b_e, s, h), tile(k_e, s, h)], axis=0)) for h in heads]
        if whole_seqs:
            for h in heads:
                sout_ref[s, h] = state[h]
    if not whole_seqs:
        for h in heads:
            state_ref[h] = state[h]
    y = jnp.concatenate(y_rows, axis=0)

    mean = _bdot(y, seg) * (1.0 / N)
    yc = y - mean
    var = _bdot(yc * yc, seg) * (1.0 / N)
    yn = yc * lax.rsqrt(var + GN_EPS) * lg_ref[...] + lb_ref[...]
    bonus = _bdot(r * k * rk_ref[...], seg) * v
    y_ref[...] = ((yn + bonus) * g).astype(y_ref.dtype)

    if not whole_seqs:
        @pl.when(c == nc - 1)
        def _():
            sout_ref[0] = state_ref[...]


def _rwkv(pm, plo, shift_m, shift_l, s0, wts, seq_len, chunk, chunks_per_step):
    t_total = pm.shape[0]
    n_seq = t_total // seq_len
    whole_seqs = chunk == seq_len
    rows = chunk * chunks_per_step
    seqs = chunks_per_step if whole_seqs else 1
    ncnk = 1 if whole_seqs else seq_len // rows
    c8 = rows // SUBLANES
    prev8 = lambda b, c: (jnp.maximum(b * (rows if whole_seqs else seq_len) // SUBLANES + c * c8 - 1, 0), 0)
    cur = lambda b, c: (b * ncnk + c, 0)
    per_seq3 = lambda n: pl.BlockSpec((seqs, 1, n), lambda b, c: (b, 0, 0))
    state_spec = pl.BlockSpec((seqs, RWKV_HEADS, RWKV_HEAD_DIM, RWKV_HEAD_DIM), lambda b, c: (b, 0, 0, 0))
    return pl.pallas_call(
        functools.partial(_rwkv_body, chunk=chunk, whole_seqs=whole_seqs),
        grid=(n_seq // seqs, ncnk),
        in_specs=[pl.BlockSpec((rows, 3 * RWKV_WIDTH), cur), pl.BlockSpec((rows, LORA_PAD), cur),
                  pl.BlockSpec((SUBLANES, 3 * RWKV_WIDTH), prev8), pl.BlockSpec((SUBLANES, LORA_PAD), prev8),
                  per_seq3(3 * RWKV_WIDTH), per_seq3(LORA_PAD), state_spec]
                 + [_full_spec(a) for a in wts],
        out_specs=[pl.BlockSpec((rows, RWKV_WIDTH), cur), state_spec],
        out_shape=[jax.ShapeDtypeStruct((t_total, RWKV_WIDTH), BF16 if chunk % 16 == 0 else F32),
                   jax.ShapeDtypeStruct(s0.shape, F32)],
        scratch_shapes=[pltpu.VMEM((RWKV_HEADS, RWKV_HEAD_DIM, RWKV_HEAD_DIM), F32)],
        compiler_params=_params("parallel", "arbitrary"),
        name="rwkv",
    )(pm, plo, pm, plo, shift_m, shift_l, s0, *wts)


def _lambda(lam_ref, lam_init):
    lv = lam_ref[...]
    return (jnp.exp(jnp.sum(lv[0:1, :] * lv[1:2, :], axis=-1, keepdims=True))
            - jnp.exp(jnp.sum(lv[2:3, :] * lv[3:4, :], axis=-1, keepdims=True)) + lam_init)


def _subln(o, g, lam_init):
    return o * lax.rsqrt(jnp.mean(o * o, axis=-1, keepdims=True) + NORM_EPS) * g * (1.0 - lam_init)


def _attn_prompt_body(qi_ref, ki_ref, q_ref, kt_ref, v_ref, lam_ref, sg_ref, o_ref,
                      qm_ref, m_ref, acc_ref, *, lam_init):
    p = pl.program_id(2)
    qi = qi_ref[p]
    ki = ki_ref[p]
    tq = q_ref.shape[0]
    tk = kt_ref.shape[1]
    dv = v_ref.shape[1]

    @pl.when(ki == 0)
    def _():
        q = q_ref[...]
        lane = lax.broadcasted_iota(I32, q.shape, 1)
        qm_ref[0] = jnp.where(lane < DIFF_DK, q, jnp.zeros_like(q))
        qm_ref[1] = jnp.where(lane >= DIFF_DK, q, jnp.zeros_like(q))
        m_ref[...] = jnp.full(m_ref.shape, NEG_INF, F32)
        acc_ref[...] = jnp.zeros(acc_ref.shape, F32)

    def accumulate(masked):
        kt = kt_ref[...].astype(BF16)
        v1 = jnp.concatenate([v_ref[...], jnp.ones((tk, dv), BF16)], axis=1)
        for c in range(2):
            for r0 in range(0, tq, ATTN_ROW_CHUNK):
                rows = slice(r0, r0 + ATTN_ROW_CHUNK)
                ncol = min(tk, r0 + ATTN_ROW_CHUNK) if masked else tk
                s = jnp.dot(qm_ref[c, rows, :], kt[:, 0:ncol], preferred_element_type=F32)
                if masked:
                    row = lax.broadcasted_iota(I32, s.shape, 0) + r0
                    col = lax.broadcasted_iota(I32, s.shape, 1)
                    s = jnp.where(col <= row, s, NEG_INF)
                m_old = m_ref[c, rows, :]
                m_new = jnp.maximum(m_old, jnp.max(s, axis=-1, keepdims=True))
                alpha = jnp.exp(m_old - m_new)
                pr = jnp.exp(s - jnp.concatenate([m_new] * (ncol // LANES), axis=1))
                acc_ref[c, rows, :] = (jnp.concatenate([alpha] * (2 * dv // LANES), axis=1) * acc_ref[c, rows, :]
                                       + jnp.dot(pr.astype(BF16), v1[0:ncol, :], preferred_element_type=F32))
                m_ref[c, rows, :] = m_new

    @pl.when(ki < qi)
    def _():
        accumulate(False)

    @pl.when(ki == qi)
    def _():
        accumulate(True)
        lam = _lambda(lam_ref, lam_init)
        a0 = acc_ref[0]
        a1 = acc_ref[1]
        o = a0[:, 0:dv] / a0[:, dv:2 * dv] - lam * (a1[:, 0:dv] / a1[:, dv:2 * dv])
        o_ref[...] = _subln(o, sg_ref[...], lam_init).astype(o_ref.dtype)


def _attn_prompt(qb, kt, vb, lam_vecs, subln_g, seq_len, lam_init):
    t_total = qb.shape[0]
    nb = t_total // seq_len
    tq = min(ATTN_TILE, seq_len)
    nq = seq_len // tq
    pairs = [(i, j) for i in range(nq) for j in range(i + 1)]
    qi_tab = jnp.asarray([a for a, _ in pairs], I32)
    ki_tab = jnp.asarray([b for _, b in pairs], I32)
    qmap = lambda b, h, p, qi, ki: (b * nq + qi[p], h)
    vmap = lambda b, h, p, qi, ki: (b * nq + ki[p], h)
    ktmap = lambda b, h, p, qi, ki: (b * DIFF_HEADS + h, ki[p])
    grid_spec = pltpu.PrefetchScalarGridSpec(
        num_scalar_prefetch=2,
        grid=(nb, DIFF_HEADS, len(pairs)),
        in_specs=[pl.BlockSpec((tq, DIFF_DV), qmap), pl.BlockSpec((2 * DIFF_DK, tq), ktmap),
                  pl.BlockSpec((tq, DIFF_DV), vmap),
                  pl.BlockSpec(lam_vecs.shape, lambda *_: (0, 0)), pl.BlockSpec(subln_g.shape, lambda *_: (0, 0))],
        out_specs=pl.BlockSpec((tq, DIFF_DV), qmap),
        scratch_shapes=[pltpu.VMEM((2, tq, DIFF_DV), BF16), pltpu.VMEM((2, tq, LANES), F32),
                        pltpu.VMEM((2, tq, 2 * DIFF_DV), F32)],
    )
    return pl.pallas_call(
        functools.partial(_attn_prompt_body, lam_init=lam_init),
        grid_spec=grid_spec,
        out_shape=jax.ShapeDtypeStruct((t_total, DIFF_V_WIDTH), BF16),
        compiler_params=_params("parallel", "parallel", "arbitrary"),
        name="attn_prompt",
    )(qi_tab, ki_tab, qb, kt, vb, lam_vecs, subln_g)


def _attn_sample_body(pt_ref, q_ref, k_ref, v_ref, lam_ref, sg_ref, *rest, n_pages, lam_init):
    kp_refs = rest[:n_pages]
    vp_refs = rest[n_pages:2 * n_pages]
    o_ref = rest[2 * n_pages]
    s_new = q_ref.shape[0]
    n_maps = DIFF_HEADS * 2
    page = kp_refs[0].shape[1]
    nrow = n_maps * s_new

    def value_page(vr):
        return jnp.concatenate([vr[pl.ds(h, page, stride=DIFF_HEADS), :] for h in range(DIFF_HEADS)], axis=1)

    qt = jnp.concatenate([q_ref[...] * DIFF_SCALE] * (DIFF_HEADS * 2), axis=0)
    row = lax.broadcasted_iota(I32, qt.shape, 0)
    col = lax.broadcasted_iota(I32, qt.shape, 1)
    qbd = jnp.where(col // DIFF_DK == row // s_new, qt, 0.0).astype(BF16)
    pad = jnp.zeros((page - s_new, k_ref.shape[1]), F32)
    k_new = jnp.concatenate([k_ref[...], pad], axis=0)
    v_new = jnp.concatenate([v_ref[...], pad], axis=0)
    scores = [_bdot(qbd, kr[...]) for kr in kp_refs]
    s_n = _bdot_nt(qbd, k_new)
    rn = lax.broadcasted_iota(I32, s_n.shape, 0)
    cn = lax.broadcasted_iota(I32, s_n.shape, 1)
    scores.append(jnp.where(cn <= rn % s_new, s_n, NEG_INF))
    m = functools.reduce(jnp.maximum, [jnp.max(s, axis=-1, keepdims=True) for s in scores])
    values = [value_page(vr) for vr in vp_refs] + [v_new]
    l = jnp.zeros_like(m)
    acc = jnp.zeros((nrow, v_new.shape[1]), F32)
    for s, val in zip(scores, values):
        pr = jnp.exp(s - m)
        l = l + jnp.sum(pr, axis=-1, keepdims=True)
        acc = acc + _bdot(pr, val)
    full = acc / l
    lam = _lambda(lam_ref, lam_init)
    outs = []
    for h in range(DIFF_HEADS):
        r0 = h * 2 * s_new
        cols = slice(h * DIFF_DV, (h + 1) * DIFF_DV)
        o = full[r0:r0 + s_new, cols] - lam * full[r0 + s_new:r0 + 2 * s_new, cols]
        outs.append(_subln(o, sg_ref[...], lam_init))
    o_ref[...] = jnp.concatenate(outs, axis=1)


def _attn_sample(q, k, v, cache_k, cache_v, n_pool_pages, page_table, page_offset, lam_vecs, subln_g, s_new,
                 lam_init):
    nseq, n_pages = page_table.shape
    width = q.shape[1]
    k_rows = cache_k.shape[0] // n_pool_pages
    v_rows = cache_v.shape[0] // n_pool_pages
    new_spec = pl.BlockSpec((s_new, width), lambda b, pt: (b, 0))
    page_spec = lambda rows, lanes, j: pl.BlockSpec((rows, lanes), lambda b, pt: (pt[b, j] + page_offset, 0))
    grid_spec = pltpu.PrefetchScalarGridSpec(
        num_scalar_prefetch=1,
        grid=(nseq,),
        in_specs=[new_spec, new_spec, new_spec,
                  pl.BlockSpec(lam_vecs.shape, lambda *_: (0, 0)), pl.BlockSpec(subln_g.shape, lambda *_: (0, 0))]
                 + [page_spec(k_rows, cache_k.shape[1], j) for j in range(n_pages)]
                 + [page_spec(v_rows, DIFF_DV, j) for j in range(n_pages)],
        out_specs=new_spec,
    )
    return pl.pallas_call(
        functools.partial(_attn_sample_body, n_pages=n_pages, lam_init=lam_init),
        grid_spec=grid_spec,
        out_shape=jax.ShapeDtypeStruct((nseq * s_new, width), F32),
        compiler_params=_params("parallel"),
        name="attn_sample",
    )(page_table, q, k, v, lam_vecs, subln_g, *([cache_k] * n_pages), *([cache_v] * n_pages))


def _outproj_body(yr_ref, o_ref, x_ref, g1_ref, g2n_ref, sh_ref, sc_ref, g2_ref, woa_ref, wob_ref,
                  wsg_ref, wsu_ref, wsd_ref, wrh_ref, wrl_ref,
                  base_ref, h2t_ref, lg_ref):
    tm = x_ref.shape[0]
    mixed = (jnp.dot(yr_ref[...].astype(BF16), woa_ref[...], preferred_element_type=F32)
             + jnp.dot(o_ref[...].astype(BF16), wob_ref[...], preferred_element_type=F32))
    x1 = x_ref[...] + g1_ref[...] * mixed
    h2 = _rms(x1, g2n_ref[...]) * (1.0 + sc_ref[...]) + sh_ref[...]
    h2b = h2.astype(BF16)
    hidden = _silu(jnp.dot(h2b, wsg_ref[...], preferred_element_type=F32)) * jnp.dot(
        h2b, wsu_ref[...], preferred_element_type=F32)
    shared = jnp.dot(hidden.astype(BF16), wsd_ref[...], preferred_element_type=F32)
    base_ref[...] = x1 + g2_ref[...] * shared
    h2l = (h2 - h2b.astype(F32)).astype(BF16)
    nt = lambda a, b: lax.dot_general(a, b, (((1,), (1,)), ((), ())), preferred_element_type=F32)
    lg_ref[...] = nt(wrh_ref[...], h2b) + nt(wrl_ref[...], h2b) + nt(wrh_ref[...], h2l)
    for c in range(ROW_TILE):
        h2t_ref[pl.ds(c, tm, stride=ROW_TILE), :] = h2[:, c * LANES:(c + 1) * LANES]


def _outproj(yr, o, x, gate1, norm2_g, shift2, scale2, gate2, wts, tm, tiles_per_seq):
    t, d = x.shape
    row = lambda n: pl.BlockSpec((tm, n), lambda i: (i, 0))
    mod = lambda m: _mod_spec(m, tm, tiles_per_seq)
    return pl.pallas_call(
        _outproj_body,
        grid=(t // tm,),
        in_specs=[row(RWKV_WIDTH), row(DIFF_V_WIDTH), row(d), mod(gate1), _full_spec(norm2_g), mod(shift2),
                  mod(scale2), mod(gate2)] + [_full_spec(a) for a in wts],
        out_specs=[row(d), pl.BlockSpec((tm * ROW_TILE, LANES), lambda i: (i, 0)),
                   pl.BlockSpec((N_EXPERTS, tm), lambda i: (0, i))],
        out_shape=[jax.ShapeDtypeStruct((t, d), F32),
                   jax.ShapeDtypeStruct((t * ROW_TILE, LANES), F32),
                   jax.ShapeDtypeStruct((N_EXPERTS, t), F32)],
        compiler_params=_params("parallel"),
        name="outproj",
    )(yr, o, x, gate1, norm2_g, shift2, scale2, gate2, *wts)


def _router_body(lg_ref, bias_ref, idx_ref, gate_ref, pos_ref, cnt_ref, run_ref):
    i = pl.program_id(0)
    tm = lg_ref.shape[1]

    @pl.when(i == 0)
    def _():
        run_ref[...] = jnp.zeros(run_ref.shape, F32)

    scores = _sigmoid(lg_ref[...])
    biased = scores + bias_ref[...]
    erow = lax.broadcasted_iota(I32, (N_EXPERTS, tm), 0)
    grow = lax.broadcasted_iota(I32, (GROUP_SIZE, tm), 0)

    def first_argmax(x, rows, limit):
        mx = jnp.max(x, axis=0, keepdims=True)
        return mx, jnp.min(jnp.where(x == mx, rows, limit), axis=0, keepdims=True)

    group_scores = []
    for gidx in range(N_GROUPS):
        xg = biased[gidx * GROUP_SIZE:(gidx + 1) * GROUP_SIZE, :]
        m1, i1 = first_argmax(xg, grow, GROUP_SIZE)
        m2 = jnp.max(jnp.where(grow == i1, NEG_INF, xg), axis=0, keepdims=True)
        group_scores.append(m1 + m2)
    gs = jnp.concatenate(group_scores, axis=0)
    g8 = lax.broadcasted_iota(I32, (N_GROUPS, tm), 0)
    chosen = jnp.zeros((N_GROUPS, tm), I32)
    for _ in range(TOPK_GROUPS):
        _, gi = first_argmax(gs, g8, N_GROUPS)
        hit = g8 == gi
        chosen = jnp.where(hit, 1, chosen)
        gs = jnp.where(hit, NEG_INF, gs)
    cand = jnp.concatenate(
        [jnp.where(chosen[gidx:gidx + 1, :] > 0, biased[gidx * GROUP_SIZE:(gidx + 1) * GROUP_SIZE, :], NEG_INF)
         for gidx in range(N_GROUPS)], axis=0)

    idxs, raws = [], []
    onehot = jnp.zeros((N_EXPERTS, tm), F32)
    for _ in range(TOP_K):
        _, ei = first_argmax(cand, erow, N_EXPERTS)
        hit = erow == ei
        idxs.append(ei)
        raws.append(jnp.sum(jnp.where(hit, scores, 0.0), axis=0, keepdims=True))
        onehot = onehot + hit.astype(F32)
        cand = jnp.where(hit, NEG_INF, cand)
    raw = jnp.concatenate(raws, axis=0)
    gate_ref[...] = raw / jnp.sum(raw, axis=0, keepdims=True) * ROUTE_SCALE
    idx_ref[...] = jnp.concatenate(idxs, axis=0)

    ti = lax.broadcasted_iota(I32, (tm, tm), 0)
    tj = lax.broadcasted_iota(I32, (tm, tm), 1)
    oh = onehot.astype(BF16)
    before = jnp.dot(oh, (ti < tj).astype(BF16), preferred_element_type=F32) + run_ref[...]
    run_ref[...] = run_ref[...] + jnp.dot(oh, jnp.ones((tm, tm), BF16), preferred_element_type=F32)
    pos_ref[...] = jnp.concatenate(
        [jnp.sum(jnp.where(erow == ei, before, 0.0), axis=0, keepdims=True) for ei in idxs], axis=0).astype(I32)
    cnt_ref[...] = run_ref[...]


def _router(logits_t, bias_col):
    t = logits_t.shape[1]
    tm = ROUTE_TILE
    tok = pl.BlockSpec((TOP_K, tm), lambda i: (0, i))
    return pl.pallas_call(
        _router_body,
        grid=(t // tm,),
        in_specs=[pl.BlockSpec((N_EXPERTS, tm), lambda i: (0, i)), _full_spec(bias_col)],
        out_specs=[tok, tok, tok, pl.BlockSpec((N_EXPERTS, tm), lambda i: (0, 0))],
        out_shape=[jax.ShapeDtypeStruct((TOP_K, t), I32), jax.ShapeDtypeStruct((TOP_K, t), F32),
                   jax.ShapeDtypeStruct((TOP_K, t), I32), jax.ShapeDtypeStruct((N_EXPERTS, tm), F32)],
        scratch_shapes=[pltpu.VMEM((N_EXPERTS, tm), F32)],
        compiler_params=_params("arbitrary"),
        name="router",
    )(logits_t, bias_col)


def _assign_body(idx_ref, pos_ref, start_ref, dest_ref):
    tm = idx_ref.shape[1]
    erow = lax.broadcasted_iota(I32, (N_EXPERTS, tm), 0)
    start = start_ref[...]
    idx = idx_ref[...]
    first = jnp.concatenate(
        [jnp.sum(jnp.where(erow == idx[j:j + 1, :], start, 0.0), axis=0, keepdims=True) for j in range(TOP_K)],
        axis=0)
    dest_ref[...] = first.astype(I32) + pos_ref[...]


def _assign(idx, pos, start_col):
    t = idx.shape[1]
    tm = ROUTE_TILE
    tok = pl.BlockSpec((TOP_K, tm), lambda i: (0, i))
    return pl.pallas_call(
        _assign_body,
        grid=(t // tm,),
        in_specs=[tok, tok, _full_spec(start_col)],
        out_specs=tok,
        out_shape=jax.ShapeDtypeStruct((TOP_K, t), I32),
        compiler_params=_params("parallel"),
        name="assign",
    )(idx, pos, start_col)


def _row_copy(src, src_row, dst, dst_row, sem):
    return pltpu.make_async_copy(src.at[pl.ds(pl.multiple_of(src_row * ROW_TILE, ROW_TILE), ROW_TILE), :],
                                 dst.at[pl.ds(pl.multiple_of(dst_row * ROW_TILE, ROW_TILE), ROW_TILE), :], sem)


def _dispatch_body(zflag_ref, dest_ref, h_ref, xs_ref, zero_ref, hbuf_ref, zsem, lsem, rsem):
    i = pl.program_id(0)
    n = pl.num_programs(0)
    tm = dest_ref.shape[1]
    tile_rows = tm * ROW_TILE
    blk_rows = EXPERT_BLOCK * ROW_TILE
    n_blocks = xs_ref.shape[0] // blk_rows

    def load(tile, slot):
        rows = pl.ds(pl.multiple_of(tile * tile_rows, tile_rows), tile_rows)
        return pltpu.make_async_copy(h_ref.at[rows, :], hbuf_ref.at[slot], lsem.at[slot])

    def scatter(slot):
        def issue_rows(t, carry):
            for j in range(TOP_K):
                _row_copy(hbuf_ref.at[slot], t, xs_ref, dest_ref[j, t], rsem.at[slot]).start()
            return carry

        lax.fori_loop(0, tm, issue_rows, 0)

    def wait_scatter(slot):
        for _ in range(TOP_K):
            pltpu.make_async_copy(hbuf_ref.at[slot], xs_ref.at[pl.ds(0, tile_rows), :], rsem.at[slot]).wait()

    def zero_copy(b):
        start = pl.multiple_of(b * blk_rows, blk_rows)
        return pltpu.make_async_copy(zero_ref, xs_ref.at[pl.ds(start, blk_rows), :], zsem)

    @pl.when(i == 0)
    def _():
        zero_ref[...] = jnp.zeros(zero_ref.shape, F32)

        def issue(b, carry):
            @pl.when(zflag_ref[b] > 0)
            def _():
                zero_copy(b).start()
            return carry

        def drain(b, carry):
            @pl.when(zflag_ref[b] > 0)
            def _():
                zero_copy(b).wait()
            return carry

        lax.fori_loop(0, n_blocks, issue, 0)
        lax.fori_loop(0, n_blocks, drain, 0)
        load(0, 0).start()

        @pl.when(n > 1)
        def _():
            load(1, 1).start()

    n_buf = hbuf_ref.shape[0]
    for k in range(n_buf):
        @pl.when(i % n_buf == k)
        def _():
            prev = (k + n_buf - 1) % n_buf
            load(i, k).wait()
            scatter(k)

            @pl.when(i >= 1)
            def _():
                wait_scatter(prev)

            @pl.when(i + 2 < n)
            def _():
                load(i + 2, prev).start()

            @pl.when(i == n - 1)
            def _():
                wait_scatter(k)


def _dispatch(h2t, dest, zero_flags, n_rows):
    t = dest.shape[1]
    tm = ROUTE_TILE
    grid_spec = pltpu.PrefetchScalarGridSpec(
        num_scalar_prefetch=1,
        grid=(t // tm,),
        in_specs=[pl.BlockSpec((TOP_K, tm), lambda i, *_: (0, i), memory_space=pltpu.SMEM),
                  pl.BlockSpec(memory_space=pl.ANY)],
        out_specs=pl.BlockSpec(memory_space=pl.ANY),
        scratch_shapes=[pltpu.VMEM((EXPERT_BLOCK * ROW_TILE, LANES), F32),
                        pltpu.VMEM((3, tm * ROW_TILE, LANES), F32),
                        pltpu.SemaphoreType.DMA, pltpu.SemaphoreType.DMA((3,)), pltpu.SemaphoreType.DMA((3,))],
    )
    return pl.pallas_call(
        _dispatch_body,
        grid_spec=grid_spec,
        out_shape=jax.ShapeDtypeStruct((n_rows * ROW_TILE, LANES), F32),
        compiler_params=_params("arbitrary"),
        name="dispatch",
    )(zero_flags, dest, h2t)


def _expert_body(first_ref, count_ref, nu_ref, xs_ref, wg_ref, wu_ref, wd_ref, y_ref,
                 xin_ref, yout_ref, wgb_ref, wub_ref, wdb_ref, in_sem, out_sem):
    e = pl.program_id(0)
    rows = EXPERT_BLOCK
    blk_rows = EXPERT_BLOCK * ROW_TILE
    nb = count_ref[e]
    b0 = first_ref[e]

    n_used = nu_ref[0]
    n_in = xin_ref.shape[0]
    n_out = yout_ref.shape[0]
    ahead = n_in - 1
    part = blk_rows // EXPERT_DMA_SPLIT

    def hbm_rows(g, k=0, n=blk_rows):
        return pl.ds(pl.multiple_of(g * blk_rows + k * part, part), n)

    def start_in(g):
        slot = g % n_in
        for k in range(EXPERT_DMA_SPLIT):
            pltpu.make_async_copy(xs_ref.at[hbm_rows(g, k, part), :], xin_ref.at[slot, pl.ds(k * part, part), :],
                                  in_sem.at[slot]).start()

    def wait_in(g):
        slot = g % n_in
        pltpu.make_async_copy(xs_ref.at[hbm_rows(g), :], xin_ref.at[slot], in_sem.at[slot]).wait()

    def start_out(g):
        slot = g % n_out
        for k in range(EXPERT_DMA_SPLIT):
            pltpu.make_async_copy(yout_ref.at[slot, pl.ds(k * part, part), :], y_ref.at[hbm_rows(g, k, part), :],
                                  out_sem.at[slot]).start()

    def wait_out(g):
        slot = g % n_out
        pltpu.make_async_copy(yout_ref.at[slot], y_ref.at[hbm_rows(g), :], out_sem.at[slot]).wait()

    @pl.when(e == 0)
    def _():
        for g in range(ahead):
            @pl.when(g < n_used)
            def _():
                start_in(g)

    @pl.when(nb > 0)
    def _():
        wgb_ref[...] = wg_ref[...].astype(BF16)
        wub_ref[...] = wu_ref[...].astype(BF16)
        wdb_ref[...] = wd_ref[...].astype(BF16)

        def step(b, carry):
            g = b0 + b
            wait_in(g)

            @pl.when(g + ahead < n_used)
            def _():
                start_in(g + ahead)

            @pl.when(g >= n_out)
            def _():
                wait_out(g - n_out)

            islot = g % n_in
            oslot = g % n_out
            x = jnp.concatenate([xin_ref[islot, pl.ds(c, rows, stride=ROW_TILE), :] for c in range(ROW_TILE)],
                                axis=1).astype(BF16)
            hidden = _silu(jnp.dot(x, wgb_ref[...], preferred_element_type=F32)) * jnp.dot(
                x, wub_ref[...], preferred_element_type=F32)
            y = jnp.dot(hidden.astype(BF16), wdb_ref[...], preferred_element_type=F32)
            for c in range(ROW_TILE):
                yout_ref[oslot, pl.ds(c, rows, stride=ROW_TILE), :] = y[:, c * LANES:(c + 1) * LANES]
            start_out(g)
            return carry

        lax.fori_loop(0, nb, step, 0)

    @pl.when(e == pl.num_programs(0) - 1)
    def _():
        def drain(g, carry):
            wait_out(g)
            return carry

        lax.fori_loop(jnp.maximum(n_used - n_out, 0), n_used, drain, 0)
        n_blocks = y_ref.shape[0] // blk_rows
        yout_ref[0] = jnp.zeros(yout_ref.shape[1:], F32)

        def tail(b, carry):
            cp = pltpu.make_async_copy(
                yout_ref.at[0], y_ref.at[pl.ds(pl.multiple_of(b * blk_rows, blk_rows), blk_rows), :], out_sem.at[0])
            cp.start()
            cp.wait()
            return carry

        lax.fori_loop(nu_ref[0], n_blocks, tail, 0)


def _experts(xs, first_block, block_count, n_used, we_gate, we_up, we_down):
    d, f = we_gate.shape[1], we_gate.shape[2]
    blk = (EXPERT_BLOCK * ROW_TILE, LANES)
    wspec = lambda a, b: pl.BlockSpec((None, a, b), lambda e, *_: (e, 0, 0))
    grid_spec = pltpu.PrefetchScalarGridSpec(
        num_scalar_prefetch=3,
        grid=(N_EXPERTS,),
        in_specs=[pl.BlockSpec(memory_space=pl.ANY), wspec(d, f), wspec(d, f), wspec(f, d)],
        out_specs=pl.BlockSpec(memory_space=pl.ANY),
        scratch_shapes=[pltpu.VMEM((EXPERT_IN_SLOTS,) + blk, F32), pltpu.VMEM((EXPERT_OUT_SLOTS,) + blk, F32),
                        pltpu.VMEM((d, f), BF16), pltpu.VMEM((d, f), BF16), pltpu.VMEM((f, d), BF16),
                        pltpu.SemaphoreType.DMA((EXPERT_IN_SLOTS,)), pltpu.SemaphoreType.DMA((EXPERT_OUT_SLOTS,))],
    )
    return pl.pallas_call(
        _expert_body,
        grid_spec=grid_spec,
        out_shape=jax.ShapeDtypeStruct(xs.shape, F32),
        compiler_params=_params("arbitrary"),
        name="experts",
    )(first_block, block_count, n_used, xs, we_gate, we_up, we_down)


def _combine_body(base_ref, g2_ref, gate_ref, dest_ref, next_ref, y_ref, out_ref, buf_ref, sem):
    i = pl.program_id(0)
    n = pl.num_programs(0)
    tm = base_ref.shape[0]
    slot_rows = tm * ROW_TILE

    def issue_tile(idx_ref, slot):
        def issue(t, carry):
            for j in range(TOP_K):
                _row_copy(y_ref, idx_ref[j, t], buf_ref.at[slot], j * tm + t, sem.at[slot]).start()
            return carry

        lax.fori_loop(0, tm, issue, 0)

    def finish_tile(slot):
        pltpu.make_async_copy(y_ref.at[pl.ds(0, TOP_K * slot_rows), :], buf_ref.at[slot], sem.at[slot]).wait()
        gates = gate_ref[...]
        gcols = [jnp.broadcast_to(gates[:, j:j + 1], (tm, LANES)) for j in range(TOP_K)]
        for c in range(ROW_TILE):
            cols = slice(c * LANES, (c + 1) * LANES)
            routed = sum(gcols[j] * buf_ref[slot, pl.ds(j * slot_rows + c, tm, stride=ROW_TILE), :]
                         for j in range(TOP_K))
            out_ref[:, cols] = base_ref[:, cols] + g2_ref[:, cols] * routed

    @pl.when(i == 0)
    def _():
        issue_tile(dest_ref, 0)

    for parity in range(2):
        @pl.when(i % 2 == parity)
        def _():
            @pl.when(i + 1 < n)
            def _():
                issue_tile(next_ref, 1 - parity)

            finish_tile(parity)


def _combine(base, gate2, gates, dest, y_rows, tok_offset, tiles_per_seq):
    t, d = base.shape
    tm = ROUTE_TILE
    off = tok_offset // tm
    steps = t // tm
    return pl.pallas_call(
        _combine_body,
        grid=(steps,),
        in_specs=[pl.BlockSpec((tm, d), lambda i: (i, 0)), _mod_spec(gate2, tm, tiles_per_seq),
                  pl.BlockSpec((tm, TOP_K), lambda i: (i + off, 0)),
                  pl.BlockSpec((TOP_K, tm), lambda i: (0, i + off), memory_space=pltpu.SMEM),
                  pl.BlockSpec((TOP_K, tm), lambda i: (0, jnp.minimum(i + 1, steps - 1) + off),
                               memory_space=pltpu.SMEM),
                  pl.BlockSpec(memory_space=pl.ANY)],
        out_specs=pl.BlockSpec((tm, d), lambda i: (i, 0)),
        out_shape=jax.ShapeDtypeStruct((t, d), F32),
        scratch_shapes=[pltpu.VMEM((2, TOP_K * tm * ROW_TILE, LANES), F32), pltpu.SemaphoreType.DMA((2,))],
        compiler_params=_params("arbitrary"),
        name="combine",
    )(base, gate2, gates, dest, dest, y_rows)


def _segment_ones(width, seg):
    ids = jnp.arange(width) // seg
    return (ids[:, None] == ids[None, :]).astype(BF16)


def _pad_rows(w, start, total):
    return jnp.zeros((total, w.shape[1]), w.dtype).at[start:start + w.shape[0]].set(w)


def kernel(x_prompt, x_sample, cache_k, cache_v, state_wkv, state_shift, page_table, c_prompt, c_sample, w_ada, b_ada, norm1_g, norm2_g, w_in, mu_shift, w0, w_lora_up, a0, a_lora_up, g_lora_up, k_k, k_a, r_k, lnx_g, lnx_b, qn_g, kn_g, lam_q1, lam_k1, lam_q2, lam_k2, subln_g, w_out, w_router, router_bias, we_gate, we_up, we_down, ws_gate, ws_up, ws_down):
    depth = w_in.shape[0]
    nb, seq, d = x_prompt.shape
    db, dseq, _ = x_sample.shape
    n_pool, page = cache_k.shape[1], cache_k.shape[2]
    tp, ts = nb * seq, db * dseq
    w3 = 3 * RWKV_WIDTH
    row = lambda a: a.reshape(1, -1)
    seg512 = _segment_ones(RWKV_WIDTH, RWKV_HEAD_DIM)
    tm_p = min(512, seq)
    tm_s = min(256, ts)
    chunk_p = min(64, seq)

    xp = x_prompt.reshape(tp, d)
    xs = x_sample.reshape(ts, d)
    cache_k2 = jnp.transpose(cache_k, (0, 1, 3, 4, 5, 2)).reshape(depth * n_pool * DIFF_QK_WIDTH, page)
    cache_v2 = cache_v.reshape(depth * n_pool * page * DIFF_HEADS, DIFF_DV)
    n_cond = nb + db
    cond = jnp.concatenate([c_prompt, c_sample], axis=0)
    cond = jnp.pad(cond, ((0, -n_cond % SUBLANES), (0, 0)))

    outs = {name: [] for name in ("kp", "vp", "wp", "sp", "ks", "vs", "ws", "ss")}
    for layer in range(depth):
        lam_init = 0.8 - 0.6 * math.exp(-0.3 * layer)
        mods = _ada(cond, w_ada[layer], row(b_ada[layer]))
        mod_p = [m.reshape(nb, 1, d) for m in jnp.split(mods[:nb], 6, axis=-1)]
        mod_s = [jnp.repeat(m, dseq, axis=0) for m in jnp.split(mods[nb:n_cond], 6, axis=-1)]

        wi = w_in[layer]
        in_wts = (wi[:, :w3].astype(BF16),
                  jnp.pad(wi[:, w3:RWKV_PROJ], ((0, 0), (0, LORA_PAD - LORA_WIDTH))).astype(BF16),
                  wi[:, RWKV_PROJ:RWKV_PROJ + DIFF_QK_WIDTH].astype(BF16),
                  wi[:, RWKV_PROJ + DIFF_QK_WIDTH:RWKV_PROJ + 2 * DIFF_QK_WIDTH].astype(BF16),
                  wi[:, RWKV_PROJ + 2 * DIFF_QK_WIDTH:].astype(BF16),
                  row(jnp.tile(qn_g[layer], DIFF_QK_WIDTH // DIFF_DK)),
                  row(jnp.tile(kn_g[layer], DIFF_QK_WIDTH // DIFF_DK)),
                  seg512)
        mu = mu_shift[layer]
        rwkv_wts = (row(mu[:w3]), row(jnp.pad(mu[w3:], (0, LORA_PAD - LORA_WIDTH))), row(w0[layer]),
                    _pad_rows(w_lora_up[layer], 0, LORA_PAD).astype(BF16), row(a0[layer]),
                    _pad_rows(a_lora_up[layer], DECAY_LORA, LORA_PAD).astype(BF16),
                    _pad_rows(g_lora_up[layer], DECAY_LORA + ICLR_LORA, LORA_PAD).astype(BF16),
                    row(k_k[layer]), row(k_a[layer]), row(r_k[layer]), row(lnx_g[layer]), row(lnx_b[layer]), seg512)
        lam_vecs = jnp.stack([lam_q1[layer], lam_k1[layer], lam_q2[layer], lam_k2[layer]])
        sg = row(subln_g[layer])
        wr = w_router[layer].T
        wr_hi = wr.astype(BF16)
        out_wts = (w_out[layer][:RWKV_WIDTH].astype(BF16), w_out[layer][RWKV_WIDTH:].astype(BF16),
                   ws_gate[layer].astype(BF16), ws_up[layer].astype(BF16), ws_down[layer].astype(BF16),
                   wr_hi, (wr - wr_hi.astype(F32)).astype(BF16))

        def token_mix(x, mod, tm, tiles_per_seq, seq_len, chunk, shift_prev, wkv0, long_seq):
            pm, plo, *qkv = _inproj(x, row(norm1_g[layer]), mod[0], mod[1], in_wts, tm, tiles_per_seq, long_seq)
            n_seq = x.shape[0] // seq_len
            shift_m = shift_prev[:, :, :w3]
            shift_l = jnp.pad(shift_prev[:, :, w3:], ((0, 0), (0, 0), (0, LORA_PAD - LORA_WIDTH)))
            if chunk == seq_len:
                per_step = RWKV_SEQS_PER_STEP if n_seq % RWKV_SEQS_PER_STEP == 0 else 1
            else:
                per_step = RWKV_CHUNKS_PER_STEP if seq_len % (chunk * RWKV_CHUNKS_PER_STEP) == 0 else 1
            yr, wkv = _rwkv(pm, plo, shift_m, shift_l, wkv0, rwkv_wts, seq_len, chunk, per_step)
            last = jnp.concatenate([pm.reshape(n_seq, seq_len, w3)[:, -1:],
                                    plo.reshape(n_seq, seq_len, LORA_PAD)[:, -1:, :LORA_WIDTH]], axis=-1)
            return yr, wkv, last, qkv

        yr_p, wkv_p, last_p, (kt_p, v4_p, qb_p, vb_p) = token_mix(
            xp, mod_p, tm_p, seq // tm_p, seq, chunk_p,
            jnp.zeros((nb, 1, RWKV_PROJ), F32), jnp.zeros((nb, RWKV_HEADS, RWKV_HEAD_DIM, RWKV_HEAD_DIM), F32), True)
        o_p = _attn_prompt(qb_p, kt_p, vb_p, lam_vecs, sg, seq, lam_init)
        yr_s, wkv_s, last_s, (q_s, k_s, v_s) = token_mix(
            xs, mod_s, tm_s, 1, dseq, dseq, state_shift[layer], state_wkv[layer], False)
        o_s = _attn_sample(q_s, k_s, v_s, cache_k2, cache_v2, depth * n_pool, page_table, layer * n_pool, lam_vecs,
                           sg, dseq, lam_init)

        base_p, h2t_p, lg_p = _outproj(yr_p, o_p, xp, mod_p[2], row(norm2_g[layer]), mod_p[3], mod_p[4], mod_p[5],
                                       out_wts, tm_p, seq // tm_p)
        base_s, h2t_s, lg_s = _outproj(yr_s, o_s, xs, mod_s[2], row(norm2_g[layer]), mod_s[3], mod_s[4], mod_s[5],
                                       out_wts, tm_s, 1)

        t_all = tp + ts
        idx, gate, pos, cnt = _router(jnp.concatenate([lg_p, lg_s], axis=1), router_bias[layer].reshape(-1, 1))
        counts = cnt[:, 0].astype(I32)
        padded = (counts + EXPERT_BLOCK - 1) // EXPERT_BLOCK * EXPERT_BLOCK
        end_padded = jnp.cumsum(padded)
        dest = _assign(idx, pos, (end_padded - padded).astype(F32).reshape(-1, 1))
        n_rows = (t_all * TOP_K + N_EXPERTS * (EXPERT_BLOCK - 1) + EXPERT_BLOCK - 1) // EXPERT_BLOCK * EXPERT_BLOCK
        n_blocks = n_rows // EXPERT_BLOCK
        blk_ids = jnp.arange(n_blocks, dtype=I32)
        n_used = (end_padded[-1:] // EXPERT_BLOCK).astype(I32)
        is_expert_end = jnp.any(end_padded[None, :] == (blk_ids[:, None] + 1) * EXPERT_BLOCK, axis=1)
        zero_flags = jnp.logical_or(blk_ids >= n_used[0], is_expert_end).astype(I32)
        rows_in = _dispatch(jnp.concatenate([h2t_p, h2t_s], axis=0), dest, zero_flags, n_rows)
        rows_out = _experts(rows_in, ((end_padded - padded) // EXPERT_BLOCK).astype(I32),
                            (padded // EXPERT_BLOCK).astype(I32), n_used, we_gate[layer], we_up[layer],
                            we_down[layer])
        gates_t = gate.T
        xp = _combine(base_p, mod_p[5], gates_t, dest, rows_out, 0, seq // ROUTE_TILE)
        xs = _combine(base_s, mod_s[5], gates_t, dest, rows_out, tp, 1)

        outs["kp"].append(jnp.transpose(kt_p.reshape(nb, DIFF_HEADS, 2, DIFF_DK, seq), (0, 4, 1, 2, 3)))
        outs["vp"].append(v4_p.reshape(nb, seq, DIFF_HEADS, DIFF_DV))
        outs["wp"].append(wkv_p)
        outs["sp"].append(last_p)
        outs["ks"].append(k_s.reshape(db, dseq, DIFF_HEADS, 2, DIFF_DK))
        outs["vs"].append(v_s.reshape(db, dseq, DIFF_HEADS, DIFF_DV))
        outs["ws"].append(wkv_s)
        outs["ss"].append(last_s)

    st = {name: jnp.stack(v) for name, v in outs.items()}
    return (xp.reshape(nb, seq, d), xs.reshape(db, dseq, d), st["kp"], st["vp"], st["wp"], st["sp"],
            st["ks"], st["vs"], st["ws"], st["ss"])
```

```python
import functools
import math

import jax
import jax.numpy as jnp
from jax import lax
from jax.experimental import pallas as pl
from jax.experimental.pallas import tpu as pltpu

F32 = jnp.float32
BF16 = jnp.bfloat16
I32 = jnp.int32

RWKV_HEADS = 8
RWKV_HEAD_DIM = 64
RWKV_WIDTH = RWKV_HEADS * RWKV_HEAD_DIM
DECAY_LORA = 32
ICLR_LORA = 32
GATE_LORA = 96
LORA_WIDTH = DECAY_LORA + ICLR_LORA + GATE_LORA
LORA_PAD = 256
RWKV_PROJ = 3 * RWKV_WIDTH + LORA_WIDTH
GN_EPS = 64e-5
DIFF_HEADS = 4
DIFF_DK = 64
DIFF_DV = 2 * DIFF_DK
DIFF_QK_WIDTH = DIFF_HEADS * 2 * DIFF_DK
DIFF_V_WIDTH = DIFF_HEADS * DIFF_DV
DIFF_SCALE = DIFF_DK ** -0.5
N_EXPERTS = 256
N_GROUPS = 8
GROUP_SIZE = N_EXPERTS // N_GROUPS
TOPK_GROUPS = 4
TOP_K = 8
ROUTE_SCALE = 2.5
NORM_EPS = 1e-6

LANES = 128
SUBLANES = 8
ROW_TILE = 8
VMEM_LIMIT = 48 * 1024 * 1024

EXPERT_BLOCK = 128
EXPERT_IN_SLOTS = 4
EXPERT_OUT_SLOTS = 3
EXPERT_DMA_SPLIT = 4
ROUTE_TILE = 128
RWKV_CHUNKS_PER_STEP = 4
RWKV_SEQS_PER_STEP = 8
ATTN_TILE = 2048
ATTN_ROW_CHUNK = 256
NEG_INF = float("-inf")


def _bdot(a, b):
    return jnp.dot(a.astype(BF16), b.astype(BF16), preferred_element_type=F32)


def _bdot_nt(a, b):
    return lax.dot_general(a.astype(BF16), b.astype(BF16), (((1,), (1,)), ((), ())), preferred_element_type=F32)


def _bdot_tn(a, b):
    return lax.dot_general(a.astype(BF16), b.astype(BF16), (((0,), (0,)), ((), ())), preferred_element_type=F32)


def _sigmoid(x):
    return 1.0 / (1.0 + jnp.exp(-x))


def _silu(x):
    return x * _sigmoid(x)


def _params(*sem, vmem=VMEM_LIMIT):
    return pltpu.CompilerParams(dimension_semantics=sem, vmem_limit_bytes=vmem)


def _ada_body(c_ref, w_ref, b_ref, o_ref):
    o_ref[...] = _bdot(_silu(c_ref[...]), w_ref[...]) + b_ref[...]


def _ada(c, w, b):
    rows, d = c.shape
    n = w.shape[1]
    tn = 512
    return pl.pallas_call(
        _ada_body,
        grid=(n // tn,),
        in_specs=[pl.BlockSpec((rows, d), lambda j: (0, 0)),
                  pl.BlockSpec((d, tn), lambda j: (0, j)),
                  pl.BlockSpec((1, tn), lambda j: (0, j))],
        out_specs=pl.BlockSpec((rows, tn), lambda j: (0, j)),
        out_shape=jax.ShapeDtypeStruct((rows, n), F32),
        compiler_params=_params("parallel"),
        name="ada",
    )(c, w, b)


def _mod_spec(mod, tm, tiles_per_seq):
    if mod.ndim == 3:
        return pl.BlockSpec((None, 1, mod.shape[-1]), lambda i: (i // tiles_per_seq, 0, 0))
    return pl.BlockSpec((tm, mod.shape[-1]), lambda i: (i, 0))


def _full_spec(a):
    nd = a.ndim
    return pl.BlockSpec(a.shape, lambda *_: (0,) * nd)


def _rms(x, g):
    return x * lax.rsqrt(jnp.mean(x * x, axis=-1, keepdims=True) + NORM_EPS) * g


def _inproj_body(x_ref, g_ref, sh_ref, sc_ref, wm_ref, wl_ref, wq_ref, wk_ref, wv_ref, qg_ref, kg_ref, seg_ref,
                 pm_ref, pl_ref, *out_refs, long_seq):
    tm = x_ref.shape[0]
    h = (_rms(x_ref[...], g_ref[...]) * (1.0 + sc_ref[...]) + sh_ref[...]).astype(BF16)
    pm_ref[...] = jnp.dot(h, wm_ref[...], preferred_element_type=F32)
    pl_ref[...] = jnp.dot(h, wl_ref[...], preferred_element_type=F32)
    seg = seg_ref[...]

    def head_norm(z, gain):
        ms = _bdot(z * z, seg) * (1.0 / DIFF_DK)
        return z * lax.rsqrt(ms + NORM_EPS) * gain

    q = head_norm(jnp.dot(h, wq_ref[...], preferred_element_type=F32), qg_ref[...])
    k = head_norm(jnp.dot(h, wk_ref[...], preferred_element_type=F32), kg_ref[...])
    v = jnp.dot(h, wv_ref[...], preferred_element_type=F32)
    if long_seq:
        kt_ref, v4_ref, qb_ref, vb_ref = out_refs
        kt_ref[...] = k.T
        for hd in range(DIFF_HEADS):
            v4_ref[pl.ds(hd, tm, stride=DIFF_HEADS), :] = v[:, hd * DIFF_DV:(hd + 1) * DIFF_DV]
        qb_ref[...] = (q * DIFF_SCALE).astype(BF16)
        vb_ref[...] = v.astype(BF16)
    else:
        q_ref, k_ref, v_ref = out_refs
        q_ref[...] = q
        k_ref[...] = k
        v_ref[...] = v


def _inproj(x, g, shift, scale, wts, tm, tiles_per_seq, long_seq):
    t, d = x.shape
    wm, wl, wq, wk, wv, qg, kg, seg = wts
    row = lambda n: pl.BlockSpec((tm, n), lambda i: (i, 0))
    out_specs = [row(3 * RWKV_WIDTH), row(LORA_PAD)]
    out_shape = [jax.ShapeDtypeStruct((t, 3 * RWKV_WIDTH), F32), jax.ShapeDtypeStruct((t, LORA_PAD), F32)]
    if long_seq:
        n_seq = t // (tm * tiles_per_seq)
        out_specs += [pl.BlockSpec((DIFF_QK_WIDTH, tm), lambda i: (i // tiles_per_seq, i % tiles_per_seq)),
                      pl.BlockSpec((tm * DIFF_HEADS, DIFF_DV), lambda i: (i, 0)),
                      row(DIFF_QK_WIDTH), row(DIFF_V_WIDTH)]
        out_shape += [jax.ShapeDtypeStruct((n_seq * DIFF_QK_WIDTH, tm * tiles_per_seq), F32),
                      jax.ShapeDtypeStruct((t * DIFF_HEADS, DIFF_DV), F32),
                      jax.ShapeDtypeStruct((t, DIFF_QK_WIDTH), BF16), jax.ShapeDtypeStruct((t, DIFF_V_WIDTH), BF16)]
    else:
        out_specs += [row(DIFF_QK_WIDTH), row(DIFF_QK_WIDTH), row(DIFF_V_WIDTH)]
        out_shape += [jax.ShapeDtypeStruct((t, n), F32) for n in (DIFF_QK_WIDTH, DIFF_QK_WIDTH, DIFF_V_WIDTH)]
    return pl.pallas_call(
        functools.partial(_inproj_body, long_seq=long_seq),
        grid=(t // tm,),
        in_specs=[row(d), _full_spec(g), _mod_spec(shift, tm, tiles_per_seq), _mod_spec(scale, tm, tiles_per_seq)]
                 + [_full_spec(a) for a in (wm, wl, wq, wk, wv, qg, kg, seg)],
        out_specs=out_specs,
        out_shape=out_shape,
        compiler_params=_params("parallel"),
        name="inproj",
    )(x, g, shift, scale, wm, wl, wq, wk, wv, qg, kg, seg)


def _split3(x):
    hi = x.astype(BF16)
    r1 = x - hi.astype(F32)
    mid = r1.astype(BF16)
    lo = (r1 - mid.astype(F32)).astype(BF16)
    return hi, mid, lo


def _rwkv_body(pm_ref, pl_ref, pm8_ref, pl8_ref, sm_ref, sl_ref, s0_ref,
               mum_ref, mul_ref, w0_ref, wupw_ref, a0_ref, wupa_ref, wupg_ref, kk_ref, ka_ref, rk_ref,
               lg_ref, lb_ref, seg_ref,
               y_ref, sout_ref, state_ref, *, chunk, whole_seqs):
    c = pl.program_id(1)
    nc = pl.num_programs(1)
    C = chunk
    W = RWKV_WIDTH
    N = RWKV_HEAD_DIM
    pm = pm_ref[...]
    plo = pl_ref[...]

    if whole_seqs:
        n_seq = pm.shape[0] // C
        rows_of = lambda ref: jnp.concatenate(
            [jnp.broadcast_to(ref[s], (C, ref.shape[-1])) for s in range(n_seq)], axis=0)
        prev_m, prev_l = rows_of(sm_ref), rows_of(sl_ref)

        def shifted(cur, prev_rows):
            rows = lax.broadcasted_iota(I32, cur.shape, 0)
            return jnp.where(rows % C == 0, prev_rows, pltpu.roll(cur, 1, 0))
    else:
        @pl.when(c == 0)
        def _():
            state_ref[...] = s0_ref[0]

        first = c == 0
        prev_m = jnp.where(first, sm_ref[0], pm8_ref[SUBLANES - 1:SUBLANES, :])
        prev_l = jnp.where(first, sl_ref[0], pl8_ref[SUBLANES - 1:SUBLANES, :])

        def shifted(cur, prev_row):
            rows = lax.broadcasted_iota(I32, cur.shape, 0)
            return jnp.where(rows == 0, prev_row, pltpu.roll(cur, 1, 0))

    xm = pm + (shifted(pm, prev_m) - pm) * mum_ref[...]
    xl = plo + (shifted(plo, prev_l) - plo) * mul_ref[...]
    r = xm[:, 0:W]
    k = xm[:, W:2 * W]
    v = xm[:, 2 * W:3 * W]
    seg = seg_ref[...]

    z = -(w0_ref[...] + _bdot(jnp.tanh(xl), wupw_ref[...]))
    softplus = jnp.maximum(z, 0.0) + jnp.log(1.0 + jnp.exp(-jnp.abs(z)))
    w = -softplus - 0.5
    a = _sigmoid(a0_ref[...] + _bdot(xl, wupa_ref[...]))
    g = _bdot(_sigmoid(xl), wupg_ref[...])
    kk = k * kk_ref[...]
    kk = kk / jnp.maximum(jnp.sqrt(_bdot(kk * kk, seg)), 1e-12)
    k = k * (1.0 + (a - 1.0) * ka_ref[...])
    logdec = -jnp.exp(w)

    rows_blk = pm.shape[0]
    n_sub = rows_blk // C
    bi = lax.broadcasted_iota(I32, (rows_blk, rows_blk), 0)
    bj = lax.broadcasted_iota(I32, (rows_blk, rows_blk), 1)
    lower = jnp.logical_and(bi >= bj, bi // C == bj // C).astype(BF16)
    cum = sum(jnp.dot(lower, part, preferred_element_type=F32) for part in _split3(logdec))
    cum_end = jnp.concatenate(
        [jnp.broadcast_to(cum[(s + 1) * C - 1:(s + 1) * C, :], (C, W)) for s in range(n_sub)], axis=0)
    a_t = -kk * jnp.exp(cum - logdec)
    r_t = r * jnp.exp(cum)
    inv = jnp.exp(-cum)
    b_t = kk * a * inv
    k_t = k * inv
    to_end = jnp.exp(cum_end - cum)
    b_e = kk * a * to_end
    k_e = k * to_end
    g_end = jnp.exp(cum_end)

    ti = lax.broadcasted_iota(I32, (C, C), 0)
    tj = lax.broadcasted_iota(I32, (C, C), 1)
    eye = (ti == tj).astype(F32)
    ri = lax.broadcasted_iota(I32, (2 * C, 2 * C), 0)
    ci = lax.broadcasted_iota(I32, (2 * C, 2 * C), 1)
    tr = jnp.where(ri >= C, ri - C, ri)
    tc = jnp.where(ci >= C, ci - C, ci)
    mask = jnp.logical_or(tr > tc, jnp.logical_and(ri >= C, tr == tc))
    zeros_cn = jnp.zeros((C, N), F32)
    levels = int(math.log2(C))
    heads = range(RWKV_HEADS)
    pairs = [(s, h) for s in range(n_sub) for h in heads]
    tile = lambda z, s, h: z[s * C:(s + 1) * C, h * N:(h + 1) * N]
    ah = [tile(a_t, s, h) for s, h in pairs]
    rh = [tile(r_t, s, h) for s, h in pairs]
    vh = [tile(v, s, h) for s, h in pairs]
    m_all = [jnp.where(mask, _bdot_nt(jnp.concatenate([ah[i], rh[i]], axis=0),
                                      jnp.concatenate([tile(b_t, s, h), tile(k_t, s, h)], axis=0)), 0.0)
             for i, (s, h) in enumerate(pairs)]
    m_top = [m[0:C, :] for m in m_all]
    m_bot = [m[C:2 * C, :] for m in m_all]
    akv = [_bdot(m_top[i], jnp.concatenate([zeros_cn, vh[i]], axis=0)) for i in range(len(pairs))]
    power = [m[:, 0:C] for m in m_top]
    t_inv = [eye + p for p in power]
    for _ in range(levels - 1):
        power = [_bdot(p, p) for p in power]
        t_inv = [t + _bdot(t, p) for t, p in zip(t_inv, power)]
    w_mat = [_bdot(t_inv[i], ah[i]) for i in range(len(pairs))]
    u0 = [_bdot(t_inv[i], akv[i]) for i in range(len(pairs))]
    state = None if whole_seqs else [state_ref[h] for h in heads]
    y_rows = []
    for s in range(n_sub):
        at = lambda h: s * RWKV_HEADS + h
        if whole_seqs:
            state = [s0_ref[s, h] for h in heads]
        x = [_bdot_nt(jnp.concatenate([w_mat[at(h)], rh[at(h)]], axis=0), state[h]) for h in heads]
        uv = [jnp.concatenate([x[h][0:C, :] + u0[at(h)], vh[at(h)]], axis=0) for h in heads]
        y_rows.append(jnp.concatenate([x[h][C:2 * C, :] + _bdot(m_bot[at(h)], uv[h]) for h in heads], axis=1))
        state = [state[h] * tile(g_end, s, h)[0:1, :]
                 + _bdot_tn(uv[h], jnp.concatenate([tile(b_e, s, h), tile(k_e, s, h)], axis=0)) for h in heads]
        if whole_seqs:
            for h in heads:
                sout_ref[s, h] = state[h]
    if not whole_seqs:
        for h in heads:
            state_ref[h] = state[h]
    y = jnp.concatenate(y_rows, axis=0)

    mean = _bdot(y, seg) * (1.0 / N)
    yc = y - mean
    var = _bdot(yc * yc, seg) * (1.0 / N)
    yn = yc * lax.rsqrt(var + GN_EPS) * lg_ref[...] + lb_ref[...]
    bonus = _bdot(r * k * rk_ref[...], seg) * v
    y_ref[...] = ((yn + bonus) * g).astype(y_ref.dtype)

    if not whole_seqs:
        @pl.when(c == nc - 1)
        def _():
            sout_ref[0] = state_ref[...]


def _rwkv(pm, plo, shift_m, shift_l, s0, wts, seq_len, chunk, chunks_per_step):
    t_total = pm.shape[0]
    n_seq = t_total // seq_len
    whole_seqs = chunk == seq_len
    rows = chunk * chunks_per_step
    seqs = chunks_per_step if whole_seqs else 1
    ncnk = 1 if whole_seqs else seq_len // rows
    c8 = rows // SUBLANES
    prev8 = lambda b, c: (jnp.maximum(b * (rows if whole_seqs else seq_len) // SUBLANES + c * c8 - 1, 0), 0)
    cur = lambda b, c: (b * ncnk + c, 0)
    per_seq3 = lambda n: pl.BlockSpec((seqs, 1, n), lambda b, c: (b, 0, 0))
    state_spec = pl.BlockSpec((seqs, RWKV_HEADS, RWKV_HEAD_DIM, RWKV_HEAD_DIM), lambda b, c: (b, 0, 0, 0))
    return pl.pallas_call(
        functools.partial(_rwkv_body, chunk=chunk, whole_seqs=whole_seqs),
        grid=(n_seq // seqs, ncnk),
        in_specs=[pl.BlockSpec((rows, 3 * RWKV_WIDTH), cur), pl.BlockSpec((rows, LORA_PAD), cur),
                  pl.BlockSpec((SUBLANES, 3 * RWKV_WIDTH), prev8), pl.BlockSpec((SUBLANES, LORA_PAD), prev8),
                  per_seq3(3 * RWKV_WIDTH), per_seq3(LORA_PAD), state_spec]
                 + [_full_spec(a) for a in wts],
        out_specs=[pl.BlockSpec((rows, RWKV_WIDTH), cur), state_spec],
        out_shape=[jax.ShapeDtypeStruct((t_total, RWKV_WIDTH), BF16 if chunk % 16 == 0 else F32),
                   jax.ShapeDtypeStruct(s0.shape, F32)],
        scratch_shapes=[pltpu.VMEM((RWKV_HEADS, RWKV_HEAD_DIM, RWKV_HEAD_DIM), F32)],
        compiler_params=_params("parallel", "arbitrary"),
        name="rwkv",
    )(pm, plo, pm, plo, shift_m, shift_l, s0, *wts)


def _lambda(lam_ref, lam_init):
    lv = lam_ref[...]
    return (jnp.exp(jnp.sum(lv[0:1, :] * lv[1:2, :], axis=-1, keepdims=True))
            - jnp.exp(jnp.sum(lv[2:3, :] * lv[3:4, :], axis=-1, keepdims=True)) + lam_init)


def _subln(o, g, lam_init):
    return o * lax.rsqrt(jnp.mean(o * o, axis=-1, keepdims=True) + NORM_EPS) * g * (1.0 - lam_init)


def _attn_prompt_body(qi_ref, ki_ref, q_ref, kt_ref, v_ref, lam_ref, sg_ref, o_ref,
                      qm_ref, m_ref, acc_ref, *, lam_init):
    p = pl.program_id(2)
    qi = qi_ref[p]
    ki = ki_ref[p]
    tq = q_ref.shape[0]
    tk = kt_ref.shape[1]
    dv = v_ref.shape[1]

    @pl.when(ki == 0)
    def _():
        q = q_ref[...]
        lane = lax.broadcasted_iota(I32, q.shape, 1)
        qm_ref[0] = jnp.where(lane < DIFF_DK, q, jnp.zeros_like(q))
        qm_ref[1] = jnp.where(lane >= DIFF_DK, q, jnp.zeros_like(q))
        m_ref[...] = jnp.full(m_ref.shape, NEG_INF, F32)
        acc_ref[...] = jnp.zeros(acc_ref.shape, F32)

    def accumulate(masked):
        kt = kt_ref[...].astype(BF16)
        v1 = jnp.concatenate([v_ref[...], jnp.ones((tk, dv), BF16)], axis=1)
        for c in range(2):
            for r0 in range(0, tq, ATTN_ROW_CHUNK):
                rows = slice(r0, r0 + ATTN_ROW_CHUNK)
                ncol = min(tk, r0 + ATTN_ROW_CHUNK) if masked else tk
                s = jnp.dot(qm_ref[c, rows, :], kt[:, 0:ncol], preferred_element_type=F32)
                if masked:
                    row = lax.broadcasted_iota(I32, s.shape, 0) + r0
                    col = lax.broadcasted_iota(I32, s.shape, 1)
                    s = jnp.where(col <= row, s, NEG_INF)
                m_old = m_ref[c, rows, :]
                m_new = jnp.maximum(m_old, jnp.max(s, axis=-1, keepdims=True))
                alpha = jnp.exp(m_old - m_new)
                pr = jnp.exp(s - jnp.concatenate([m_new] * (ncol // LANES), axis=1))
                acc_ref[c, rows, :] = (jnp.concatenate([alpha] * (2 * dv // LANES), axis=1) * acc_ref[c, rows, :]
                                       + jnp.dot(pr.astype(BF16), v1[0:ncol, :], preferred_element_type=F32))
                m_ref[c, rows, :] = m_new

    @pl.when(ki < qi)
    def _():
        accumulate(False)

    @pl.when(ki == qi)
    def _():
        accumulate(True)
        lam = _lambda(lam_ref, lam_init)
        a0 = acc_ref[0]
        a1 = acc_ref[1]
        o = a0[:, 0:dv] / a0[:, dv:2 * dv] - lam * (a1[:, 0:dv] / a1[:, dv:2 * dv])
        o_ref[...] = _subln(o, sg_ref[...], lam_init).astype(o_ref.dtype)


def _attn_prompt(qb, kt, vb, lam_vecs, subln_g, seq_len, lam_init):
    t_total = qb.shape[0]
    nb = t_total // seq_len
    tq = min(ATTN_TILE, seq_len)
    nq = seq_len // tq
    pairs = [(i, j) for i in range(nq) for j in range(i + 1)]
    qi_tab = jnp.asarray([a for a, _ in pairs], I32)
    ki_tab = jnp.asarray([b for _, b in pairs], I32)
    qmap = lambda b, h, p, qi, ki: (b * nq + qi[p], h)
    vmap = lambda b, h, p, qi, ki: (b * nq + ki[p], h)
    ktmap = lambda b, h, p, qi, ki: (b * DIFF_HEADS + h, ki[p])
    grid_spec = pltpu.PrefetchScalarGridSpec(
        num_scalar_prefetch=2,
        grid=(nb, DIFF_HEADS, len(pairs)),
        in_specs=[pl.BlockSpec((tq, DIFF_DV), qmap), pl.BlockSpec((2 * DIFF_DK, tq), ktmap),
                  pl.BlockSpec((tq, DIFF_DV), vmap),
                  pl.BlockSpec(lam_vecs.shape, lambda *_: (0, 0)), pl.BlockSpec(subln_g.shape, lambda *_: (0, 0))],
        out_specs=pl.BlockSpec((tq, DIFF_DV), qmap),
        scratch_shapes=[pltpu.VMEM((2, tq, DIFF_DV), BF16), pltpu.VMEM((2, tq, LANES), F32),
                        pltpu.VMEM((2, tq, 2 * DIFF_DV), F32)],
    )
    return pl.pallas_call(
        functools.partial(_attn_prompt_body, lam_init=lam_init),
        grid_spec=grid_spec,
        out_shape=jax.ShapeDtypeStruct((t_total, DIFF_V_WIDTH), BF16),
        compiler_params=_params("parallel", "parallel", "arbitrary"),
        name="attn_prompt",
    )(qi_tab, ki_tab, qb, kt, vb, lam_vecs, subln_g)


def _attn_sample_body(pt_ref, q_ref, k_ref, v_ref, lam_ref, sg_ref, *rest, n_pages, lam_init):
    kp_refs = rest[:n_pages]
    vp_refs = rest[n_pages:2 * n_pages]
    o_ref = rest[2 * n_pages]
    s_new = q_ref.shape[0]
    n_maps = DIFF_HEADS * 2
    page = kp_refs[0].shape[1]
    nrow = n_maps * s_new

    def value_page(vr):
        return jnp.concatenate([vr[pl.ds(h, page, stride=DIFF_HEADS), :] for h in range(DIFF_HEADS)], axis=1)

    qt = jnp.concatenate([q_ref[...] * DIFF_SCALE] * (DIFF_HEADS * 2), axis=0)
    row = lax.broadcasted_iota(I32, qt.shape, 0)
    col = lax.broadcasted_iota(I32, qt.shape, 1)
    qbd = jnp.where(col // DIFF_DK == row // s_new, qt, 0.0).astype(BF16)
    pad = jnp.zeros((page - s_new, k_ref.shape[1]), F32)
    k_new = jnp.concatenate([k_ref[...], pad], axis=0)
    v_new = jnp.concatenate([v_ref[...], pad], axis=0)
    scores = [_bdot(qbd, kr[...]) for kr in kp_refs]
    s_n = _bdot_nt(qbd, k_new)
    rn = lax.broadcasted_iota(I32, s_n.shape, 0)
    cn = lax.broadcasted_iota(I32, s_n.shape, 1)
    scores.append(jnp.where(cn <= rn % s_new, s_n, NEG_INF))
    m = functools.reduce(jnp.maximum, [jnp.max(s, axis=-1, keepdims=True) for s in scores])
    values = [value_page(vr) for vr in vp_refs] + [v_new]
    l = jnp.zeros_like(m)
    acc = jnp.zeros((nrow, v_new.shape[1]), F32)
    for s, val in zip(scores, values):
        pr = jnp.exp(s - m)
        l = l + jnp.sum(pr, axis=-1, keepdims=True)
        acc = acc + _bdot(pr, val)
    full = acc / l
    lam = _lambda(lam_ref, lam_init)
    outs = []
    for h in range(DIFF_HEADS):
        r0 = h * 2 * s_new
        cols = slice(h * DIFF_DV, (h + 1) * DIFF_DV)
        o = full[r0:r0 + s_new, cols] - lam * full[r0 + s_new:r0 + 2 * s_new, cols]
        outs.append(_subln(o, sg_ref[...], lam_init))
    o_ref[...] = jnp.concatenate(outs, axis=1)


def _attn_sample(q, k, v, cache_k, cache_v, n_pool_pages, page_table, page_offset, lam_vecs, subln_g, s_new,
                 lam_init):
    nseq, n_pages = page_table.shape
    width = q.shape[1]
    k_rows = cache_k.shape[0] // n_pool_pages
    v_rows = cache_v.shape[0] // n_pool_pages
    new_spec = pl.BlockSpec((s_new, width), lambda b, pt: (b, 0))
    page_spec = lambda rows, lanes, j: pl.BlockSpec((rows, lanes), lambda b, pt: (pt[b, j] + page_offset, 0))
    grid_spec = pltpu.PrefetchScalarGridSpec(
        num_scalar_prefetch=1,
        grid=(nseq,),
        in_specs=[new_spec, new_spec, new_spec,
                  pl.BlockSpec(lam_vecs.shape, lambda *_: (0, 0)), pl.BlockSpec(subln_g.shape, lambda *_: (0, 0))]
                 + [page_spec(k_rows, cache_k.shape[1], j) for j in range(n_pages)]
                 + [page_spec(v_rows, DIFF_DV, j) for j in range(n_pages)],
        out_specs=new_spec,
    )
    return pl.pallas_call(
        functools.partial(_attn_sample_body, n_pages=n_pages, lam_init=lam_init),
        grid_spec=grid_spec,
        out_shape=jax.ShapeDtypeStruct((nseq * s_new, width), F32),
        compiler_params=_params("parallel"),
        name="attn_sample",
    )(page_table, q, k, v, lam_vecs, subln_g, *([cache_k] * n_pages), *([cache_v] * n_pages))


def _outproj_body(yr_ref, o_ref, x_ref, g1_ref, g2n_ref, sh_ref, sc_ref, g2_ref, woa_ref, wob_ref,
                  wsg_ref, wsu_ref, wsd_ref, wrh_ref, wrl_ref,
                  base_ref, h2t_ref, lg_ref):
    tm = x_ref.shape[0]
    mixed = (jnp.dot(yr_ref[...].astype(BF16), woa_ref[...], preferred_element_type=F32)
             + jnp.dot(o_ref[...].astype(BF16), wob_ref[...], preferred_element_type=F32))
    x1 = x_ref[...] + g1_ref[...] * mixed
    h2 = _rms(x1, g2n_ref[...]) * (1.0 + sc_ref[...]) + sh_ref[...]
    h2b = h2.astype(BF16)
    hidden = _silu(jnp.dot(h2b, wsg_ref[...], preferred_element_type=F32)) * jnp.dot(
        h2b, wsu_ref[...], preferred_element_type=F32)
    shared = jnp.dot(hidden.astype(BF16), wsd_ref[...], preferred_element_type=F32)
    base_ref[...] = x1 + g2_ref[...] * shared
    h2l = (h2 - h2b.astype(F32)).astype(BF16)
    nt = lambda a, b: lax.dot_general(a, b, (((1,), (1,)), ((), ())), preferred_element_type=F32)
    lg_ref[...] = nt(wrh_ref[...], h2b) + nt(wrl_ref[...], h2b) + nt(wrh_ref[...], h2l)
    for c in range(ROW_TILE):
        h2t_ref[pl.ds(c, tm, stride=ROW_TILE), :] = h2[:, c * LANES:(c + 1) * LANES]


def _outproj(yr, o, x, gate1, norm2_g, shift2, scale2, gate2, wts, tm, tiles_per_seq):
    t, d = x.shape
    row = lambda n: pl.BlockSpec((tm, n), lambda i: (i, 0))
    mod = lambda m: _mod_spec(m, tm, tiles_per_seq)
    return pl.pallas_call(
        _outproj_body,
        grid=(t // tm,),
        in_specs=[row(RWKV_WIDTH), row(DIFF_V_WIDTH), row(d), mod(gate1), _full_spec(norm2_g), mod(shift2),
                  mod(scale2), mod(gate2)] + [_full_spec(a) for a in wts],
        out_specs=[row(d), pl.BlockSpec((tm * ROW_TILE, LANES), lambda i: (i, 0)),
                   pl.BlockSpec((N_EXPERTS, tm), lambda i: (0, i))],
        out_shape=[jax.ShapeDtypeStruct((t, d), F32),
                   jax.ShapeDtypeStruct((t * ROW_TILE, LANES), F32),
                   jax.ShapeDtypeStruct((N_EXPERTS, t), F32)],
        compiler_params=_params("parallel"),
        name="outproj",
    )(yr, o, x, gate1, norm2_g, shift2, scale2, gate2, *wts)


def _router_body(lg_ref, bias_ref, idx_ref, gate_ref, pos_ref, cnt_ref, run_ref):
    i = pl.program_id(0)
    tm = lg_ref.shape[1]

    @pl.when(i == 0)
    def _():
        run_ref[...] = jnp.zeros(run_ref.shape, F32)

    scores = _sigmoid(lg_ref[...])
    biased = scores + bias_ref[...]
    erow = lax.broadcasted_iota(I32, (N_EXPERTS, tm), 0)
    grow = lax.broadcasted_iota(I32, (GROUP_SIZE, tm), 0)

    def first_argmax(x, rows, limit):
        mx = jnp.max(x, axis=0, keepdims=True)
        return mx, jnp.min(jnp.where(x == mx, rows, limit), axis=0, keepdims=True)

    group_scores = []
    for gidx in range(N_GROUPS):
        xg = biased[gidx * GROUP_SIZE:(gidx + 1) * GROUP_SIZE, :]
        m1, i1 = first_argmax(xg, grow, GROUP_SIZE)
        m2 = jnp.max(jnp.where(grow == i1, NEG_INF, xg), axis=0, keepdims=True)
        group_scores.append(m1 + m2)
    gs = jnp.concatenate(group_scores, axis=0)
    g8 = lax.broadcasted_iota(I32, (N_GROUPS, tm), 0)
    chosen = jnp.zeros((N_GROUPS, tm), I32)
    for _ in range(TOPK_GROUPS):
        _, gi = first_argmax(gs, g8, N_GROUPS)
        hit = g8 == gi
        chosen = jnp.where(hit, 1, chosen)
        gs = jnp.where(hit, NEG_INF, gs)
    cand = jnp.concatenate(
        [jnp.where(chosen[gidx:gidx + 1, :] > 0, biased[gidx * GROUP_SIZE:(gidx + 1) * GROUP_SIZE, :], NEG_INF)
         for gidx in range(N_GROUPS)], axis=0)

    idxs, raws = [], []
    onehot = jnp.zeros((N_EXPERTS, tm), F32)
    for _ in range(TOP_K):
        _, ei = first_argmax(cand, erow, N_EXPERTS)
        hit = erow == ei
        idxs.append(ei)
        raws.append(jnp.sum(jnp.where(hit, scores, 0.0), axis=0, keepdims=True))
        onehot = onehot + hit.astype(F32)
        cand = jnp.where(hit, NEG_INF, cand)
    raw = jnp.concatenate(raws, axis=0)
    gate_ref[...] = raw / jnp.sum(raw, axis=0, keepdims=True) * ROUTE_SCALE
    idx_ref[...] = jnp.concatenate(idxs, axis=0)

    ti = lax.broadcasted_iota(I32, (tm, tm), 0)
    tj = lax.broadcasted_iota(I32, (tm, tm), 1)
    oh = onehot.astype(BF16)
    before = jnp.dot(oh, (ti < tj).astype(BF16), preferred_element_type=F32) + run_ref[...]
    run_ref[...] = run_ref[...] + jnp.dot(oh, jnp.ones((tm, tm), BF16), preferred_element_type=F32)
    pos_ref[...] = jnp.concatenate(
        [jnp.sum(jnp.where(erow == ei, before, 0.0), axis=0, keepdims=True) for ei in idxs], axis=0).astype(I32)
    cnt_ref[...] = run_ref[...]


def _router(logits_t, bias_col):
    t = logits_t.shape[1]
    tm = ROUTE_TILE
    tok = pl.BlockSpec((TOP_K, tm), lambda i: (0, i))
    return pl.pallas_call(
        _router_body,
        grid=(t // tm,),
        in_specs=[pl.BlockSpec((N_EXPERTS, tm), lambda i: (0, i)), _full_spec(bias_col)],
        out_specs=[tok, tok, tok, pl.BlockSpec((N_EXPERTS, tm), lambda i: (0, 0))],
        out_shape=[jax.ShapeDtypeStruct((TOP_K, t), I32), jax.ShapeDtypeStruct((TOP_K, t), F32),
                   jax.ShapeDtypeStruct((TOP_K, t), I32), jax.ShapeDtypeStruct((N_EXPERTS, tm), F32)],
        scratch_shapes=[pltpu.VMEM((N_EXPERTS, tm), F32)],
        compiler_params=_params("arbitrary"),
        name="router",
    )(logits_t, bias_col)


def _assign_body(idx_ref, pos_ref, start_ref, dest_ref):
    tm = idx_ref.shape[1]
    erow = lax.broadcasted_iota(I32, (N_EXPERTS, tm), 0)
    start = start_ref[...]
    idx = idx_ref[...]
    first = jnp.concatenate(
        [jnp.sum(jnp.where(erow == idx[j:j + 1, :], start, 0.0), axis=0, keepdims=True) for j in range(TOP_K)],
        axis=0)
    dest_ref[...] = first.astype(I32) + pos_ref[...]


def _assign(idx, pos, start_col):
    t = idx.shape[1]
    tm = ROUTE_TILE
    tok = pl.BlockSpec((TOP_K, tm), lambda i: (0, i))
    return pl.pallas_call(
        _assign_body,
        grid=(t // tm,),
        in_specs=[tok, tok, _full_spec(start_col)],
        out_specs=tok,
        out_shape=jax.ShapeDtypeStruct((TOP_K, t), I32),
        compiler_params=_params("parallel"),
        name="assign",
    )(idx, pos, start_col)


def _row_copy(src, src_row, dst, dst_row, sem):
    return pltpu.make_async_copy(src.at[pl.ds(pl.multiple_of(src_row * ROW_TILE, ROW_TILE), ROW_TILE), :],
                                 dst.at[pl.ds(pl.multiple_of(dst_row * ROW_TILE, ROW_TILE), ROW_TILE), :], sem)


def _dispatch_body(zflag_ref, dest_ref, h_ref, xs_ref, zero_ref, hbuf_ref, zsem, lsem, rsem):
    i = pl.program_id(0)
    n = pl.num_programs(0)
    tm = dest_ref.shape[1]
    tile_rows = tm * ROW_TILE
    blk_rows = EXPERT_BLOCK * ROW_TILE
    n_blocks = xs_ref.shape[0] // blk_rows

    def load(tile, slot):
        rows = pl.ds(pl.multiple_of(tile * tile_rows, tile_rows), tile_rows)
        return pltpu.make_async_copy(h_ref.at[rows, :], hbuf_ref.at[slot], lsem.at[slot])

    def scatter(slot):
        def issue_rows(t, carry):
            for j in range(TOP_K):
                _row_copy(hbuf_ref.at[slot], t, xs_ref, dest_ref[j, t], rsem.at[slot]).start(priority=j % 2)
            return carry

        lax.fori_loop(0, tm, issue_rows, 0)

    def wait_scatter(slot):
        for _ in range(TOP_K):
            pltpu.make_async_copy(hbuf_ref.at[slot], xs_ref.at[pl.ds(0, tile_rows), :], rsem.at[slot]).wait()

    def zero_copy(b):
        start = pl.multiple_of(b * blk_rows, blk_rows)
        return pltpu.make_async_copy(zero_ref, xs_ref.at[pl.ds(start, blk_rows), :], zsem)

    @pl.when(i == 0)
    def _():
        zero_ref[...] = jnp.zeros(zero_ref.shape, F32)

        def issue(b, carry):
            @pl.when(zflag_ref[b] > 0)
            def _():
                zero_copy(b).start()
            return carry

        def drain(b, carry):
            @pl.when(zflag_ref[b] > 0)
            def _():
                zero_copy(b).wait()
            return carry

        lax.fori_loop(0, n_blocks, issue, 0)
        lax.fori_loop(0, n_blocks, drain, 0)
        load(0, 0).start()

        @pl.when(n > 1)
        def _():
            load(1, 1).start()

    n_buf = hbuf_ref.shape[0]
    for k in range(n_buf):
        @pl.when(i % n_buf == k)
        def _():
            prev = (k + n_buf - 1) % n_buf
            load(i, k).wait()
            scatter(k)

            @pl.when(i >= 1)
            def _():
                wait_scatter(prev)

            @pl.when(i + 2 < n)
            def _():
                load(i + 2, prev).start()

            @pl.when(i == n - 1)
            def _():
                wait_scatter(k)


def _dispatch(h2t, dest, zero_flags, n_rows):
    t = dest.shape[1]
    tm = ROUTE_TILE
    grid_spec = pltpu.PrefetchScalarGridSpec(
        num_scalar_prefetch=1,
        grid=(t // tm,),
        in_specs=[pl.BlockSpec((TOP_K, tm), lambda i, *_: (0, i), memory_space=pltpu.SMEM),
                  pl.BlockSpec(memory_space=pl.ANY)],
        out_specs=pl.BlockSpec(memory_space=pl.ANY),
        scratch_shapes=[pltpu.VMEM((EXPERT_BLOCK * ROW_TILE, LANES), F32),
                        pltpu.VMEM((3, tm * ROW_TILE, LANES), F32),
                        pltpu.SemaphoreType.DMA, pltpu.SemaphoreType.DMA((3,)), pltpu.SemaphoreType.DMA((3,))],
    )
    return pl.pallas_call(
        _dispatch_body,
        grid_spec=grid_spec,
        out_shape=jax.ShapeDtypeStruct((n_rows * ROW_TILE, LANES), F32),
        compiler_params=_params("arbitrary"),
        name="dispatch",
    )(zero_flags, dest, h2t)


def _expert_body(first_ref, count_ref, nu_ref, xs_ref, wg_ref, wu_ref, wd_ref, y_ref,
                 xin_ref, yout_ref, wgb_ref, wub_ref, wdb_ref, in_sem, out_sem):
    e = pl.program_id(0)
    rows = EXPERT_BLOCK
    blk_rows = EXPERT_BLOCK * ROW_TILE
    nb = count_ref[e]
    b0 = first_ref[e]

    n_used = nu_ref[0]
    n_in = xin_ref.shape[0]
    n_out = yout_ref.shape[0]
    ahead = n_in - 1
    part = blk_rows // EXPERT_DMA_SPLIT

    def hbm_rows(g, k=0, n=blk_rows):
        return pl.ds(pl.multiple_of(g * blk_rows + k * part, part), n)

    def start_in(g):
        slot = g % n_in
        for k in range(EXPERT_DMA_SPLIT):
            pltpu.make_async_copy(xs_ref.at[hbm_rows(g, k, part), :], xin_ref.at[slot, pl.ds(k * part, part), :],
                                  in_sem.at[slot]).start()

    def wait_in(g):
        slot = g % n_in
        pltpu.make_async_copy(xs_ref.at[hbm_rows(g), :], xin_ref.at[slot], in_sem.at[slot]).wait()

    def start_out(g):
        slot = g % n_out
        for k in range(EXPERT_DMA_SPLIT):
            pltpu.make_async_copy(yout_ref.at[slot, pl.ds(k * part, part), :], y_ref.at[hbm_rows(g, k, part), :],
                                  out_sem.at[slot]).start()

    def wait_out(g):
        slot = g % n_out
        pltpu.make_async_copy(yout_ref.at[slot], y_ref.at[hbm_rows(g), :], out_sem.at[slot]).wait()

    @pl.when(e == 0)
    def _():
        for g in range(ahead):
            @pl.when(g < n_used)
            def _():
                start_in(g)

    @pl.when(nb > 0)
    def _():
        wgb_ref[...] = wg_ref[...].astype(BF16)
        wub_ref[...] = wu_ref[...].astype(BF16)
        wdb_ref[...] = wd_ref[...].astype(BF16)

        def step(b, carry):
            g = b0 + b
            wait_in(g)

            @pl.when(g + ahead < n_used)
            def _():
                start_in(g + ahead)

            @pl.when(g >= n_out)
            def _():
                wait_out(g - n_out)

            islot = g % n_in
            oslot = g % n_out
            x = jnp.concatenate([xin_ref[islot, pl.ds(c, rows, stride=ROW_TILE), :] for c in range(ROW_TILE)],
                                axis=1).astype(BF16)
            hidden = _silu(jnp.dot(x, wgb_ref[...], preferred_element_type=F32)) * jnp.dot(
                x, wub_ref[...], preferred_element_type=F32)
            y = jnp.dot(hidden.astype(BF16), wdb_ref[...], preferred_element_type=F32)
            for c in range(ROW_TILE):
                yout_ref[oslot, pl.ds(c, rows, stride=ROW_TILE), :] = y[:, c * LANES:(c + 1) * LANES]
            start_out(g)
            return carry

        lax.fori_loop(0, nb, step, 0)

    @pl.when(e == pl.num_programs(0) - 1)
    def _():
        def drain(g, carry):
            wait_out(g)
            return carry

        lax.fori_loop(jnp.maximum(n_used - n_out, 0), n_used, drain, 0)
        n_blocks = y_ref.shape[0] // blk_rows
        yout_ref[0] = jnp.zeros(yout_ref.shape[1:], F32)

        def tail(b, carry):
            cp = pltpu.make_async_copy(
                yout_ref.at[0], y_ref.at[pl.ds(pl.multiple_of(b * blk_rows, blk_rows), blk_rows), :], out_sem.at[0])
            cp.start()
            cp.wait()
            return carry

        lax.fori_loop(nu_ref[0], n_blocks, tail, 0)


def _experts(xs, first_block, block_count, n_used, we_gate, we_up, we_down):
    d, f = we_gate.shape[1], we_gate.shape[2]
    blk = (EXPERT_BLOCK * ROW_TILE, LANES)
    wspec = lambda a, b: pl.BlockSpec((None, a, b), lambda e, *_: (e, 0, 0))
    grid_spec = pltpu.PrefetchScalarGridSpec(
        num_scalar_prefetch=3,
        grid=(N_EXPERTS,),
        in_specs=[pl.BlockSpec(memory_space=pl.ANY), wspec(d, f), wspec(d, f), wspec(f, d)],
        out_specs=pl.BlockSpec(memory_space=pl.ANY),
        scratch_shapes=[pltpu.VMEM((EXPERT_IN_SLOTS,) + blk, F32), pltpu.VMEM((EXPERT_OUT_SLOTS,) + blk, F32),
                        pltpu.VMEM((d, f), BF16), pltpu.VMEM((d, f), BF16), pltpu.VMEM((f, d), BF16),
                        pltpu.SemaphoreType.DMA((EXPERT_IN_SLOTS,)), pltpu.SemaphoreType.DMA((EXPERT_OUT_SLOTS,))],
    )
    return pl.pallas_call(
        _expert_body,
        grid_spec=grid_spec,
        out_shape=jax.ShapeDtypeStruct(xs.shape, F32),
        compiler_params=_params("arbitrary"),
        name="experts",
    )(first_block, block_count, n_used, xs, we_gate, we_up, we_down)


def _combine_body(base_ref, g2_ref, gate_ref, dest_ref, next_ref, y_ref, out_ref, buf_ref, sem):
    i = pl.program_id(0)
    n = pl.num_programs(0)
    tm = base_ref.shape[0]
    slot_rows = tm * ROW_TILE

    def issue_tile(idx_ref, slot):
        def issue(t, carry):
            for j in range(TOP_K):
                _row_copy(y_ref, idx_ref[j, t], buf_ref.at[slot], j * tm + t, sem.at[slot]).start(priority=j % 2)
            return carry

        lax.fori_loop(0, tm, issue, 0)

    def finish_tile(slot):
        pltpu.make_async_copy(y_ref.at[pl.ds(0, TOP_K * slot_rows), :], buf_ref.at[slot], sem.at[slot]).wait()
        gates = gate_ref[...]
        gcols = [jnp.broadcast_to(gates[:, j:j + 1], (tm, LANES)) for j in range(TOP_K)]
        for c in range(ROW_TILE):
            cols = slice(c * LANES, (c + 1) * LANES)
            routed = sum(gcols[j] * buf_ref[slot, pl.ds(j * slot_rows + c, tm, stride=ROW_TILE), :]
                         for j in range(TOP_K))
            out_ref[:, cols] = base_ref[:, cols] + g2_ref[:, cols] * routed

    @pl.when(i == 0)
    def _():
        issue_tile(dest_ref, 0)

    for parity in range(2):
        @pl.when(i % 2 == parity)
        def _():
            @pl.when(i + 1 < n)
            def _():
                issue_tile(next_ref, 1 - parity)

            finish_tile(parity)


def _combine(base, gate2, gates, dest, y_rows, tok_offset, tiles_per_seq):
    t, d = base.shape
    tm = ROUTE_TILE
    off = tok_offset // tm
    steps = t // tm
    return pl.pallas_call(
        _combine_body,
        grid=(steps,),
        in_specs=[pl.BlockSpec((tm, d), lambda i: (i, 0)), _mod_spec(gate2, tm, tiles_per_seq),
                  pl.BlockSpec((tm, TOP_K), lambda i: (i + off, 0)),
                  pl.BlockSpec((TOP_K, tm), lambda i: (0, i + off), memory_space=pltpu.SMEM),
                  pl.BlockSpec((TOP_K, tm), lambda i: (0, jnp.minimum(i + 1, steps - 1) + off),
                               memory_space=pltpu.SMEM),
                  pl.BlockSpec(memory_space=pl.ANY)],
        out_specs=pl.BlockSpec((tm, d), lambda i: (i, 0)),
        out_shape=jax.ShapeDtypeStruct((t, d), F32),
        scratch_shapes=[pltpu.VMEM((2, TOP_K * tm * ROW_TILE, LANES), F32), pltpu.SemaphoreType.DMA((2,))],
        compiler_params=_params("arbitrary"),
        name="combine",
    )(base, gate2, gates, dest, dest, y_rows)


def _segment_ones(width, seg):
    ids = jnp.arange(width) // seg
    return (ids[:, None] == ids[None, :]).astype(BF16)


def _pad_rows(w, start, total):
    return jnp.zeros((total, w.shape[1]), w.dtype).at[start:start + w.shape[0]].set(w)


def kernel(x_prompt, x_sample, cache_k, cache_v, state_wkv, state_shift, page_table, c_prompt, c_sample, w_ada, b_ada, norm1_g, norm2_g, w_in, mu_shift, w0, w_lora_up, a0, a_lora_up, g_lora_up, k_k, k_a, r_k, lnx_g, lnx_b, qn_g, kn_g, lam_q1, lam_k1, lam_q2, lam_k2, subln_g, w_out, w_router, router_bias, we_gate, we_up, we_down, ws_gate, ws_up, ws_down):
    depth = w_in.shape[0]
    nb, seq, d = x_prompt.shape
    db, dseq, _ = x_sample.shape
    n_pool, page = cache_k.shape[1], cache_k.shape[2]
    tp, ts = nb * seq, db * dseq
    w3 = 3 * RWKV_WIDTH
    row = lambda a: a.reshape(1, -1)
    seg512 = _segment_ones(RWKV_WIDTH, RWKV_HEAD_DIM)
    tm_p = min(512, seq)
    tm_s = min(256, ts)
    chunk_p = min(64, seq)

    xp = x_prompt.reshape(tp, d)
    xs = x_sample.reshape(ts, d)
    cache_k2 = jnp.transpose(cache_k, (0, 1, 3, 4, 5, 2)).reshape(depth * n_pool * DIFF_QK_WIDTH, page)
    cache_v2 = cache_v.reshape(depth * n_pool * page * DIFF_HEADS, DIFF_DV)
    n_cond = nb + db
    cond = jnp.concatenate([c_prompt, c_sample], axis=0)
    cond = jnp.pad(cond, ((0, -n_cond % SUBLANES), (0, 0)))

    outs = {name: [] for name in ("kp", "vp", "wp", "sp", "ks", "vs", "ws", "ss")}
    for layer in range(depth):
        lam_init = 0.8 - 0.6 * math.exp(-0.3 * layer)
        mods = _ada(cond, w_ada[layer], row(b_ada[layer]))
        mod_p = [m.reshape(nb, 1, d) for m in jnp.split(mods[:nb], 6, axis=-1)]
        mod_s = [jnp.repeat(m, dseq, axis=0) for m in jnp.split(mods[nb:n_cond], 6, axis=-1)]

        wi = w_in[layer]
        in_wts = (wi[:, :w3].astype(BF16),
                  jnp.pad(wi[:, w3:RWKV_PROJ], ((0, 0), (0, LORA_PAD - LORA_WIDTH))).astype(BF16),
                  wi[:, RWKV_PROJ:RWKV_PROJ + DIFF_QK_WIDTH].astype(BF16),
                  wi[:, RWKV_PROJ + DIFF_QK_WIDTH:RWKV_PROJ + 2 * DIFF_QK_WIDTH].astype(BF16),
                  wi[:, RWKV_PROJ + 2 * DIFF_QK_WIDTH:].astype(BF16),
                  row(jnp.tile(qn_g[layer], DIFF_QK_WIDTH // DIFF_DK)),
                  row(jnp.tile(kn_g[layer], DIFF_QK_WIDTH // DIFF_DK)),
                  seg512)
        mu = mu_shift[layer]
        rwkv_wts = (row(mu[:w3]), row(jnp.pad(mu[w3:], (0, LORA_PAD - LORA_WIDTH))), row(w0[layer]),
                    _pad_rows(w_lora_up[layer], 0, LORA_PAD).astype(BF16), row(a0[layer]),
                    _pad_rows(a_lora_up[layer], DECAY_LORA, LORA_PAD).astype(BF16),
                    _pad_rows(g_lora_up[layer], DECAY_LORA + ICLR_LORA, LORA_PAD).astype(BF16),
                    row(k_k[layer]), row(k_a[layer]), row(r_k[layer]), row(lnx_g[layer]), row(lnx_b[layer]), seg512)
        lam_vecs = jnp.stack([lam_q1[layer], lam_k1[layer], lam_q2[layer], lam_k2[layer]])
        sg = row(subln_g[layer])
        wr = w_router[layer].T
        wr_hi = wr.astype(BF16)
        out_wts = (w_out[layer][:RWKV_WIDTH].astype(BF16), w_out[layer][RWKV_WIDTH:].astype(BF16),
                   ws_gate[layer].astype(BF16), ws_up[layer].astype(BF16), ws_down[layer].astype(BF16),
                   wr_hi, (wr - wr_hi.astype(F32)).astype(BF16))

        def token_mix(x, mod, tm, tiles_per_seq, seq_len, chunk, shift_prev, wkv0, long_seq):
            pm, plo, *qkv = _inproj(x, row(norm1_g[layer]), mod[0], mod[1], in_wts, tm, tiles_per_seq, long_seq)
            n_seq = x.shape[0] // seq_len
            shift_m = shift_prev[:, :, :w3]
            shift_l = jnp.pad(shift_prev[:, :, w3:], ((0, 0), (0, 0), (0, LORA_PAD - LORA_WIDTH)))
            if chunk == seq_len:
                per_step = RWKV_SEQS_PER_STEP if n_seq % RWKV_SEQS_PER_STEP == 0 else 1
            else:
                per_step = RWKV_CHUNKS_PER_STEP if seq_len % (chunk * RWKV_CHUNKS_PER_STEP) == 0 else 1
            yr, wkv = _rwkv(pm, plo, shift_m, shift_l, wkv0, rwkv_wts, seq_len, chunk, per_step)
            last = jnp.concatenate([pm.reshape(n_seq, seq_len, w3)[:, -1:],
                                    plo.reshape(n_seq, seq_len, LORA_PAD)[:, -1:, :LORA_WIDTH]], axis=-1)
            return yr, wkv, last, qkv

        yr_p, wkv_p, last_p, (kt_p, v4_p, qb_p, vb_p) = token_mix(
            xp, mod_p, tm_p, seq // tm_p, seq, chunk_p,
            jnp.zeros((nb, 1, RWKV_PROJ), F32), jnp.zeros((nb, RWKV_HEADS, RWKV_HEAD_DIM, RWKV_HEAD_DIM), F32), True)
        o_p = _attn_prompt(qb_p, kt_p, vb_p, lam_vecs, sg, seq, lam_init)
        yr_s, wkv_s, last_s, (q_s, k_s, v_s) = token_mix(
            xs, mod_s, tm_s, 1, dseq, dseq, state_shift[layer], state_wkv[layer], False)
        o_s = _attn_sample(q_s, k_s, v_s, cache_k2, cache_v2, depth * n_pool, page_table, layer * n_pool, lam_vecs,
                           sg, dseq, lam_init)

        base_p, h2t_p, lg_p = _outproj(yr_p, o_p, xp, mod_p[2], row(norm2_g[layer]), mod_p[3], mod_p[4], mod_p[5],
                                       out_wts, tm_p, seq // tm_p)
        base_s, h2t_s, lg_s = _outproj(yr_s, o_s, xs, mod_s[2], row(norm2_g[layer]), mod_s[3], mod_s[4], mod_s[5],
                                       out_wts, tm_s, 1)

        t_all = tp + ts
        idx, gate, pos, cnt = _router(jnp.concatenate([lg_p, lg_s], axis=1), router_bias[layer].reshape(-1, 1))
        counts = cnt[:, 0].astype(I32)
        padded = (counts + EXPERT_BLOCK - 1) // EXPERT_BLOCK * EXPERT_BLOCK
        end_padded = jnp.cumsum(padded)
        dest = _assign(idx, pos, (end_padded - padded).astype(F32).reshape(-1, 1))
        n_rows = (t_all * TOP_K + N_EXPERTS * (EXPERT_BLOCK - 1) + EXPERT_BLOCK - 1) // EXPERT_BLOCK * EXPERT_BLOCK
        n_blocks = n_rows // EXPERT_BLOCK
        blk_ids = jnp.arange(n_blocks, dtype=I32)
        n_used = (end_padded[-1:] // EXPERT_BLOCK).astype(I32)
        is_expert_end = jnp.any(end_padded[None, :] == (blk_ids[:, None] + 1) * EXPERT_BLOCK, axis=1)
        zero_flags = jnp.logical_or(blk_ids >= n_used[0], is_expert_end).astype(I32)
        rows_in = _dispatch(jnp.concatenate([h2t_p, h2t_s], axis=0), dest, zero_flags, n_rows)
        rows_out = _experts(rows_in, ((end_padded - padded) // EXPERT_BLOCK).astype(I32),
                            (padded // EXPERT_BLOCK).astype(I32), n_used, we_gate[layer], we_up[layer],
                            we_down[layer])
        gates_t = gate.T
        xp = _combine(base_p, mod_p[5], gates_t, dest, rows_out, 0, seq // ROUTE_TILE)
        xs = _combine(base_s, mod_s[5], gates_t, dest, rows_out, tp, 1)

        outs["kp"].append(jnp.transpose(kt_p.reshape(nb, DIFF_HEADS, 2, DIFF_DK, seq), (0, 4, 1, 2, 3)))
        outs["vp"].append(v4_p.reshape(nb, seq, DIFF_HEADS, DIFF_DV))
        outs["wp"].append(wkv_p)
        outs["sp"].append(last_p)
        outs["ks"].append(k_s.reshape(db, dseq, DIFF_HEADS, 2, DIFF_DK))
        outs["vs"].append(v_s.reshape(db, dseq, DIFF_HEADS, DIFF_DV))
        outs["ws"].append(wkv_s)
        outs["ss"].append(last_s)

    st = {name: jnp.stack(v) for name, v in outs.items()}
    return (xp.reshape(nb, seq, d), xs.reshape(db, dseq, d), st["kp"], st["vp"], st["wp"], st["sp"],
            st["ks"], st["vs"], st["ws"], st["ss"])
```
